```python
import math
import jax, jax.numpy as jnp
from jax import lax
import numpy as np

D_MODEL = 1024
BATCH = 16
SEQ = 4096
DEPTH = 4

HEAD_DIM = 64
BLOCK_Q = 128
ROPE_THETA = 500000.0
ROPE_DIMS = HEAD_DIM // 4
EPS = 1e-6
SGU_GROUPS = 4
SGU_WIDTH = SGU_GROUPS * HEAD_DIM
SGU_CHUNK = 128
SB_HEADS = 4
SB_WIDTH = SB_HEADS * HEAD_DIM
NSA_HEADS = 4
NSA_WIDTH = NSA_HEADS * HEAD_DIM
NSA_KV = 6 * HEAD_DIM
CMP_LEN = 32
CMP_STRIDE = 16
CMP_HIDDEN = 256
SEL_LEN = 64
SEL_TOPN = 16
WINDOW = 512
FORCE_SCORE = 1e4
N_BRANCH = 3
IN_WIDTHS = (2 * SGU_WIDTH, SB_WIDTH, SB_WIDTH, SB_WIDTH, NSA_WIDTH, NSA_KV, 3 * NSA_HEADS, N_BRANCH * D_MODEL)
C_IN = sum(IN_WIDTHS)
IN_SPLITS = tuple(int(v) for v in np.cumsum(IN_WIDTHS)[:-1])
D_FF = 2752
N_EXPERTS = 8
TOP_K = 2
D_FF_EXPERT = 3584
MOE_TOKEN_BLOCK = 512

kernel_name = "hybrid_sgu_stickbreak_nsa_moe"


def rmsnorm(x, g):
    xf = x.astype(jnp.float32)
    y = xf * lax.rsqrt(jnp.mean(xf * xf, axis=-1, keepdims=True) + EPS)
    return (y * g.astype(jnp.float32)).astype(x.dtype)


def rope_tables(s):
    inv_freq = ROPE_THETA ** (-jnp.arange(0, ROPE_DIMS, 2, dtype=jnp.float32) / ROPE_DIMS)
    ang = jnp.arange(s, dtype=jnp.float32)[:, None] * inv_freq[None, :]
    return jnp.cos(ang), jnp.sin(ang)


def rope_partial(x, cos, sin):
    half = ROPE_DIMS // 2
    xr = x[..., :ROPE_DIMS].astype(jnp.float32)
    x1, x2 = xr[..., :half], xr[..., half:]
    c, s = cos[None, :, None, :], sin[None, :, None, :]
    rot = jnp.concatenate([x1 * c - x2 * s, x2 * c + x1 * s], axis=-1).astype(x.dtype)
    return jnp.concatenate([rot, x[..., ROPE_DIMS:]], axis=-1)


def masked_softmax(s, mask, axis=-1):
    s = jnp.where(mask, s, -jnp.inf)
    m = jnp.max(s, axis=axis, keepdims=True)
    m = jnp.where(jnp.isfinite(m), m, 0.0)
    p = jnp.exp(s - m)
    d = jnp.sum(p, axis=axis, keepdims=True)
    return p / jnp.where(d > 0, d, 1.0)


def to_blocks(t, nb):
    return jnp.moveaxis(t.reshape(t.shape[0], nb, BLOCK_Q, *t.shape[2:]), 1, 0)


def from_blocks(t):
    t = jnp.moveaxis(t, 0, 1)
    return t.reshape(t.shape[0], t.shape[1] * t.shape[2], *t.shape[3:])


def sgu_mixer(z, norm_g, w_s, b_s):
    b, s, _ = z.shape
    z = jax.nn.gelu(z)
    u, v = jnp.split(z, 2, axis=-1)
    v = rmsnorm(v, norm_g)
    v = v.reshape(b, s // SGU_CHUNK, SGU_CHUNK, SGU_GROUPS, HEAD_DIM)
    w = jnp.where(jnp.tril(jnp.ones((SGU_CHUNK, SGU_CHUNK), bool)), w_s, 0.0).astype(v.dtype)
    mix = jnp.einsum('gts,bcsgd->bctgd', w, v) + b_s.T[:, :, None].astype(v.dtype)
    return u * mix.reshape(b, s, SGU_WIDTH)


def stick_breaking(q, k, v):
    b, s, h, d = q.shape
    nb = s // BLOCK_Q
    scale = d ** -0.5
    kpos = jnp.arange(s)

    def block(args):
        qi, i = args
        qpos = i * BLOCK_Q + jnp.arange(BLOCK_Q)
        mask = kpos[None, :] < qpos[:, None]
        z = jnp.einsum('bthd,bshd->bhts', qi, k).astype(jnp.float32) * scale
        log_beta = jax.nn.log_sigmoid(z)
        log_1m = jnp.where(mask, jax.nn.log_sigmoid(-z), 0.0)
        between = lax.cumsum(log_1m, axis=3, reverse=True) - log_1m
        a = jnp.where(mask, jnp.exp(log_beta + between), 0.0)
        return jnp.einsum('bhts,bshd->bthd', a.astype(v.dtype), v)

    out = lax.map(block, (to_blocks(q, nb), jnp.arange(nb)))
    return from_blocks(out).reshape(b, s, h * d)


def compress(t, pos_emb, w1, w2):
    b, s, d = t.shape
    m = (s - CMP_LEN) // CMP_STRIDE + 1
    idx = jnp.arange(m)[:, None] * CMP_STRIDE + jnp.arange(CMP_LEN)[None, :]
    blocks = t[:, idx] + pos_emb.astype(t.dtype)
    hid = jax.nn.gelu(blocks.reshape(b, m, CMP_LEN * d) @ w1)
    return hid @ w2


def nsa(q, k_cmp, v_cmp, k_slc, v_slc, k_win, v_win, gates, cmp_pos, cmp_w1, cmp_w2):
    b, s, h, d = q.shape
    nb = s // BLOCK_Q
    scale = d ** -0.5
    kc = compress(k_cmp, cmp_pos[0], cmp_w1[0], cmp_w2[0])
    vc = compress(v_cmp, cmp_pos[1], cmp_w1[1], cmp_w2[1])
    m = kc.shape[1]
    cmp_end = jnp.arange(m) * CMP_STRIDE + CMP_LEN - 1
    ns = s // SEL_LEN
    n_top = min(SEL_TOPN, ns)
    ks_blocks = k_slc.reshape(b, ns, SEL_LEN, d)
    vs_blocks = v_slc.reshape(b, ns, SEL_LEN, d)
    cs = (jnp.arange(m) * CMP_STRIDE)[:, None]
    ss = (jnp.arange(ns) * SEL_LEN)[None, :]
    overlap = jnp.clip(jnp.minimum(cs + CMP_LEN, ss + SEL_LEN) - jnp.maximum(cs, ss), 0).astype(jnp.float32) / CMP_LEN
    blk_ids = jnp.arange(ns)
    kw_pad = jnp.pad(k_win, ((0, 0), (WINDOW, 0), (0, 0)))
    vw_pad = jnp.pad(v_win, ((0, 0), (WINDOW, 0), (0, 0)))
    gather = jax.vmap(lambda blk, ix: blk[ix])

    def block(args):
        qi, gi, i = args
        qpos = i * BLOCK_Q + jnp.arange(BLOCK_Q)
        s_cmp = jnp.einsum('bthd,bmd->bhtm', qi, kc).astype(jnp.float32) * scale
        p_cmp = masked_softmax(s_cmp, cmp_end[None, :] <= qpos[:, None])
        o_cmp = jnp.einsum('bhtm,bmd->bthd', p_cmp.astype(vc.dtype), vc)
        imp = jnp.einsum('bhtm,mj->btj', p_cmp, overlap)
        valid = (blk_ids * SEL_LEN)[None, :] <= qpos[:, None]
        forced = (blk_ids[None, :] == 0) | (blk_ids[None, :] == (qpos // SEL_LEN)[:, None])
        score = jnp.where(valid, imp + jnp.where(forced, FORCE_SCORE, 0.0), -jnp.inf)
        top_val, top_idx = lax.top_k(score, n_top)
        sel_ok = jnp.isfinite(top_val)
        kg = gather(ks_blocks, top_idx)
        vg = gather(vs_blocks, top_idx)
        s_slc = jnp.einsum('bthd,btkld->bhtkl', qi, kg).astype(jnp.float32) * scale
        tok_pos = top_idx[..., None] * SEL_LEN + jnp.arange(SEL_LEN)
        m_slc = sel_ok[..., None] & (tok_pos <= qpos[None, :, None, None])
        p_slc = masked_softmax(s_slc.reshape(b, h, BLOCK_Q, n_top * SEL_LEN),
                               m_slc.reshape(b, 1, BLOCK_Q, n_top * SEL_LEN))
        o_slc = jnp.einsum('bhtkl,btkld->bthd', p_slc.reshape(s_slc.shape).astype(vg.dtype), vg)
        kw = lax.dynamic_slice_in_dim(kw_pad, i * BLOCK_Q, WINDOW + BLOCK_Q, axis=1)
        vw = lax.dynamic_slice_in_dim(vw_pad, i * BLOCK_Q, WINDOW + BLOCK_Q, axis=1)
        kwpos = i * BLOCK_Q - WINDOW + jnp.arange(WINDOW + BLOCK_Q)
        m_win = (kwpos[None, :] <= qpos[:, None]) & (kwpos[None, :] > qpos[:, None] - WINDOW) & (kwpos[None, :] >= 0)
        s_win = jnp.einsum('bthd,bsd->bhts', qi, kw).astype(jnp.float32) * scale
        p_win = masked_softmax(s_win, m_win)
        o_win = jnp.einsum('bhts,bsd->bthd', p_win.astype(vw.dtype), vw)
        return gi[..., 0:1] * o_cmp + gi[..., 1:2] * o_slc + gi[..., 2:3] * o_win

    out = lax.map(block, (to_blocks(q, nb), to_blocks(gates, nb), jnp.arange(nb)))
    return from_blocks(out).reshape(b, s, h * d)


def hybrid_mixer(hn, w_in, sgu_norm_g, sgu_w, sgu_b, cmp_pos, cmp_w1, cmp_w2, w_branch, w_out, cos, sin):
    b, s, _ = hn.shape
    proj = hn @ w_in
    a_in, sb_q, sb_k, sb_v, c_q, c_kv, c_g, br_g = jnp.split(proj, IN_SPLITS, axis=-1)
    heads = lambda t, n: t.reshape(b, s, n, HEAD_DIM)
    y_a = sgu_mixer(a_in, sgu_norm_g, sgu_w, sgu_b)
    y_b = stick_breaking(heads(sb_q, SB_HEADS), heads(sb_k, SB_HEADS), heads(sb_v, SB_HEADS))
    q = rope_partial(heads(c_q, NSA_HEADS), cos, sin)
    k_cmp, v_cmp, k_slc, v_slc, k_win, v_win = jnp.split(c_kv, 6, axis=-1)
    rk = lambda t: rope_partial(t[:, :, None, :], cos, sin)[:, :, 0, :]
    nsa_g = jax.nn.sigmoid(c_g.astype(jnp.float32)).reshape(b, s, NSA_HEADS, 3).astype(hn.dtype)
    y_c = nsa(q, rk(k_cmp), v_cmp, rk(k_slc), v_slc, rk(k_win), v_win, nsa_g, cmp_pos, cmp_w1, cmp_w2)
    g = jax.nn.sigmoid(br_g.astype(jnp.float32)).reshape(b, s, N_BRANCH, D_MODEL).astype(hn.dtype)
    merged = (g[:, :, 0] * (y_a @ w_branch[0]) + g[:, :, 1] * (y_b @ w_branch[1])
              + g[:, :, 2] * (y_c @ w_branch[2]))
    return merged @ w_out


def swiglu(h, w_gate, w_up, w_down):
    return (jax.nn.silu(h @ w_gate) * (h @ w_up)) @ w_down


def moe_swiglu(h, router_w, w_gate, w_up, w_down):
    b, s, d = h.shape
    n = b * s
    chunk = MOE_TOKEN_BLOCK if n % MOE_TOKEN_BLOCK == 0 else BLOCK_Q
    logits = (h @ router_w).astype(jnp.float32)
    top_val, top_idx = lax.top_k(logits, TOP_K)
    wts = jax.nn.softmax(top_val, axis=-1)
    gate = jnp.sum(jax.nn.one_hot(top_idx, N_EXPERTS, dtype=jnp.float32) * wts[..., None], axis=-2)

    def run(args):
        xc, gc = args
        a = jnp.einsum('nd,edf->nef', xc, w_gate)
        u = jnp.einsum('nd,edf->nef', xc, w_up)
        hid = jax.nn.silu(a) * u * gc[..., None].astype(xc.dtype)
        return jnp.einsum('nef,efd->nd', hid, w_down)

    out = lax.map(run, (h.reshape(n // chunk, chunk, d), gate.reshape(n // chunk, chunk, N_EXPERTS)))
    return out.reshape(b, s, d)


def setup_inputs(seed: int = 0) -> dict:
    key = jax.random.key(seed)
    ks = jax.random.split(key, 24)
    n_dense = (DEPTH + 1) // 2
    n_moe = DEPTH // 2
    nrm = lambda k, shape, sc: jax.random.normal(k, shape, jnp.float32) * sc
    return {
        "x": nrm(ks[0], (BATCH, SEQ, D_MODEL), 1.0),
        "norm1_g": 1.0 + nrm(ks[1], (DEPTH, D_MODEL), 0.02),
        "w_in": nrm(ks[2], (DEPTH, D_MODEL, C_IN), D_MODEL ** -0.5),
        "sgu_norm_g": 1.0 + nrm(ks[3], (DEPTH, SGU_WIDTH), 0.02),
        "sgu_w": nrm(ks[4], (DEPTH, SGU_GROUPS, SGU_CHUNK, SGU_CHUNK), SGU_CHUNK ** -0.5),
        "sgu_b": 1.0 + nrm(ks[5], (DEPTH, SGU_GROUPS, SGU_CHUNK), 0.1),
        "cmp_pos": nrm(ks[6], (DEPTH, 2, CMP_LEN, HEAD_DIM), 0.02),
        "cmp_w1": nrm(ks[7], (DEPTH, 2, CMP_LEN * HEAD_DIM, CMP_HIDDEN), (CMP_LEN * HEAD_DIM) ** -0.5),
        "cmp_w2": nrm(ks[8], (DEPTH, 2, CMP_HIDDEN, HEAD_DIM), CMP_HIDDEN ** -0.5),
        "w_branch": nrm(ks[9], (DEPTH, N_BRANCH, SB_WIDTH, D_MODEL), SB_WIDTH ** -0.5),
        "w_out": nrm(ks[10], (DEPTH, D_MODEL, D_MODEL), D_MODEL ** -0.5),
        "norm2_g": 1.0 + nrm(ks[11], (DEPTH, D_MODEL), 0.02),
        "ffn_w_gate": nrm(ks[12], (n_dense, D_MODEL, D_FF), D_MODEL ** -0.5),
        "ffn_w_up": nrm(ks[13], (n_dense, D_MODEL, D_FF), D_MODEL ** -0.5),
        "ffn_w_down": nrm(ks[14], (n_dense, D_FF, D_MODEL), D_FF ** -0.5),
        "router_w": nrm(ks[15], (n_moe, D_MODEL, N_EXPERTS), D_MODEL ** -0.5),
        "moe_w_gate": nrm(ks[16], (n_moe, N_EXPERTS, D_MODEL, D_FF_EXPERT), D_MODEL ** -0.5),
        "moe_w_up": nrm(ks[17], (n_moe, N_EXPERTS, D_MODEL, D_FF_EXPERT), D_MODEL ** -0.5),
        "moe_w_down": nrm(ks[18], (n_moe, N_EXPERTS, D_FF_EXPERT, D_MODEL), D_FF_EXPERT ** -0.5),
        "final_norm_g": 1.0 + nrm(ks[19], (D_MODEL,), 0.02),
    }


def reference(x, norm1_g, w_in, sgu_norm_g, sgu_w, sgu_b, cmp_pos, cmp_w1, cmp_w2, w_branch, w_out,
              norm2_g, ffn_w_gate, ffn_w_up, ffn_w_down, router_w, moe_w_gate, moe_w_up, moe_w_down,
              final_norm_g):
    cos, sin = rope_tables(x.shape[1])
    h = x
    for layer in range(DEPTH):
        hn = rmsnorm(h, norm1_g[layer])
        h = h + hybrid_mixer(hn, w_in[layer], sgu_norm_g[layer], sgu_w[layer], sgu_b[layer], cmp_pos[layer],
                             cmp_w1[layer], cmp_w2[layer], w_branch[layer], w_out[layer], cos, sin)
        hn = rmsnorm(h, norm2_g[layer])
        j = layer // 2
        if layer % 2 == 0:
            h = h + swiglu(hn, ffn_w_gate[j], ffn_w_up[j], ffn_w_down[j])
        else:
            h = h + moe_swiglu(hn, router_w[j], moe_w_gate[j], moe_w_up[j], moe_w_down[j])
    return rmsnorm(h, final_norm_g)
```

```python
import functools

import numpy as np
import jax
import jax.numpy as jnp
from jax import lax
from jax.experimental import pallas as pl
from jax.experimental.pallas import tpu as pltpu

F32 = jnp.float32
BF16 = jnp.bfloat16

D_MODEL = 1024
HEAD_DIM = 64
N_HEADS = 4
MIX_WIDTH = N_HEADS * HEAD_DIM
ROPE_DIMS = HEAD_DIM // 4
ROPE_THETA = 500000.0
EPS = 1e-6
SGU_CHUNK = 128
CMP_LEN = 32
CMP_STRIDE = 16
CMP_HIDDEN = 256
SEL_LEN = 64
SEL_TOPN = 16
WINDOW = 512
FORCE_SCORE = 1e4
N_BRANCH = 3
D_FF = 2752
N_EXPERTS = 8
D_FF_EXPERT = 3584
NEG = -1e30

LANES = 128
VMEM_LIMIT = 56 * 1024 * 1024

PROJ_TILE = 512
COL_A = 0
COL_SBQ, COL_SBK, COL_SBV = 512, 768, 1024
COL_VX = 1280
COL_CQ = 1536
COL_KX = 1792
COL_BRG = 2048
PROJ_W = COL_BRG + N_BRANCH * D_MODEL
ROPE_TILE = COL_CQ // PROJ_TILE


def _cparams(sem):
    return pltpu.CompilerParams(dimension_semantics=sem, vmem_limit_bytes=VMEM_LIMIT)


def _sigmoid(x):
    return 1.0 / (1.0 + jnp.exp(-x))


def _rms(x, g):
    return x * lax.rsqrt(jnp.mean(x * x, axis=-1, keepdims=True) + EPS) * g


def _norm_proj_kernel(h_ref, g_ref, w_ref, c_ref, s1_ref, s2_ref, o_ref, xn_ref):
    j = pl.program_id(1)

    @pl.when(j == 0)
    def _():
        xn_ref[...] = _rms(h_ref[...], g_ref[...]).astype(BF16)

    acc = jnp.dot(xn_ref[...], w_ref[...], preferred_element_type=F32)

    @pl.when(j != ROPE_TILE)
    def _():
        o_ref[...] = acc.astype(o_ref.dtype)

    @pl.when(j == ROPE_TILE)
    def _():
        c, s1, s2 = c_ref[...], s1_ref[...], s2_ref[...]
        for g in range(PROJ_TILE // LANES):
            x = acc[:, g * LANES:(g + 1) * LANES]
            y = x * c + pltpu.roll(x, LANES - 8, 1) * s1 + pltpu.roll(x, 8, 1) * s2
            o_ref[:, g * LANES:(g + 1) * LANES] = y.astype(o_ref.dtype)


def _norm_proj(h, g, w, rope_c, rope_s1, rope_s2, seq, tm=1024):
    n = h.shape[0]
    nseq = seq // tm
    return pl.pallas_call(
        _norm_proj_kernel,
        grid=(n // tm, PROJ_W // PROJ_TILE),
        in_specs=[
            pl.BlockSpec((tm, D_MODEL), lambda i, j: (i, 0)),
            pl.BlockSpec((1, D_MODEL), lambda i, j: (0, 0)),
            pl.BlockSpec((D_MODEL, PROJ_TILE), lambda i, j: (0, j)),
            pl.BlockSpec((tm, LANES), lambda i, j: (i % nseq, 0)),
            pl.BlockSpec((tm, LANES), lambda i, j: (i % nseq, 0)),
            pl.BlockSpec((tm, LANES), lambda i, j: (i % nseq, 0)),
        ],
        out_specs=pl.BlockSpec((tm, PROJ_TILE), lambda i, j: (i, j)),
        out_shape=jax.ShapeDtypeStruct((n, PROJ_W), BF16),
        scratch_shapes=[pltpu.VMEM((tm, D_MODEL), BF16)],
        compiler_params=_cparams(("parallel", "arbitrary")),
        name="norm_proj",
    )(h, g, w, rope_c, rope_s1, rope_s2)


def _sgu_kernel(z_ref, g_ref, w_ref, b_ref, o_ref):
    tm = z_ref.shape[0]
    a = jax.nn.gelu(z_ref[...].astype(F32))
    u = a[:, :MIX_WIDTH]
    v = _rms(a[:, MIX_WIDTH:], g_ref[...]).astype(BF16)
    row = lax.broadcasted_iota(jnp.int32, (SGU_CHUNK, SGU_CHUNK), 0)
    col = lax.broadcasted_iota(jnp.int32, (SGU_CHUNK, SGU_CHUNK), 1)
    ws = [jnp.where(row >= col, w_ref[gi], 0.0).astype(BF16) for gi in range(N_HEADS)]
    bias = b_ref[...]
    for c in range(tm // SGU_CHUNK):
        rows = slice(c * SGU_CHUNK, (c + 1) * SGU_CHUNK)
        mix = jnp.concatenate(
            [jnp.dot(ws[gi], v[rows, gi * HEAD_DIM:(gi + 1) * HEAD_DIM], preferred_element_type=F32)
             for gi in range(N_HEADS)], axis=1)
        o_ref[rows, :] = (u[rows, :] * (mix + bias)).astype(o_ref.dtype)


def _sgu(proj, g, w, bias, tm=512):
    n = proj.shape[0]
    return pl.pallas_call(
        _sgu_kernel,
        grid=(n // tm,),
        in_specs=[
            pl.BlockSpec((tm, 2 * MIX_WIDTH), lambda i: (i, COL_A // (2 * MIX_WIDTH))),
            pl.BlockSpec((1, MIX_WIDTH), lambda i: (0, 0)),
            pl.BlockSpec((N_HEADS, SGU_CHUNK, SGU_CHUNK), lambda i: (0, 0, 0)),
            pl.BlockSpec((SGU_CHUNK, MIX_WIDTH), lambda i: (0, 0)),
        ],
        out_specs=pl.BlockSpec((tm, MIX_WIDTH), lambda i: (i, 0)),
        out_shape=jax.ShapeDtypeStruct((n, MIX_WIDTH), BF16),
        compiler_params=_cparams(("parallel",)),
        name="sgu",
    )(proj, g, w, bias)


SB_T = 256
SB_SUB = 128


def _sb_kernel(qi_ref, ki_ref, q_ref, k_ref, v_ref, u_ref, o_ref, acc_ref, carry_ref):
    step = pl.program_id(1)
    qi = qi_ref[step]
    ki = ki_ref[step]

    @pl.when(ki == qi)
    def _():
        acc_ref[...] = jnp.zeros_like(acc_ref)
        carry_ref[...] = jnp.zeros_like(carry_ref)

    qpos = qi * SB_T + lax.broadcasted_iota(jnp.int32, (SB_T, SB_T), 0)
    kpos = ki * SB_T + lax.broadcasted_iota(jnp.int32, (SB_T, SB_T), 1)
    mask = kpos < qpos
    q, k, v = q_ref[...], k_ref[...], v_ref[...]
    u = u_ref[...]
    pvs = []
    for h in range(N_HEADS):
        hs = slice(h * HEAD_DIM, (h + 1) * HEAD_DIM)
        z = lax.dot_general(q[:, hs], k[:, hs], (((1,), (1,)), ((), ())), preferred_element_type=F32)
        log_beta = jnp.minimum(z, 0.0) - jnp.log1p(jnp.exp(-jnp.abs(z)))
        log_1m = jnp.where(mask, log_beta - z, 0.0)
        carry = carry_ref[h]
        pv = jnp.zeros((SB_T, HEAD_DIM), F32)
        for sb in reversed(range(SB_T // SB_SUB)):
            cs = slice(sb * SB_SUB, (sb + 1) * SB_SUB)
            l = log_1m[:, cs]
            hi = l.astype(BF16)
            lo = (l - hi.astype(F32)).astype(BF16)
            r = jnp.dot(jnp.concatenate([hi, lo], axis=1), u, preferred_element_type=F32)
            between = r[:, :SB_SUB] + carry
            a = jnp.where(mask[:, cs], jnp.exp(log_beta[:, cs] + between), 0.0)
            pv = pv + jnp.dot(a.astype(BF16), v[cs, hs], preferred_element_type=F32)
            carry = carry + r[:, SB_SUB:]
        carry_ref[h] = carry
        pvs.append(pv)
    acc_ref[...] += jnp.concatenate(pvs, axis=1)

    @pl.when(ki == 0)
    def _():
        o_ref[...] = acc_ref[...].astype(o_ref.dtype)


def _stick_breaking(proj, batch, seq):
    n = proj.shape[0]
    nq = seq // SB_T
    qi_tab = np.array([qi for qi in range(nq) for _ in range(qi + 1)], np.int32)
    ki_tab = np.array([ki for qi in range(nq) for ki in range(qi, -1, -1)], np.int32)
    j = np.arange(2 * SB_SUB)[:, None] % SB_SUB
    c = np.arange(2 * SB_SUB)[None, :]
    u = jnp.asarray(np.where(c < SB_SUB, j > c, True).astype(np.float32), BF16)
    qb, kb, vb = COL_SBQ // MIX_WIDTH, COL_SBK // MIX_WIDTH, COL_SBV // MIX_WIDTH
    grid_spec = pltpu.PrefetchScalarGridSpec(
        num_scalar_prefetch=2,
        grid=(batch, len(qi_tab)),
        in_specs=[
            pl.BlockSpec((SB_T, MIX_WIDTH), lambda b, s, qt, kt: (b * nq + qt[s], qb)),
            pl.BlockSpec((SB_T, MIX_WIDTH), lambda b, s, qt, kt: (b * nq + kt[s], kb)),
            pl.BlockSpec((SB_T, MIX_WIDTH), lambda b, s, qt, kt: (b * nq + kt[s], vb)),
            pl.BlockSpec((2 * SB_SUB, 2 * SB_SUB), lambda b, s, qt, kt: (0, 0)),
        ],
        out_specs=pl.BlockSpec((SB_T, MIX_WIDTH), lambda b, s, qt, kt: (b * nq + qt[s], 0)),
        scratch_shapes=[pltpu.VMEM((SB_T, MIX_WIDTH), F32), pltpu.VMEM((N_HEADS, SB_T, SB_SUB), F32)],
    )
    return pl.pallas_call(
        _sb_kernel,
        grid_spec=grid_spec,
        out_shape=jax.ShapeDtypeStruct((n, MIX_WIDTH), BF16),
        compiler_params=_cparams(("parallel", "arbitrary")),
        name="stick_breaking",
    )(jnp.asarray(qi_tab), jnp.asarray(ki_tab), proj, proj, proj, u)


def _compress_kernel(gk_ref, gv_ref, pos_ref, w1_ref, w2_ref, o_ref):
    half = CMP_STRIDE * HEAD_DIM
    outs = []
    for t, g_ref in enumerate((gk_ref, gv_ref)):
        g = g_ref[0].astype(F32)
        top = (g + pos_ref[t, :, :half]).astype(BF16)
        bot = (g + pos_ref[t, :, half:]).astype(BF16)
        a = jnp.dot(top, w1_ref[t, :half, :], preferred_element_type=F32)
        b = jnp.dot(bot, w1_ref[t, half:, :], preferred_element_type=F32)
        hid = jax.nn.gelu(a + pltpu.roll(b, b.shape[0] - 1, 0))
        outs.append(jnp.dot(hid.astype(BF16), w2_ref[t], preferred_element_type=F32))
    o_ref[0] = jnp.concatenate(outs, axis=1).astype(o_ref.dtype)


def _compress(gk, gv, pos, w1, w2):
    b, m, width = gk.shape
    return pl.pallas_call(
        _compress_kernel,
        grid=(b,),
        in_specs=[
            pl.BlockSpec((1, m, width), lambda i: (i, 0, 0)),
            pl.BlockSpec((1, m, width), lambda i: (i, 0, 0)),
            pl.BlockSpec((2, 1, CMP_LEN * HEAD_DIM), lambda i: (0, 0, 0)),
            pl.BlockSpec((2, CMP_LEN * HEAD_DIM, CMP_HIDDEN), lambda i: (0, 0, 0)),
            pl.BlockSpec((2, CMP_HIDDEN, HEAD_DIM), lambda i: (0, 0, 0)),
        ],
        out_specs=pl.BlockSpec((1, m, 2 * HEAD_DIM), lambda i: (i, 0, 0)),
        out_shape=jax.ShapeDtypeStruct((b, m, 2 * HEAD_DIM), BF16),
        compiler_params=_cparams(("parallel",)),
        name="nsa_compress",
    )(gk, gv, pos, w1, w2)


NSA_TQ = 128
NSA_CK = 512


def _nsa_kernel(q_ref, kx_ref, vx_ref, kw_ref, vw_ref, kvc_ref, ov_ref, ex_ref, o_ref,
                mask_ref, m_ref, l_ref, acc_ref):
    seq = kx_ref.shape[0]
    n_sel = seq // SEL_LEN
    n_cmp = kvc_ref.shape[1]
    n_chunks = seq // NSA_CK
    qb = pl.program_id(1)
    q0 = qb * NSA_TQ
    lane = lax.broadcasted_iota(jnp.int32, (NSA_TQ, LANES), 1)
    lo_half = lane < HEAD_DIM

    q_lo, q_hi = [], []
    for pair in range(N_HEADS // 2):
        x = q_ref[:, pair * LANES:(pair + 1) * LANES].astype(F32)
        xr = pltpu.roll(x, HEAD_DIM, 1)
        q_lo += [jnp.where(lo_half, x, 0.0).astype(BF16), jnp.where(lo_half, xr, 0.0).astype(BF16)]
        q_hi += [jnp.where(lo_half, 0.0, xr).astype(BF16), jnp.where(lo_half, 0.0, x).astype(BF16)]

    qpos_c = q0 + lax.broadcasted_iota(jnp.int32, (NSA_TQ, 1), 0)

    def softmax_rows(s, mask):
        s = jnp.where(mask, s, NEG)
        p = jnp.where(mask, jnp.exp(s - jnp.max(s, axis=-1, keepdims=True)), 0.0)
        d = jnp.sum(p, axis=-1, keepdims=True)
        return p / jnp.where(d > 0, d, 1.0)

    kvc = kvc_ref[0]
    cmp_end = lax.broadcasted_iota(jnp.int32, (NSA_TQ, n_cmp), 1) * CMP_STRIDE + (CMP_LEN - 1)
    m_cmp = cmp_end <= qpos_c
    r_cmp, p_sum = [], jnp.zeros((NSA_TQ, n_cmp), F32)
    for h in range(N_HEADS):
        s = lax.dot_general(q_lo[h], kvc, (((1,), (1,)), ((), ())), preferred_element_type=F32)
        p = softmax_rows(s, m_cmp)
        p_sum = p_sum + p
        r_cmp.append(jnp.dot(p.astype(BF16), kvc, preferred_element_type=F32))

    hi = p_sum.astype(BF16)
    lo = (p_sum - hi.astype(F32)).astype(BF16)
    imp = jnp.dot(jnp.concatenate([hi, lo], axis=1), ov_ref[...], preferred_element_type=F32)
    blk = lax.broadcasted_iota(jnp.int32, (NSA_TQ, n_sel), 1)
    valid = blk * SEL_LEN <= qpos_c
    forced = (blk == 0) | (blk == jnp.right_shift(qpos_c, 6))
    score = jnp.where(valid, imp + jnp.where(forced, FORCE_SCORE, 0.0), -jnp.inf)
    rank = jnp.zeros((NSA_TQ, n_sel), jnp.int32)
    for i in range(n_sel):
        ci = score[:, i:i + 1]
        rank = rank + ((ci > score) | ((ci == score) & (blk > i))).astype(jnp.int32)
    sel = (valid & (rank < SEL_TOPN)).astype(BF16)
    for c in range(n_chunks):
        mask_ref[c] = jnp.dot(sel, ex_ref[:, c * NSA_CK:(c + 1) * NSA_CK], preferred_element_type=F32)

    m_ref[...] = jnp.full(m_ref.shape, NEG, F32)
    l_ref[...] = jnp.zeros(l_ref.shape, F32)
    acc_ref[...] = jnp.zeros(acc_ref.shape, F32)
    kpos_l = lax.broadcasted_iota(jnp.int32, (NSA_TQ, NSA_CK), 1)

    def chunk_body(c, carry):
        start = pl.multiple_of(c * NSA_CK, NSA_CK)
        kx = kx_ref[pl.ds(start, NSA_CK), :]
        vx = vx_ref[pl.ds(start, NSA_CK), :]
        ok = (mask_ref[c] > 0.5) & (start + kpos_l <= qpos_c)
        for h in range(N_HEADS):
            s = lax.dot_general(q_hi[h], kx, (((1,), (1,)), ((), ())), preferred_element_type=F32)
            s = jnp.where(ok, s, NEG)
            m_old = m_ref[h]
            m_new = jnp.maximum(m_old, jnp.max(s, axis=-1, keepdims=True))
            p = jnp.where(ok, jnp.exp(s - m_new), 0.0)
            alpha = jnp.exp(m_old - m_new)
            l_ref[h] = alpha * l_ref[h] + jnp.sum(p, axis=-1, keepdims=True)
            acc_ref[h] = alpha * acc_ref[h] + jnp.dot(p.astype(BF16), vx, preferred_element_type=F32)
            m_ref[h] = m_new
        return carry

    lax.fori_loop(0, (q0 + NSA_TQ - 1) // NSA_CK + 1, chunk_body, 0)

    w_start = pl.multiple_of(jnp.maximum(q0 - WINDOW, 0), NSA_TQ)
    kw = kw_ref[pl.ds(w_start, WINDOW + NSA_TQ), :]
    vw = vw_ref[pl.ds(w_start, WINDOW + NSA_TQ), :]
    kwpos = w_start + lax.broadcasted_iota(jnp.int32, (NSA_TQ, WINDOW + NSA_TQ), 1)
    m_win = (kwpos <= qpos_c) & (kwpos > qpos_c - WINDOW)

    gates = _sigmoid(vw_ref[pl.ds(pl.multiple_of(q0, NSA_TQ), NSA_TQ), :].astype(F32))
    ys = []
    for h in range(N_HEADS):
        s = lax.dot_general(q_lo[h], kw, (((1,), (1,)), ((), ())), preferred_element_type=F32)
        r_win = jnp.dot(softmax_rows(s, m_win).astype(BF16), vw, preferred_element_type=F32)
        l = l_ref[h]
        r_slc = acc_ref[h] / jnp.where(l > 0, l, 1.0)
        g0, g1, g2 = (gates[:, HEAD_DIM + 3 * h + t:HEAD_DIM + 3 * h + t + 1] for t in range(3))
        y = pltpu.roll(g0 * r_cmp[h] + g1 * r_slc, HEAD_DIM, 1) + g2 * r_win
        ys.append(y)
    for pair in range(N_HEADS // 2):
        y = jnp.where(lo_half, ys[2 * pair], pltpu.roll(ys[2 * pair + 1], HEAD_DIM, 1))
        o_ref[:, pair * LANES:(pair + 1) * LANES] = y.astype(o_ref.dtype)


def _nsa(proj, kvc, batch, seq):
    n = proj.shape[0]
    nqb = seq // NSA_TQ
    n_sel = seq // SEL_LEN
    n_cmp = kvc.shape[1]
    cs = (np.arange(n_cmp) * CMP_STRIDE)[:, None]
    ss = (np.arange(n_sel) * SEL_LEN)[None, :]
    ov = np.clip(np.minimum(cs + CMP_LEN, ss + SEL_LEN) - np.maximum(cs, ss), 0, None).astype(np.float32) / CMP_LEN
    ov = jnp.asarray(np.concatenate([ov, ov], axis=0), BF16)
    ex = jnp.asarray((np.arange(seq)[None, :] // SEL_LEN == np.arange(n_sel)[:, None]).astype(np.float32), BF16)
    seq_spec = lambda col: pl.BlockSpec((seq, LANES), lambda b, i: (b, col))
    return pl.pallas_call(
        _nsa_kernel,
        grid=(batch, nqb),
        in_specs=[
            pl.BlockSpec((NSA_TQ, MIX_WIDTH), lambda b, i: (b * nqb + i, COL_CQ // MIX_WIDTH)),
            seq_spec(COL_KX // LANES),
            seq_spec(COL_VX // LANES),
            seq_spec(COL_KX // LANES + 1),
            seq_spec(COL_VX // LANES + 1),
            pl.BlockSpec((1, n_cmp, LANES), lambda b, i: (b, 0, 0)),
            pl.BlockSpec((2 * n_cmp, n_sel), lambda b, i: (0, 0)),
            pl.BlockSpec((n_sel, seq), lambda b, i: (0, 0)),
        ],
        out_specs=pl.BlockSpec((NSA_TQ, MIX_WIDTH), lambda b, i: (b * nqb + i, 0)),
        out_shape=jax.ShapeDtypeStruct((n, MIX_WIDTH), BF16),
        scratch_shapes=[
            pltpu.VMEM((seq // NSA_CK, NSA_TQ, NSA_CK), F32),
            pltpu.VMEM((N_HEADS, NSA_TQ, 1), F32),
            pltpu.VMEM((N_HEADS, NSA_TQ, 1), F32),
            pltpu.VMEM((N_HEADS, NSA_TQ, LANES), F32),
        ],
        compiler_params=_cparams(("parallel", "arbitrary")),
        name="nsa",
    )(proj, proj, proj, proj, proj, kvc, ov, ex)


def _merge_kernel(ya_ref, yb_ref, yc_ref, g0_ref, g1_ref, g2_ref, wb_ref, wo_ref, h_ref, o_ref):
    merged = None
    for y_ref, g_ref, t in ((ya_ref, g0_ref, 0), (yb_ref, g1_ref, 1), (yc_ref, g2_ref, 2)):
        term = _sigmoid(g_ref[...].astype(F32)) * jnp.dot(y_ref[...], wb_ref[t], preferred_element_type=F32)
        merged = term if merged is None else merged + term
    o_ref[...] = h_ref[...] + jnp.dot(merged.astype(BF16), wo_ref[...], preferred_element_type=F32)


def _merge(ya, yb, yc, proj, wb, wo, h, tm=512):
    n = h.shape[0]
    y_spec = pl.BlockSpec((tm, MIX_WIDTH), lambda i: (i, 0))
    g_spec = lambda t: pl.BlockSpec((tm, D_MODEL), lambda i: (i, COL_BRG // D_MODEL + t))
    return pl.pallas_call(
        _merge_kernel,
        grid=(n // tm,),
        in_specs=[y_spec, y_spec, y_spec, g_spec(0), g_spec(1), g_spec(2),
                  pl.BlockSpec((N_BRANCH, MIX_WIDTH, D_MODEL), lambda i: (0, 0, 0)),
                  pl.BlockSpec((D_MODEL, D_MODEL), lambda i: (0, 0)),
                  pl.BlockSpec((tm, D_MODEL), lambda i: (i, 0))],
        out_specs=pl.BlockSpec((tm, D_MODEL), lambda i: (i, 0)),
        out_shape=jax.ShapeDtypeStruct((n, D_MODEL), F32),
        compiler_params=_cparams(("parallel",)),
        name="merge",
    )(ya, yb, yc, proj, proj, proj, wb, wo, h)


def _ffn_kernel(h_ref, g_ref, wg_ref, wu_ref, wd_ref, o_ref, xn_ref, acc_ref):
    f = pl.program_id(1)

    @pl.when(f == 0)
    def _():
        xn_ref[...] = _rms(h_ref[...], g_ref[...]).astype(BF16)
        acc_ref[...] = h_ref[...]

    xn = xn_ref[...]
    a = jnp.dot(xn, wg_ref[...], preferred_element_type=F32)
    u = jnp.dot(xn, wu_ref[...], preferred_element_type=F32)
    hid = (a * _sigmoid(a) * u).astype(BF16)
    acc_ref[...] += jnp.dot(hid, wd_ref[...], preferred_element_type=F32)

    @pl.when(f == pl.num_programs(1) - 1)
    def _():
        o_ref[...] = acc_ref[...]


def _ffn(h, g, wg, wu, wd, tm=1024, tf=256):
    n = h.shape[0]
    dff = wg.shape[1]
    return pl.pallas_call(
        _ffn_kernel,
        grid=(n // tm, dff // tf),
        in_specs=[
            pl.BlockSpec((tm, D_MODEL), lambda i, f: (i, 0)),
            pl.BlockSpec((1, D_MODEL), lambda i, f: (0, 0)),
            pl.BlockSpec((D_MODEL, tf), lambda i, f: (0, f)),
            pl.BlockSpec((D_MODEL, tf), lambda i, f: (0, f)),
            pl.BlockSpec((tf, D_MODEL), lambda i, f: (f, 0)),
        ],
        out_specs=pl.BlockSpec((tm, D_MODEL), lambda i, f: (i, 0)),
        out_shape=jax.ShapeDtypeStruct((n, D_MODEL), F32),
        scratch_shapes=[pltpu.VMEM((tm, D_MODEL), BF16), pltpu.VMEM((tm, D_MODEL), F32)],
        compiler_params=_cparams(("parallel", "arbitrary")),
        name="ffn",
    )(h, g, wg, wu, wd)


def _moe_kernel(h_ref, g_ref, rw_ref, wg_ref, wu_ref, wd_ref, o_ref, xn_ref, gate_ref, acc_ref):
    e = pl.program_id(1)
    f = pl.program_id(2)
    tm = h_ref.shape[0]
    lane = lax.broadcasted_iota(jnp.int32, (tm, LANES), 1)

    @pl.when((e == 0) & (f == 0))
    def _():
        xn = _rms(h_ref[...], g_ref[...])
        xn_ref[...] = xn.astype(BF16)
        acc_ref[...] = h_ref[...]
        logits = jnp.dot(xn, rw_ref[...], preferred_element_type=F32, precision=lax.Precision.HIGHEST)
        logits = jnp.where(lane < N_EXPERTS, logits, -jnp.inf)
        m1 = jnp.max(logits, axis=-1, keepdims=True)
        i1 = jnp.min(jnp.where(logits == m1, lane, LANES), axis=-1, keepdims=True)
        rest = jnp.where(lane == i1, -jnp.inf, logits)
        m2 = jnp.max(rest, axis=-1, keepdims=True)
        i2 = jnp.min(jnp.where(rest == m2, lane, LANES), axis=-1, keepdims=True)
        e2 = jnp.exp(m2 - m1)
        gate_ref[...] = jnp.where(lane == i1, 1.0 / (1.0 + e2), 0.0) + jnp.where(lane == i2, e2 / (1.0 + e2), 0.0)

    xn = xn_ref[...]
    ge = jnp.sum(jnp.where(lane == e, gate_ref[...], 0.0), axis=-1, keepdims=True)
    a = jnp.dot(xn, wg_ref[0], preferred_element_type=F32)
    u = jnp.dot(xn, wu_ref[0], preferred_element_type=F32)
    hid = (a * _sigmoid(a) * u * ge).astype(BF16)
    acc_ref[...] += jnp.dot(hid, wd_ref[0], preferred_element_type=F32)

    @pl.when((e == pl.num_programs(1) - 1) & (f == pl.num_programs(2) - 1))
    def _():
        o_ref[...] = acc_ref[...]


def _moe(h, g, rw, wg, wu, wd, tm=1024, tf=512):
    n = h.shape[0]
    ne, _, dff = wg.shape
    return pl.pallas_call(
        _moe_kernel,
        grid=(n // tm, ne, dff // tf),
        in_specs=[
            pl.BlockSpec((tm, D_MODEL), lambda i, e, f: (i, 0)),
            pl.BlockSpec((1, D_MODEL), lambda i, e, f: (0, 0)),
            pl.BlockSpec((D_MODEL, LANES), lambda i, e, f: (0, 0)),
            pl.BlockSpec((1, D_MODEL, tf), lambda i, e, f: (e, 0, f)),
            pl.BlockSpec((1, D_MODEL, tf), lambda i, e, f: (e, 0, f)),
            pl.BlockSpec((1, tf, D_MODEL), lambda i, e, f: (e, f, 0)),
        ],
        out_specs=pl.BlockSpec((tm, D_MODEL), lambda i, e, f: (i, 0)),
        out_shape=jax.ShapeDtypeStruct((n, D_MODEL), F32),
        scratch_shapes=[pltpu.VMEM((tm, D_MODEL), BF16), pltpu.VMEM((tm, LANES), F32),
                        pltpu.VMEM((tm, D_MODEL), F32)],
        compiler_params=_cparams(("parallel", "arbitrary", "arbitrary")),
        name="moe",
    )(h, g, rw, wg, wu, wd)


def _final_norm_kernel(h_ref, g_ref, o_ref):
    o_ref[...] = _rms(h_ref[...], g_ref[...])


def _final_norm(h, g, tm=1024):
    n = h.shape[0]
    return pl.pallas_call(
        _final_norm_kernel,
        grid=(n // tm,),
        in_specs=[pl.BlockSpec((tm, D_MODEL), lambda i: (i, 0)), pl.BlockSpec((1, D_MODEL), lambda i: (0, 0))],
        out_specs=pl.BlockSpec((tm, D_MODEL), lambda i: (i, 0)),
        out_shape=jax.ShapeDtypeStruct((n, D_MODEL), F32),
        compiler_params=_cparams(("parallel",)),
        name="final_norm",
    )(h, g)


def _rope_tables(seq):
    half = ROPE_DIMS // 2
    inv_freq = ROPE_THETA ** (-jnp.arange(0, ROPE_DIMS, 2, dtype=F32) / ROPE_DIMS)
    ang = jnp.arange(seq, dtype=F32)[:, None] * inv_freq[None, :]
    cos, sin = jnp.cos(ang), jnp.sin(ang)
    ones = jnp.ones((seq, HEAD_DIM - ROPE_DIMS), F32)
    zeros_h = jnp.zeros((seq, half), F32)
    zeros_r = jnp.zeros((seq, HEAD_DIM - ROPE_DIMS), F32)
    c = jnp.concatenate([cos, cos, ones], axis=1)
    s1 = jnp.concatenate([-sin, zeros_h, zeros_r], axis=1)
    s2 = jnp.concatenate([zeros_h, sin, zeros_r], axis=1)
    rep = LANES // HEAD_DIM
    return tuple(jnp.tile(t, (1, rep)) for t in (c, s1, s2))


def _layout_w_in(w):
    scale = HEAD_DIM ** -0.5
    a_in, sb_q, sb_k, sb_v, c_q, c_kv, c_g, br_g = jnp.split(
        w, np.cumsum([2 * MIX_WIDTH, MIX_WIDTH, MIX_WIDTH, MIX_WIDTH, MIX_WIDTH, 6 * HEAD_DIM, 3 * N_HEADS])[:].tolist(),
        axis=-1)
    k_cmp, v_cmp, k_slc, v_slc, k_win, v_win = jnp.split(c_kv, 6, axis=-1)
    zeros = lambda width: jnp.zeros((w.shape[0], width), w.dtype)
    cols = [a_in, sb_q * scale, sb_k, sb_v,
            v_cmp, v_slc, v_win, c_g, zeros(HEAD_DIM - 3 * N_HEADS),
            c_q * scale, k_cmp, k_slc, k_win, zeros(HEAD_DIM),
            br_g]
    out = jnp.concatenate(cols, axis=-1)
    assert out.shape[-1] == PROJ_W
    return out.astype(BF16)


def kernel(x, norm1_g, w_in, sgu_norm_g, sgu_w, sgu_b, cmp_pos, cmp_w1, cmp_w2, w_branch, w_out, norm2_g,
           ffn_w_gate, ffn_w_up, ffn_w_down, router_w, moe_w_gate, moe_w_up, moe_w_down, final_norm_g):
    batch, seq, _ = x.shape
    n = batch * seq
    depth = norm1_g.shape[0]
    rope_c, rope_s1, rope_s2 = _rope_tables(seq)
    dff_pad = -(-D_FF // 256) * 256
    h = x.reshape(n, D_MODEL)
    for layer in range(depth):
        proj = _norm_proj(h, norm1_g[layer][None, :], _layout_w_in(w_in[layer]), rope_c, rope_s1, rope_s2, seq)
        sgu_bias = jnp.repeat(sgu_b[layer].T, HEAD_DIM, axis=1)
        y_a = _sgu(proj, sgu_norm_g[layer][None, :], sgu_w[layer], sgu_bias)
        y_b = _stick_breaking(proj, batch, seq)
        groups = seq // CMP_STRIDE
        gk = proj[:, COL_KX:COL_KX + HEAD_DIM].reshape(batch, groups, CMP_STRIDE * HEAD_DIM)
        gv = proj[:, COL_VX:COL_VX + HEAD_DIM].reshape(batch, groups, CMP_STRIDE * HEAD_DIM)
        kvc = _compress(gk, gv, cmp_pos[layer].reshape(2, 1, CMP_LEN * HEAD_DIM),
                        cmp_w1[layer].astype(BF16), cmp_w2[layer].astype(BF16))
        y_c = _nsa(proj, kvc, batch, seq)
        h = _merge(y_a, y_b, y_c, proj, w_branch[layer].astype(BF16), w_out[layer].astype(BF16), h)
        j = layer // 2
        g2 = norm2_g[layer][None, :]
        if layer % 2 == 0:
            pad = dff_pad - D_FF
            wg = jnp.pad(ffn_w_gate[j], ((0, 0), (0, pad))).astype(BF16)
            wu = jnp.pad(ffn_w_up[j], ((0, 0), (0, pad))).astype(BF16)
            wd = jnp.pad(ffn_w_down[j], ((0, pad), (0, 0))).astype(BF16)
            h = _ffn(h, g2, wg, wu, wd)
        else:
            rw = jnp.pad(router_w[j], ((0, 0), (0, LANES - N_EXPERTS)))
            h = _moe(h, g2, rw, moe_w_gate[j].astype(BF16), moe_w_up[j].astype(BF16), moe_w_down[j].astype(BF16))
    return _final_norm(h, final_norm_g[None, :]).reshape(batch, seq, D_MODEL)
```

```python
import functools

import numpy as np
import jax
import jax.numpy as jnp
from jax import lax
from jax.experimental import pallas as pl
from jax.experimental.pallas import tpu as pltpu

F32 = jnp.float32
BF16 = jnp.bfloat16

D_MODEL = 1024
HEAD_DIM = 64
N_HEADS = 4
MIX_WIDTH = N_HEADS * HEAD_DIM
ROPE_DIMS = HEAD_DIM // 4
ROPE_THETA = 500000.0
EPS = 1e-6
SGU_CHUNK = 128
CMP_LEN = 32
CMP_STRIDE = 16
CMP_HIDDEN = 256
SEL_LEN = 64
SEL_TOPN = 16
WINDOW = 512
FORCE_SCORE = 1e4
N_BRANCH = 3
D_FF = 2752
N_EXPERTS = 8
D_FF_EXPERT = 3584
NEG = -1e30
LOG2E = 1.4426950408889634

LANES = 128
VMEM_LIMIT = 56 * 1024 * 1024

PROJ_TILE = 512
COL_A = 0
COL_SBQ, COL_SBK, COL_SBV = 512, 768, 1024
COL_VX = 1280
COL_CQ = 1536
COL_KX = 1792
COL_BRG = 2048
PROJ_W = COL_BRG + N_BRANCH * D_MODEL
ROPE_TILE = COL_CQ // PROJ_TILE


def _cparams(sem):
    return pltpu.CompilerParams(dimension_semantics=sem, vmem_limit_bytes=VMEM_LIMIT)


def _sigmoid(x):
    return 1.0 / (1.0 + jnp.exp(-x))


def _rms(x, g):
    return x * lax.rsqrt(jnp.mean(x * x, axis=-1, keepdims=True) + EPS) * g


def _norm_proj_kernel(h_ref, g_ref, w_ref, c_ref, s1_ref, s2_ref, o_ref, xn_ref):
    j = pl.program_id(1)

    @pl.when(j == 0)
    def _():
        xn_ref[...] = _rms(h_ref[...], g_ref[...]).astype(BF16)

    acc = jnp.dot(xn_ref[...], w_ref[...], preferred_element_type=F32)

    @pl.when(j != ROPE_TILE)
    def _():
        o_ref[...] = acc.astype(o_ref.dtype)

    @pl.when(j == ROPE_TILE)
    def _():
        c, s1, s2 = c_ref[...], s1_ref[...], s2_ref[...]
        for g in range(PROJ_TILE // LANES):
            x = acc[:, g * LANES:(g + 1) * LANES]
            y = x * c + pltpu.roll(x, LANES - 8, 1) * s1 + pltpu.roll(x, 8, 1) * s2
            o_ref[:, g * LANES:(g + 1) * LANES] = y.astype(o_ref.dtype)


def _norm_proj(h, g, w, rope_c, rope_s1, rope_s2, seq, tm=1024):
    n = h.shape[0]
    nseq = seq // tm
    return pl.pallas_call(
        _norm_proj_kernel,
        grid=(n // tm, PROJ_W // PROJ_TILE),
        in_specs=[
            pl.BlockSpec((tm, D_MODEL), lambda i, j: (i, 0)),
            pl.BlockSpec((1, D_MODEL), lambda i, j: (0, 0)),
            pl.BlockSpec((D_MODEL, PROJ_TILE), lambda i, j: (0, j)),
            pl.BlockSpec((tm, LANES), lambda i, j: (i % nseq, 0)),
            pl.BlockSpec((tm, LANES), lambda i, j: (i % nseq, 0)),
            pl.BlockSpec((tm, LANES), lambda i, j: (i % nseq, 0)),
        ],
        out_specs=pl.BlockSpec((tm, PROJ_TILE), lambda i, j: (i, j)),
        out_shape=jax.ShapeDtypeStruct((n, PROJ_W), BF16),
        scratch_shapes=[pltpu.VMEM((tm, D_MODEL), BF16)],
        compiler_params=_cparams(("parallel", "arbitrary")),
        name="norm_proj",
    )(h, g, w, rope_c, rope_s1, rope_s2)


def _sgu_kernel(z_ref, g_ref, w_ref, b_ref, o_ref):
    tm = z_ref.shape[0]
    a = jax.nn.gelu(z_ref[...].astype(F32))
    u = a[:, :MIX_WIDTH]
    v = _rms(a[:, MIX_WIDTH:], g_ref[...]).astype(BF16)
    row = lax.broadcasted_iota(jnp.int32, (SGU_CHUNK, SGU_CHUNK), 0)
    col = lax.broadcasted_iota(jnp.int32, (SGU_CHUNK, SGU_CHUNK), 1)
    ws = [jnp.where(row >= col, w_ref[gi], 0.0).astype(BF16) for gi in range(N_HEADS)]
    bias = b_ref[...]
    for c in range(tm // SGU_CHUNK):
        rows = slice(c * SGU_CHUNK, (c + 1) * SGU_CHUNK)
        mix = jnp.concatenate(
            [jnp.dot(ws[gi], v[rows, gi * HEAD_DIM:(gi + 1) * HEAD_DIM], preferred_element_type=F32)
             for gi in range(N_HEADS)], axis=1)
        o_ref[rows, :] = (u[rows, :] * (mix + bias)).astype(o_ref.dtype)


def _sgu(proj, g, w, bias, tm=512):
    n = proj.shape[0]
    return pl.pallas_call(
        _sgu_kernel,
        grid=(n // tm,),
        in_specs=[
            pl.BlockSpec((tm, 2 * MIX_WIDTH), lambda i: (i, COL_A // (2 * MIX_WIDTH))),
            pl.BlockSpec((1, MIX_WIDTH), lambda i: (0, 0)),
            pl.BlockSpec((N_HEADS, SGU_CHUNK, SGU_CHUNK), lambda i: (0, 0, 0)),
            pl.BlockSpec((SGU_CHUNK, MIX_WIDTH), lambda i: (0, 0)),
        ],
        out_specs=pl.BlockSpec((tm, MIX_WIDTH), lambda i: (i, 0)),
        out_shape=jax.ShapeDtypeStruct((n, MIX_WIDTH), BF16),
        compiler_params=_cparams(("parallel",)),
        name="sgu",
    )(proj, g, w, bias)


SB_TQ = 512
SB_TK = 256
SB_SUB = 128
SB_R = SB_TQ // SB_TK


def _sb_tile(q_ref, k_ref, v_ref, u_ref, acc_ref, carry_ref, k_off):
    q, k, v = q_ref[...], k_ref[...], v_ref[...]
    u = u_ref[...]
    diag = k_off is not None
    if diag:
        mask = (k_off + lax.broadcasted_iota(jnp.int32, (SB_TQ, SB_TK), 1)
                < lax.broadcasted_iota(jnp.int32, (SB_TQ, SB_TK), 0))
    pvs = []
    for h in range(N_HEADS):
        hs = slice(h * HEAD_DIM, (h + 1) * HEAD_DIM)
        z = lax.dot_general(q[:, hs], k[:, hs], (((1,), (1,)), ((), ())), preferred_element_type=F32)
        log_beta = jnp.minimum(z, 0.0) - jnp.log(1.0 + jnp.exp2(-jnp.abs(z))) * LOG2E
        log_1m = log_beta - z
        if diag:
            log_1m = jnp.where(mask, log_1m, 0.0)
        carry = carry_ref[h]
        pv = jnp.zeros((SB_TQ, HEAD_DIM), F32)
        for sb in reversed(range(SB_TK // SB_SUB)):
            cs = slice(sb * SB_SUB, (sb + 1) * SB_SUB)
            l = log_1m[:, cs]
            hi = lax.bitcast_convert_type(lax.bitcast_convert_type(l, jnp.uint32) & jnp.uint32(0xFFFF0000), F32)
            lo = l - hi
            r = jnp.dot(jnp.concatenate([hi.astype(BF16), lo.astype(BF16)], axis=1), u, preferred_element_type=F32)
            a = jnp.exp2(log_beta[:, cs] + r[:, :SB_SUB] + carry)
            if diag:
                a = jnp.where(mask[:, cs], a, 0.0)
            pv = pv + jnp.dot(a.astype(BF16), v[cs, hs], preferred_element_type=F32)
            carry = carry + r[:, SB_SUB:]
        carry_ref[h] = carry
        pvs.append(pv)
    acc_ref[...] += jnp.concatenate(pvs, axis=1)


def _sb_kernel(qi_ref, ki_ref, q_ref, k_ref, v_ref, u_ref, o_ref, acc_ref, carry_ref):
    step = pl.program_id(1)
    qi = qi_ref[step]
    ki = ki_ref[step]

    @pl.when(ki == SB_R * qi + SB_R - 1)
    def _():
        acc_ref[...] = jnp.zeros_like(acc_ref)
        carry_ref[...] = jnp.zeros_like(carry_ref)

    @pl.when(ki >= SB_R * qi)
    def _():
        _sb_tile(q_ref, k_ref, v_ref, u_ref, acc_ref, carry_ref, ki * SB_TK - qi * SB_TQ)

    @pl.when(ki < SB_R * qi)
    def _():
        _sb_tile(q_ref, k_ref, v_ref, u_ref, acc_ref, carry_ref, None)

    @pl.when(ki == 0)
    def _():
        o_ref[...] = acc_ref[...].astype(o_ref.dtype)


def _stick_breaking(proj, batch, seq):
    n = proj.shape[0]
    nq, nk = seq // SB_TQ, seq // SB_TK
    qi_tab = np.array([qi for qi in range(nq) for _ in range(SB_R * (qi + 1))], np.int32)
    ki_tab = np.array([ki for qi in range(nq) for ki in range(SB_R * (qi + 1) - 1, -1, -1)], np.int32)
    j = np.arange(2 * SB_SUB)[:, None] % SB_SUB
    c = np.arange(2 * SB_SUB)[None, :]
    u = jnp.asarray(np.where(c < SB_SUB, j > c, True).astype(np.float32), BF16)
    qb, kb, vb = COL_SBQ // MIX_WIDTH, COL_SBK // MIX_WIDTH, COL_SBV // MIX_WIDTH
    grid_spec = pltpu.PrefetchScalarGridSpec(
        num_scalar_prefetch=2,
        grid=(batch, len(qi_tab)),
        in_specs=[
            pl.BlockSpec((SB_TQ, MIX_WIDTH), lambda b, s, qt, kt: (b * nq + qt[s], qb)),
            pl.BlockSpec((SB_TK, MIX_WIDTH), lambda b, s, qt, kt: (b * nk + kt[s], kb)),
            pl.BlockSpec((SB_TK, MIX_WIDTH), lambda b, s, qt, kt: (b * nk + kt[s], vb)),
            pl.BlockSpec((2 * SB_SUB, 2 * SB_SUB), lambda b, s, qt, kt: (0, 0)),
        ],
        out_specs=pl.BlockSpec((SB_TQ, MIX_WIDTH), lambda b, s, qt, kt: (b * nq + qt[s], 0)),
        scratch_shapes=[pltpu.VMEM((SB_TQ, MIX_WIDTH), F32), pltpu.VMEM((N_HEADS, SB_TQ, SB_SUB), F32)],
    )
    return pl.pallas_call(
        _sb_kernel,
        grid_spec=grid_spec,
        out_shape=jax.ShapeDtypeStruct((n, MIX_WIDTH), BF16),
        compiler_params=_cparams(("parallel", "arbitrary")),
        name="stick_breaking",
    )(jnp.asarray(qi_tab), jnp.asarray(ki_tab), proj, proj, proj, u)


def _compress_kernel(gk_ref, gv_ref, pos_ref, w1_ref, w2_ref, kc_ref, vc_ref):
    half = CMP_STRIDE * HEAD_DIM
    for t, (g_ref, o_ref) in enumerate(((gk_ref, kc_ref), (gv_ref, vc_ref))):
        g = g_ref[0].astype(F32)
        top = (g + pos_ref[t, :, :half]).astype(BF16)
        bot = (g + pos_ref[t, :, half:]).astype(BF16)
        a = jnp.dot(top, w1_ref[t, :half, :], preferred_element_type=F32)
        b = jnp.dot(bot, w1_ref[t, half:, :], preferred_element_type=F32)
        hid = jax.nn.gelu(a + pltpu.roll(b, b.shape[0] - 1, 0))
        out = jnp.dot(hid.astype(BF16), w2_ref[t], preferred_element_type=F32)
        o_ref[0] = jnp.concatenate([out, jnp.zeros_like(out)], axis=1).astype(o_ref.dtype)


def _compress(gk, gv, pos, w1, w2):
    b, m, width = gk.shape
    out_spec = pl.BlockSpec((1, m, LANES), lambda i: (i, 0, 0))
    out_shape = jax.ShapeDtypeStruct((b, m, LANES), BF16)
    return pl.pallas_call(
        _compress_kernel,
        grid=(b,),
        in_specs=[
            pl.BlockSpec((1, m, width), lambda i: (i, 0, 0)),
            pl.BlockSpec((1, m, width), lambda i: (i, 0, 0)),
            pl.BlockSpec((2, 1, CMP_LEN * HEAD_DIM), lambda i: (0, 0, 0)),
            pl.BlockSpec((2, CMP_LEN * HEAD_DIM, CMP_HIDDEN), lambda i: (0, 0, 0)),
            pl.BlockSpec((2, CMP_HIDDEN, HEAD_DIM), lambda i: (0, 0, 0)),
        ],
        out_specs=[out_spec, out_spec],
        out_shape=[out_shape, out_shape],
        compiler_params=_cparams(("parallel",)),
        name="nsa_compress",
    )(gk, gv, pos, w1, w2)


NSA_TQ = 128
NSA_CK = 512


NSA_ROWS = N_HEADS * NSA_TQ
SEL_OFF = 1e30


def _per_head(x, fn):
    return jnp.concatenate([fn(x[h * NSA_TQ:(h + 1) * NSA_TQ]) for h in range(N_HEADS)], axis=0)


def _nsa_kernel(q_ref, kx_ref, vx_ref, kw_ref, vw_ref, kb_ref, kc_ref, vc_ref, ovt_ref, o_ref, m_ref, l_ref, acc_ref):
    seq = kx_ref.shape[0]
    n_sel = seq // SEL_LEN
    n_cmp = kc_ref.shape[1]
    q0 = pl.program_id(1) * NSA_TQ
    NT = (((1,), (1,)), ((), ()))
    lo_half = lax.broadcasted_iota(jnp.int32, (NSA_TQ, LANES), 1) < HEAD_DIM
    qpos_c = q0 + lax.broadcasted_iota(jnp.int32, (NSA_TQ, 1), 0)

    qs = []
    for pair in range(N_HEADS // 2):
        x = q_ref[:, pair * LANES:(pair + 1) * LANES].astype(F32)
        qs += [jnp.where(lo_half, x, 0.0), jnp.where(lo_half, pltpu.roll(x, HEAD_DIM, 1), 0.0)]
    q4 = jnp.concatenate(qs, axis=0).astype(BF16)

    def softmax_rows(s, mask):
        s = _per_head(s, lambda t: jnp.where(mask, t, NEG))
        p = jnp.exp2(s - jnp.max(s, axis=-1, keepdims=True))
        p = _per_head(p, lambda t: jnp.where(mask, t, 0.0))
        d = jnp.sum(p, axis=-1, keepdims=True)
        return p / jnp.where(d > 0, d, 1.0)

    cmp_end = lax.broadcasted_iota(jnp.int32, (NSA_TQ, n_cmp), 1) * CMP_STRIDE + (CMP_LEN - 1)
    p_cmp = softmax_rows(lax.dot_general(q4, kc_ref[0], NT, preferred_element_type=F32), cmp_end <= qpos_c)
    r_cmp = jnp.dot(p_cmp.astype(BF16), vc_ref[0], preferred_element_type=F32)
    p_sum = sum(p_cmp[h * NSA_TQ:(h + 1) * NSA_TQ] for h in range(N_HEADS))

    hi = p_sum.astype(BF16)
    lo = (p_sum - hi.astype(F32)).astype(BF16)
    imp = lax.dot_general(ovt_ref[...], jnp.concatenate([hi, lo], axis=1), NT, preferred_element_type=F32)
    blk = lax.broadcasted_iota(jnp.int32, (n_sel, NSA_TQ), 0)
    qpos_r = q0 + lax.broadcasted_iota(jnp.int32, (n_sel, NSA_TQ), 1)
    valid = blk * SEL_LEN <= qpos_r
    forced = (blk == 0) | (blk == jnp.right_shift(qpos_r, 6))
    score = jnp.where(valid, imp + jnp.where(forced, FORCE_SCORE, 0.0), -jnp.inf)
    groups = [score[8 * g:8 * g + 8] for g in range(n_sel // 8)]
    ranks = [jnp.zeros((8, NSA_TQ), F32) for _ in groups]
    row8 = lax.broadcasted_iota(jnp.int32, (8, NSA_TQ), 0)
    for i in range(n_sel):
        ci = jnp.broadcast_to(score[i:i + 1], (8, NSA_TQ))
        for g, sg in enumerate(groups):
            if 8 * g > i:
                beats = ci >= sg
            elif 8 * g + 7 < i:
                beats = ci > sg
            else:
                beats = (ci > sg) | ((ci == sg) & (row8 + 8 * g > i))
            ranks[g] = ranks[g] + jnp.where(beats, 1.0, 0.0)
    sel = jnp.where(valid & (jnp.concatenate(ranks, axis=0) < SEL_TOPN), 0.0, -SEL_OFF)
    sel = jnp.concatenate([sel, jnp.full((LANES - n_sel, NSA_TQ), -SEL_OFF, F32)], axis=0).T
    q_aug = jnp.concatenate([q4, jnp.concatenate([sel.astype(BF16)] * N_HEADS, axis=0)], axis=1)

    w_start = pl.multiple_of(jnp.maximum(q0 - WINDOW, 0), NSA_TQ)
    kwpos = w_start + lax.broadcasted_iota(jnp.int32, (NSA_TQ, WINDOW + NSA_TQ), 1)
    s_win = lax.dot_general(q4, kw_ref[pl.ds(w_start, WINDOW + NSA_TQ), :], NT, preferred_element_type=F32)
    p_win = softmax_rows(s_win, (kwpos <= qpos_c) & (kwpos > qpos_c - WINDOW))
    r_win = jnp.dot(p_win.astype(BF16), vw_ref[pl.ds(w_start, WINDOW + NSA_TQ), :], preferred_element_type=F32)

    m_ref[...] = jnp.full(m_ref.shape, NEG, F32)
    l_ref[...] = jnp.zeros(l_ref.shape, F32)
    acc_ref[...] = jnp.zeros(acc_ref.shape, F32)

    def chunk_stats(c):
        start = pl.multiple_of(c * NSA_CK, NSA_CK)
        keys = jnp.concatenate([kx_ref[pl.ds(start, NSA_CK), :], kb_ref[pl.ds(start, NSA_CK), :]], axis=1)
        s = lax.dot_general(q_aug, keys, NT, preferred_element_type=F32)
        ok = start + lax.broadcasted_iota(jnp.int32, (NSA_TQ, NSA_CK), 1) <= qpos_c
        s = _per_head(s, lambda t: jnp.where(ok, t, NEG))
        m_c = jnp.max(s, axis=-1, keepdims=True)
        p = jnp.exp2(s - m_c)
        a_c = jnp.dot(p.astype(BF16), vx_ref[pl.ds(start, NSA_CK), :], preferred_element_type=F32)
        return m_c, jnp.sum(p, axis=-1, keepdims=True), a_c

    def pair_body(i, carry):
        m0, l0, a0 = chunk_stats(2 * i)
        m1, l1, a1 = chunk_stats(2 * i + 1)
        m_old = m_ref[...]
        m_new = jnp.maximum(m_old, jnp.maximum(m0, m1))
        w_old, w0, w1 = jnp.exp2(m_old - m_new), jnp.exp2(m0 - m_new), jnp.exp2(m1 - m_new)
        l_ref[...] = w_old * l_ref[...] + w0 * l0 + w1 * l1
        acc_ref[...] = w_old * acc_ref[...] + w0 * a0 + w1 * a1
        m_ref[...] = m_new
        return carry

    lax.fori_loop(0, (q0 + NSA_TQ - 1) // (2 * NSA_CK) + 1, pair_body, 0)

    l = l_ref[...]
    r_slc = acc_ref[...] / jnp.where(l > 0, l, 1.0)
    gates = _sigmoid(vw_ref[pl.ds(pl.multiple_of(q0, NSA_TQ), NSA_TQ), :].astype(F32))
    g = [jnp.concatenate([gates[:, HEAD_DIM + 3 * h + t:HEAD_DIM + 3 * h + t + 1] for h in range(N_HEADS)], axis=0)
         for t in range(N_BRANCH)]
    y = g[0] * r_cmp + g[1] * r_slc + g[2] * r_win
    for pair in range(N_HEADS // 2):
        even = y[2 * pair * NSA_TQ:(2 * pair + 1) * NSA_TQ]
        odd = y[(2 * pair + 1) * NSA_TQ:(2 * pair + 2) * NSA_TQ]
        o_ref[:, pair * LANES:(pair + 1) * LANES] = jnp.where(lo_half, even, pltpu.roll(odd, HEAD_DIM, 1)).astype(o_ref.dtype)


def _nsa(proj, kc, vc, batch, seq):
    n = proj.shape[0]
    nqb = seq // NSA_TQ
    n_sel = seq // SEL_LEN
    n_cmp = kc.shape[1]
    assert n_sel % 8 == 0 and n_sel <= LANES and seq % (2 * NSA_CK) == 0
    cs = (np.arange(n_cmp) * CMP_STRIDE)[None, :]
    ss = (np.arange(n_sel) * SEL_LEN)[:, None]
    ovt = np.clip(np.minimum(cs + CMP_LEN, ss + SEL_LEN) - np.maximum(cs, ss), 0, None).astype(np.float32) / CMP_LEN
    ovt = jnp.asarray(np.concatenate([ovt, ovt], axis=1), BF16)
    kb = jnp.asarray((np.arange(seq)[:, None] // SEL_LEN == np.arange(LANES)[None, :]).astype(np.float32), BF16)
    seq_spec = lambda col: pl.BlockSpec((seq, LANES), lambda b, i: (b, col))
    cmp_spec = pl.BlockSpec((1, n_cmp, LANES), lambda b, i: (b, 0, 0))
    return pl.pallas_call(
        _nsa_kernel,
        grid=(batch, nqb),
        in_specs=[
            pl.BlockSpec((NSA_TQ, MIX_WIDTH), lambda b, i: (b * nqb + i, COL_CQ // MIX_WIDTH)),
            seq_spec(COL_KX // LANES),
            seq_spec(COL_VX // LANES),
            seq_spec(COL_KX // LANES + 1),
            seq_spec(COL_VX // LANES + 1),
            pl.BlockSpec((seq, LANES), lambda b, i: (0, 0)),
            cmp_spec, cmp_spec,
            pl.BlockSpec((n_sel, 2 * n_cmp), lambda b, i: (0, 0)),
        ],
        out_specs=pl.BlockSpec((NSA_TQ, MIX_WIDTH), lambda b, i: (b * nqb + i, 0)),
        out_shape=jax.ShapeDtypeStruct((n, MIX_WIDTH), BF16),
        scratch_shapes=[
            pltpu.VMEM((NSA_ROWS, 1), F32),
            pltpu.VMEM((NSA_ROWS, 1), F32),
            pltpu.VMEM((NSA_ROWS, LANES), F32),
        ],
        compiler_params=_cparams(("parallel", "arbitrary")),
        name="nsa",
    )(proj, proj, proj, proj, proj, kb, kc, vc, ovt)


def _merge_kernel(ya_ref, yb_ref, yc_ref, g0_ref, g1_ref, g2_ref, wb_ref, wo_ref, h_ref, o_ref):
    merged = None
    for y_ref, g_ref, t in ((ya_ref, g0_ref, 0), (yb_ref, g1_ref, 1), (yc_ref, g2_ref, 2)):
        term = _sigmoid(g_ref[...].astype(F32)) * jnp.dot(y_ref[...], wb_ref[t], preferred_element_type=F32)
        merged = term if merged is None else merged + term
    o_ref[...] = h_ref[...] + jnp.dot(merged.astype(BF16), wo_ref[...], preferred_element_type=F32)


def _merge(ya, yb, yc, proj, wb, wo, h, tm=512):
    n = h.shape[0]
    y_spec = pl.BlockSpec((tm, MIX_WIDTH), lambda i: (i, 0))
    g_spec = lambda t: pl.BlockSpec((tm, D_MODEL), lambda i: (i, COL_BRG // D_MODEL + t))
    return pl.pallas_call(
        _merge_kernel,
        grid=(n // tm,),
        in_specs=[y_spec, y_spec, y_spec, g_spec(0), g_spec(1), g_spec(2),
                  pl.BlockSpec((N_BRANCH, MIX_WIDTH, D_MODEL), lambda i: (0, 0, 0)),
                  pl.BlockSpec((D_MODEL, D_MODEL), lambda i: (0, 0)),
                  pl.BlockSpec((tm, D_MODEL), lambda i: (i, 0))],
        out_specs=pl.BlockSpec((tm, D_MODEL), lambda i: (i, 0)),
        out_shape=jax.ShapeDtypeStruct((n, D_MODEL), F32),
        compiler_params=_cparams(("parallel",)),
        name="merge",
    )(ya, yb, yc, proj, proj, proj, wb, wo, h)


def _ffn_kernel(h_ref, g_ref, wg_ref, wu_ref, wd_ref, o_ref, xn_ref, acc_ref):
    f = pl.program_id(1)

    @pl.when(f == 0)
    def _():
        xn_ref[...] = _rms(h_ref[...], g_ref[...]).astype(BF16)
        acc_ref[...] = h_ref[...]

    xn = xn_ref[...]
    a = jnp.dot(xn, wg_ref[...], preferred_element_type=F32)
    u = jnp.dot(xn, wu_ref[...], preferred_element_type=F32)
    hid = (a * _sigmoid(a) * u).astype(BF16)
    acc_ref[...] += jnp.dot(hid, wd_ref[...], preferred_element_type=F32)

    @pl.when(f == pl.num_programs(1) - 1)
    def _():
        o_ref[...] = acc_ref[...]


def _ffn(h, g, wg, wu, wd, tm=1024, tf=256):
    n = h.shape[0]
    dff = wg.shape[1]
    return pl.pallas_call(
        _ffn_kernel,
        grid=(n // tm, dff // tf),
        in_specs=[
            pl.BlockSpec((tm, D_MODEL), lambda i, f: (i, 0)),
            pl.BlockSpec((1, D_MODEL), lambda i, f: (0, 0)),
            pl.BlockSpec((D_MODEL, tf), lambda i, f: (0, f)),
            pl.BlockSpec((D_MODEL, tf), lambda i, f: (0, f)),
            pl.BlockSpec((tf, D_MODEL), lambda i, f: (f, 0)),
        ],
        out_specs=pl.BlockSpec((tm, D_MODEL), lambda i, f: (i, 0)),
        out_shape=jax.ShapeDtypeStruct((n, D_MODEL), F32),
        scratch_shapes=[pltpu.VMEM((tm, D_MODEL), BF16), pltpu.VMEM((tm, D_MODEL), F32)],
        compiler_params=_cparams(("parallel", "arbitrary")),
        name="ffn",
    )(h, g, wg, wu, wd)


def _moe_kernel(h_ref, g_ref, rw_ref, wg_ref, wu_ref, wd_ref, o_ref, xn_ref, gate_ref, acc_ref):
    e = pl.program_id(1)
    f = pl.program_id(2)
    tm = h_ref.shape[0]
    lane = lax.broadcasted_iota(jnp.int32, (tm, LANES), 1)

    @pl.when((e == 0) & (f == 0))
    def _():
        xn = _rms(h_ref[...], g_ref[...])
        xn_ref[...] = xn.astype(BF16)
        acc_ref[...] = h_ref[...]
        logits = jnp.dot(xn, rw_ref[...], preferred_element_type=F32, precision=lax.Precision.HIGHEST)
        logits = jnp.where(lane < N_EXPERTS, logits, -jnp.inf)
        m1 = jnp.max(logits, axis=-1, keepdims=True)
        i1 = jnp.min(jnp.where(logits == m1, lane, LANES), axis=-1, keepdims=True)
        rest = jnp.where(lane == i1, -jnp.inf, logits)
        m2 = jnp.max(rest, axis=-1, keepdims=True)
        i2 = jnp.min(jnp.where(rest == m2, lane, LANES), axis=-1, keepdims=True)
        e2 = jnp.exp(m2 - m1)
        gate_ref[...] = jnp.where(lane == i1, 1.0 / (1.0 + e2), 0.0) + jnp.where(lane == i2, e2 / (1.0 + e2), 0.0)

    xn = xn_ref[...]
    ge = jnp.sum(jnp.where(lane == e, gate_ref[...], 0.0), axis=-1, keepdims=True)
    a = jnp.dot(xn, wg_ref[0], preferred_element_type=F32)
    u = jnp.dot(xn, wu_ref[0], preferred_element_type=F32)
    hid = (a * _sigmoid(a) * u * ge).astype(BF16)
    acc_ref[...] += jnp.dot(hid, wd_ref[0], preferred_element_type=F32)

    @pl.when((e == pl.num_programs(1) - 1) & (f == pl.num_programs(2) - 1))
    def _():
        o_ref[...] = acc_ref[...]


def _moe(h, g, rw, wg, wu, wd, tm=1024, tf=512):
    n = h.shape[0]
    ne, _, dff = wg.shape
    return pl.pallas_call(
        _moe_kernel,
        grid=(n // tm, ne, dff // tf),
        in_specs=[
            pl.BlockSpec((tm, D_MODEL), lambda i, e, f: (i, 0)),
            pl.BlockSpec((1, D_MODEL), lambda i, e, f: (0, 0)),
            pl.BlockSpec((D_MODEL, LANES), lambda i, e, f: (0, 0)),
            pl.BlockSpec((1, D_MODEL, tf), lambda i, e, f: (e, 0, f)),
            pl.BlockSpec((1, D_MODEL, tf), lambda i, e, f: (e, 0, f)),
            pl.BlockSpec((1, tf, D_MODEL), lambda i, e, f: (e, f, 0)),
        ],
        out_specs=pl.BlockSpec((tm, D_MODEL), lambda i, e, f: (i, 0)),
        out_shape=jax.ShapeDtypeStruct((n, D_MODEL), F32),
        scratch_shapes=[pltpu.VMEM((tm, D_MODEL), BF16), pltpu.VMEM((tm, LANES), F32),
                        pltpu.VMEM((tm, D_MODEL), F32)],
        compiler_params=_cparams(("parallel", "arbitrary", "arbitrary")),
        name="moe",
    )(h, g, rw, wg, wu, wd)


def _final_norm_kernel(h_ref, g_ref, o_ref):
    o_ref[...] = _rms(h_ref[...], g_ref[...])


def _final_norm(h, g, tm=1024):
    n = h.shape[0]
    return pl.pallas_call(
        _final_norm_kernel,
        grid=(n // tm,),
        in_specs=[pl.BlockSpec((tm, D_MODEL), lambda i: (i, 0)), pl.BlockSpec((1, D_MODEL), lambda i: (0, 0))],
        out_specs=pl.BlockSpec((tm, D_MODEL), lambda i: (i, 0)),
        out_shape=jax.ShapeDtypeStruct((n, D_MODEL), F32),
        compiler_params=_cparams(("parallel",)),
        name="final_norm",
    )(h, g)


def _rope_tables(seq):
    half = ROPE_DIMS // 2
    inv_freq = ROPE_THETA ** (-jnp.arange(0, ROPE_DIMS, 2, dtype=F32) / ROPE_DIMS)
    ang = jnp.arange(seq, dtype=F32)[:, None] * inv_freq[None, :]
    cos, sin = jnp.cos(ang), jnp.sin(ang)
    ones = jnp.ones((seq, HEAD_DIM - ROPE_DIMS), F32)
    zeros_h = jnp.zeros((seq, half), F32)
    zeros_r = jnp.zeros((seq, HEAD_DIM - ROPE_DIMS), F32)
    c = jnp.concatenate([cos, cos, ones], axis=1)
    s1 = jnp.concatenate([-sin, zeros_h, zeros_r], axis=1)
    s2 = jnp.concatenate([zeros_h, sin, zeros_r], axis=1)
    rep = LANES // HEAD_DIM
    return tuple(jnp.tile(t, (1, rep)) for t in (c, s1, s2))


def _layout_w_in(w):
    scale = HEAD_DIM ** -0.5 * LOG2E
    a_in, sb_q, sb_k, sb_v, c_q, c_kv, c_g, br_g = jnp.split(
        w, np.cumsum([2 * MIX_WIDTH, MIX_WIDTH, MIX_WIDTH, MIX_WIDTH, MIX_WIDTH, 6 * HEAD_DIM, 3 * N_HEADS])[:].tolist(),
        axis=-1)
    k_cmp, v_cmp, k_slc, v_slc, k_win, v_win = jnp.split(c_kv, 6, axis=-1)
    zeros = lambda width: jnp.zeros((w.shape[0], width), w.dtype)
    cols = [a_in, sb_q * scale, sb_k, sb_v,
            v_slc, v_cmp, v_win, c_g, zeros(HEAD_DIM - 3 * N_HEADS),
            c_q * scale, k_slc, k_cmp, k_win, zeros(HEAD_DIM),
            br_g]
    out = jnp.concatenate(cols, axis=-1)
    assert out.shape[-1] == PROJ_W
    return out.astype(BF16)


def kernel(x, norm1_g, w_in, sgu_norm_g, sgu_w, sgu_b, cmp_pos, cmp_w1, cmp_w2, w_branch, w_out, norm2_g,
           ffn_w_gate, ffn_w_up, ffn_w_down, router_w, moe_w_gate, moe_w_up, moe_w_down, final_norm_g):
    batch, seq, _ = x.shape
    n = batch * seq
    depth = norm1_g.shape[0]
    rope_c, rope_s1, rope_s2 = _rope_tables(seq)
    dff_pad = -(-D_FF // 256) * 256
    h = x.reshape(n, D_MODEL)
    for layer in range(depth):
        proj = _norm_proj(h, norm1_g[layer][None, :], _layout_w_in(w_in[layer]), rope_c, rope_s1, rope_s2, seq)
        sgu_bias = jnp.repeat(sgu_b[layer].T, HEAD_DIM, axis=1)
        y_a = _sgu(proj, sgu_norm_g[layer][None, :], sgu_w[layer], sgu_bias)
        y_b = _stick_breaking(proj, batch, seq)
        groups = seq // CMP_STRIDE
        gk = proj[:, COL_KX + HEAD_DIM:COL_KX + 2 * HEAD_DIM].reshape(batch, groups, CMP_STRIDE * HEAD_DIM)
        gv = proj[:, COL_VX + HEAD_DIM:COL_VX + 2 * HEAD_DIM].reshape(batch, groups, CMP_STRIDE * HEAD_DIM)
        kc, vc = _compress(gk, gv, cmp_pos[layer].reshape(2, 1, CMP_LEN * HEAD_DIM),
                           cmp_w1[layer].astype(BF16), cmp_w2[layer].astype(BF16))
        y_c = _nsa(proj, kc, vc, batch, seq)
        h = _merge(y_a, y_b, y_c, proj, w_branch[layer].astype(BF16), w_out[layer].astype(BF16), h)
        j = layer // 2
        g2 = norm2_g[layer][None, :]
        if layer % 2 == 0:
            pad = dff_pad - D_FF
            wg = jnp.pad(ffn_w_gate[j], ((0, 0), (0, pad))).astype(BF16)
            wu = jnp.pad(ffn_w_up[j], ((0, 0), (0, pad))).astype(BF16)
            wd = jnp.pad(ffn_w_down[j], ((0, pad), (0, 0))).astype(BF16)
            h = _ffn(h, g2, wg, wu, wd)
        else:
            rw = jnp.pad(router_w[j], ((0, 0), (0, LANES - N_EXPERTS)))
            h = _moe(h, g2, rw, moe_w_gate[j].astype(BF16), moe_w_up[j].astype(BF16), moe_w_down[j].astype(BF16))
    return _final_norm(h, final_norm_g[None, :]).reshape(batch, seq, D_MODEL)
```

```python
import functools

import numpy as np
import jax
import jax.numpy as jnp
from jax import lax
from jax.experimental import pallas as pl
from jax.experimental.pallas import tpu as pltpu

F32 = jnp.float32
BF16 = jnp.bfloat16

D_MODEL = 1024
HEAD_DIM = 64
N_HEADS = 4
MIX_WIDTH = N_HEADS * HEAD_DIM
ROPE_DIMS = HEAD_DIM // 4
ROPE_THETA = 500000.0
EPS = 1e-6
SGU_CHUNK = 128
CMP_LEN = 32
CMP_STRIDE = 16
CMP_HIDDEN = 256
SEL_LEN = 64
SEL_TOPN = 16
WINDOW = 512
FORCE_SCORE = 1e4
N_BRANCH = 3
D_FF = 2752
N_EXPERTS = 8
D_FF_EXPERT = 3584
NEG = -1e30
LOG2E = 1.4426950408889634

LANES = 128
VMEM_LIMIT = 56 * 1024 * 1024

PROJ_TILE = 512
COL_A = 0
COL_SBQ, COL_SBK, COL_SBV = 512, 768, 1024
COL_VX = 1280
COL_CQ = 1536
COL_KX = 1792
COL_BRG = 2048
PROJ_W = COL_BRG + N_BRANCH * D_MODEL
ROPE_TILE = COL_CQ // PROJ_TILE


def _cparams(sem):
    return pltpu.CompilerParams(dimension_semantics=sem, vmem_limit_bytes=VMEM_LIMIT)


def _sigmoid(x):
    return 1.0 / (1.0 + jnp.exp(-x))


def _rms(x, g):
    return x * lax.rsqrt(jnp.mean(x * x, axis=-1, keepdims=True) + EPS) * g


def _norm_proj_kernel(h_ref, g_ref, w_ref, c_ref, s1_ref, s2_ref, o_ref, xn_ref):
    j = pl.program_id(1)

    @pl.when(j == 0)
    def _():
        xn_ref[...] = _rms(h_ref[...], g_ref[...]).astype(BF16)

    acc = jnp.dot(xn_ref[...], w_ref[...], preferred_element_type=F32)

    @pl.when(j != ROPE_TILE)
    def _():
        o_ref[...] = acc.astype(o_ref.dtype)

    @pl.when(j == ROPE_TILE)
    def _():
        c, s1, s2 = c_ref[...], s1_ref[...], s2_ref[...]
        for g in range(PROJ_TILE // LANES):
            x = acc[:, g * LANES:(g + 1) * LANES]
            y = x * c + pltpu.roll(x, LANES - 8, 1) * s1 + pltpu.roll(x, 8, 1) * s2
            o_ref[:, g * LANES:(g + 1) * LANES] = y.astype(o_ref.dtype)


def _norm_proj(h, g, w, rope_c, rope_s1, rope_s2, seq, tm=1024):
    n = h.shape[0]
    nseq = seq // tm
    return pl.pallas_call(
        _norm_proj_kernel,
        grid=(n // tm, PROJ_W // PROJ_TILE),
        in_specs=[
            pl.BlockSpec((tm, D_MODEL), lambda i, j: (i, 0)),
            pl.BlockSpec((1, D_MODEL), lambda i, j: (0, 0)),
            pl.BlockSpec((D_MODEL, PROJ_TILE), lambda i, j: (0, j)),
            pl.BlockSpec((tm, LANES), lambda i, j: (i % nseq, 0)),
            pl.BlockSpec((tm, LANES), lambda i, j: (i % nseq, 0)),
            pl.BlockSpec((tm, LANES), lambda i, j: (i % nseq, 0)),
        ],
        out_specs=pl.BlockSpec((tm, PROJ_TILE), lambda i, j: (i, j)),
        out_shape=jax.ShapeDtypeStruct((n, PROJ_W), BF16),
        scratch_shapes=[pltpu.VMEM((tm, D_MODEL), BF16)],
        compiler_params=_cparams(("parallel", "arbitrary")),
        name="norm_proj",
    )(h, g, w, rope_c, rope_s1, rope_s2)


def _sgu_kernel(z_ref, g_ref, w_ref, b_ref, o_ref):
    tm = z_ref.shape[0]
    a = jax.nn.gelu(z_ref[...].astype(F32))
    u = a[:, :MIX_WIDTH]
    v = _rms(a[:, MIX_WIDTH:], g_ref[...]).astype(BF16)
    row = lax.broadcasted_iota(jnp.int32, (SGU_CHUNK, SGU_CHUNK), 0)
    col = lax.broadcasted_iota(jnp.int32, (SGU_CHUNK, SGU_CHUNK), 1)
    ws = [jnp.where(row >= col, w_ref[gi], 0.0).astype(BF16) for gi in range(N_HEADS)]
    bias = b_ref[...]
    for c in range(tm // SGU_CHUNK):
        rows = slice(c * SGU_CHUNK, (c + 1) * SGU_CHUNK)
        mix = jnp.concatenate(
            [jnp.dot(ws[gi], v[rows, gi * HEAD_DIM:(gi + 1) * HEAD_DIM], preferred_element_type=F32)
             for gi in range(N_HEADS)], axis=1)
        o_ref[rows, :] = (u[rows, :] * (mix + bias)).astype(o_ref.dtype)


def _sgu(proj, g, w, bias, tm=512):
    n = proj.shape[0]
    return pl.pallas_call(
        _sgu_kernel,
        grid=(n // tm,),
        in_specs=[
            pl.BlockSpec((tm, 2 * MIX_WIDTH), lambda i: (i, COL_A // (2 * MIX_WIDTH))),
            pl.BlockSpec((1, MIX_WIDTH), lambda i: (0, 0)),
            pl.BlockSpec((N_HEADS, SGU_CHUNK, SGU_CHUNK), lambda i: (0, 0, 0)),
            pl.BlockSpec((SGU_CHUNK, MIX_WIDTH), lambda i: (0, 0)),
        ],
        out_specs=pl.BlockSpec((tm, MIX_WIDTH), lambda i: (i, 0)),
        out_shape=jax.ShapeDtypeStruct((n, MIX_WIDTH), BF16),
        compiler_params=_cparams(("parallel",)),
        name="sgu",
    )(proj, g, w, bias)


SB_TQ = 512
SB_TK = 256
SB_SUB = 128
SB_R = SB_TQ // SB_TK


def _sb_tile(q_ref, k_ref, v_ref, u_ref, acc_ref, carry_ref, k_off):
    q, k, v = q_ref[...], k_ref[...], v_ref[...]
    u = u_ref[...]
    diag = k_off is not None
    if diag:
        mask = (k_off + lax.broadcasted_iota(jnp.int32, (SB_TQ, SB_TK), 1)
                < lax.broadcasted_iota(jnp.int32, (SB_TQ, SB_TK), 0))
    pvs = []
    for h in range(N_HEADS):
        hs = slice(h * HEAD_DIM, (h + 1) * HEAD_DIM)
        z = lax.dot_general(q[:, hs], k[:, hs], (((1,), (1,)), ((), ())), preferred_element_type=F32)
        log_beta = jnp.minimum(z, 0.0) - jnp.log(1.0 + jnp.exp2(-jnp.abs(z))) * LOG2E
        log_1m = log_beta - z
        if diag:
            log_1m = jnp.where(mask, log_1m, 0.0)
        carry = carry_ref[h]
        pv = jnp.zeros((SB_TQ, HEAD_DIM), F32)
        for sb in reversed(range(SB_TK // SB_SUB)):
            cs = slice(sb * SB_SUB, (sb + 1) * SB_SUB)
            l = log_1m[:, cs]
            hi = lax.bitcast_convert_type(lax.bitcast_convert_type(l, jnp.uint32) & jnp.uint32(0xFFFF0000), F32)
            lo = l - hi
            r = jnp.dot(jnp.concatenate([hi.astype(BF16), lo.astype(BF16)], axis=1), u, preferred_element_type=F32)
            a = jnp.exp2(log_beta[:, cs] + r[:, :SB_SUB] + carry)
            if diag:
                a = jnp.where(mask[:, cs], a, 0.0)
            pv = pv + jnp.dot(a.astype(BF16), v[cs, hs], preferred_element_type=F32)
            carry = carry + r[:, SB_SUB:]
        carry_ref[h] = carry
        pvs.append(pv)
    acc_ref[...] += jnp.concatenate(pvs, axis=1)


def _sb_kernel(qi_ref, ki_ref, q_ref, k_ref, v_ref, u_ref, o_ref, acc_ref, carry_ref):
    step = pl.program_id(1)
    qi = qi_ref[step]
    ki = ki_ref[step]

    @pl.when(ki == SB_R * qi + SB_R - 1)
    def _():
        acc_ref[...] = jnp.zeros_like(acc_ref)
        carry_ref[...] = jnp.zeros_like(carry_ref)

    @pl.when(ki >= SB_R * qi)
    def _():
        _sb_tile(q_ref, k_ref, v_ref, u_ref, acc_ref, carry_ref, ki * SB_TK - qi * SB_TQ)

    @pl.when(ki < SB_R * qi)
    def _():
        _sb_tile(q_ref, k_ref, v_ref, u_ref, acc_ref, carry_ref, None)

    @pl.when(ki == 0)
    def _():
        o_ref[...] = acc_ref[...].astype(o_ref.dtype)


def _stick_breaking(proj, batch, seq):
    n = proj.shape[0]
    nq, nk = seq // SB_TQ, seq // SB_TK
    qi_tab = np.array([qi for qi in range(nq) for _ in range(SB_R * (qi + 1))], np.int32)
    ki_tab = np.array([ki for qi in range(nq) for ki in range(SB_R * (qi + 1) - 1, -1, -1)], np.int32)
    j = np.arange(2 * SB_SUB)[:, None] % SB_SUB
    c = np.arange(2 * SB_SUB)[None, :]
    u = jnp.asarray(np.where(c < SB_SUB, j > c, True).astype(np.float32), BF16)
    qb, kb, vb = COL_SBQ // MIX_WIDTH, COL_SBK // MIX_WIDTH, COL_SBV // MIX_WIDTH
    grid_spec = pltpu.PrefetchScalarGridSpec(
        num_scalar_prefetch=2,
        grid=(batch, len(qi_tab)),
        in_specs=[
            pl.BlockSpec((SB_TQ, MIX_WIDTH), lambda b, s, qt, kt: (b * nq + qt[s], qb)),
            pl.BlockSpec((SB_TK, MIX_WIDTH), lambda b, s, qt, kt: (b * nk + kt[s], kb)),
            pl.BlockSpec((SB_TK, MIX_WIDTH), lambda b, s, qt, kt: (b * nk + kt[s], vb)),
            pl.BlockSpec((2 * SB_SUB, 2 * SB_SUB), lambda b, s, qt, kt: (0, 0)),
        ],
        out_specs=pl.BlockSpec((SB_TQ, MIX_WIDTH), lambda b, s, qt, kt: (b * nq + qt[s], 0)),
        scratch_shapes=[pltpu.VMEM((SB_TQ, MIX_WIDTH), F32), pltpu.VMEM((N_HEADS, SB_TQ, SB_SUB), F32)],
    )
    return pl.pallas_call(
        _sb_kernel,
        grid_spec=grid_spec,
        out_shape=jax.ShapeDtypeStruct((n, MIX_WIDTH), BF16),
        compiler_params=_cparams(("parallel", "arbitrary")),
        name="stick_breaking",
    )(jnp.asarray(qi_tab), jnp.asarray(ki_tab), proj, proj, proj, u)


def _compress_kernel(gk_ref, gv_ref, pos_ref, w1_ref, w2_ref, kc_ref, vc_ref):
    half = CMP_STRIDE * HEAD_DIM
    for t, (g_ref, o_ref) in enumerate(((gk_ref, kc_ref), (gv_ref, vc_ref))):
        g = g_ref[0].astype(F32)
        top = (g + pos_ref[t, :, :half]).astype(BF16)
        bot = (g + pos_ref[t, :, half:]).astype(BF16)
        a = jnp.dot(top, w1_ref[t, :half, :], preferred_element_type=F32)
        b = jnp.dot(bot, w1_ref[t, half:, :], preferred_element_type=F32)
        hid = jax.nn.gelu(a + pltpu.roll(b, b.shape[0] - 1, 0))
        out = jnp.dot(hid.astype(BF16), w2_ref[t], preferred_element_type=F32)
        o_ref[0] = jnp.concatenate([out, jnp.zeros_like(out)], axis=1).astype(o_ref.dtype)


def _compress(gk, gv, pos, w1, w2):
    b, m, width = gk.shape
    out_spec = pl.BlockSpec((1, m, LANES), lambda i: (i, 0, 0))
    out_shape = jax.ShapeDtypeStruct((b, m, LANES), BF16)
    return pl.pallas_call(
        _compress_kernel,
        grid=(b,),
        in_specs=[
            pl.BlockSpec((1, m, width), lambda i: (i, 0, 0)),
            pl.BlockSpec((1, m, width), lambda i: (i, 0, 0)),
            pl.BlockSpec((2, 1, CMP_LEN * HEAD_DIM), lambda i: (0, 0, 0)),
            pl.BlockSpec((2, CMP_LEN * HEAD_DIM, CMP_HIDDEN), lambda i: (0, 0, 0)),
            pl.BlockSpec((2, CMP_HIDDEN, HEAD_DIM), lambda i: (0, 0, 0)),
        ],
        out_specs=[out_spec, out_spec],
        out_shape=[out_shape, out_shape],
        compiler_params=_cparams(("parallel",)),
        name="nsa_compress",
    )(gk, gv, pos, w1, w2)


NSA_TQ = 128
NSA_CK = 512


NSA_ROWS = N_HEADS * NSA_TQ
SEL_OFF = 1e30


def _per_head(x, fn):
    return jnp.concatenate([fn(x[h * NSA_TQ:(h + 1) * NSA_TQ]) for h in range(N_HEADS)], axis=0)


def _nsa_kernel(q_ref, kx_ref, vx_ref, kw_ref, vw_ref, kb_ref, kc_ref, vc_ref, ovt_ref, o_ref, m_ref, l_ref, acc_ref):
    seq = kx_ref.shape[0]
    n_sel = seq // SEL_LEN
    n_cmp = kc_ref.shape[1]
    q0 = pl.program_id(1) * NSA_TQ
    NT = (((1,), (1,)), ((), ()))
    lo_half = lax.broadcasted_iota(jnp.int32, (NSA_TQ, LANES), 1) < HEAD_DIM
    qpos_c = q0 + lax.broadcasted_iota(jnp.int32, (NSA_TQ, 1), 0)

    qs = []
    for pair in range(N_HEADS // 2):
        x = q_ref[:, pair * LANES:(pair + 1) * LANES].astype(F32)
        qs += [jnp.where(lo_half, x, 0.0), jnp.where(lo_half, pltpu.roll(x, HEAD_DIM, 1), 0.0)]
    q4 = jnp.concatenate(qs, axis=0).astype(BF16)

    def softmax_rows(s, mask):
        s = _per_head(s, lambda t: jnp.where(mask, t, NEG))
        p = jnp.exp2(s - jnp.max(s, axis=-1, keepdims=True))
        p = _per_head(p, lambda t: jnp.where(mask, t, 0.0))
        d = jnp.sum(p, axis=-1, keepdims=True)
        return p / jnp.where(d > 0, d, 1.0)

    cmp_end = lax.broadcasted_iota(jnp.int32, (NSA_TQ, n_cmp), 1) * CMP_STRIDE + (CMP_LEN - 1)
    p_cmp = softmax_rows(lax.dot_general(q4, kc_ref[0], NT, preferred_element_type=F32), cmp_end <= qpos_c)
    r_cmp = jnp.dot(p_cmp.astype(BF16), vc_ref[0], preferred_element_type=F32)
    p_sum = sum(p_cmp[h * NSA_TQ:(h + 1) * NSA_TQ] for h in range(N_HEADS))

    hi = p_sum.astype(BF16)
    lo = (p_sum - hi.astype(F32)).astype(BF16)
    imp = lax.dot_general(ovt_ref[...], jnp.concatenate([hi, lo], axis=1), NT, preferred_element_type=F32)
    blk = lax.broadcasted_iota(jnp.int32, (n_sel, NSA_TQ), 0)
    qpos_r = q0 + lax.broadcasted_iota(jnp.int32, (n_sel, NSA_TQ), 1)
    valid = blk * SEL_LEN <= qpos_r
    forced = (blk == 0) | (blk == jnp.right_shift(qpos_r, 6))
    score = jnp.where(valid, imp + jnp.where(forced, FORCE_SCORE, 0.0), -jnp.inf)
    groups = [score[8 * g:8 * g + 8] for g in range(n_sel // 8)]
    ranks = [jnp.zeros((8, NSA_TQ), F32) for _ in groups]
    row8 = lax.broadcasted_iota(jnp.int32, (8, NSA_TQ), 0)
    for i in range(n_sel):
        ci = jnp.broadcast_to(score[i:i + 1], (8, NSA_TQ))
        for g, sg in enumerate(groups):
            if 8 * g > i:
                beats = ci >= sg
            elif 8 * g + 7 < i:
                beats = ci > sg
            else:
                beats = (ci > sg) | ((ci == sg) & (row8 + 8 * g > i))
            ranks[g] = ranks[g] + jnp.where(beats, 1.0, 0.0)
    sel = jnp.where(valid & (jnp.concatenate(ranks, axis=0) < SEL_TOPN), 0.0, -SEL_OFF)
    sel = jnp.concatenate([sel, jnp.full((LANES - n_sel, NSA_TQ), -SEL_OFF, F32)], axis=0).T
    q_aug = jnp.concatenate([q4, jnp.concatenate([sel.astype(BF16)] * N_HEADS, axis=0)], axis=1)

    w_start = pl.multiple_of(jnp.maximum(q0 - WINDOW, 0), NSA_TQ)
    kwpos = w_start + lax.broadcasted_iota(jnp.int32, (NSA_TQ, WINDOW + NSA_TQ), 1)
    s_win = lax.dot_general(q4, kw_ref[pl.ds(w_start, WINDOW + NSA_TQ), :], NT, preferred_element_type=F32)
    p_win = softmax_rows(s_win, (kwpos <= qpos_c) & (kwpos > qpos_c - WINDOW))
    r_win = jnp.dot(p_win.astype(BF16), vw_ref[pl.ds(w_start, WINDOW + NSA_TQ), :], preferred_element_type=F32)

    m_ref[...] = jnp.full(m_ref.shape, NEG, F32)
    l_ref[...] = jnp.zeros(l_ref.shape, F32)
    acc_ref[...] = jnp.zeros(acc_ref.shape, F32)

    def chunk_stats(c):
        start = pl.multiple_of(c * NSA_CK, NSA_CK)
        keys = jnp.concatenate([kx_ref[pl.ds(start, NSA_CK), :], kb_ref[pl.ds(start, NSA_CK), :]], axis=1)
        s = lax.dot_general(q_aug, keys, NT, preferred_element_type=F32)
        ok = start + lax.broadcasted_iota(jnp.int32, (NSA_TQ, NSA_CK), 1) <= qpos_c
        s = _per_head(s, lambda t: jnp.where(ok, t, NEG))
        m_c = jnp.max(s, axis=-1, keepdims=True)
        p = jnp.exp2(s - m_c)
        a_c = jnp.dot(p.astype(BF16), vx_ref[pl.ds(start, NSA_CK), :], preferred_element_type=F32)
        return m_c, jnp.sum(p, axis=-1, keepdims=True), a_c

    def pair_body(i, carry):
        m0, l0, a0 = chunk_stats(2 * i)
        m1, l1, a1 = chunk_stats(2 * i + 1)
        m_old = m_ref[...]
        m_new = jnp.maximum(m_old, jnp.maximum(m0, m1))
        w_old, w0, w1 = jnp.exp2(m_old - m_new), jnp.exp2(m0 - m_new), jnp.exp2(m1 - m_new)
        l_ref[...] = w_old * l_ref[...] + w0 * l0 + w1 * l1
        acc_ref[...] = w_old * acc_ref[...] + w0 * a0 + w1 * a1
        m_ref[...] = m_new
        return carry

    lax.fori_loop(0, (q0 + NSA_TQ - 1) // (2 * NSA_CK) + 1, pair_body, 0)

    l = l_ref[...]
    r_slc = acc_ref[...] / jnp.where(l > 0, l, 1.0)
    gates = _sigmoid(vw_ref[pl.ds(pl.multiple_of(q0, NSA_TQ), NSA_TQ), :].astype(F32))
    g = [jnp.concatenate([gates[:, HEAD_DIM + 3 * h + t:HEAD_DIM + 3 * h + t + 1] for h in range(N_HEADS)], axis=0)
         for t in range(N_BRANCH)]
    y = g[0] * r_cmp + g[1] * r_slc + g[2] * r_win
    for pair in range(N_HEADS // 2):
        even = y[2 * pair * NSA_TQ:(2 * pair + 1) * NSA_TQ]
        odd = y[(2 * pair + 1) * NSA_TQ:(2 * pair + 2) * NSA_TQ]
        o_ref[:, pair * LANES:(pair + 1) * LANES] = jnp.where(lo_half, even, pltpu.roll(odd, HEAD_DIM, 1)).astype(o_ref.dtype)


def _nsa(proj, kc, vc, batch, seq):
    n = proj.shape[0]
    nqb = seq // NSA_TQ
    n_sel = seq // SEL_LEN
    n_cmp = kc.shape[1]
    assert n_sel % 8 == 0 and n_sel <= LANES and seq % (2 * NSA_CK) == 0
    cs = (np.arange(n_cmp) * CMP_STRIDE)[None, :]
    ss = (np.arange(n_sel) * SEL_LEN)[:, None]
    ovt = np.clip(np.minimum(cs + CMP_LEN, ss + SEL_LEN) - np.maximum(cs, ss), 0, None).astype(np.float32) / CMP_LEN
    ovt = jnp.asarray(np.concatenate([ovt, ovt], axis=1), BF16)
    kb = jnp.asarray((np.arange(seq)[:, None] // SEL_LEN == np.arange(LANES)[None, :]).astype(np.float32), BF16)
    seq_spec = lambda col: pl.BlockSpec((seq, LANES), lambda b, i: (b, col))
    cmp_spec = pl.BlockSpec((1, n_cmp, LANES), lambda b, i: (b, 0, 0))
    return pl.pallas_call(
        _nsa_kernel,
        grid=(batch, nqb),
        in_specs=[
            pl.BlockSpec((NSA_TQ, MIX_WIDTH), lambda b, i: (b * nqb + i, COL_CQ // MIX_WIDTH)),
            seq_spec(COL_KX // LANES),
            seq_spec(COL_VX // LANES),
            seq_spec(COL_KX // LANES + 1),
            seq_spec(COL_VX // LANES + 1),
            pl.BlockSpec((seq, LANES), lambda b, i: (0, 0)),
            cmp_spec, cmp_spec,
            pl.BlockSpec((n_sel, 2 * n_cmp), lambda b, i: (0, 0)),
        ],
        out_specs=pl.BlockSpec((NSA_TQ, MIX_WIDTH), lambda b, i: (b * nqb + i, 0)),
        out_shape=jax.ShapeDtypeStruct((n, MIX_WIDTH), BF16),
        scratch_shapes=[
            pltpu.VMEM((NSA_ROWS, 1), F32),
            pltpu.VMEM((NSA_ROWS, 1), F32),
            pltpu.VMEM((NSA_ROWS, LANES), F32),
        ],
        compiler_params=_cparams(("parallel", "arbitrary")),
        name="nsa",
    )(proj, proj, proj, proj, proj, kb, kc, vc, ovt)


def _merge_kernel(ya_ref, yb_ref, yc_ref, g0_ref, g1_ref, g2_ref, wb_ref, wo_ref, h_ref, o_ref):
    merged = None
    for y_ref, g_ref, t in ((ya_ref, g0_ref, 0), (yb_ref, g1_ref, 1), (yc_ref, g2_ref, 2)):
        term = _sigmoid(g_ref[...].astype(F32)) * jnp.dot(y_ref[...], wb_ref[t], preferred_element_type=F32)
        merged = term if merged is None else merged + term
    o_ref[...] = h_ref[...] + jnp.dot(merged.astype(BF16), wo_ref[...], preferred_element_type=F32)


def _merge(ya, yb, yc, proj, wb, wo, h, tm=512):
    n = h.shape[0]
    y_spec = pl.BlockSpec((tm, MIX_WIDTH), lambda i: (i, 0))
    g_spec = lambda t: pl.BlockSpec((tm, D_MODEL), lambda i: (i, COL_BRG // D_MODEL + t))
    return pl.pallas_call(
        _merge_kernel,
        grid=(n // tm,),
        in_specs=[y_spec, y_spec, y_spec, g_spec(0), g_spec(1), g_spec(2),
                  pl.BlockSpec((N_BRANCH, MIX_WIDTH, D_MODEL), lambda i: (0, 0, 0)),
                  pl.BlockSpec((D_MODEL, D_MODEL), lambda i: (0, 0)),
                  pl.BlockSpec((tm, D_MODEL), lambda i: (i, 0))],
        out_specs=pl.BlockSpec((tm, D_MODEL), lambda i: (i, 0)),
        out_shape=jax.ShapeDtypeStruct((n, D_MODEL), F32),
        compiler_params=_cparams(("parallel",)),
        name="merge",
    )(ya, yb, yc, proj, proj, proj, wb, wo, h)


def _ffn_kernel(h_ref, g_ref, wg_ref, wu_ref, wd_ref, o_ref, xn_ref, acc_ref):
    f = pl.program_id(1)

    @pl.when(f == 0)
    def _():
        xn_ref[...] = _rms(h_ref[...], g_ref[...]).astype(BF16)
        acc_ref[...] = h_ref[...]

    xn = xn_ref[...]
    a = jnp.dot(xn, wg_ref[...], preferred_element_type=F32)
    u = jnp.dot(xn, wu_ref[...], preferred_element_type=F32)
    hid = (a * _sigmoid(a) * u).astype(BF16)
    acc_ref[...] += jnp.dot(hid, wd_ref[...], preferred_element_type=F32)

    @pl.when(f == pl.num_programs(1) - 1)
    def _():
        o_ref[...] = acc_ref[...]


def _ffn(h, g, wg, wu, wd, tm=1024, tf=256):
    n = h.shape[0]
    dff = wg.shape[1]
    return pl.pallas_call(
        _ffn_kernel,
        grid=(n // tm, dff // tf),
        in_specs=[
            pl.BlockSpec((tm, D_MODEL), lambda i, f: (i, 0)),
            pl.BlockSpec((1, D_MODEL), lambda i, f: (0, 0)),
            pl.BlockSpec((D_MODEL, tf), lambda i, f: (0, f)),
            pl.BlockSpec((D_MODEL, tf), lambda i, f: (0, f)),
            pl.BlockSpec((tf, D_MODEL), lambda i, f: (f, 0)),
        ],
        out_specs=pl.BlockSpec((tm, D_MODEL), lambda i, f: (i, 0)),
        out_shape=jax.ShapeDtypeStruct((n, D_MODEL), F32),
        scratch_shapes=[pltpu.VMEM((tm, D_MODEL), BF16), pltpu.VMEM((tm, D_MODEL), F32)],
        compiler_params=_cparams(("parallel", "arbitrary")),
        name="ffn",
    )(h, g, wg, wu, wd)


def _route_kernel(h_ref, g_ref, rw_ref, xn_ref, gate_ref):
    tm = h_ref.shape[0]
    lane = lax.broadcasted_iota(jnp.int32, (tm, LANES), 1)
    xn = _rms(h_ref[...], g_ref[...])
    xn_ref[...] = xn.astype(BF16)
    logits = jnp.dot(xn, rw_ref[...], preferred_element_type=F32, precision=lax.Precision.HIGHEST)
    logits = jnp.where(lane < N_EXPERTS, logits, -jnp.inf)
    m1 = jnp.max(logits, axis=-1, keepdims=True)
    i1 = jnp.min(jnp.where(logits == m1, lane, LANES), axis=-1, keepdims=True)
    rest = jnp.where(lane == i1, -jnp.inf, logits)
    m2 = jnp.max(rest, axis=-1, keepdims=True)
    i2 = jnp.min(jnp.where(rest == m2, lane, LANES), axis=-1, keepdims=True)
    e2 = jnp.exp(m2 - m1)
    gate_ref[...] = jnp.where(lane == i1, 1.0 / (1.0 + e2), 0.0) + jnp.where(lane == i2, e2 / (1.0 + e2), 0.0)


def _route(h, g, rw, tm=1024):
    n = h.shape[0]
    return pl.pallas_call(
        _route_kernel,
        grid=(n // tm,),
        in_specs=[
            pl.BlockSpec((tm, D_MODEL), lambda i: (i, 0)),
            pl.BlockSpec((1, D_MODEL), lambda i: (0, 0)),
            pl.BlockSpec((D_MODEL, LANES), lambda i: (0, 0)),
        ],
        out_specs=[pl.BlockSpec((tm, D_MODEL), lambda i: (i, 0)), pl.BlockSpec((tm, LANES), lambda i: (i, 0))],
        out_shape=[jax.ShapeDtypeStruct((n, D_MODEL), BF16), jax.ShapeDtypeStruct((n, LANES), F32)],
        compiler_params=_cparams(("parallel",)),
        name="moe_route",
    )(h, g, rw)


MOE_TB = 2048
MOE_R = 256
MOE_TF = 512


def _moe_kernel(nsub_ref, xn_ref, h_ref, rrow_ref, rcol_ref, gate_ref, wg_ref, wu_ref, wd_ref, o_ref, xg_ref, y_ref):
    b, e, f = pl.program_id(0), pl.program_id(1), pl.program_id(2)
    ns = nsub_ref[b * N_EXPERTS + e]
    rows_of = lambda j: pl.ds(pl.multiple_of(j * MOE_R, MOE_R), MOE_R)
    base_of = lambda j: (j * MOE_R).astype(F32)

    @pl.when((e == 0) & (f == 0))
    def _():
        o_ref[...] = h_ref[...]

    @pl.when(f == 0)
    def _():
        rank_row = rrow_ref[0, pl.ds(e, 1), :]
        slot = lax.broadcasted_iota(jnp.int32, (MOE_R, MOE_TB), 0).astype(F32)

        def gather(j, carry):
            onehot = jnp.where(rank_row == slot + base_of(j), 1.0, 0.0).astype(BF16)
            xg_ref[rows_of(j), :] = jnp.dot(onehot, xn_ref[...], preferred_element_type=F32).astype(BF16)
            return carry

        lax.fori_loop(0, ns, gather, 0)

    def expert(j, carry):
        x = xg_ref[rows_of(j), :]
        a = jnp.dot(x, wg_ref[0], preferred_element_type=F32)
        u = jnp.dot(x, wu_ref[0], preferred_element_type=F32)
        y = jnp.dot((a * _sigmoid(a) * u).astype(BF16), wd_ref[0], preferred_element_type=F32)

        @pl.when(f == 0)
        def _():
            y_ref[rows_of(j), :] = y

        @pl.when(f != 0)
        def _():
            y_ref[rows_of(j), :] += y

        return carry

    lax.fori_loop(0, ns, expert, 0)

    @pl.when(f == pl.num_programs(2) - 1)
    def _():
        lane = lax.broadcasted_iota(jnp.int32, (MOE_TB, LANES), 1)
        rank_col = jnp.sum(jnp.where(lane == e, rcol_ref[...], 0.0), axis=-1, keepdims=True)
        gate_col = jnp.sum(jnp.where(lane == e, gate_ref[...], 0.0), axis=-1, keepdims=True)
        slot = lax.broadcasted_iota(jnp.int32, (MOE_R, MOE_R), 1).astype(F32)

        def scatter(j, carry):
            y = y_ref[rows_of(j), :].astype(BF16)
            for c in range(MOE_TB // MOE_R):
                tok = slice(c * MOE_R, (c + 1) * MOE_R)
                weights = jnp.where(rank_col[tok] == slot + base_of(j), gate_col[tok], 0.0).astype(BF16)
                o_ref[tok, :] += jnp.dot(weights, y, preferred_element_type=F32)
            return carry

        lax.fori_loop(0, ns, scatter, 0)


def _moe_experts(nsub, xn, h, rank_row, rank_col, gate, wg, wu, wd):
    n = h.shape[0]
    ne, _, dff = wg.shape
    blk = lambda dtype_width: pl.BlockSpec((MOE_TB, dtype_width), lambda b, e, f, ns: (b, 0))
    once = lambda width: pl.BlockSpec((MOE_TB, width), lambda b, e, f, ns: (b, 0), pipeline_mode=pl.Buffered(1))
    grid_spec = pltpu.PrefetchScalarGridSpec(
        num_scalar_prefetch=1,
        grid=(n // MOE_TB, ne, dff // MOE_TF),
        in_specs=[
            once(D_MODEL),
            once(D_MODEL),
            pl.BlockSpec((1, ne, MOE_TB), lambda b, e, f, ns: (b, 0, 0)),
            blk(LANES),
            blk(LANES),
            pl.BlockSpec((1, D_MODEL, MOE_TF), lambda b, e, f, ns: (e, 0, f)),
            pl.BlockSpec((1, D_MODEL, MOE_TF), lambda b, e, f, ns: (e, 0, f)),
            pl.BlockSpec((1, MOE_TF, D_MODEL), lambda b, e, f, ns: (e, f, 0)),
        ],
        out_specs=pl.BlockSpec((MOE_TB, D_MODEL), lambda b, e, f, ns: (b, 0)),
        scratch_shapes=[pltpu.VMEM((MOE_TB, D_MODEL), BF16), pltpu.VMEM((MOE_TB, D_MODEL), F32)],
    )
    return pl.pallas_call(
        _moe_kernel,
        grid_spec=grid_spec,
        out_shape=jax.ShapeDtypeStruct((n, D_MODEL), F32),
        compiler_params=_cparams(("parallel", "arbitrary", "arbitrary")),
        name="moe",
    )(nsub, xn, h, rank_row, rank_col, gate, wg, wu, wd)


def _moe(h, g, rw, wg, wu, wd):
    n = h.shape[0]
    nb = n // MOE_TB
    xn, gate = _route(h, g, rw)
    routed = (gate[:, :N_EXPERTS] > 0).reshape(nb, MOE_TB, N_EXPERTS).astype(jnp.int32)
    incl = jnp.cumsum(routed, axis=1)
    rank = jnp.where(routed > 0, incl - 1, -1).astype(F32)
    nsub = ((incl[:, -1, :] + MOE_R - 1) // MOE_R).reshape(-1).astype(jnp.int32)
    rank_row = rank.transpose(0, 2, 1)
    rank_col = jnp.pad(rank.reshape(n, N_EXPERTS), ((0, 0), (0, LANES - N_EXPERTS)), constant_values=-1.0)
    return _moe_experts(nsub, xn, h, rank_row, rank_col, gate, wg, wu, wd)


def _final_norm_kernel(h_ref, g_ref, o_ref):
    o_ref[...] = _rms(h_ref[...], g_ref[...])


def _final_norm(h, g, tm=1024):
    n = h.shape[0]
    return pl.pallas_call(
        _final_norm_kernel,
        grid=(n // tm,),
        in_specs=[pl.BlockSpec((tm, D_MODEL), lambda i: (i, 0)), pl.BlockSpec((1, D_MODEL), lambda i: (0, 0))],
        out_specs=pl.BlockSpec((tm, D_MODEL), lambda i: (i, 0)),
        out_shape=jax.ShapeDtypeStruct((n, D_MODEL), F32),
        compiler_params=_cparams(("parallel",)),
        name="final_norm",
    )(h, g)


def _rope_tables(seq):
    half = ROPE_DIMS // 2
    inv_freq = ROPE_THETA ** (-jnp.arange(0, ROPE_DIMS, 2, dtype=F32) / ROPE_DIMS)
    ang = jnp.arange(seq, dtype=F32)[:, None] * inv_freq[None, :]
    cos, sin = jnp.cos(ang), jnp.sin(ang)
    ones = jnp.ones((seq, HEAD_DIM - ROPE_DIMS), F32)
    zeros_h = jnp.zeros((seq, half), F32)
    zeros_r = jnp.zeros((seq, HEAD_DIM - ROPE_DIMS), F32)
    c = jnp.concatenate([cos, cos, ones], axis=1)
    s1 = jnp.concatenate([-sin, zeros_h, zeros_r], axis=1)
    s2 = jnp.concatenate([zeros_h, sin, zeros_r], axis=1)
    rep = LANES // HEAD_DIM
    return tuple(jnp.tile(t, (1, rep)) for t in (c, s1, s2))


def _layout_w_in(w):
    scale = HEAD_DIM ** -0.5 * LOG2E
    a_in, sb_q, sb_k, sb_v, c_q, c_kv, c_g, br_g = jnp.split(
        w, np.cumsum([2 * MIX_WIDTH, MIX_WIDTH, MIX_WIDTH, MIX_WIDTH, MIX_WIDTH, 6 * HEAD_DIM, 3 * N_HEADS])[:].tolist(),
        axis=-1)
    k_cmp, v_cmp, k_slc, v_slc, k_win, v_win = jnp.split(c_kv, 6, axis=-1)
    zeros = lambda width: jnp.zeros((w.shape[0], width), w.dtype)
    cols = [a_in, sb_q * scale, sb_k, sb_v,
            v_slc, v_cmp, v_win, c_g, zeros(HEAD_DIM - 3 * N_HEADS),
            c_q * scale, k_slc, k_cmp, k_win, zeros(HEAD_DIM),
            br_g]
    out = jnp.concatenate(cols, axis=-1)
    assert out.shape[-1] == PROJ_W
    return out.astype(BF16)


def kernel(x, norm1_g, w_in, sgu_norm_g, sgu_w, sgu_b, cmp_pos, cmp_w1, cmp_w2, w_branch, w_out, norm2_g,
           ffn_w_gate, ffn_w_up, ffn_w_down, router_w, moe_w_gate, moe_w_up, moe_w_down, final_norm_g):
    batch, seq, _ = x.shape
    n = batch * seq
    depth = norm1_g.shape[0]
    rope_c, rope_s1, rope_s2 = _rope_tables(seq)
    dff_pad = -(-D_FF // 256) * 256
    h = x.reshape(n, D_MODEL)
    for layer in range(depth):
        proj = _norm_proj(h, norm1_g[layer][None, :], _layout_w_in(w_in[layer]), rope_c, rope_s1, rope_s2, seq)
        sgu_bias = jnp.repeat(sgu_b[layer].T, HEAD_DIM, axis=1)
        y_a = _sgu(proj, sgu_norm_g[layer][None, :], sgu_w[layer], sgu_bias)
        y_b = _stick_breaking(proj, batch, seq)
        groups = seq // CMP_STRIDE
        gk = proj[:, COL_KX + HEAD_DIM:COL_KX + 2 * HEAD_DIM].reshape(batch, groups, CMP_STRIDE * HEAD_DIM)
        gv = proj[:, COL_VX + HEAD_DIM:COL_VX + 2 * HEAD_DIM].reshape(batch, groups, CMP_STRIDE * HEAD_DIM)
        kc, vc = _compress(gk, gv, cmp_pos[layer].reshape(2, 1, CMP_LEN * HEAD_DIM),
                           cmp_w1[layer].astype(BF16), cmp_w2[layer].astype(BF16))
        y_c = _nsa(proj, kc, vc, batch, seq)
        h = _merge(y_a, y_b, y_c, proj, w_branch[layer].astype(BF16), w_out[layer].astype(BF16), h)
        j = layer // 2
        g2 = norm2_g[layer][None, :]
        if layer % 2 == 0:
            pad = dff_pad - D_FF
            wg = jnp.pad(ffn_w_gate[j], ((0, 0), (0, pad))).astype(BF16)
            wu = jnp.pad(ffn_w_up[j], ((0, 0), (0, pad))).astype(BF16)
            wd = jnp.pad(ffn_w_down[j], ((0, pad), (0, 0))).astype(BF16)
            h = _ffn(h, g2, wg, wu, wd)
        else:
            rw = jnp.pad(router_w[j], ((0, 0), (0, LANES - N_EXPERTS)))
            h = _moe(h, g2, rw, moe_w_gate[j].astype(BF16), moe_w_up[j].astype(BF16), moe_w_down[j].astype(BF16))
    return _final_norm(h, final_norm_g[None, :]).reshape(batch, seq, D_MODEL)
```

```python
import functools

import numpy as np
import jax
import jax.numpy as jnp
from jax import lax
from jax.experimental import pallas as pl
from jax.experimental.pallas import tpu as pltpu

F32 = jnp.float32
BF16 = jnp.bfloat16

D_MODEL = 1024
HEAD_DIM = 64
N_HEADS = 4
MIX_WIDTH = N_HEADS * HEAD_DIM
ROPE_DIMS = HEAD_DIM // 4
ROPE_THETA = 500000.0
EPS = 1e-6
SGU_CHUNK = 128
CMP_LEN = 32
CMP_STRIDE = 16
CMP_HIDDEN = 256
SEL_LEN = 64
SEL_TOPN = 16
WINDOW = 512
FORCE_SCORE = 1e4
N_BRANCH = 3
D_FF = 2752
N_EXPERTS = 8
D_FF_EXPERT = 3584
NEG = -1e30
LOG2E = 1.4426950408889634

LANES = 128
VMEM_LIMIT = 56 * 1024 * 1024

PROJ_TILE = 512
COL_A = 0
COL_SBQ, COL_SBK, COL_SBV = 512, 768, 1024
COL_VX = 1280
COL_CQ = 1536
COL_KX = 1792
COL_BRG = 2048
PROJ_W = COL_BRG + N_BRANCH * D_MODEL
ROPE_TILE = COL_CQ // PROJ_TILE


def _cparams(sem):
    return pltpu.CompilerParams(dimension_semantics=sem, vmem_limit_bytes=VMEM_LIMIT)


def _sigmoid(x):
    return 1.0 / (1.0 + jnp.exp(-x))


def _rms(x, g):
    return x * lax.rsqrt(jnp.mean(x * x, axis=-1, keepdims=True) + EPS) * g


def _norm_proj_kernel(h_ref, g_ref, w_ref, c_ref, s1_ref, s2_ref, o_ref, xn_ref):
    j = pl.program_id(1)

    @pl.when(j == 0)
    def _():
        xn_ref[...] = _rms(h_ref[...], g_ref[...]).astype(BF16)

    acc = jnp.dot(xn_ref[...], w_ref[...], preferred_element_type=F32)

    @pl.when(j != ROPE_TILE)
    def _():
        o_ref[...] = acc.astype(o_ref.dtype)

    @pl.when(j == ROPE_TILE)
    def _():
        c, s1, s2 = c_ref[...], s1_ref[...], s2_ref[...]
        for g in range(PROJ_TILE // LANES):
            x = acc[:, g * LANES:(g + 1) * LANES]
            y = x * c + pltpu.roll(x, LANES - 8, 1) * s1 + pltpu.roll(x, 8, 1) * s2
            o_ref[:, g * LANES:(g + 1) * LANES] = y.astype(o_ref.dtype)


def _norm_proj(h, g, w, rope_c, rope_s1, rope_s2, seq, tm=1024):
    n = h.shape[0]
    nseq = seq // tm
    return pl.pallas_call(
        _norm_proj_kernel,
        grid=(n // tm, PROJ_W // PROJ_TILE),
        in_specs=[
            pl.BlockSpec((tm, D_MODEL), lambda i, j: (i, 0)),
            pl.BlockSpec((1, D_MODEL), lambda i, j: (0, 0)),
            pl.BlockSpec((D_MODEL, PROJ_TILE), lambda i, j: (0, j)),
            pl.BlockSpec((tm, LANES), lambda i, j: (i % nseq, 0)),
            pl.BlockSpec((tm, LANES), lambda i, j: (i % nseq, 0)),
            pl.BlockSpec((tm, LANES), lambda i, j: (i % nseq, 0)),
        ],
        out_specs=pl.BlockSpec((tm, PROJ_TILE), lambda i, j: (i, j)),
        out_shape=jax.ShapeDtypeStruct((n, PROJ_W), BF16),
        scratch_shapes=[pltpu.VMEM((tm, D_MODEL), BF16)],
        compiler_params=_cparams(("parallel", "arbitrary")),
        name="norm_proj",
    )(h, g, w, rope_c, rope_s1, rope_s2)


def _sgu_kernel(z_ref, g_ref, w_ref, b_ref, o_ref):
    tm = z_ref.shape[0]
    a = jax.nn.gelu(z_ref[...].astype(F32))
    u = a[:, :MIX_WIDTH]
    v = _rms(a[:, MIX_WIDTH:], g_ref[...]).astype(BF16)
    row = lax.broadcasted_iota(jnp.int32, (SGU_CHUNK, SGU_CHUNK), 0)
    col = lax.broadcasted_iota(jnp.int32, (SGU_CHUNK, SGU_CHUNK), 1)
    ws = [jnp.where(row >= col, w_ref[gi], 0.0).astype(BF16) for gi in range(N_HEADS)]
    bias = b_ref[...]
    for c in range(tm // SGU_CHUNK):
        rows = slice(c * SGU_CHUNK, (c + 1) * SGU_CHUNK)
        mix = jnp.concatenate(
            [jnp.dot(ws[gi], v[rows, gi * HEAD_DIM:(gi + 1) * HEAD_DIM], preferred_element_type=F32)
             for gi in range(N_HEADS)], axis=1)
        o_ref[rows, :] = (u[rows, :] * (mix + bias)).astype(o_ref.dtype)


def _sgu(proj, g, w, bias, tm=512):
    n = proj.shape[0]
    return pl.pallas_call(
        _sgu_kernel,
        grid=(n // tm,),
        in_specs=[
            pl.BlockSpec((tm, 2 * MIX_WIDTH), lambda i: (i, COL_A // (2 * MIX_WIDTH))),
            pl.BlockSpec((1, MIX_WIDTH), lambda i: (0, 0)),
            pl.BlockSpec((N_HEADS, SGU_CHUNK, SGU_CHUNK), lambda i: (0, 0, 0)),
            pl.BlockSpec((SGU_CHUNK, MIX_WIDTH), lambda i: (0, 0)),
        ],
        out_specs=pl.BlockSpec((tm, MIX_WIDTH), lambda i: (i, 0)),
        out_shape=jax.ShapeDtypeStruct((n, MIX_WIDTH), BF16),
        compiler_params=_cparams(("parallel",)),
        name="sgu",
    )(proj, g, w, bias)


SB_TQ = 1024
SB_TK = 256
SB_SUB = 128
SB_R = SB_TQ // SB_TK


def _sb_tile(q_ref, k_ref, v_ref, u_ref, acc_ref, carry_ref, k_off):
    q, k, v = q_ref[...], k_ref[...], v_ref[...]
    u = u_ref[...]
    diag = k_off is not None
    if diag:
        mask = (k_off + lax.broadcasted_iota(jnp.int32, (SB_TQ, SB_TK), 1)
                < lax.broadcasted_iota(jnp.int32, (SB_TQ, SB_TK), 0))
    pvs = []
    for h in range(N_HEADS):
        hs = slice(h * HEAD_DIM, (h + 1) * HEAD_DIM)
        z = lax.dot_general(q[:, hs], k[:, hs], (((1,), (1,)), ((), ())), preferred_element_type=F32)
        log_beta = jnp.minimum(z, 0.0) - jnp.log(1.0 + jnp.exp2(-jnp.abs(z))) * LOG2E
        log_1m = log_beta - z
        if diag:
            log_1m = jnp.where(mask, log_1m, 0.0)
        carry = carry_ref[h]
        pv = jnp.zeros((SB_TQ, HEAD_DIM), F32)
        for sb in reversed(range(SB_TK // SB_SUB)):
            cs = slice(sb * SB_SUB, (sb + 1) * SB_SUB)
            l = log_1m[:, cs]
            hi = lax.bitcast_convert_type(lax.bitcast_convert_type(l, jnp.uint32) & jnp.uint32(0xFFFF0000), F32)
            lo = l - hi
            r = jnp.dot(jnp.concatenate([hi.astype(BF16), lo.astype(BF16)], axis=1), u, preferred_element_type=F32)
            a = jnp.exp2(log_beta[:, cs] + r[:, :SB_SUB] + carry)
            if diag:
                a = jnp.where(mask[:, cs], a, 0.0)
            pv = pv + jnp.dot(a.astype(BF16), v[cs, hs], preferred_element_type=F32)
            carry = carry + r[:, SB_SUB:]
        carry_ref[h] = carry
        pvs.append(pv)
    acc_ref[...] += jnp.concatenate(pvs, axis=1)


def _sb_kernel(qi_ref, ki_ref, q_ref, k_ref, v_ref, u_ref, o_ref, acc_ref, carry_ref):
    step = pl.program_id(1)
    qi = qi_ref[step]
    ki = ki_ref[step]

    @pl.when(ki == SB_R * qi + SB_R - 1)
    def _():
        acc_ref[...] = jnp.zeros_like(acc_ref)
        carry_ref[...] = jnp.zeros_like(carry_ref)

    @pl.when(ki >= SB_R * qi)
    def _():
        _sb_tile(q_ref, k_ref, v_ref, u_ref, acc_ref, carry_ref, ki * SB_TK - qi * SB_TQ)

    @pl.when(ki < SB_R * qi)
    def _():
        _sb_tile(q_ref, k_ref, v_ref, u_ref, acc_ref, carry_ref, None)

    @pl.when(ki == 0)
    def _():
        o_ref[...] = acc_ref[...].astype(o_ref.dtype)


def _stick_breaking(proj, batch, seq):
    n = proj.shape[0]
    nq, nk = seq // SB_TQ, seq // SB_TK
    qi_tab = np.array([qi for qi in range(nq) for _ in range(SB_R * (qi + 1))], np.int32)
    ki_tab = np.array([ki for qi in range(nq) for ki in range(SB_R * (qi + 1) - 1, -1, -1)], np.int32)
    j = np.arange(2 * SB_SUB)[:, None] % SB_SUB
    c = np.arange(2 * SB_SUB)[None, :]
    u = jnp.asarray(np.where(c < SB_SUB, j > c, True).astype(np.float32), BF16)
    qb, kb, vb = COL_SBQ // MIX_WIDTH, COL_SBK // MIX_WIDTH, COL_SBV // MIX_WIDTH
    grid_spec = pltpu.PrefetchScalarGridSpec(
        num_scalar_prefetch=2,
        grid=(batch, len(qi_tab)),
        in_specs=[
            pl.BlockSpec((SB_TQ, MIX_WIDTH), lambda b, s, qt, kt: (b * nq + qt[s], qb)),
            pl.BlockSpec((SB_TK, MIX_WIDTH), lambda b, s, qt, kt: (b * nk + kt[s], kb)),
            pl.BlockSpec((SB_TK, MIX_WIDTH), lambda b, s, qt, kt: (b * nk + kt[s], vb)),
            pl.BlockSpec((2 * SB_SUB, 2 * SB_SUB), lambda b, s, qt, kt: (0, 0)),
        ],
        out_specs=pl.BlockSpec((SB_TQ, MIX_WIDTH), lambda b, s, qt, kt: (b * nq + qt[s], 0)),
        scratch_shapes=[pltpu.VMEM((SB_TQ, MIX_WIDTH), F32), pltpu.VMEM((N_HEADS, SB_TQ, SB_SUB), F32)],
    )
    return pl.pallas_call(
        _sb_kernel,
        grid_spec=grid_spec,
        out_shape=jax.ShapeDtypeStruct((n, MIX_WIDTH), BF16),
        compiler_params=_cparams(("parallel", "arbitrary")),
        name="stick_breaking",
    )(jnp.asarray(qi_tab), jnp.asarray(ki_tab), proj, proj, proj, u)


def _compress_kernel(gk_ref, gv_ref, pos_ref, w1_ref, w2_ref, kc_ref, vc_ref):
    half = CMP_STRIDE * HEAD_DIM
    for t, (g_ref, o_ref) in enumerate(((gk_ref, kc_ref), (gv_ref, vc_ref))):
        g = g_ref[0].astype(F32)
        top = (g + pos_ref[t, :, :half]).astype(BF16)
        bot = (g + pos_ref[t, :, half:]).astype(BF16)
        a = jnp.dot(top, w1_ref[t, :half, :], preferred_element_type=F32)
        b = jnp.dot(bot, w1_ref[t, half:, :], preferred_element_type=F32)
        hid = jax.nn.gelu(a + pltpu.roll(b, b.shape[0] - 1, 0))
        out = jnp.dot(hid.astype(BF16), w2_ref[t], preferred_element_type=F32)
        o_ref[0] = jnp.concatenate([out, jnp.zeros_like(out)], axis=1).astype(o_ref.dtype)


def _compress(gk, gv, pos, w1, w2):
    b, m, width = gk.shape
    out_spec = pl.BlockSpec((1, m, LANES), lambda i: (i, 0, 0))
    out_shape = jax.ShapeDtypeStruct((b, m, LANES), BF16)
    return pl.pallas_call(
        _compress_kernel,
        grid=(b,),
        in_specs=[
            pl.BlockSpec((1, m, width), lambda i: (i, 0, 0)),
            pl.BlockSpec((1, m, width), lambda i: (i, 0, 0)),
            pl.BlockSpec((2, 1, CMP_LEN * HEAD_DIM), lambda i: (0, 0, 0)),
            pl.BlockSpec((2, CMP_LEN * HEAD_DIM, CMP_HIDDEN), lambda i: (0, 0, 0)),
            pl.BlockSpec((2, CMP_HIDDEN, HEAD_DIM), lambda i: (0, 0, 0)),
        ],
        out_specs=[out_spec, out_spec],
        out_shape=[out_shape, out_shape],
        compiler_params=_cparams(("parallel",)),
        name="nsa_compress",
    )(gk, gv, pos, w1, w2)


NSA_TQ = 256
NSA_CK = 512


NSA_ROWS = N_HEADS * NSA_TQ
SEL_OFF = 1e30


def _per_head(x, fn):
    return jnp.concatenate([fn(x[h * NSA_TQ:(h + 1) * NSA_TQ]) for h in range(N_HEADS)], axis=0)


def _nsa_kernel(q_ref, kx_ref, vx_ref, kw_ref, vw_ref, kb_ref, kc_ref, vc_ref, ovt_ref, o_ref, m_ref, acc_ref):
    seq = kx_ref.shape[0]
    n_sel = seq // SEL_LEN
    n_cmp = kc_ref.shape[1]
    q0 = pl.program_id(1) * NSA_TQ
    NT = (((1,), (1,)), ((), ()))
    lo_half = lax.broadcasted_iota(jnp.int32, (NSA_TQ, LANES), 1) < HEAD_DIM
    qpos_c = q0 + lax.broadcasted_iota(jnp.int32, (NSA_TQ, 1), 0)

    qs = []
    for pair in range(N_HEADS // 2):
        x = q_ref[:, pair * LANES:(pair + 1) * LANES].astype(F32)
        qs += [jnp.where(lo_half, x, 0.0), jnp.where(lo_half, pltpu.roll(x, HEAD_DIM, 1), 0.0)]
    q4 = jnp.concatenate(qs, axis=0).astype(BF16)

    ones_half = lax.broadcasted_iota(jnp.int32, (1, LANES), 1) >= HEAD_DIM

    def masked_exp(s, mask):
        s = _per_head(s, lambda t: jnp.where(mask, t, NEG))
        return jnp.exp2(s - jnp.max(s, axis=-1, keepdims=True))

    def attend(p, vals):
        return jnp.dot(p.astype(BF16), jnp.where(ones_half, 1.0, vals).astype(BF16), preferred_element_type=F32)

    def normalized(r):
        return r / jnp.where(ones_half, 1.0, pltpu.roll(r, HEAD_DIM, 1))

    cmp_end = lax.broadcasted_iota(jnp.int32, (NSA_TQ, n_cmp), 1) * CMP_STRIDE + (CMP_LEN - 1)
    p_cmp = masked_exp(lax.dot_general(q4, kc_ref[0], NT, preferred_element_type=F32), cmp_end <= qpos_c)
    has_cmp = jnp.concatenate([qpos_c >= CMP_LEN - 1] * N_HEADS, axis=0)
    p_cmp = p_cmp * jnp.where(has_cmp, 1.0 / jnp.sum(p_cmp, axis=-1, keepdims=True), 0.0)
    r_cmp = jnp.dot(p_cmp.astype(BF16), vc_ref[0], preferred_element_type=F32)
    p_sum = sum(p_cmp[h * NSA_TQ:(h + 1) * NSA_TQ] for h in range(N_HEADS))

    hi = p_sum.astype(BF16)
    lo = (p_sum - hi.astype(F32)).astype(BF16)
    imp = lax.dot_general(ovt_ref[...], jnp.concatenate([hi, lo], axis=1), NT, preferred_element_type=F32)
    blk = lax.broadcasted_iota(jnp.int32, (n_sel, NSA_TQ), 0)
    qpos_r = q0 + lax.broadcasted_iota(jnp.int32, (n_sel, NSA_TQ), 1)
    valid = blk * SEL_LEN <= qpos_r
    forced = (blk == 0) | (blk == jnp.right_shift(qpos_r, 6))
    score = jnp.where(valid, imp + jnp.where(forced, FORCE_SCORE, 0.0), -jnp.inf)
    groups = [score[8 * g:8 * g + 8] for g in range(n_sel // 8)]
    ranks = [jnp.zeros((8, NSA_TQ), F32) for _ in groups]
    row8 = lax.broadcasted_iota(jnp.int32, (8, NSA_TQ), 0)
    for i in range(n_sel):
        ci = jnp.broadcast_to(score[i:i + 1], (8, NSA_TQ))
        for g, sg in enumerate(groups):
            if 8 * g > i:
                beats = ci >= sg
            elif 8 * g + 7 < i:
                beats = ci > sg
            else:
                beats = (ci > sg) | ((ci == sg) & (row8 + 8 * g > i))
            ranks[g] = ranks[g] + jnp.where(beats, 1.0, 0.0)
    sel = jnp.where(valid & (jnp.concatenate(ranks, axis=0) < SEL_TOPN), 0.0, -SEL_OFF)
    sel = jnp.concatenate([sel, jnp.full((LANES - n_sel, NSA_TQ), -SEL_OFF, F32)], axis=0).T
    aug = lambda t: jnp.concatenate([q4, jnp.concatenate([t.astype(BF16)] * N_HEADS, axis=0)], axis=1)
    q_aug = aug(sel)
    blk_lane = lax.broadcasted_iota(jnp.int32, (NSA_TQ, LANES), 1)
    q_aug_past = aug(jnp.where(blk_lane * SEL_LEN >= q0, -SEL_OFF, sel))

    w_start = pl.multiple_of(jnp.maximum(q0 - WINDOW, 0), NSA_TQ)
    kwpos = w_start + lax.broadcasted_iota(jnp.int32, (NSA_TQ, WINDOW + NSA_TQ), 1)
    s_win = lax.dot_general(q4, kw_ref[pl.ds(w_start, WINDOW + NSA_TQ), :], NT, preferred_element_type=F32)
    p_win = masked_exp(s_win, (kwpos <= qpos_c) & (kwpos > qpos_c - WINDOW))
    r_win = normalized(attend(p_win, vw_ref[pl.ds(w_start, WINDOW + NSA_TQ), :]))

    def slab_stats(q, start, size, causal):
        keys = jnp.concatenate([kx_ref[pl.ds(start, size), :], kb_ref[pl.ds(start, size), :]], axis=1)
        s = lax.dot_general(q, keys, NT, preferred_element_type=F32)
        if causal:
            ok = start + lax.broadcasted_iota(jnp.int32, (NSA_TQ, size), 1) <= qpos_c
            s = _per_head(s, lambda t: jnp.where(ok, t, NEG))
        m_c = jnp.max(s, axis=-1, keepdims=True)
        return m_c, attend(jnp.exp2(s - m_c), vx_ref[pl.ds(start, size), :])

    m_d, a_d = slab_stats(q_aug, pl.multiple_of(q0, NSA_TQ), NSA_TQ, True)
    m_ref[...] = m_d
    acc_ref[...] = a_d

    def pair_body(i, carry):
        m0, a0 = slab_stats(q_aug_past, pl.multiple_of(2 * i * NSA_CK, NSA_CK), NSA_CK, False)
        m1, a1 = slab_stats(q_aug_past, pl.multiple_of((2 * i + 1) * NSA_CK, NSA_CK), NSA_CK, False)
        m_old = m_ref[...]
        m_new = jnp.maximum(m_old, jnp.maximum(m0, m1))
        acc_ref[...] = (jnp.exp2(m_old - m_new) * acc_ref[...] + jnp.exp2(m0 - m_new) * a0
                        + jnp.exp2(m1 - m_new) * a1)
        m_ref[...] = m_new
        return carry

    lax.fori_loop(0, (q0 + 2 * NSA_CK - 1) // (2 * NSA_CK), pair_body, 0)

    r_slc = normalized(acc_ref[...])
    gates = _sigmoid(vw_ref[pl.ds(pl.multiple_of(q0, NSA_TQ), NSA_TQ), :].astype(F32))
    g = [jnp.concatenate([gates[:, HEAD_DIM + 3 * h + t:HEAD_DIM + 3 * h + t + 1] for h in range(N_HEADS)], axis=0)
         for t in range(N_BRANCH)]
    y = g[0] * r_cmp + g[1] * r_slc + g[2] * r_win
    for pair in range(N_HEADS // 2):
        even = y[2 * pair * NSA_TQ:(2 * pair + 1) * NSA_TQ]
        odd = y[(2 * pair + 1) * NSA_TQ:(2 * pair + 2) * NSA_TQ]
        o_ref[:, pair * LANES:(pair + 1) * LANES] = jnp.where(lo_half, even, pltpu.roll(odd, HEAD_DIM, 1)).astype(o_ref.dtype)


def _nsa(proj, kc, vc, batch, seq):
    n = proj.shape[0]
    nqb = seq // NSA_TQ
    n_sel = seq // SEL_LEN
    n_cmp = kc.shape[1]
    assert n_sel % 8 == 0 and n_sel <= LANES and seq % (2 * NSA_CK) == 0
    cs = (np.arange(n_cmp) * CMP_STRIDE)[None, :]
    ss = (np.arange(n_sel) * SEL_LEN)[:, None]
    ovt = np.clip(np.minimum(cs + CMP_LEN, ss + SEL_LEN) - np.maximum(cs, ss), 0, None).astype(np.float32) / CMP_LEN
    ovt = jnp.asarray(np.concatenate([ovt, ovt], axis=1), BF16)
    kb = jnp.asarray((np.arange(seq)[:, None] // SEL_LEN == np.arange(LANES)[None, :]).astype(np.float32), BF16)
    seq_spec = lambda col: pl.BlockSpec((seq, LANES), lambda b, i: (b, col))
    cmp_spec = pl.BlockSpec((1, n_cmp, LANES), lambda b, i: (b, 0, 0))
    return pl.pallas_call(
        _nsa_kernel,
        grid=(batch, nqb),
        in_specs=[
            pl.BlockSpec((NSA_TQ, MIX_WIDTH), lambda b, i: (b * nqb + i, COL_CQ // MIX_WIDTH)),
            seq_spec(COL_KX // LANES),
            seq_spec(COL_VX // LANES),
            seq_spec(COL_KX // LANES + 1),
            seq_spec(COL_VX // LANES + 1),
            pl.BlockSpec((seq, LANES), lambda b, i: (0, 0)),
            cmp_spec, cmp_spec,
            pl.BlockSpec((n_sel, 2 * n_cmp), lambda b, i: (0, 0)),
        ],
        out_specs=pl.BlockSpec((NSA_TQ, MIX_WIDTH), lambda b, i: (b * nqb + i, 0)),
        out_shape=jax.ShapeDtypeStruct((n, MIX_WIDTH), BF16),
        scratch_shapes=[pltpu.VMEM((NSA_ROWS, 1), F32), pltpu.VMEM((NSA_ROWS, LANES), F32)],
        compiler_params=_cparams(("parallel", "arbitrary")),
        name="nsa",
    )(proj, proj, proj, proj, proj, kb, kc, vc, ovt)


def _merge_kernel(ya_ref, yb_ref, yc_ref, g0_ref, g1_ref, g2_ref, wb_ref, wo_ref, h_ref, o_ref):
    merged = None
    for y_ref, g_ref, t in ((ya_ref, g0_ref, 0), (yb_ref, g1_ref, 1), (yc_ref, g2_ref, 2)):
        term = _sigmoid(g_ref[...].astype(F32)) * jnp.dot(y_ref[...], wb_ref[t], preferred_element_type=F32)
        merged = term if merged is None else merged + term
    o_ref[...] = h_ref[...] + jnp.dot(merged.astype(BF16), wo_ref[...], preferred_element_type=F32)


def _merge(ya, yb, yc, proj, wb, wo, h, tm=512):
    n = h.shape[0]
    y_spec = pl.BlockSpec((tm, MIX_WIDTH), lambda i: (i, 0))
    g_spec = lambda t: pl.BlockSpec((tm, D_MODEL), lambda i: (i, COL_BRG // D_MODEL + t))
    return pl.pallas_call(
        _merge_kernel,
        grid=(n // tm,),
        in_specs=[y_spec, y_spec, y_spec, g_spec(0), g_spec(1), g_spec(2),
                  pl.BlockSpec((N_BRANCH, MIX_WIDTH, D_MODEL), lambda i: (0, 0, 0)),
                  pl.BlockSpec((D_MODEL, D_MODEL), lambda i: (0, 0)),
                  pl.BlockSpec((tm, D_MODEL), lambda i: (i, 0))],
        out_specs=pl.BlockSpec((tm, D_MODEL), lambda i: (i, 0)),
        out_shape=jax.ShapeDtypeStruct((n, D_MODEL), F32),
        compiler_params=_cparams(("parallel",)),
        name="merge",
    )(ya, yb, yc, proj, proj, proj, wb, wo, h)


def _ffn_kernel(h_ref, g_ref, wg_ref, wu_ref, wd_ref, o_ref, xn_ref, acc_ref):
    f = pl.program_id(1)

    @pl.when(f == 0)
    def _():
        xn_ref[...] = _rms(h_ref[...], g_ref[...]).astype(BF16)
        acc_ref[...] = h_ref[...]

    xn = xn_ref[...]
    a = jnp.dot(xn, wg_ref[...], preferred_element_type=F32)
    u = jnp.dot(xn, wu_ref[...], preferred_element_type=F32)
    hid = (a * _sigmoid(a) * u).astype(BF16)
    acc_ref[...] += jnp.dot(hid, wd_ref[...], preferred_element_type=F32)

    @pl.when(f == pl.num_programs(1) - 1)
    def _():
        o_ref[...] = acc_ref[...]


def _ffn(h, g, wg, wu, wd, tm=1024, tf=256):
    n = h.shape[0]
    dff = wg.shape[1]
    return pl.pallas_call(
        _ffn_kernel,
        grid=(n // tm, dff // tf),
        in_specs=[
            pl.BlockSpec((tm, D_MODEL), lambda i, f: (i, 0)),
            pl.BlockSpec((1, D_MODEL), lambda i, f: (0, 0)),
            pl.BlockSpec((D_MODEL, tf), lambda i, f: (0, f)),
            pl.BlockSpec((D_MODEL, tf), lambda i, f: (0, f)),
            pl.BlockSpec((tf, D_MODEL), lambda i, f: (f, 0)),
        ],
        out_specs=pl.BlockSpec((tm, D_MODEL), lambda i, f: (i, 0)),
        out_shape=jax.ShapeDtypeStruct((n, D_MODEL), F32),
        scratch_shapes=[pltpu.VMEM((tm, D_MODEL), BF16), pltpu.VMEM((tm, D_MODEL), F32)],
        compiler_params=_cparams(("parallel", "arbitrary")),
        name="ffn",
    )(h, g, wg, wu, wd)


def _route_kernel(h_ref, g_ref, rw_ref, xn_ref, gate_ref):
    tm = h_ref.shape[0]
    lane = lax.broadcasted_iota(jnp.int32, (tm, LANES), 1)
    xn = _rms(h_ref[...], g_ref[...])
    xn_ref[...] = xn.astype(BF16)
    logits = jnp.dot(xn, rw_ref[...], preferred_element_type=F32, precision=lax.Precision.HIGHEST)
    logits = jnp.where(lane < N_EXPERTS, logits, -jnp.inf)
    m1 = jnp.max(logits, axis=-1, keepdims=True)
    i1 = jnp.min(jnp.where(logits == m1, lane, LANES), axis=-1, keepdims=True)
    rest = jnp.where(lane == i1, -jnp.inf, logits)
    m2 = jnp.max(rest, axis=-1, keepdims=True)
    i2 = jnp.min(jnp.where(rest == m2, lane, LANES), axis=-1, keepdims=True)
    e2 = jnp.exp(m2 - m1)
    gate_ref[...] = jnp.where(lane == i1, 1.0 / (1.0 + e2), 0.0) + jnp.where(lane == i2, e2 / (1.0 + e2), 0.0)


def _route(h, g, rw, tm=1024):
    n = h.shape[0]
    return pl.pallas_call(
        _route_kernel,
        grid=(n // tm,),
        in_specs=[
            pl.BlockSpec((tm, D_MODEL), lambda i: (i, 0)),
            pl.BlockSpec((1, D_MODEL), lambda i: (0, 0)),
            pl.BlockSpec((D_MODEL, LANES), lambda i: (0, 0)),
        ],
        out_specs=[pl.BlockSpec((tm, D_MODEL), lambda i: (i, 0)), pl.BlockSpec((tm, LANES), lambda i: (i, 0))],
        out_shape=[jax.ShapeDtypeStruct((n, D_MODEL), BF16), jax.ShapeDtypeStruct((n, LANES), F32)],
        compiler_params=_cparams(("parallel",)),
        name="moe_route",
    )(h, g, rw)


MOE_TB = 2048
MOE_R = 256
MOE_TF = 512


def _moe_kernel(nsub_ref, xn_ref, h_ref, rrow_ref, rcol_ref, gate_ref, wg_ref, wu_ref, wd_ref, o_ref, xg_ref, y_ref):
    b, e, f = pl.program_id(0), pl.program_id(1), pl.program_id(2)
    ns = nsub_ref[b * N_EXPERTS + e]
    rows_of = lambda j: pl.ds(pl.multiple_of(j * MOE_R, MOE_R), MOE_R)
    base_of = lambda j: (j * MOE_R).astype(F32)

    @pl.when((e == 0) & (f == 0))
    def _():
        o_ref[...] = h_ref[...]

    @pl.when(f == 0)
    def _():
        rank_row = rrow_ref[0, pl.ds(e, 1), :]
        slot = lax.broadcasted_iota(jnp.int32, (MOE_R, MOE_TB), 0).astype(F32)

        def gather(j, carry):
            onehot = jnp.where(rank_row == slot + base_of(j), 1.0, 0.0).astype(BF16)
            xg_ref[rows_of(j), :] = jnp.dot(onehot, xn_ref[...], preferred_element_type=F32).astype(BF16)
            return carry

        lax.fori_loop(0, ns, gather, 0)

    def expert(j, carry):
        x = xg_ref[rows_of(j), :]
        a = jnp.dot(x, wg_ref[0], preferred_element_type=F32)
        u = jnp.dot(x, wu_ref[0], preferred_element_type=F32)
        y = jnp.dot((a * _sigmoid(a) * u).astype(BF16), wd_ref[0], preferred_element_type=F32)

        @pl.when(f == 0)
        def _():
            y_ref[rows_of(j), :] = y

        @pl.when(f != 0)
        def _():
            y_ref[rows_of(j), :] += y

        return carry

    lax.fori_loop(0, ns, expert, 0)

    @pl.when(f == pl.num_programs(2) - 1)
    def _():
        lane = lax.broadcasted_iota(jnp.int32, (MOE_TB, LANES), 1)
        rank_col = jnp.sum(jnp.where(lane == e, rcol_ref[...], 0.0), axis=-1, keepdims=True)
        gate_col = jnp.sum(jnp.where(lane == e, gate_ref[...], 0.0), axis=-1, keepdims=True)
        slot = lax.broadcasted_iota(jnp.int32, (MOE_R, MOE_R), 1).astype(F32)

        def scatter(j, carry):
            y = y_ref[rows_of(j), :].astype(BF16)
            for c in range(MOE_TB // MOE_R):
                tok = slice(c * MOE_R, (c + 1) * MOE_R)
                weights = jnp.where(rank_col[tok] == slot + base_of(j), gate_col[tok], 0.0).astype(BF16)
                o_ref[tok, :] += jnp.dot(weights, y, preferred_element_type=F32)
            return carry

        lax.fori_loop(0, ns, scatter, 0)


def _moe_experts(nsub, xn, h, rank_row, rank_col, gate, wg, wu, wd):
    n = h.shape[0]
    ne, _, dff = wg.shape
    blk = lambda dtype_width: pl.BlockSpec((MOE_TB, dtype_width), lambda b, e, f, ns: (b, 0))
    once = lambda width: pl.BlockSpec((MOE_TB, width), lambda b, e, f, ns: (b, 0), pipeline_mode=pl.Buffered(1))
    grid_spec = pltpu.PrefetchScalarGridSpec(
        num_scalar_prefetch=1,
        grid=(n // MOE_TB, ne, dff // MOE_TF),
        in_specs=[
            once(D_MODEL),
            once(D_MODEL),
            pl.BlockSpec((1, ne, MOE_TB), lambda b, e, f, ns: (b, 0, 0)),
            blk(LANES),
            blk(LANES),
            pl.BlockSpec((1, D_MODEL, MOE_TF), lambda b, e, f, ns: (e, 0, f)),
            pl.BlockSpec((1, D_MODEL, MOE_TF), lambda b, e, f, ns: (e, 0, f)),
            pl.BlockSpec((1, MOE_TF, D_MODEL), lambda b, e, f, ns: (e, f, 0)),
        ],
        out_specs=pl.BlockSpec((MOE_TB, D_MODEL), lambda b, e, f, ns: (b, 0)),
        scratch_shapes=[pltpu.VMEM((MOE_TB, D_MODEL), BF16), pltpu.VMEM((MOE_TB, D_MODEL), F32)],
    )
    return pl.pallas_call(
        _moe_kernel,
        grid_spec=grid_spec,
        out_shape=jax.ShapeDtypeStruct((n, D_MODEL), F32),
        compiler_params=_cparams(("parallel", "arbitrary", "arbitrary")),
        name="moe",
    )(nsub, xn, h, rank_row, rank_col, gate, wg, wu, wd)


def _moe(h, g, rw, wg, wu, wd):
    n = h.shape[0]
    nb = n // MOE_TB
    xn, gate = _route(h, g, rw)
    routed = (gate[:, :N_EXPERTS] > 0).reshape(nb, MOE_TB, N_EXPERTS).astype(jnp.int32)
    incl = jnp.cumsum(routed, axis=1)
    rank = jnp.where(routed > 0, incl - 1, -1).astype(F32)
    nsub = ((incl[:, -1, :] + MOE_R - 1) // MOE_R).reshape(-1).astype(jnp.int32)
    rank_row = rank.transpose(0, 2, 1)
    rank_col = jnp.pad(rank.reshape(n, N_EXPERTS), ((0, 0), (0, LANES - N_EXPERTS)), constant_values=-1.0)
    return _moe_experts(nsub, xn, h, rank_row, rank_col, gate, wg, wu, wd)


def _final_norm_kernel(h_ref, g_ref, o_ref):
    o_ref[...] = _rms(h_ref[...], g_ref[...])


def _final_norm(h, g, tm=1024):
    n = h.shape[0]
    return pl.pallas_call(
        _final_norm_kernel,
        grid=(n // tm,),
        in_specs=[pl.BlockSpec((tm, D_MODEL), lambda i: (i, 0)), pl.BlockSpec((1, D_MODEL), lambda i: (0, 0))],
        out_specs=pl.BlockSpec((tm, D_MODEL), lambda i: (i, 0)),
        out_shape=jax.ShapeDtypeStruct((n, D_MODEL), F32),
        compiler_params=_cparams(("parallel",)),
        name="final_norm",
    )(h, g)


def _rope_tables(seq):
    half = ROPE_DIMS // 2
    inv_freq = ROPE_THETA ** (-jnp.arange(0, ROPE_DIMS, 2, dtype=F32) / ROPE_DIMS)
    ang = jnp.arange(seq, dtype=F32)[:, None] * inv_freq[None, :]
    cos, sin = jnp.cos(ang), jnp.sin(ang)
    ones = jnp.ones((seq, HEAD_DIM - ROPE_DIMS), F32)
    zeros_h = jnp.zeros((seq, half), F32)
    zeros_r = jnp.zeros((seq, HEAD_DIM - ROPE_DIMS), F32)
    c = jnp.concatenate([cos, cos, ones], axis=1)
    s1 = jnp.concatenate([-sin, zeros_h, zeros_r], axis=1)
    s2 = jnp.concatenate([zeros_h, sin, zeros_r], axis=1)
    rep = LANES // HEAD_DIM
    return tuple(jnp.tile(t, (1, rep)) for t in (c, s1, s2))


def _layout_w_in(w):
    scale = HEAD_DIM ** -0.5 * LOG2E
    a_in, sb_q, sb_k, sb_v, c_q, c_kv, c_g, br_g = jnp.split(
        w, np.cumsum([2 * MIX_WIDTH, MIX_WIDTH, MIX_WIDTH, MIX_WIDTH, MIX_WIDTH, 6 * HEAD_DIM, 3 * N_HEADS])[:].tolist(),
        axis=-1)
    k_cmp, v_cmp, k_slc, v_slc, k_win, v_win = jnp.split(c_kv, 6, axis=-1)
    zeros = lambda width: jnp.zeros((w.shape[0], width), w.dtype)
    cols = [a_in, sb_q * scale, sb_k, sb_v,
            v_slc, v_cmp, v_win, c_g, zeros(HEAD_DIM - 3 * N_HEADS),
            c_q * scale, k_slc, k_cmp, k_win, zeros(HEAD_DIM),
            br_g]
    out = jnp.concatenate(cols, axis=-1)
    assert out.shape[-1] == PROJ_W
    return out.astype(BF16)


def kernel(x, norm1_g, w_in, sgu_norm_g, sgu_w, sgu_b, cmp_pos, cmp_w1, cmp_w2, w_branch, w_out, norm2_g,
           ffn_w_gate, ffn_w_up, ffn_w_down, router_w, moe_w_gate, moe_w_up, moe_w_down, final_norm_g):
    batch, seq, _ = x.shape
    n = batch * seq
    depth = norm1_g.shape[0]
    rope_c, rope_s1, rope_s2 = _rope_tables(seq)
    dff_pad = -(-D_FF // 256) * 256
    h = x.reshape(n, D_MODEL)
    for layer in range(depth):
        proj = _norm_proj(h, norm1_g[layer][None, :], _layout_w_in(w_in[layer]), rope_c, rope_s1, rope_s2, seq)
        sgu_bias = jnp.repeat(sgu_b[layer].T, HEAD_DIM, axis=1)
        y_a = _sgu(proj, sgu_norm_g[layer][None, :], sgu_w[layer], sgu_bias)
        y_b = _stick_breaking(proj, batch, seq)
        groups = seq // CMP_STRIDE
        gk = proj[:, COL_KX + HEAD_DIM:COL_KX + 2 * HEAD_DIM].reshape(batch, groups, CMP_STRIDE * HEAD_DIM)
        gv = proj[:, COL_VX + HEAD_DIM:COL_VX + 2 * HEAD_DIM].reshape(batch, groups, CMP_STRIDE * HEAD_DIM)
        kc, vc = _compress(gk, gv, cmp_pos[layer].reshape(2, 1, CMP_LEN * HEAD_DIM),
                           cmp_w1[layer].astype(BF16), cmp_w2[layer].astype(BF16))
        y_c = _nsa(proj, kc, vc, batch, seq)
        h = _merge(y_a, y_b, y_c, proj, w_branch[layer].astype(BF16), w_out[layer].astype(BF16), h)
        j = layer // 2
        g2 = norm2_g[layer][None, :]
        if layer % 2 == 0:
            pad = dff_pad - D_FF
            wg = jnp.pad(ffn_w_gate[j], ((0, 0), (0, pad))).astype(BF16)
            wu = jnp.pad(ffn_w_up[j], ((0, 0), (0, pad))).astype(BF16)
            wd = jnp.pad(ffn_w_down[j], ((0, pad), (0, 0))).astype(BF16)
            h = _ffn(h, g2, wg, wu, wd)
        else:
            rw = jnp.pad(router_w[j], ((0, 0), (0, LANES - N_EXPERTS)))
            h = _moe(h, g2, rw, moe_w_gate[j].astype(BF16), moe_w_up[j].astype(BF16), moe_w_down[j].astype(BF16))
    return _final_norm(h, final_norm_g[None, :]).reshape(batch, seq, D_MODEL)
```

```python
import functools

import numpy as np
import jax
import jax.numpy as jnp
from jax import lax
from jax.experimental import pallas as pl
from jax.experimental.pallas import tpu as pltpu

F32 = jnp.float32
BF16 = jnp.bfloat16

D_MODEL = 1024
HEAD_DIM = 64
N_HEADS = 4
MIX_WIDTH = N_HEADS * HEAD_DIM
ROPE_DIMS = HEAD_DIM // 4
ROPE_THETA = 500000.0
EPS = 1e-6
SGU_CHUNK = 128
CMP_LEN = 32
CMP_STRIDE = 16
CMP_HIDDEN = 256
SEL_LEN = 64
SEL_TOPN = 16
WINDOW = 512
FORCE_SCORE = 1e4
N_BRANCH = 3
D_FF = 2752
N_EXPERTS = 8
D_FF_EXPERT = 3584
NEG = -1e30
LOG2E = 1.4426950408889634

LANES = 128
VMEM_LIMIT = 56 * 1024 * 1024

PROJ_TILE = 1024
COL_A = 0
COL_SBQ, COL_SBK, COL_SBV = 512, 768, 1024
COL_VX = 1280
COL_CQ = 1536
COL_KX = 1792
COL_BRG = 2048
PROJ_W = COL_BRG + N_BRANCH * D_MODEL
ROPE_TILE = COL_CQ // PROJ_TILE
ROPE_START = COL_CQ % PROJ_TILE
assert COL_BRG == (ROPE_TILE + 1) * PROJ_TILE


def _cparams(sem):
    return pltpu.CompilerParams(dimension_semantics=sem, vmem_limit_bytes=VMEM_LIMIT)


def _sigmoid(x):
    return 1.0 / (1.0 + jnp.exp(-x))


def _rms(x, g):
    return x * lax.rsqrt(jnp.mean(x * x, axis=-1, keepdims=True) + EPS) * g


def _norm_proj_kernel(h_ref, g_ref, w_ref, c_ref, s1_ref, s2_ref, o_ref, xn_ref):
    j = pl.program_id(1)

    @pl.when(j == 0)
    def _():
        xn_ref[...] = _rms(h_ref[...], g_ref[...]).astype(BF16)

    acc = jnp.dot(xn_ref[...], w_ref[...], preferred_element_type=F32)

    @pl.when(j != ROPE_TILE)
    def _():
        o_ref[...] = acc.astype(o_ref.dtype)

    @pl.when(j == ROPE_TILE)
    def _():
        c, s1, s2 = c_ref[...], s1_ref[...], s2_ref[...]
        o_ref[:, :ROPE_START] = acc[:, :ROPE_START].astype(o_ref.dtype)
        for g in range(ROPE_START // LANES, PROJ_TILE // LANES):
            x = acc[:, g * LANES:(g + 1) * LANES]
            y = x * c + pltpu.roll(x, LANES - 8, 1) * s1 + pltpu.roll(x, 8, 1) * s2
            o_ref[:, g * LANES:(g + 1) * LANES] = y.astype(o_ref.dtype)


def _norm_proj(h, g, w, rope_c, rope_s1, rope_s2, seq, tm=1024):
    n = h.shape[0]
    nseq = seq // tm
    return pl.pallas_call(
        _norm_proj_kernel,
        grid=(n // tm, PROJ_W // PROJ_TILE),
        in_specs=[
            pl.BlockSpec((tm, D_MODEL), lambda i, j: (i, 0)),
            pl.BlockSpec((1, D_MODEL), lambda i, j: (0, 0)),
            pl.BlockSpec((D_MODEL, PROJ_TILE), lambda i, j: (0, j)),
            pl.BlockSpec((tm, LANES), lambda i, j: (i % nseq, 0)),
            pl.BlockSpec((tm, LANES), lambda i, j: (i % nseq, 0)),
            pl.BlockSpec((tm, LANES), lambda i, j: (i % nseq, 0)),
        ],
        out_specs=pl.BlockSpec((tm, PROJ_TILE), lambda i, j: (i, j)),
        out_shape=jax.ShapeDtypeStruct((n, PROJ_W), BF16),
        scratch_shapes=[pltpu.VMEM((tm, D_MODEL), BF16)],
        compiler_params=_cparams(("parallel", "arbitrary")),
        name="norm_proj",
    )(h, g, w, rope_c, rope_s1, rope_s2)


def _sgu_kernel(z_ref, g_ref, w_ref, b_ref, o_ref):
    tm = z_ref.shape[0]
    a = jax.nn.gelu(z_ref[...].astype(F32))
    u = a[:, :MIX_WIDTH]
    v = _rms(a[:, MIX_WIDTH:], g_ref[...]).astype(BF16)
    row = lax.broadcasted_iota(jnp.int32, (SGU_CHUNK, SGU_CHUNK), 0)
    col = lax.broadcasted_iota(jnp.int32, (SGU_CHUNK, SGU_CHUNK), 1)
    ws = [jnp.where(row >= col, w_ref[gi], 0.0).astype(BF16) for gi in range(N_HEADS)]
    bias = b_ref[...]
    for c in range(tm // SGU_CHUNK):
        rows = slice(c * SGU_CHUNK, (c + 1) * SGU_CHUNK)
        mix = jnp.concatenate(
            [jnp.dot(ws[gi], v[rows, gi * HEAD_DIM:(gi + 1) * HEAD_DIM], preferred_element_type=F32)
             for gi in range(N_HEADS)], axis=1)
        o_ref[rows, :] = (u[rows, :] * (mix + bias)).astype(o_ref.dtype)


def _sgu(proj, g, w, bias, tm=512):
    n = proj.shape[0]
    return pl.pallas_call(
        _sgu_kernel,
        grid=(n // tm,),
        in_specs=[
            pl.BlockSpec((tm, 2 * MIX_WIDTH), lambda i: (i, COL_A // (2 * MIX_WIDTH))),
            pl.BlockSpec((1, MIX_WIDTH), lambda i: (0, 0)),
            pl.BlockSpec((N_HEADS, SGU_CHUNK, SGU_CHUNK), lambda i: (0, 0, 0)),
            pl.BlockSpec((SGU_CHUNK, MIX_WIDTH), lambda i: (0, 0)),
        ],
        out_specs=pl.BlockSpec((tm, MIX_WIDTH), lambda i: (i, 0)),
        out_shape=jax.ShapeDtypeStruct((n, MIX_WIDTH), BF16),
        compiler_params=_cparams(("parallel",)),
        name="sgu",
    )(proj, g, w, bias)


SB_TQ = 1024
SB_TK = 256
SB_R = SB_TQ // SB_TK


def _sb_tile(q_ref, k_ref, v_ref, u_ref, acc_ref, carry_ref, k_off):
    q, k, v = q_ref[...], k_ref[...], v_ref[...]
    u = u_ref[...]
    diag = k_off is not None
    if diag:
        mask = (k_off + lax.broadcasted_iota(jnp.int32, (SB_TQ, SB_TK), 1)
                < lax.broadcasted_iota(jnp.int32, (SB_TQ, SB_TK), 0))
    pvs = []
    for h in range(N_HEADS):
        hs = slice(h * HEAD_DIM, (h + 1) * HEAD_DIM)
        z = lax.dot_general(q[:, hs], k[:, hs], (((1,), (1,)), ((), ())), preferred_element_type=F32)
        log_beta = jnp.minimum(z, 0.0) - jnp.log(1.0 + jnp.exp2(-jnp.abs(z))) * LOG2E
        log_1m = log_beta - z
        if diag:
            log_1m = jnp.where(mask, log_1m, 0.0)
        carry = carry_ref[h]
        r = jnp.dot(log_1m.astype(BF16), u, preferred_element_type=F32)
        a = jnp.exp2(log_beta + r[:, :SB_TK] + jnp.concatenate([carry] * (SB_TK // LANES), axis=1))
        if diag:
            a = jnp.where(mask, a, 0.0)
        carry_ref[h] = carry + r[:, SB_TK:]
        pvs.append(jnp.dot(a.astype(BF16), v[:, hs], preferred_element_type=F32))
    acc_ref[...] += jnp.concatenate(pvs, axis=1)


def _sb_kernel(qi_ref, ki_ref, q_ref, k_ref, v_ref, u_ref, o_ref, acc_ref, carry_ref):
    step = pl.program_id(1)
    qi = qi_ref[step]
    ki = ki_ref[step]

    @pl.when(ki == SB_R * qi + SB_R - 1)
    def _():
        acc_ref[...] = jnp.zeros_like(acc_ref)
        carry_ref[...] = jnp.zeros_like(carry_ref)

    @pl.when(ki >= SB_R * qi)
    def _():
        _sb_tile(q_ref, k_ref, v_ref, u_ref, acc_ref, carry_ref, ki * SB_TK - qi * SB_TQ)

    @pl.when(ki < SB_R * qi)
    def _():
        _sb_tile(q_ref, k_ref, v_ref, u_ref, acc_ref, carry_ref, None)

    @pl.when(ki == 0)
    def _():
        o_ref[...] = acc_ref[...].astype(o_ref.dtype)


def _stick_breaking(proj, batch, seq):
    n = proj.shape[0]
    nq, nk = seq // SB_TQ, seq // SB_TK
    qi_tab = np.array([qi for qi in range(nq) for _ in range(SB_R * (qi + 1))], np.int32)
    ki_tab = np.array([ki for qi in range(nq) for ki in range(SB_R * (qi + 1) - 1, -1, -1)], np.int32)
    j = np.arange(SB_TK)[:, None]
    c = np.arange(SB_TK + LANES)[None, :]
    u = jnp.asarray(np.where(c < SB_TK, j > c, True).astype(np.float32), BF16)
    qb, kb, vb = COL_SBQ // MIX_WIDTH, COL_SBK // MIX_WIDTH, COL_SBV // MIX_WIDTH
    grid_spec = pltpu.PrefetchScalarGridSpec(
        num_scalar_prefetch=2,
        grid=(batch, len(qi_tab)),
        in_specs=[
            pl.BlockSpec((SB_TQ, MIX_WIDTH), lambda b, s, qt, kt: (b * nq + qt[s], qb)),
            pl.BlockSpec((SB_TK, MIX_WIDTH), lambda b, s, qt, kt: (b * nk + kt[s], kb)),
            pl.BlockSpec((SB_TK, MIX_WIDTH), lambda b, s, qt, kt: (b * nk + kt[s], vb)),
            pl.BlockSpec((SB_TK, SB_TK + LANES), lambda b, s, qt, kt: (0, 0)),
        ],
        out_specs=pl.BlockSpec((SB_TQ, MIX_WIDTH), lambda b, s, qt, kt: (b * nq + qt[s], 0)),
        scratch_shapes=[pltpu.VMEM((SB_TQ, MIX_WIDTH), F32), pltpu.VMEM((N_HEADS, SB_TQ, LANES), F32)],
    )
    return pl.pallas_call(
        _sb_kernel,
        grid_spec=grid_spec,
        out_shape=jax.ShapeDtypeStruct((n, MIX_WIDTH), BF16),
        compiler_params=_cparams(("parallel", "arbitrary")),
        name="stick_breaking",
    )(jnp.asarray(qi_tab), jnp.asarray(ki_tab), proj, proj, proj, u)


def _compress_kernel(gk_ref, gv_ref, pos_ref, w1_ref, w2_ref, kc_ref, vc_ref):
    half = CMP_STRIDE * HEAD_DIM
    for t, (g_ref, o_ref) in enumerate(((gk_ref, kc_ref), (gv_ref, vc_ref))):
        g = g_ref[0].astype(F32)
        top = (g + pos_ref[t, :, :half]).astype(BF16)
        bot = (g + pos_ref[t, :, half:]).astype(BF16)
        a = jnp.dot(top, w1_ref[t, :half, :], preferred_element_type=F32)
        b = jnp.dot(bot, w1_ref[t, half:, :], preferred_element_type=F32)
        hid = jax.nn.gelu(a + pltpu.roll(b, b.shape[0] - 1, 0))
        out = jnp.dot(hid.astype(BF16), w2_ref[t], preferred_element_type=F32)
        o_ref[0] = jnp.concatenate([out, jnp.zeros_like(out)], axis=1).astype(o_ref.dtype)


def _compress(gk, gv, pos, w1, w2):
    b, m, width = gk.shape
    out_spec = pl.BlockSpec((1, m, LANES), lambda i: (i, 0, 0))
    out_shape = jax.ShapeDtypeStruct((b, m, LANES), BF16)
    return pl.pallas_call(
        _compress_kernel,
        grid=(b,),
        in_specs=[
            pl.BlockSpec((1, m, width), lambda i: (i, 0, 0)),
            pl.BlockSpec((1, m, width), lambda i: (i, 0, 0)),
            pl.BlockSpec((2, 1, CMP_LEN * HEAD_DIM), lambda i: (0, 0, 0)),
            pl.BlockSpec((2, CMP_LEN * HEAD_DIM, CMP_HIDDEN), lambda i: (0, 0, 0)),
            pl.BlockSpec((2, CMP_HIDDEN, HEAD_DIM), lambda i: (0, 0, 0)),
        ],
        out_specs=[out_spec, out_spec],
        out_shape=[out_shape, out_shape],
        compiler_params=_cparams(("parallel",)),
        name="nsa_compress",
    )(gk, gv, pos, w1, w2)


NSA_TQ = 256
NSA_CK = 512


NSA_ROWS = N_HEADS * NSA_TQ
SEL_OFF = 1e30


def _per_head(x, fn):
    return jnp.concatenate([fn(x[h * NSA_TQ:(h + 1) * NSA_TQ]) for h in range(N_HEADS)], axis=0)


def _nsa_kernel(q_ref, kx_ref, vx_ref, kw_ref, vw_ref, kb_ref, kc_ref, vc_ref, ovt_ref, o_ref, m_ref, acc_ref):
    seq = kx_ref.shape[0]
    n_sel = seq // SEL_LEN
    n_cmp = kc_ref.shape[1]
    q0 = pl.program_id(1) * NSA_TQ
    NT = (((1,), (1,)), ((), ()))
    lo_half = lax.broadcasted_iota(jnp.int32, (NSA_TQ, LANES), 1) < HEAD_DIM
    qpos_c = q0 + lax.broadcasted_iota(jnp.int32, (NSA_TQ, 1), 0)

    qs = []
    for pair in range(N_HEADS // 2):
        x = q_ref[:, pair * LANES:(pair + 1) * LANES].astype(F32)
        qs += [jnp.where(lo_half, x, 0.0), jnp.where(lo_half, pltpu.roll(x, HEAD_DIM, 1), 0.0)]
    q4 = jnp.concatenate(qs, axis=0).astype(BF16)

    ones_half = lax.broadcasted_iota(jnp.int32, (1, LANES), 1) >= HEAD_DIM

    def masked_exp(s, mask):
        s = _per_head(s, lambda t: jnp.where(mask, t, NEG))
        return jnp.exp2(s - jnp.max(s, axis=-1, keepdims=True))

    def attend(p, vals):
        return jnp.dot(p.astype(BF16), jnp.where(ones_half, 1.0, vals).astype(BF16), preferred_element_type=F32)

    def normalized(r):
        return r / jnp.where(ones_half, 1.0, pltpu.roll(r, HEAD_DIM, 1))

    cmp_end = lax.broadcasted_iota(jnp.int32, (NSA_TQ, n_cmp), 1) * CMP_STRIDE + (CMP_LEN - 1)
    p_cmp = masked_exp(lax.dot_general(q4, kc_ref[0], NT, preferred_element_type=F32), cmp_end <= qpos_c)
    has_cmp = jnp.concatenate([qpos_c >= CMP_LEN - 1] * N_HEADS, axis=0)
    p_cmp = p_cmp * jnp.where(has_cmp, 1.0 / jnp.sum(p_cmp, axis=-1, keepdims=True), 0.0)
    r_cmp = jnp.dot(p_cmp.astype(BF16), vc_ref[0], preferred_element_type=F32)
    p_sum = sum(p_cmp[h * NSA_TQ:(h + 1) * NSA_TQ] for h in range(N_HEADS))

    hi = p_sum.astype(BF16)
    lo = (p_sum - hi.astype(F32)).astype(BF16)
    imp = lax.dot_general(ovt_ref[...], jnp.concatenate([hi, lo], axis=1), NT, preferred_element_type=F32)
    blk = lax.broadcasted_iota(jnp.int32, (n_sel, NSA_TQ), 0)
    qpos_r = q0 + lax.broadcasted_iota(jnp.int32, (n_sel, NSA_TQ), 1)
    valid = blk * SEL_LEN <= qpos_r
    forced = (blk == 0) | (blk == jnp.right_shift(qpos_r, 6))
    score = jnp.where(valid, imp + jnp.where(forced, FORCE_SCORE, 0.0), -jnp.inf)
    groups = [score[8 * g:8 * g + 8] for g in range(n_sel // 8)]
    ranks = [jnp.zeros((8, NSA_TQ), F32) for _ in groups]
    row8 = lax.broadcasted_iota(jnp.int32, (8, NSA_TQ), 0)
    for i in range(n_sel):
        ci = jnp.broadcast_to(score[i:i + 1], (8, NSA_TQ))
        for g, sg in enumerate(groups):
            if 8 * g > i:
                beats = ci >= sg
            elif 8 * g + 7 < i:
                beats = ci > sg
            else:
                beats = (ci > sg) | ((ci == sg) & (row8 + 8 * g > i))
            ranks[g] = ranks[g] + jnp.where(beats, 1.0, 0.0)
    sel = jnp.where(valid & (jnp.concatenate(ranks, axis=0) < SEL_TOPN), 0.0, -SEL_OFF)
    sel = jnp.concatenate([sel, jnp.full((LANES - n_sel, NSA_TQ), -SEL_OFF, F32)], axis=0).T
    aug = lambda t: jnp.concatenate([q4, jnp.concatenate([t.astype(BF16)] * N_HEADS, axis=0)], axis=1)
    q_aug = aug(sel)
    blk_lane = lax.broadcasted_iota(jnp.int32, (NSA_TQ, LANES), 1)
    q_aug_past = aug(jnp.where(blk_lane * SEL_LEN >= q0, -SEL_OFF, sel))

    w_start = pl.multiple_of(jnp.maximum(q0 - WINDOW, 0), NSA_TQ)
    kwpos = w_start + lax.broadcasted_iota(jnp.int32, (NSA_TQ, WINDOW + NSA_TQ), 1)
    s_win = lax.dot_general(q4, kw_ref[pl.ds(w_start, WINDOW + NSA_TQ), :], NT, preferred_element_type=F32)
    p_win = masked_exp(s_win, (kwpos <= qpos_c) & (kwpos > qpos_c - WINDOW))
    r_win = normalized(attend(p_win, vw_ref[pl.ds(w_start, WINDOW + NSA_TQ), :]))

    def slab_stats(q, start, size, causal):
        keys = jnp.concatenate([kx_ref[pl.ds(start, size), :], kb_ref[pl.ds(start, size), :]], axis=1)
        s = lax.dot_general(q, keys, NT, preferred_element_type=F32)
        if causal:
            ok = start + lax.broadcasted_iota(jnp.int32, (NSA_TQ, size), 1) <= qpos_c
            s = _per_head(s, lambda t: jnp.where(ok, t, NEG))
        m_c = jnp.max(s, axis=-1, keepdims=True)
        return m_c, attend(jnp.exp2(s - m_c), vx_ref[pl.ds(start, size), :])

    m_d, a_d = slab_stats(q_aug, pl.multiple_of(q0, NSA_TQ), NSA_TQ, True)
    m_ref[...] = m_d
    acc_ref[...] = a_d

    def pair_body(i, carry):
        m0, a0 = slab_stats(q_aug_past, pl.multiple_of(2 * i * NSA_CK, NSA_CK), NSA_CK, False)
        m1, a1 = slab_stats(q_aug_past, pl.multiple_of((2 * i + 1) * NSA_CK, NSA_CK), NSA_CK, False)
        m_old = m_ref[...]
        m_new = jnp.maximum(m_old, jnp.maximum(m0, m1))
        acc_ref[...] = (jnp.exp2(m_old - m_new) * acc_ref[...] + jnp.exp2(m0 - m_new) * a0
                        + jnp.exp2(m1 - m_new) * a1)
        m_ref[...] = m_new
        return carry

    lax.fori_loop(0, (q0 + 2 * NSA_CK - 1) // (2 * NSA_CK), pair_body, 0)

    r_slc = normalized(acc_ref[...])
    gates = _sigmoid(vw_ref[pl.ds(pl.multiple_of(q0, NSA_TQ), NSA_TQ), :].astype(F32))
    g = [jnp.concatenate([gates[:, HEAD_DIM + 3 * h + t:HEAD_DIM + 3 * h + t + 1] for h in range(N_HEADS)], axis=0)
         for t in range(N_BRANCH)]
    y = g[0] * r_cmp + g[1] * r_slc + g[2] * r_win
    for pair in range(N_HEADS // 2):
        even = y[2 * pair * NSA_TQ:(2 * pair + 1) * NSA_TQ]
        odd = y[(2 * pair + 1) * NSA_TQ:(2 * pair + 2) * NSA_TQ]
        o_ref[:, pair * LANES:(pair + 1) * LANES] = jnp.where(lo_half, even, pltpu.roll(odd, HEAD_DIM, 1)).astype(o_ref.dtype)


def _nsa(proj, kc, vc, batch, seq):
    n = proj.shape[0]
    nqb = seq // NSA_TQ
    n_sel = seq // SEL_LEN
    n_cmp = kc.shape[1]
    assert n_sel % 8 == 0 and n_sel <= LANES and seq % (2 * NSA_CK) == 0
    cs = (np.arange(n_cmp) * CMP_STRIDE)[None, :]
    ss = (np.arange(n_sel) * SEL_LEN)[:, None]
    ovt = np.clip(np.minimum(cs + CMP_LEN, ss + SEL_LEN) - np.maximum(cs, ss), 0, None).astype(np.float32) / CMP_LEN
    ovt = jnp.asarray(np.concatenate([ovt, ovt], axis=1), BF16)
    kb = jnp.asarray((np.arange(seq)[:, None] // SEL_LEN == np.arange(LANES)[None, :]).astype(np.float32), BF16)
    seq_spec = lambda col: pl.BlockSpec((seq, LANES), lambda b, i: (b, col))
    cmp_spec = pl.BlockSpec((1, n_cmp, LANES), lambda b, i: (b, 0, 0))
    return pl.pallas_call(
        _nsa_kernel,
        grid=(batch, nqb),
        in_specs=[
            pl.BlockSpec((NSA_TQ, MIX_WIDTH), lambda b, i: (b * nqb + i, COL_CQ // MIX_WIDTH)),
            seq_spec(COL_KX // LANES),
            seq_spec(COL_VX // LANES),
            seq_spec(COL_KX // LANES + 1),
            seq_spec(COL_VX // LANES + 1),
            pl.BlockSpec((seq, LANES), lambda b, i: (0, 0)),
            cmp_spec, cmp_spec,
            pl.BlockSpec((n_sel, 2 * n_cmp), lambda b, i: (0, 0)),
        ],
        out_specs=pl.BlockSpec((NSA_TQ, MIX_WIDTH), lambda b, i: (b * nqb + i, 0)),
        out_shape=jax.ShapeDtypeStruct((n, MIX_WIDTH), BF16),
        scratch_shapes=[pltpu.VMEM((NSA_ROWS, 1), F32), pltpu.VMEM((NSA_ROWS, LANES), F32)],
        compiler_params=_cparams(("parallel", "arbitrary")),
        name="nsa",
    )(proj, proj, proj, proj, proj, kb, kc, vc, ovt)


def _merge_kernel(ya_ref, yb_ref, yc_ref, g0_ref, g1_ref, g2_ref, wb_ref, wo_ref, h_ref, o_ref):
    merged = None
    for y_ref, g_ref, t in ((ya_ref, g0_ref, 0), (yb_ref, g1_ref, 1), (yc_ref, g2_ref, 2)):
        term = _sigmoid(g_ref[...].astype(F32)) * jnp.dot(y_ref[...], wb_ref[t], preferred_element_type=F32)
        merged = term if merged is None else merged + term
    o_ref[...] = h_ref[...] + jnp.dot(merged.astype(BF16), wo_ref[...], preferred_element_type=F32)


def _merge(ya, yb, yc, proj, wb, wo, h, tm=512):
    n = h.shape[0]
    y_spec = pl.BlockSpec((tm, MIX_WIDTH), lambda i: (i, 0))
    g_spec = lambda t: pl.BlockSpec((tm, D_MODEL), lambda i: (i, COL_BRG // D_MODEL + t))
    return pl.pallas_call(
        _merge_kernel,
        grid=(n // tm,),
        in_specs=[y_spec, y_spec, y_spec, g_spec(0), g_spec(1), g_spec(2),
                  pl.BlockSpec((N_BRANCH, MIX_WIDTH, D_MODEL), lambda i: (0, 0, 0)),
                  pl.BlockSpec((D_MODEL, D_MODEL), lambda i: (0, 0)),
                  pl.BlockSpec((tm, D_MODEL), lambda i: (i, 0))],
        out_specs=pl.BlockSpec((tm, D_MODEL), lambda i: (i, 0)),
        out_shape=jax.ShapeDtypeStruct((n, D_MODEL), F32),
        compiler_params=_cparams(("parallel",)),
        name="merge",
    )(ya, yb, yc, proj, proj, proj, wb, wo, h)


def _ffn_kernel(h_ref, g_ref, wg_ref, wu_ref, wd_ref, o_ref, xn_ref, acc_ref):
    f = pl.program_id(1)

    @pl.when(f == 0)
    def _():
        xn_ref[...] = _rms(h_ref[...], g_ref[...]).astype(BF16)
        acc_ref[...] = h_ref[...]

    xn = xn_ref[...]
    a = jnp.dot(xn, wg_ref[...], preferred_element_type=F32)
    u = jnp.dot(xn, wu_ref[...], preferred_element_type=F32)
    hid = (a * _sigmoid(a) * u).astype(BF16)
    acc_ref[...] += jnp.dot(hid, wd_ref[...], preferred_element_type=F32)

    @pl.when(f == pl.num_programs(1) - 1)
    def _():
        o_ref[...] = acc_ref[...]


def _ffn(h, g, wg, wu, wd, tm=1024, tf=256):
    n = h.shape[0]
    dff = wg.shape[1]
    return pl.pallas_call(
        _ffn_kernel,
        grid=(n // tm, dff // tf),
        in_specs=[
            pl.BlockSpec((tm, D_MODEL), lambda i, f: (i, 0)),
            pl.BlockSpec((1, D_MODEL), lambda i, f: (0, 0)),
            pl.BlockSpec((D_MODEL, tf), lambda i, f: (0, f)),
            pl.BlockSpec((D_MODEL, tf), lambda i, f: (0, f)),
            pl.BlockSpec((tf, D_MODEL), lambda i, f: (f, 0)),
        ],
        out_specs=pl.BlockSpec((tm, D_MODEL), lambda i, f: (i, 0)),
        out_shape=jax.ShapeDtypeStruct((n, D_MODEL), F32),
        scratch_shapes=[pltpu.VMEM((tm, D_MODEL), BF16), pltpu.VMEM((tm, D_MODEL), F32)],
        compiler_params=_cparams(("parallel", "arbitrary")),
        name="ffn",
    )(h, g, wg, wu, wd)


def _route_kernel(h_ref, g_ref, rw_ref, xn_ref, gate_ref):
    tm = h_ref.shape[0]
    lane = lax.broadcasted_iota(jnp.int32, (tm, LANES), 1)
    xn = _rms(h_ref[...], g_ref[...])
    xn_ref[...] = xn.astype(BF16)
    logits = jnp.dot(xn, rw_ref[...], preferred_element_type=F32, precision=lax.Precision.HIGHEST)
    logits = jnp.where(lane < N_EXPERTS, logits, -jnp.inf)
    m1 = jnp.max(logits, axis=-1, keepdims=True)
    i1 = jnp.min(jnp.where(logits == m1, lane, LANES), axis=-1, keepdims=True)
    rest = jnp.where(lane == i1, -jnp.inf, logits)
    m2 = jnp.max(rest, axis=-1, keepdims=True)
    i2 = jnp.min(jnp.where(rest == m2, lane, LANES), axis=-1, keepdims=True)
    e2 = jnp.exp(m2 - m1)
    gate_ref[...] = jnp.where(lane == i1, 1.0 / (1.0 + e2), 0.0) + jnp.where(lane == i2, e2 / (1.0 + e2), 0.0)


def _route(h, g, rw, tm=1024):
    n = h.shape[0]
    return pl.pallas_call(
        _route_kernel,
        grid=(n // tm,),
        in_specs=[
            pl.BlockSpec((tm, D_MODEL), lambda i: (i, 0)),
            pl.BlockSpec((1, D_MODEL), lambda i: (0, 0)),
            pl.BlockSpec((D_MODEL, LANES), lambda i: (0, 0)),
        ],
        out_specs=[pl.BlockSpec((tm, D_MODEL), lambda i: (i, 0)), pl.BlockSpec((tm, LANES), lambda i: (i, 0))],
        out_shape=[jax.ShapeDtypeStruct((n, D_MODEL), BF16), jax.ShapeDtypeStruct((n, LANES), F32)],
        compiler_params=_cparams(("parallel",)),
        name="moe_route",
    )(h, g, rw)


MOE_TB = 2048
MOE_R = 256
MOE_RX = 256
MOE_TF = 512


def _moe_kernel(nsub_ref, nxsub_ref, xn_ref, h_ref, rrow_ref, rcol_ref, gate_ref, wg_ref, wu_ref, wd_ref, o_ref, xg_ref, y_ref):
    b, e, f = pl.program_id(0), pl.program_id(1), pl.program_id(2)
    ns = nsub_ref[b * N_EXPERTS + e]
    rows_of = lambda j: pl.ds(pl.multiple_of(j * MOE_R, MOE_R), MOE_R)
    base_of = lambda j: (j * MOE_R).astype(F32)

    @pl.when((e == 0) & (f == 0))
    def _():
        o_ref[...] = h_ref[...]

    @pl.when(f == 0)
    def _():
        rank_row = rrow_ref[0, pl.ds(e, 1), :]
        slot = lax.broadcasted_iota(jnp.int32, (MOE_R, MOE_TB), 0).astype(F32)

        def gather(j, carry):
            onehot = jnp.where(rank_row == slot + base_of(j), 1.0, 0.0).astype(BF16)
            xg_ref[rows_of(j), :] = jnp.dot(onehot, xn_ref[...], preferred_element_type=F32).astype(BF16)
            y_ref[rows_of(j), :] = jnp.zeros((MOE_R, D_MODEL), F32)
            return carry

        lax.fori_loop(0, ns, gather, 0)

    def expert(j, carry):
        rows = pl.ds(pl.multiple_of(j * MOE_RX, MOE_RX), MOE_RX)
        x = xg_ref[rows, :]
        a = jnp.dot(x, wg_ref[0], preferred_element_type=F32)
        u = jnp.dot(x, wu_ref[0], preferred_element_type=F32)
        y_ref[rows, :] += jnp.dot((a * _sigmoid(a) * u).astype(BF16), wd_ref[0], preferred_element_type=F32)
        return carry

    lax.fori_loop(0, nxsub_ref[b * N_EXPERTS + e], expert, 0)

    @pl.when(f == pl.num_programs(2) - 1)
    def _():
        lane = lax.broadcasted_iota(jnp.int32, (MOE_TB, LANES), 1)
        rank_col = jnp.sum(jnp.where(lane == e, rcol_ref[...], 0.0), axis=-1, keepdims=True)
        gate_col = jnp.sum(jnp.where(lane == e, gate_ref[...], 0.0), axis=-1, keepdims=True)
        slot = lax.broadcasted_iota(jnp.int32, (MOE_R, MOE_R), 1).astype(F32)

        def scatter(j, carry):
            y = y_ref[rows_of(j), :].astype(BF16)
            for c in range(MOE_TB // MOE_R):
                tok = slice(c * MOE_R, (c + 1) * MOE_R)
                weights = jnp.where(rank_col[tok] == slot + base_of(j), gate_col[tok], 0.0).astype(BF16)
                o_ref[tok, :] += jnp.dot(weights, y, preferred_element_type=F32)
            return carry

        lax.fori_loop(0, ns, scatter, 0)


def _moe_experts(nsub, nxsub, xn, h, rank_row, rank_col, gate, wg, wu, wd):
    n = h.shape[0]
    ne, _, dff = wg.shape
    blk = lambda width: pl.BlockSpec((MOE_TB, width), lambda b, e, f, *_: (b, 0))
    once = lambda width: pl.BlockSpec((MOE_TB, width), lambda b, e, f, *_: (b, 0), pipeline_mode=pl.Buffered(1))
    grid_spec = pltpu.PrefetchScalarGridSpec(
        num_scalar_prefetch=2,
        grid=(n // MOE_TB, ne, dff // MOE_TF),
        in_specs=[
            once(D_MODEL),
            once(D_MODEL),
            pl.BlockSpec((1, ne, MOE_TB), lambda b, e, f, *_: (b, 0, 0)),
            blk(LANES),
            blk(LANES),
            pl.BlockSpec((1, D_MODEL, MOE_TF), lambda b, e, f, *_: (e, 0, f)),
            pl.BlockSpec((1, D_MODEL, MOE_TF), lambda b, e, f, *_: (e, 0, f)),
            pl.BlockSpec((1, MOE_TF, D_MODEL), lambda b, e, f, *_: (e, f, 0)),
        ],
        out_specs=pl.BlockSpec((MOE_TB, D_MODEL), lambda b, e, f, *_: (b, 0)),
        scratch_shapes=[pltpu.VMEM((MOE_TB, D_MODEL), BF16), pltpu.VMEM((MOE_TB, D_MODEL), F32)],
    )
    return pl.pallas_call(
        _moe_kernel,
        grid_spec=grid_spec,
        out_shape=jax.ShapeDtypeStruct((n, D_MODEL), F32),
        compiler_params=_cparams(("parallel", "arbitrary", "arbitrary")),
        name="moe",
    )(nsub, nxsub, xn, h, rank_row, rank_col, gate, wg, wu, wd)


def _moe(h, g, rw, wg, wu, wd):
    n = h.shape[0]
    nb = n // MOE_TB
    xn, gate = _route(h, g, rw)
    routed = (gate[:, :N_EXPERTS] > 0).reshape(nb, MOE_TB, N_EXPERTS).astype(jnp.int32)
    incl = jnp.cumsum(routed, axis=1)
    rank = jnp.where(routed > 0, incl - 1, -1).astype(F32)
    count = incl[:, -1, :].reshape(-1)
    nsub = ((count + MOE_R - 1) // MOE_R).astype(jnp.int32)
    nxsub = ((count + MOE_RX - 1) // MOE_RX).astype(jnp.int32)
    rank_row = rank.transpose(0, 2, 1)
    rank_col = jnp.pad(rank.reshape(n, N_EXPERTS), ((0, 0), (0, LANES - N_EXPERTS)), constant_values=-1.0)
    return _moe_experts(nsub, nxsub, xn, h, rank_row, rank_col, gate, wg, wu, wd)


def _final_norm_kernel(h_ref, g_ref, o_ref):
    o_ref[...] = _rms(h_ref[...], g_ref[...])


def _final_norm(h, g, tm=1024):
    n = h.shape[0]
    return pl.pallas_call(
        _final_norm_kernel,
        grid=(n // tm,),
        in_specs=[pl.BlockSpec((tm, D_MODEL), lambda i: (i, 0)), pl.BlockSpec((1, D_MODEL), lambda i: (0, 0))],
        out_specs=pl.BlockSpec((tm, D_MODEL), lambda i: (i, 0)),
        out_shape=jax.ShapeDtypeStruct((n, D_MODEL), F32),
        compiler_params=_cparams(("parallel",)),
        name="final_norm",
    )(h, g)


def _rope_tables(seq):
    half = ROPE_DIMS // 2
    inv_freq = ROPE_THETA ** (-jnp.arange(0, ROPE_DIMS, 2, dtype=F32) / ROPE_DIMS)
    ang = jnp.arange(seq, dtype=F32)[:, None] * inv_freq[None, :]
    cos, sin = jnp.cos(ang), jnp.sin(ang)
    ones = jnp.ones((seq, HEAD_DIM - ROPE_DIMS), F32)
    zeros_h = jnp.zeros((seq, half), F32)
    zeros_r = jnp.zeros((seq, HEAD_DIM - ROPE_DIMS), F32)
    c = jnp.concatenate([cos, cos, ones], axis=1)
    s1 = jnp.concatenate([-sin, zeros_h, zeros_r], axis=1)
    s2 = jnp.concatenate([zeros_h, sin, zeros_r], axis=1)
    rep = LANES // HEAD_DIM
    return tuple(jnp.tile(t, (1, rep)) for t in (c, s1, s2))


def _layout_w_in(w):
    scale = HEAD_DIM ** -0.5 * LOG2E
    a_in, sb_q, sb_k, sb_v, c_q, c_kv, c_g, br_g = jnp.split(
        w, np.cumsum([2 * MIX_WIDTH, MIX_WIDTH, MIX_WIDTH, MIX_WIDTH, MIX_WIDTH, 6 * HEAD_DIM, 3 * N_HEADS])[:].tolist(),
        axis=-1)
    k_cmp, v_cmp, k_slc, v_slc, k_win, v_win = jnp.split(c_kv, 6, axis=-1)
    zeros = lambda width: jnp.zeros((w.shape[0], width), w.dtype)
    cols = [a_in, sb_q * scale, sb_k, sb_v,
            v_slc, v_cmp, v_win, c_g, zeros(HEAD_DIM - 3 * N_HEADS),
            c_q * scale, k_slc, k_cmp, k_win, zeros(HEAD_DIM),
            br_g]
    out = jnp.concatenate(cols, axis=-1)
    assert out.shape[-1] == PROJ_W
    return out.astype(BF16)


def kernel(x, norm1_g, w_in, sgu_norm_g, sgu_w, sgu_b, cmp_pos, cmp_w1, cmp_w2, w_branch, w_out, norm2_g,
           ffn_w_gate, ffn_w_up, ffn_w_down, router_w, moe_w_gate, moe_w_up, moe_w_down, final_norm_g):
    batch, seq, _ = x.shape
    n = batch * seq
    depth = norm1_g.shape[0]
    rope_c, rope_s1, rope_s2 = _rope_tables(seq)
    dff_pad = -(-D_FF // 256) * 256
    h = x.reshape(n, D_MODEL)
    for layer in range(depth):
        proj = _norm_proj(h, norm1_g[layer][None, :], _layout_w_in(w_in[layer]), rope_c, rope_s1, rope_s2, seq)
        sgu_bias = jnp.repeat(sgu_b[layer].T, HEAD_DIM, axis=1)
        y_a = _sgu(proj, sgu_norm_g[layer][None, :], sgu_w[layer], sgu_bias)
        y_b = _stick_breaking(proj, batch, seq)
        groups = seq // CMP_STRIDE
        gk = proj[:, COL_KX + HEAD_DIM:COL_KX + 2 * HEAD_DIM].reshape(batch, groups, CMP_STRIDE * HEAD_DIM)
        gv = proj[:, COL_VX + HEAD_DIM:COL_VX + 2 * HEAD_DIM].reshape(batch, groups, CMP_STRIDE * HEAD_DIM)
        kc, vc = _compress(gk, gv, cmp_pos[layer].reshape(2, 1, CMP_LEN * HEAD_DIM),
                           cmp_w1[layer].astype(BF16), cmp_w2[layer].astype(BF16))
        y_c = _nsa(proj, kc, vc, batch, seq)
        h = _merge(y_a, y_b, y_c, proj, w_branch[layer].astype(BF16), w_out[layer].astype(BF16), h)
        j = layer // 2
        g2 = norm2_g[layer][None, :]
        if layer % 2 == 0:
            pad = dff_pad - D_FF
            wg = jnp.pad(ffn_w_gate[j], ((0, 0), (0, pad))).astype(BF16)
            wu = jnp.pad(ffn_w_up[j], ((0, 0), (0, pad))).astype(BF16)
            wd = jnp.pad(ffn_w_down[j], ((0, pad), (0, 0))).astype(BF16)
            h = _ffn(h, g2, wg, wu, wd)
        else:
            rw = jnp.pad(router_w[j], ((0, 0), (0, LANES - N_EXPERTS)))
            h = _moe(h, g2, rw, moe_w_gate[j].astype(BF16), moe_w_up[j].astype(BF16), moe_w_down[j].astype(BF16))
    return _final_norm(h, final_norm_g[None, :]).reshape(batch, seq, D_MODEL)
```

```python
import functools

import numpy as np
import jax
import jax.numpy as jnp
from jax import lax
from jax.experimental import pallas as pl
from jax.experimental.pallas import tpu as pltpu

F32 = jnp.float32
BF16 = jnp.bfloat16

D_MODEL = 1024
HEAD_DIM = 64
N_HEADS = 4
MIX_WIDTH = N_HEADS * HEAD_DIM
ROPE_DIMS = HEAD_DIM // 4
ROPE_THETA = 500000.0
EPS = 1e-6
SGU_CHUNK = 128
CMP_LEN = 32
CMP_STRIDE = 16
CMP_HIDDEN = 256
SEL_LEN = 64
SEL_TOPN = 16
WINDOW = 512
FORCE_SCORE = 1e4
N_BRANCH = 3
D_FF = 2752
N_EXPERTS = 8
D_FF_EXPERT = 3584
NEG = -1e30
LOG2E = 1.4426950408889634

LANES = 128
VMEM_LIMIT = 56 * 1024 * 1024

PROJ_TILE = 1024
COL_A = 0
COL_SBQ, COL_SBK, COL_SBV = 512, 768, 1024
COL_VX = 1280
COL_CQ = 1536
COL_KX = 1792
COL_BRG = 2048
PROJ_W = COL_BRG + N_BRANCH * D_MODEL
ROPE_TILE = COL_CQ // PROJ_TILE
ROPE_START = COL_CQ % PROJ_TILE
assert COL_BRG == (ROPE_TILE + 1) * PROJ_TILE


def _cparams(sem):
    return pltpu.CompilerParams(dimension_semantics=sem, vmem_limit_bytes=VMEM_LIMIT)


def _sigmoid(x):
    return 1.0 / (1.0 + jnp.exp(-x))


def _rms(x, g):
    return x * lax.rsqrt(jnp.mean(x * x, axis=-1, keepdims=True) + EPS) * g


def _norm_proj_kernel(h_ref, g_ref, w_ref, c_ref, s1_ref, s2_ref, o_ref, xn_ref):
    j = pl.program_id(1)

    @pl.when(j == 0)
    def _():
        xn_ref[...] = _rms(h_ref[...], g_ref[...]).astype(BF16)

    acc = jnp.dot(xn_ref[...], w_ref[...], preferred_element_type=F32)

    @pl.when(j != ROPE_TILE)
    def _():
        o_ref[...] = acc.astype(o_ref.dtype)

    @pl.when(j == ROPE_TILE)
    def _():
        c, s1, s2 = c_ref[...], s1_ref[...], s2_ref[...]
        o_ref[:, :ROPE_START] = acc[:, :ROPE_START].astype(o_ref.dtype)
        for g in range(ROPE_START // LANES, PROJ_TILE // LANES):
            x = acc[:, g * LANES:(g + 1) * LANES]
            y = x * c + pltpu.roll(x, LANES - 8, 1) * s1 + pltpu.roll(x, 8, 1) * s2
            o_ref[:, g * LANES:(g + 1) * LANES] = y.astype(o_ref.dtype)


def _norm_proj(h, g, w, rope_c, rope_s1, rope_s2, seq, tm=1024):
    n = h.shape[0]
    nseq = seq // tm
    return pl.pallas_call(
        _norm_proj_kernel,
        grid=(n // tm, PROJ_W // PROJ_TILE),
        in_specs=[
            pl.BlockSpec((tm, D_MODEL), lambda i, j: (i, 0)),
            pl.BlockSpec((1, D_MODEL), lambda i, j: (0, 0)),
            pl.BlockSpec((D_MODEL, PROJ_TILE), lambda i, j: (0, j)),
            pl.BlockSpec((tm, LANES), lambda i, j: (i % nseq, 0)),
            pl.BlockSpec((tm, LANES), lambda i, j: (i % nseq, 0)),
            pl.BlockSpec((tm, LANES), lambda i, j: (i % nseq, 0)),
        ],
        out_specs=pl.BlockSpec((tm, PROJ_TILE), lambda i, j: (i, j)),
        out_shape=jax.ShapeDtypeStruct((n, PROJ_W), BF16),
        scratch_shapes=[pltpu.VMEM((tm, D_MODEL), BF16)],
        compiler_params=_cparams(("parallel", "arbitrary")),
        name="norm_proj",
    )(h, g, w, rope_c, rope_s1, rope_s2)


def _sgu_kernel(z_ref, g_ref, w_ref, b_ref, o_ref):
    tm = z_ref.shape[0]
    a = jax.nn.gelu(z_ref[...].astype(F32))
    u = a[:, :MIX_WIDTH]
    v = _rms(a[:, MIX_WIDTH:], g_ref[...]).astype(BF16)
    row = lax.broadcasted_iota(jnp.int32, (SGU_CHUNK, SGU_CHUNK), 0)
    col = lax.broadcasted_iota(jnp.int32, (SGU_CHUNK, SGU_CHUNK), 1)
    ws = [jnp.where(row >= col, w_ref[gi], 0.0).astype(BF16) for gi in range(N_HEADS)]
    bias = b_ref[...]
    for c in range(tm // SGU_CHUNK):
        rows = slice(c * SGU_CHUNK, (c + 1) * SGU_CHUNK)
        mix = jnp.concatenate(
            [jnp.dot(ws[gi], v[rows, gi * HEAD_DIM:(gi + 1) * HEAD_DIM], preferred_element_type=F32)
             for gi in range(N_HEADS)], axis=1)
        o_ref[rows, :] = (u[rows, :] * (mix + bias)).astype(o_ref.dtype)


def _sgu(proj, g, w, bias, tm=512):
    n = proj.shape[0]
    return pl.pallas_call(
        _sgu_kernel,
        grid=(n // tm,),
        in_specs=[
            pl.BlockSpec((tm, 2 * MIX_WIDTH), lambda i: (i, COL_A // (2 * MIX_WIDTH))),
            pl.BlockSpec((1, MIX_WIDTH), lambda i: (0, 0)),
            pl.BlockSpec((N_HEADS, SGU_CHUNK, SGU_CHUNK), lambda i: (0, 0, 0)),
            pl.BlockSpec((SGU_CHUNK, MIX_WIDTH), lambda i: (0, 0)),
        ],
        out_specs=pl.BlockSpec((tm, MIX_WIDTH), lambda i: (i, 0)),
        out_shape=jax.ShapeDtypeStruct((n, MIX_WIDTH), BF16),
        compiler_params=_cparams(("parallel",)),
        name="sgu",
    )(proj, g, w, bias)


SB_TQ = 1024
SB_TK = 256
SB_R = SB_TQ // SB_TK


def _sb_tile(q_ref, k_ref, v_ref, u_ref, acc_ref, carry_ref, k_off):
    q, k, v = q_ref[...], k_ref[...], v_ref[...]
    u = u_ref[...]
    diag = k_off is not None
    if diag:
        mask = (k_off + lax.broadcasted_iota(jnp.int32, (SB_TQ, SB_TK), 1)
                < lax.broadcasted_iota(jnp.int32, (SB_TQ, SB_TK), 0))
    pvs = []
    for h in range(N_HEADS):
        hs = slice(h * HEAD_DIM, (h + 1) * HEAD_DIM)
        z = lax.dot_general(q[:, hs], k[:, hs], (((1,), (1,)), ((), ())), preferred_element_type=F32)
        log_beta = jnp.minimum(z, 0.0) - jnp.log(1.0 + jnp.exp2(-jnp.abs(z))) * LOG2E
        log_1m = log_beta - z
        if diag:
            log_1m = jnp.where(mask, log_1m, 0.0)
        carry = carry_ref[h]
        r = jnp.dot(log_1m.astype(BF16), u, preferred_element_type=F32)
        a = jnp.exp2(log_beta + r[:, :SB_TK] + jnp.concatenate([carry] * (SB_TK // LANES), axis=1))
        if diag:
            a = jnp.where(mask, a, 0.0)
        carry_ref[h] = carry + r[:, SB_TK:]
        pvs.append(jnp.dot(a.astype(BF16), v[:, hs], preferred_element_type=F32))
    acc_ref[...] += jnp.concatenate(pvs, axis=1)


def _sb_kernel(qi_ref, ki_ref, q_ref, k_ref, v_ref, u_ref, o_ref, acc_ref, carry_ref):
    step = pl.program_id(1)
    qi = qi_ref[step]
    ki = ki_ref[step]

    @pl.when(ki == SB_R * qi + SB_R - 1)
    def _():
        acc_ref[...] = jnp.zeros_like(acc_ref)
        carry_ref[...] = jnp.zeros_like(carry_ref)

    @pl.when(ki >= SB_R * qi)
    def _():
        _sb_tile(q_ref, k_ref, v_ref, u_ref, acc_ref, carry_ref, ki * SB_TK - qi * SB_TQ)

    @pl.when(ki < SB_R * qi)
    def _():
        _sb_tile(q_ref, k_ref, v_ref, u_ref, acc_ref, carry_ref, None)

    @pl.when(ki == 0)
    def _():
        o_ref[...] = acc_ref[...].astype(o_ref.dtype)


def _stick_breaking(proj, batch, seq):
    n = proj.shape[0]
    nq, nk = seq // SB_TQ, seq // SB_TK
    qi_tab = np.array([qi for qi in range(nq) for _ in range(SB_R * (qi + 1))], np.int32)
    ki_tab = np.array([ki for qi in range(nq) for ki in range(SB_R * (qi + 1) - 1, -1, -1)], np.int32)
    j = np.arange(SB_TK)[:, None]
    c = np.arange(SB_TK + LANES)[None, :]
    u = jnp.asarray(np.where(c < SB_TK, j > c, True).astype(np.float32), BF16)
    qb, kb, vb = COL_SBQ // MIX_WIDTH, COL_SBK // MIX_WIDTH, COL_SBV // MIX_WIDTH
    grid_spec = pltpu.PrefetchScalarGridSpec(
        num_scalar_prefetch=2,
        grid=(batch, len(qi_tab)),
        in_specs=[
            pl.BlockSpec((SB_TQ, MIX_WIDTH), lambda b, s, qt, kt: (b * nq + qt[s], qb)),
            pl.BlockSpec((SB_TK, MIX_WIDTH), lambda b, s, qt, kt: (b * nk + kt[s], kb)),
            pl.BlockSpec((SB_TK, MIX_WIDTH), lambda b, s, qt, kt: (b * nk + kt[s], vb)),
            pl.BlockSpec((SB_TK, SB_TK + LANES), lambda b, s, qt, kt: (0, 0)),
        ],
        out_specs=pl.BlockSpec((SB_TQ, MIX_WIDTH), lambda b, s, qt, kt: (b * nq + qt[s], 0)),
        scratch_shapes=[pltpu.VMEM((SB_TQ, MIX_WIDTH), F32), pltpu.VMEM((N_HEADS, SB_TQ, LANES), F32)],
    )
    return pl.pallas_call(
        _sb_kernel,
        grid_spec=grid_spec,
        out_shape=jax.ShapeDtypeStruct((n, MIX_WIDTH), BF16),
        compiler_params=_cparams(("parallel", "arbitrary")),
        name="stick_breaking",
    )(jnp.asarray(qi_tab), jnp.asarray(ki_tab), proj, proj, proj, u)


def _compress_kernel(gk_ref, gv_ref, pos_ref, w1_ref, w2_ref, kc_ref, vc_ref):
    half = CMP_STRIDE * HEAD_DIM
    for t, (g_ref, o_ref) in enumerate(((gk_ref, kc_ref), (gv_ref, vc_ref))):
        g = g_ref[0].astype(F32)
        top = (g + pos_ref[t, :, :half]).astype(BF16)
        bot = (g + pos_ref[t, :, half:]).astype(BF16)
        a = jnp.dot(top, w1_ref[t, :half, :], preferred_element_type=F32)
        b = jnp.dot(bot, w1_ref[t, half:, :], preferred_element_type=F32)
        hid = jax.nn.gelu(a + pltpu.roll(b, b.shape[0] - 1, 0))
        out = jnp.dot(hid.astype(BF16), w2_ref[t], preferred_element_type=F32)
        o_ref[0] = jnp.concatenate([out, jnp.zeros_like(out)], axis=1).astype(o_ref.dtype)


def _compress(gk, gv, pos, w1, w2):
    b, m, width = gk.shape
    out_spec = pl.BlockSpec((1, m, LANES), lambda i: (i, 0, 0))
    out_shape = jax.ShapeDtypeStruct((b, m, LANES), BF16)
    return pl.pallas_call(
        _compress_kernel,
        grid=(b,),
        in_specs=[
            pl.BlockSpec((1, m, width), lambda i: (i, 0, 0)),
            pl.BlockSpec((1, m, width), lambda i: (i, 0, 0)),
            pl.BlockSpec((2, 1, CMP_LEN * HEAD_DIM), lambda i: (0, 0, 0)),
            pl.BlockSpec((2, CMP_LEN * HEAD_DIM, CMP_HIDDEN), lambda i: (0, 0, 0)),
            pl.BlockSpec((2, CMP_HIDDEN, HEAD_DIM), lambda i: (0, 0, 0)),
        ],
        out_specs=[out_spec, out_spec],
        out_shape=[out_shape, out_shape],
        compiler_params=_cparams(("parallel",)),
        name="nsa_compress",
    )(gk, gv, pos, w1, w2)


NSA_TQ = 256
NSA_CK = 512


NSA_ROWS = N_HEADS * NSA_TQ
SEL_OFF = 1e30


def _per_head(x, fn):
    return jnp.concatenate([fn(x[h * NSA_TQ:(h + 1) * NSA_TQ]) for h in range(N_HEADS)], axis=0)


def _nsa_kernel(q_ref, kx_ref, vx_ref, kw_ref, vw_ref, kb_ref, kc_ref, vc_ref, ovt_ref, o_ref, m_ref, acc_ref):
    seq = kx_ref.shape[0]
    n_sel = seq // SEL_LEN
    n_cmp = kc_ref.shape[1]
    q0 = pl.program_id(1) * NSA_TQ
    NT = (((1,), (1,)), ((), ()))
    lo_half = lax.broadcasted_iota(jnp.int32, (NSA_TQ, LANES), 1) < HEAD_DIM
    qpos_c = q0 + lax.broadcasted_iota(jnp.int32, (NSA_TQ, 1), 0)

    qs = []
    for pair in range(N_HEADS // 2):
        x = q_ref[:, pair * LANES:(pair + 1) * LANES].astype(F32)
        qs += [jnp.where(lo_half, x, 0.0), jnp.where(lo_half, pltpu.roll(x, HEAD_DIM, 1), 0.0)]
    q4 = jnp.concatenate(qs, axis=0).astype(BF16)

    ones_half = lax.broadcasted_iota(jnp.int32, (1, LANES), 1) >= HEAD_DIM

    def masked_exp(s, mask):
        s = _per_head(s, lambda t: jnp.where(mask, t, NEG))
        return jnp.exp2(s - jnp.max(s, axis=-1, keepdims=True))

    def attend(p, vals):
        return jnp.dot(p.astype(BF16), jnp.where(ones_half, 1.0, vals).astype(BF16), preferred_element_type=F32)

    def normalized(r):
        return r / jnp.where(ones_half, 1.0, pltpu.roll(r, HEAD_DIM, 1))

    cmp_end = lax.broadcasted_iota(jnp.int32, (NSA_TQ, n_cmp), 1) * CMP_STRIDE + (CMP_LEN - 1)
    p_cmp = masked_exp(lax.dot_general(q4, kc_ref[0], NT, preferred_element_type=F32), cmp_end <= qpos_c)
    has_cmp = jnp.concatenate([qpos_c >= CMP_LEN - 1] * N_HEADS, axis=0)
    p_cmp = p_cmp * jnp.where(has_cmp, 1.0 / jnp.sum(p_cmp, axis=-1, keepdims=True), 0.0)
    r_cmp = jnp.dot(p_cmp.astype(BF16), vc_ref[0], preferred_element_type=F32)
    p_sum = sum(p_cmp[h * NSA_TQ:(h + 1) * NSA_TQ] for h in range(N_HEADS))

    hi = p_sum.astype(BF16)
    lo = (p_sum - hi.astype(F32)).astype(BF16)
    imp = lax.dot_general(ovt_ref[...], jnp.concatenate([hi, lo], axis=1), NT, preferred_element_type=F32)
    blk = lax.broadcasted_iota(jnp.int32, (n_sel, NSA_TQ), 0)
    qpos_r = q0 + lax.broadcasted_iota(jnp.int32, (n_sel, NSA_TQ), 1)
    valid = blk * SEL_LEN <= qpos_r
    forced = (blk == 0) | (blk == jnp.right_shift(qpos_r, 6))
    score = jnp.where(valid, imp + jnp.where(forced, FORCE_SCORE, 0.0), -jnp.inf)
    groups = [score[8 * g:8 * g + 8] for g in range(n_sel // 8)]
    ranks = [jnp.zeros((8, NSA_TQ), F32) for _ in groups]
    row8 = lax.broadcasted_iota(jnp.int32, (8, NSA_TQ), 0)
    for i in range(n_sel):
        ci = jnp.broadcast_to(score[i:i + 1], (8, NSA_TQ))
        for g, sg in enumerate(groups):
            if 8 * g > i:
                beats = ci >= sg
            elif 8 * g + 7 < i:
                beats = ci > sg
            else:
                beats = (ci > sg) | ((ci == sg) & (row8 + 8 * g > i))
            ranks[g] = ranks[g] + jnp.where(beats, 1.0, 0.0)
    sel = jnp.where(valid & (jnp.concatenate(ranks, axis=0) < SEL_TOPN), 0.0, -SEL_OFF)
    sel = jnp.concatenate([sel, jnp.full((LANES - n_sel, NSA_TQ), -SEL_OFF, F32)], axis=0).T
    aug = lambda t: jnp.concatenate([q4, jnp.concatenate([t.astype(BF16)] * N_HEADS, axis=0)], axis=1)
    q_aug = aug(sel)
    blk_lane = lax.broadcasted_iota(jnp.int32, (NSA_TQ, LANES), 1)
    q_aug_past = aug(jnp.where(blk_lane * SEL_LEN >= q0, -SEL_OFF, sel))

    w_start = pl.multiple_of(jnp.maximum(q0 - WINDOW, 0), NSA_TQ)
    kwpos = w_start + lax.broadcasted_iota(jnp.int32, (NSA_TQ, WINDOW + NSA_TQ), 1)
    s_win = lax.dot_general(q4, kw_ref[pl.ds(w_start, WINDOW + NSA_TQ), :], NT, preferred_element_type=F32)
    p_win = masked_exp(s_win, (kwpos <= qpos_c) & (kwpos > qpos_c - WINDOW))
    r_win = normalized(attend(p_win, vw_ref[pl.ds(w_start, WINDOW + NSA_TQ), :]))

    def slab_stats(q, start, size, causal):
        keys = jnp.concatenate([kx_ref[pl.ds(start, size), :], kb_ref[pl.ds(start, size), :]], axis=1)
        s = lax.dot_general(q, keys, NT, preferred_element_type=F32)
        if causal:
            ok = start + lax.broadcasted_iota(jnp.int32, (NSA_TQ, size), 1) <= qpos_c
            s = _per_head(s, lambda t: jnp.where(ok, t, NEG))
        m_c = jnp.max(s, axis=-1, keepdims=True)
        return m_c, attend(jnp.exp2(s - m_c), vx_ref[pl.ds(start, size), :])

    m_d, a_d = slab_stats(q_aug, pl.multiple_of(q0, NSA_TQ), NSA_TQ, True)
    m_ref[...] = m_d
    acc_ref[...] = a_d

    def pair_body(i, carry):
        m0, a0 = slab_stats(q_aug_past, pl.multiple_of(2 * i * NSA_CK, NSA_CK), NSA_CK, False)
        m1, a1 = slab_stats(q_aug_past, pl.multiple_of((2 * i + 1) * NSA_CK, NSA_CK), NSA_CK, False)
        m_old = m_ref[...]
        m_new = jnp.maximum(m_old, jnp.maximum(m0, m1))
        acc_ref[...] = (jnp.exp2(m_old - m_new) * acc_ref[...] + jnp.exp2(m0 - m_new) * a0
                        + jnp.exp2(m1 - m_new) * a1)
        m_ref[...] = m_new
        return carry

    lax.fori_loop(0, (q0 + 2 * NSA_CK - 1) // (2 * NSA_CK), pair_body, 0)

    r_slc = normalized(acc_ref[...])
    gates = _sigmoid(vw_ref[pl.ds(pl.multiple_of(q0, NSA_TQ), NSA_TQ), :].astype(F32))
    g = [jnp.concatenate([gates[:, HEAD_DIM + 3 * h + t:HEAD_DIM + 3 * h + t + 1] for h in range(N_HEADS)], axis=0)
         for t in range(N_BRANCH)]
    y = g[0] * r_cmp + g[1] * r_slc + g[2] * r_win
    for pair in range(N_HEADS // 2):
        even = y[2 * pair * NSA_TQ:(2 * pair + 1) * NSA_TQ]
        odd = y[(2 * pair + 1) * NSA_TQ:(2 * pair + 2) * NSA_TQ]
        o_ref[:, pair * LANES:(pair + 1) * LANES] = jnp.where(lo_half, even, pltpu.roll(odd, HEAD_DIM, 1)).astype(o_ref.dtype)


def _nsa(proj, kc, vc, batch, seq):
    n = proj.shape[0]
    nqb = seq // NSA_TQ
    n_sel = seq // SEL_LEN
    n_cmp = kc.shape[1]
    assert n_sel % 8 == 0 and n_sel <= LANES and seq % (2 * NSA_CK) == 0
    cs = (np.arange(n_cmp) * CMP_STRIDE)[None, :]
    ss = (np.arange(n_sel) * SEL_LEN)[:, None]
    ovt = np.clip(np.minimum(cs + CMP_LEN, ss + SEL_LEN) - np.maximum(cs, ss), 0, None).astype(np.float32) / CMP_LEN
    ovt = jnp.asarray(np.concatenate([ovt, ovt], axis=1), BF16)
    kb = jnp.asarray((np.arange(seq)[:, None] // SEL_LEN == np.arange(LANES)[None, :]).astype(np.float32), BF16)
    seq_spec = lambda col: pl.BlockSpec((seq, LANES), lambda b, i: (b, col))
    cmp_spec = pl.BlockSpec((1, n_cmp, LANES), lambda b, i: (b, 0, 0))
    return pl.pallas_call(
        _nsa_kernel,
        grid=(batch, nqb),
        in_specs=[
            pl.BlockSpec((NSA_TQ, MIX_WIDTH), lambda b, i: (b * nqb + i, COL_CQ // MIX_WIDTH)),
            seq_spec(COL_KX // LANES),
            seq_spec(COL_VX // LANES),
            seq_spec(COL_KX // LANES + 1),
            seq_spec(COL_VX // LANES + 1),
            pl.BlockSpec((seq, LANES), lambda b, i: (0, 0)),
            cmp_spec, cmp_spec,
            pl.BlockSpec((n_sel, 2 * n_cmp), lambda b, i: (0, 0)),
        ],
        out_specs=pl.BlockSpec((NSA_TQ, MIX_WIDTH), lambda b, i: (b * nqb + i, 0)),
        out_shape=jax.ShapeDtypeStruct((n, MIX_WIDTH), BF16),
        scratch_shapes=[pltpu.VMEM((NSA_ROWS, 1), F32), pltpu.VMEM((NSA_ROWS, LANES), F32)],
        compiler_params=_cparams(("parallel", "arbitrary")),
        name="nsa",
    )(proj, proj, proj, proj, proj, kb, kc, vc, ovt)


def _merge_kernel(ya_ref, yb_ref, yc_ref, g0_ref, g1_ref, g2_ref, wb_ref, wo_ref, h_ref, o_ref):
    merged = None
    for y_ref, g_ref, t in ((ya_ref, g0_ref, 0), (yb_ref, g1_ref, 1), (yc_ref, g2_ref, 2)):
        term = _sigmoid(g_ref[...].astype(F32)) * jnp.dot(y_ref[...], wb_ref[t], preferred_element_type=F32)
        merged = term if merged is None else merged + term
    o_ref[...] = h_ref[...] + jnp.dot(merged.astype(BF16), wo_ref[...], preferred_element_type=F32)


def _merge(ya, yb, yc, proj, wb, wo, h, tm=512):
    n = h.shape[0]
    y_spec = pl.BlockSpec((tm, MIX_WIDTH), lambda i: (i, 0))
    g_spec = lambda t: pl.BlockSpec((tm, D_MODEL), lambda i: (i, COL_BRG // D_MODEL + t))
    return pl.pallas_call(
        _merge_kernel,
        grid=(n // tm,),
        in_specs=[y_spec, y_spec, y_spec, g_spec(0), g_spec(1), g_spec(2),
                  pl.BlockSpec((N_BRANCH, MIX_WIDTH, D_MODEL), lambda i: (0, 0, 0)),
                  pl.BlockSpec((D_MODEL, D_MODEL), lambda i: (0, 0)),
                  pl.BlockSpec((tm, D_MODEL), lambda i: (i, 0))],
        out_specs=pl.BlockSpec((tm, D_MODEL), lambda i: (i, 0)),
        out_shape=jax.ShapeDtypeStruct((n, D_MODEL), F32),
        compiler_params=_cparams(("parallel",)),
        name="merge",
    )(ya, yb, yc, proj, proj, proj, wb, wo, h)


def _ffn_kernel(h_ref, g_ref, wg_ref, wu_ref, wd_ref, o_ref, xn_ref, acc_ref):
    f = pl.program_id(1)

    @pl.when(f == 0)
    def _():
        xn_ref[...] = _rms(h_ref[...], g_ref[...]).astype(BF16)
        acc_ref[...] = h_ref[...]

    xn = xn_ref[...]
    a = jnp.dot(xn, wg_ref[...], preferred_element_type=F32)
    u = jnp.dot(xn, wu_ref[...], preferred_element_type=F32)
    hid = (a * _sigmoid(a) * u).astype(BF16)
    acc_ref[...] += jnp.dot(hid, wd_ref[...], preferred_element_type=F32)

    @pl.when(f == pl.num_programs(1) - 1)
    def _():
        o_ref[...] = acc_ref[...]


def _ffn(h, g, wg, wu, wd, tm=1024, tf=256):
    n = h.shape[0]
    dff = wg.shape[1]
    return pl.pallas_call(
        _ffn_kernel,
        grid=(n // tm, dff // tf),
        in_specs=[
            pl.BlockSpec((tm, D_MODEL), lambda i, f: (i, 0)),
            pl.BlockSpec((1, D_MODEL), lambda i, f: (0, 0)),
            pl.BlockSpec((D_MODEL, tf), lambda i, f: (0, f)),
            pl.BlockSpec((D_MODEL, tf), lambda i, f: (0, f)),
            pl.BlockSpec((tf, D_MODEL), lambda i, f: (f, 0)),
        ],
        out_specs=pl.BlockSpec((tm, D_MODEL), lambda i, f: (i, 0)),
        out_shape=jax.ShapeDtypeStruct((n, D_MODEL), F32),
        scratch_shapes=[pltpu.VMEM((tm, D_MODEL), BF16), pltpu.VMEM((tm, D_MODEL), F32)],
        compiler_params=_cparams(("parallel", "arbitrary")),
        name="ffn",
    )(h, g, wg, wu, wd)


RT_TM = 1024
RT_TF = 896
RT_TD = 512


def _gate_kernel(h_ref, g_ref, rw_ref, gate_ref):
    tm = h_ref.shape[0]
    lane = lax.broadcasted_iota(jnp.int32, (tm, LANES), 1)
    xn = _rms(h_ref[...], g_ref[...])
    logits = jnp.dot(xn, rw_ref[...], preferred_element_type=F32, precision=lax.Precision.HIGHEST)
    logits = jnp.where(lane < N_EXPERTS, logits, -jnp.inf)
    m1 = jnp.max(logits, axis=-1, keepdims=True)
    i1 = jnp.min(jnp.where(logits == m1, lane, LANES), axis=-1, keepdims=True)
    rest = jnp.where(lane == i1, -jnp.inf, logits)
    m2 = jnp.max(rest, axis=-1, keepdims=True)
    i2 = jnp.min(jnp.where(rest == m2, lane, LANES), axis=-1, keepdims=True)
    e2 = jnp.exp(m2 - m1)
    gate_ref[...] = jnp.where(lane == i1, 1.0 / (1.0 + e2), 0.0) + jnp.where(lane == i2, e2 / (1.0 + e2), 0.0)


def _gate(h, g, rw, tm=1024):
    n = h.shape[0]
    return pl.pallas_call(
        _gate_kernel,
        grid=(n // tm,),
        in_specs=[
            pl.BlockSpec((tm, D_MODEL), lambda i: (i, 0)),
            pl.BlockSpec((1, D_MODEL), lambda i: (0, 0)),
            pl.BlockSpec((D_MODEL, LANES), lambda i: (0, 0)),
        ],
        out_specs=pl.BlockSpec((tm, LANES), lambda i: (i, 0)),
        out_shape=jax.ShapeDtypeStruct((n, LANES), F32),
        compiler_params=_cparams(("parallel",)),
        name="moe_gate",
    )(h, g, rw)


def _row_copy(src_ref, src_row, dst_ref, dst_row, sem):
    return pltpu.make_async_copy(src_ref.at[pl.ds(src_row, 1), :], dst_ref.at[pl.ds(dst_row, 1), :], sem)


def _dispatch_kernel(pad_ref, pa_ref, pb_ref, h_ref, xs_ref, zero_ref, sems):
    i = pl.program_id(0)

    def send(t, carry):
        _row_copy(h_ref, t, xs_ref, pa_ref[0, 0, t], sems.at[0]).start()
        _row_copy(h_ref, t, xs_ref, pb_ref[0, 0, t], sems.at[1]).start()
        return carry

    lax.fori_loop(0, RT_TD, send, 0)

    @pl.when(i == 0)
    def _():
        zero_ref[...] = jnp.zeros_like(zero_ref)
        for e in range(N_EXPERTS + 1):
            lo, hi = pad_ref[2 * e], pad_ref[2 * e + 1]
            lax.fori_loop(lo, hi, lambda r, c: (_row_copy(zero_ref, 0, xs_ref, r, sems.at[2]).start(), c)[1], 0)
        for e in range(N_EXPERTS + 1):
            lo, hi = pad_ref[2 * e], pad_ref[2 * e + 1]
            lax.fori_loop(lo, hi, lambda r, c: (_row_copy(zero_ref, 0, xs_ref, r, sems.at[2]).wait(), c)[1], 0)

    for s in range(2):
        pltpu.make_async_copy(h_ref, xs_ref.at[pl.ds(0, RT_TD), :], sems.at[s]).wait()


def _dispatch(pad_bounds, pa, pb, h, n_rows):
    n = h.shape[0]
    idx_spec = pl.BlockSpec((1, 1, RT_TD), lambda i, *_: (i, 0, 0), memory_space=pltpu.SMEM)
    grid_spec = pltpu.PrefetchScalarGridSpec(
        num_scalar_prefetch=1,
        grid=(n // RT_TD,),
        in_specs=[idx_spec, idx_spec, pl.BlockSpec((RT_TD, D_MODEL), lambda i, *_: (i, 0))],
        out_specs=pl.BlockSpec(memory_space=pl.ANY),
        scratch_shapes=[pltpu.VMEM((8, D_MODEL), F32), pltpu.SemaphoreType.DMA((3,))],
    )
    return pl.pallas_call(
        _dispatch_kernel,
        grid_spec=grid_spec,
        out_shape=jax.ShapeDtypeStruct((n_rows, D_MODEL), F32),
        compiler_params=_cparams(("arbitrary",)),
        name="moe_dispatch",
    )(pad_bounds, pa.reshape(-1, 1, RT_TD), pb.reshape(-1, 1, RT_TD), h)


def _sorted_experts_kernel(te_ref, tx_ref, live_ref, xs_ref, g_ref, wg_ref, wu_ref, wd_ref, y_ref, xb_ref):
    i, f = pl.program_id(0), pl.program_id(1)

    @pl.when(f == 0)
    def _():
        xb_ref[...] = _rms(xs_ref[...], g_ref[...]).astype(BF16)
        y_ref[...] = jnp.zeros_like(y_ref)

    @pl.when(live_ref[i] > 0)
    def _():
        x = xb_ref[...]
        a = jnp.dot(x, wg_ref[0], preferred_element_type=F32)
        u = jnp.dot(x, wu_ref[0], preferred_element_type=F32)
        y_ref[...] += jnp.dot((a * _sigmoid(a) * u).astype(BF16), wd_ref[0], preferred_element_type=F32)


def _sorted_experts(tile_expert, tile_src, tile_live, xs, g, wg, wu, wd):
    n_rows = xs.shape[0]
    ne, _, dff = wg.shape
    grid_spec = pltpu.PrefetchScalarGridSpec(
        num_scalar_prefetch=3,
        grid=(n_rows // RT_TM, dff // RT_TF),
        in_specs=[
            pl.BlockSpec((RT_TM, D_MODEL), lambda i, f, te, tx, lv: (tx[i], 0)),
            pl.BlockSpec((1, D_MODEL), lambda i, f, *_: (0, 0)),
            pl.BlockSpec((1, D_MODEL, RT_TF), lambda i, f, te, tx, lv: (te[i], 0, f)),
            pl.BlockSpec((1, D_MODEL, RT_TF), lambda i, f, te, tx, lv: (te[i], 0, f)),
            pl.BlockSpec((1, RT_TF, D_MODEL), lambda i, f, te, tx, lv: (te[i], f, 0)),
        ],
        out_specs=pl.BlockSpec((RT_TM, D_MODEL), lambda i, f, *_: (i, 0)),
        scratch_shapes=[pltpu.VMEM((RT_TM, D_MODEL), BF16)],
    )
    return pl.pallas_call(
        _sorted_experts_kernel,
        grid_spec=grid_spec,
        out_shape=jax.ShapeDtypeStruct((n_rows, D_MODEL), F32),
        compiler_params=_cparams(("parallel", "arbitrary")),
        name="moe_experts",
    )(tile_expert, tile_src, tile_live, xs, g, wg, wu, wd)


def _combine_kernel(pa_ref, pb_ref, ga_ref, gb_ref, h_ref, y_ref, o_ref, ya_ref, yb_ref, sems):
    def fetch(t, carry):
        _row_copy(y_ref, pa_ref[0, 0, t], ya_ref, t, sems.at[0]).start()
        _row_copy(y_ref, pb_ref[0, 0, t], yb_ref, t, sems.at[1]).start()
        return carry

    lax.fori_loop(0, RT_TD, fetch, 0)
    pltpu.make_async_copy(y_ref.at[pl.ds(0, RT_TD), :], ya_ref, sems.at[0]).wait()
    pltpu.make_async_copy(y_ref.at[pl.ds(0, RT_TD), :], yb_ref, sems.at[1]).wait()
    o_ref[...] = h_ref[...] + ga_ref[...] * ya_ref[...] + gb_ref[...] * yb_ref[...]


def _combine(pa, pb, ga, gb, h, y):
    n = h.shape[0]
    idx_spec = pl.BlockSpec((1, 1, RT_TD), lambda i: (i, 0, 0), memory_space=pltpu.SMEM)
    col_spec = pl.BlockSpec((RT_TD, 1), lambda i: (i, 0))
    row_spec = pl.BlockSpec((RT_TD, D_MODEL), lambda i: (i, 0))
    return pl.pallas_call(
        _combine_kernel,
        grid=(n // RT_TD,),
        in_specs=[idx_spec, idx_spec, col_spec, col_spec, row_spec, pl.BlockSpec(memory_space=pl.ANY)],
        out_specs=row_spec,
        out_shape=jax.ShapeDtypeStruct((n, D_MODEL), F32),
        scratch_shapes=[pltpu.VMEM((RT_TD, D_MODEL), F32), pltpu.VMEM((RT_TD, D_MODEL), F32),
                        pltpu.SemaphoreType.DMA((2,))],
        compiler_params=_cparams(("arbitrary",)),
        name="moe_combine",
    )(pa.reshape(-1, 1, RT_TD), pb.reshape(-1, 1, RT_TD), ga, gb, h, y)


def _moe_sorted(h, g, rw, wg, wu, wd):
    n = h.shape[0]
    n_tiles = 2 * n // RT_TM + N_EXPERTS
    gate = _gate(h, g, rw)[:, :N_EXPERTS]
    routed = gate > 0
    r = routed.astype(jnp.int32)
    incl = jnp.cumsum(r, axis=0)
    count = incl[-1]
    padded = (count + RT_TM - 1) // RT_TM * RT_TM
    ends = jnp.cumsum(padded)
    pos = (ends - padded)[None, :] + incl - 1
    order = jnp.cumsum(r, axis=1)
    first, second = routed & (order == 1), routed & (order == 2)
    pick = lambda m, v: jnp.sum(jnp.where(m, v, 0), axis=1)
    pa = pick(first, pos)
    pb = jnp.where(jnp.any(second, axis=1), pick(second, pos), pa)
    ga, gb = pick(first, gate)[:, None], pick(second, gate)[:, None]
    pad_lo = jnp.concatenate([ends - padded + count, ends[-1:]])
    pad_hi = jnp.concatenate([ends, jnp.full((1,), n_tiles * RT_TM, ends.dtype)])
    pad_bounds = jnp.stack([pad_lo, pad_hi], axis=1).reshape(-1).astype(jnp.int32)
    tile_start = jnp.arange(n_tiles, dtype=jnp.int32) * RT_TM
    tile_live = (tile_start < ends[-1]).astype(jnp.int32)
    tile_expert = jnp.minimum(jnp.sum(tile_start[:, None] >= ends[None, :], axis=1), N_EXPERTS - 1).astype(jnp.int32)
    tile_src = jnp.minimum(jnp.arange(n_tiles, dtype=jnp.int32), ends[-1] // RT_TM - 1)
    xs = _dispatch(pad_bounds, pa.astype(jnp.int32), pb.astype(jnp.int32), h, n_tiles * RT_TM)
    y = _sorted_experts(tile_expert, tile_src, tile_live, xs, g, wg, wu, wd)
    return _combine(pa.astype(jnp.int32), pb.astype(jnp.int32), ga, gb, h, y)


def _final_norm_kernel(h_ref, g_ref, o_ref):
    o_ref[...] = _rms(h_ref[...], g_ref[...])


def _final_norm(h, g, tm=1024):
    n = h.shape[0]
    return pl.pallas_call(
        _final_norm_kernel,
        grid=(n // tm,),
        in_specs=[pl.BlockSpec((tm, D_MODEL), lambda i: (i, 0)), pl.BlockSpec((1, D_MODEL), lambda i: (0, 0))],
        out_specs=pl.BlockSpec((tm, D_MODEL), lambda i: (i, 0)),
        out_shape=jax.ShapeDtypeStruct((n, D_MODEL), F32),
        compiler_params=_cparams(("parallel",)),
        name="final_norm",
    )(h, g)


def _rope_tables(seq):
    half = ROPE_DIMS // 2
    inv_freq = ROPE_THETA ** (-jnp.arange(0, ROPE_DIMS, 2, dtype=F32) / ROPE_DIMS)
    ang = jnp.arange(seq, dtype=F32)[:, None] * inv_freq[None, :]
    cos, sin = jnp.cos(ang), jnp.sin(ang)
    ones = jnp.ones((seq, HEAD_DIM - ROPE_DIMS), F32)
    zeros_h = jnp.zeros((seq, half), F32)
    zeros_r = jnp.zeros((seq, HEAD_DIM - ROPE_DIMS), F32)
    c = jnp.concatenate([cos, cos, ones], axis=1)
    s1 = jnp.concatenate([-sin, zeros_h, zeros_r], axis=1)
    s2 = jnp.concatenate([zeros_h, sin, zeros_r], axis=1)
    rep = LANES // HEAD_DIM
    return tuple(jnp.tile(t, (1, rep)) for t in (c, s1, s2))


def _layout_w_in(w):
    scale = HEAD_DIM ** -0.5 * LOG2E
    a_in, sb_q, sb_k, sb_v, c_q, c_kv, c_g, br_g = jnp.split(
        w, np.cumsum([2 * MIX_WIDTH, MIX_WIDTH, MIX_WIDTH, MIX_WIDTH, MIX_WIDTH, 6 * HEAD_DIM, 3 * N_HEADS])[:].tolist(),
        axis=-1)
    k_cmp, v_cmp, k_slc, v_slc, k_win, v_win = jnp.split(c_kv, 6, axis=-1)
    zeros = lambda width: jnp.zeros((w.shape[0], width), w.dtype)
    cols = [a_in, sb_q * scale, sb_k, sb_v,
            v_slc, v_cmp, v_win, c_g, zeros(HEAD_DIM - 3 * N_HEADS),
            c_q * scale, k_slc, k_cmp, k_win, zeros(HEAD_DIM),
            br_g]
    out = jnp.concatenate(cols, axis=-1)
    assert out.shape[-1] == PROJ_W
    return out.astype(BF16)


def kernel(x, norm1_g, w_in, sgu_norm_g, sgu_w, sgu_b, cmp_pos, cmp_w1, cmp_w2, w_branch, w_out, norm2_g,
           ffn_w_gate, ffn_w_up, ffn_w_down, router_w, moe_w_gate, moe_w_up, moe_w_down, final_norm_g):
    batch, seq, _ = x.shape
    n = batch * seq
    depth = norm1_g.shape[0]
    rope_c, rope_s1, rope_s2 = _rope_tables(seq)
    dff_pad = -(-D_FF // 256) * 256
    h = x.reshape(n, D_MODEL)
    for layer in range(depth):
        proj = _norm_proj(h, norm1_g[layer][None, :], _layout_w_in(w_in[layer]), rope_c, rope_s1, rope_s2, seq)
        sgu_bias = jnp.repeat(sgu_b[layer].T, HEAD_DIM, axis=1)
        y_a = _sgu(proj, sgu_norm_g[layer][None, :], sgu_w[layer], sgu_bias)
        y_b = _stick_breaking(proj, batch, seq)
        groups = seq // CMP_STRIDE
        gk = proj[:, COL_KX + HEAD_DIM:COL_KX + 2 * HEAD_DIM].reshape(batch, groups, CMP_STRIDE * HEAD_DIM)
        gv = proj[:, COL_VX + HEAD_DIM:COL_VX + 2 * HEAD_DIM].reshape(batch, groups, CMP_STRIDE * HEAD_DIM)
        kc, vc = _compress(gk, gv, cmp_pos[layer].reshape(2, 1, CMP_LEN * HEAD_DIM),
                           cmp_w1[layer].astype(BF16), cmp_w2[layer].astype(BF16))
        y_c = _nsa(proj, kc, vc, batch, seq)
        h = _merge(y_a, y_b, y_c, proj, w_branch[layer].astype(BF16), w_out[layer].astype(BF16), h)
        j = layer // 2
        g2 = norm2_g[layer][None, :]
        if layer % 2 == 0:
            pad = dff_pad - D_FF
            wg = jnp.pad(ffn_w_gate[j], ((0, 0), (0, pad))).astype(BF16)
            wu = jnp.pad(ffn_w_up[j], ((0, 0), (0, pad))).astype(BF16)
            wd = jnp.pad(ffn_w_down[j], ((0, pad), (0, 0))).astype(BF16)
            h = _ffn(h, g2, wg, wu, wd)
        else:
            rw = jnp.pad(router_w[j], ((0, 0), (0, LANES - N_EXPERTS)))
            h = _moe_sorted(h, g2, rw, moe_w_gate[j].astype(BF16), moe_w_up[j].astype(BF16), moe_w_down[j].astype(BF16))
    return _final_norm(h, final_norm_g[None, :]).reshape(batch, seq, D_MODEL)
```

```python
import functools

import numpy as np
import jax
import jax.numpy as jnp
from jax import lax
from jax.experimental import pallas as pl
from jax.experimental.pallas import tpu as pltpu

F32 = jnp.float32
BF16 = jnp.bfloat16

D_MODEL = 1024
HEAD_DIM = 64
N_HEADS = 4
MIX_WIDTH = N_HEADS * HEAD_DIM
ROPE_DIMS = HEAD_DIM // 4
ROPE_THETA = 500000.0
EPS = 1e-6
SGU_CHUNK = 128
CMP_LEN = 32
CMP_STRIDE = 16
CMP_HIDDEN = 256
SEL_LEN = 64
SEL_TOPN = 16
WINDOW = 512
FORCE_SCORE = 1e4
N_BRANCH = 3
D_FF = 2752
N_EXPERTS = 8
D_FF_EXPERT = 3584
NEG = -1e30
LOG2E = 1.4426950408889634

LANES = 128
VMEM_LIMIT = 56 * 1024 * 1024

PROJ_TILE = 1024
COL_A = 0
COL_SBQ, COL_SBK, COL_SBV = 512, 768, 1024
COL_VX = 1280
COL_CQ = 1536
COL_KX = 1792
COL_BRG = 2048
PROJ_W = COL_BRG + N_BRANCH * D_MODEL
ROPE_TILE = COL_CQ // PROJ_TILE
ROPE_START = COL_CQ % PROJ_TILE
assert COL_BRG == (ROPE_TILE + 1) * PROJ_TILE


def _cparams(sem):
    return pltpu.CompilerParams(dimension_semantics=sem, vmem_limit_bytes=VMEM_LIMIT)


def _sigmoid(x):
    return 1.0 / (1.0 + jnp.exp(-x))


def _rms(x, g):
    return x * lax.rsqrt(jnp.mean(x * x, axis=-1, keepdims=True) + EPS) * g


def _norm_proj_kernel(h_ref, g_ref, w_ref, c_ref, s1_ref, s2_ref, o_ref, xn_ref):
    j = pl.program_id(1)

    @pl.when(j == 0)
    def _():
        xn_ref[...] = _rms(h_ref[...], g_ref[...]).astype(BF16)

    acc = jnp.dot(xn_ref[...], w_ref[...], preferred_element_type=F32)

    @pl.when(j != ROPE_TILE)
    def _():
        o_ref[...] = acc.astype(o_ref.dtype)

    @pl.when(j == ROPE_TILE)
    def _():
        c, s1, s2 = c_ref[...], s1_ref[...], s2_ref[...]
        o_ref[:, :ROPE_START] = acc[:, :ROPE_START].astype(o_ref.dtype)
        for g in range(ROPE_START // LANES, PROJ_TILE // LANES):
            x = acc[:, g * LANES:(g + 1) * LANES]
            y = x * c + pltpu.roll(x, LANES - 8, 1) * s1 + pltpu.roll(x, 8, 1) * s2
            o_ref[:, g * LANES:(g + 1) * LANES] = y.astype(o_ref.dtype)


def _norm_proj(h, g, w, rope_c, rope_s1, rope_s2, seq, tm=1024):
    n = h.shape[0]
    nseq = seq // tm
    return pl.pallas_call(
        _norm_proj_kernel,
        grid=(n // tm, PROJ_W // PROJ_TILE),
        in_specs=[
            pl.BlockSpec((tm, D_MODEL), lambda i, j: (i, 0)),
            pl.BlockSpec((1, D_MODEL), lambda i, j: (0, 0)),
            pl.BlockSpec((D_MODEL, PROJ_TILE), lambda i, j: (0, j)),
            pl.BlockSpec((tm, LANES), lambda i, j: (i % nseq, 0)),
            pl.BlockSpec((tm, LANES), lambda i, j: (i % nseq, 0)),
            pl.BlockSpec((tm, LANES), lambda i, j: (i % nseq, 0)),
        ],
        out_specs=pl.BlockSpec((tm, PROJ_TILE), lambda i, j: (i, j)),
        out_shape=jax.ShapeDtypeStruct((n, PROJ_W), BF16),
        scratch_shapes=[pltpu.VMEM((tm, D_MODEL), BF16)],
        compiler_params=_cparams(("parallel", "arbitrary")),
        name="norm_proj",
    )(h, g, w, rope_c, rope_s1, rope_s2)


def _sgu_kernel(z_ref, g_ref, w_ref, b_ref, o_ref):
    tm = z_ref.shape[0]
    a = jax.nn.gelu(z_ref[...].astype(F32))
    u = a[:, :MIX_WIDTH]
    v = _rms(a[:, MIX_WIDTH:], g_ref[...]).astype(BF16)
    row = lax.broadcasted_iota(jnp.int32, (SGU_CHUNK, SGU_CHUNK), 0)
    col = lax.broadcasted_iota(jnp.int32, (SGU_CHUNK, SGU_CHUNK), 1)
    ws = [jnp.where(row >= col, w_ref[gi], 0.0).astype(BF16) for gi in range(N_HEADS)]
    bias = b_ref[...]
    for c in range(tm // SGU_CHUNK):
        rows = slice(c * SGU_CHUNK, (c + 1) * SGU_CHUNK)
        mix = jnp.concatenate(
            [jnp.dot(ws[gi], v[rows, gi * HEAD_DIM:(gi + 1) * HEAD_DIM], preferred_element_type=F32)
             for gi in range(N_HEADS)], axis=1)
        o_ref[rows, :] = (u[rows, :] * (mix + bias)).astype(o_ref.dtype)


def _sgu(proj, g, w, bias, tm=512):
    n = proj.shape[0]
    return pl.pallas_call(
        _sgu_kernel,
        grid=(n // tm,),
        in_specs=[
            pl.BlockSpec((tm, 2 * MIX_WIDTH), lambda i: (i, COL_A // (2 * MIX_WIDTH))),
            pl.BlockSpec((1, MIX_WIDTH), lambda i: (0, 0)),
            pl.BlockSpec((N_HEADS, SGU_CHUNK, SGU_CHUNK), lambda i: (0, 0, 0)),
            pl.BlockSpec((SGU_CHUNK, MIX_WIDTH), lambda i: (0, 0)),
        ],
        out_specs=pl.BlockSpec((tm, MIX_WIDTH), lambda i: (i, 0)),
        out_shape=jax.ShapeDtypeStruct((n, MIX_WIDTH), BF16),
        compiler_params=_cparams(("parallel",)),
        name="sgu",
    )(proj, g, w, bias)


SB_TQ = 1024
SB_TK = 256
SB_R = SB_TQ // SB_TK


def _sb_tile(q_ref, k_ref, v_ref, u_ref, acc_ref, carry_ref, k_off):
    q, k, v = q_ref[...], k_ref[...], v_ref[...]
    u = u_ref[...]
    diag = k_off is not None
    if diag:
        mask = (k_off + lax.broadcasted_iota(jnp.int32, (SB_TQ, SB_TK), 1)
                < lax.broadcasted_iota(jnp.int32, (SB_TQ, SB_TK), 0))
    pvs = []
    for h in range(N_HEADS):
        hs = slice(h * HEAD_DIM, (h + 1) * HEAD_DIM)
        z = lax.dot_general(q[:, hs], k[:, hs], (((1,), (1,)), ((), ())), preferred_element_type=F32)
        log_beta = jnp.minimum(z, 0.0) - jnp.log(1.0 + jnp.exp2(-jnp.abs(z))) * LOG2E
        log_1m = log_beta - z
        if diag:
            log_1m = jnp.where(mask, log_1m, 0.0)
        carry = carry_ref[h]
        terms = log_1m.astype(BF16)
        r = jnp.dot(terms, u, preferred_element_type=F32)
        a = jnp.exp2(log_beta + r + carry)
        if diag:
            a = jnp.where(mask, a, 0.0)
        carry_ref[h] = carry + r[:, 0:1] + terms[:, 0:1].astype(F32)
        pvs.append(jnp.dot(a.astype(BF16), v[:, hs], preferred_element_type=F32))
    acc_ref[...] += jnp.concatenate(pvs, axis=1)


def _sb_kernel(qi_ref, ki_ref, q_ref, k_ref, v_ref, u_ref, o_ref, acc_ref, carry_ref):
    step = pl.program_id(1)
    qi = qi_ref[step]
    ki = ki_ref[step]

    @pl.when(ki == SB_R * qi + SB_R - 1)
    def _():
        acc_ref[...] = jnp.zeros_like(acc_ref)
        carry_ref[...] = jnp.zeros_like(carry_ref)

    @pl.when(ki >= SB_R * qi)
    def _():
        _sb_tile(q_ref, k_ref, v_ref, u_ref, acc_ref, carry_ref, ki * SB_TK - qi * SB_TQ)

    @pl.when(ki < SB_R * qi)
    def _():
        _sb_tile(q_ref, k_ref, v_ref, u_ref, acc_ref, carry_ref, None)

    @pl.when(ki == 0)
    def _():
        o_ref[...] = acc_ref[...].astype(o_ref.dtype)


def _stick_breaking(proj, batch, seq):
    n = proj.shape[0]
    nq, nk = seq // SB_TQ, seq // SB_TK
    qi_tab = np.array([qi for qi in range(nq) for _ in range(SB_R * (qi + 1))], np.int32)
    ki_tab = np.array([ki for qi in range(nq) for ki in range(SB_R * (qi + 1) - 1, -1, -1)], np.int32)
    u = jnp.asarray((np.arange(SB_TK)[:, None] > np.arange(SB_TK)[None, :]).astype(np.float32), BF16)
    qb, kb, vb = COL_SBQ // MIX_WIDTH, COL_SBK // MIX_WIDTH, COL_SBV // MIX_WIDTH
    grid_spec = pltpu.PrefetchScalarGridSpec(
        num_scalar_prefetch=2,
        grid=(batch, len(qi_tab)),
        in_specs=[
            pl.BlockSpec((SB_TQ, MIX_WIDTH), lambda b, s, qt, kt: (b * nq + qt[s], qb)),
            pl.BlockSpec((SB_TK, MIX_WIDTH), lambda b, s, qt, kt: (b * nk + kt[s], kb)),
            pl.BlockSpec((SB_TK, MIX_WIDTH), lambda b, s, qt, kt: (b * nk + kt[s], vb)),
            pl.BlockSpec((SB_TK, SB_TK), lambda b, s, qt, kt: (0, 0)),
        ],
        out_specs=pl.BlockSpec((SB_TQ, MIX_WIDTH), lambda b, s, qt, kt: (b * nq + qt[s], 0)),
        scratch_shapes=[pltpu.VMEM((SB_TQ, MIX_WIDTH), F32), pltpu.VMEM((N_HEADS, SB_TQ, 1), F32)],
    )
    return pl.pallas_call(
        _sb_kernel,
        grid_spec=grid_spec,
        out_shape=jax.ShapeDtypeStruct((n, MIX_WIDTH), BF16),
        compiler_params=_cparams(("parallel", "arbitrary")),
        name="stick_breaking",
    )(jnp.asarray(qi_tab), jnp.asarray(ki_tab), proj, proj, proj, u)


def _compress_kernel(gk_ref, gv_ref, pos_ref, w1_ref, w2_ref, kc_ref, vc_ref):
    half = CMP_STRIDE * HEAD_DIM
    for t, (g_ref, o_ref) in enumerate(((gk_ref, kc_ref), (gv_ref, vc_ref))):
        g = g_ref[0].astype(F32)
        top = (g + pos_ref[t, :, :half]).astype(BF16)
        bot = (g + pos_ref[t, :, half:]).astype(BF16)
        a = jnp.dot(top, w1_ref[t, :half, :], preferred_element_type=F32)
        b = jnp.dot(bot, w1_ref[t, half:, :], preferred_element_type=F32)
        hid = jax.nn.gelu(a + pltpu.roll(b, b.shape[0] - 1, 0))
        out = jnp.dot(hid.astype(BF16), w2_ref[t], preferred_element_type=F32)
        o_ref[0] = jnp.concatenate([out, jnp.zeros_like(out)], axis=1).astype(o_ref.dtype)


def _compress(gk, gv, pos, w1, w2):
    b, m, width = gk.shape
    out_spec = pl.BlockSpec((1, m, LANES), lambda i: (i, 0, 0))
    out_shape = jax.ShapeDtypeStruct((b, m, LANES), BF16)
    return pl.pallas_call(
        _compress_kernel,
        grid=(b,),
        in_specs=[
            pl.BlockSpec((1, m, width), lambda i: (i, 0, 0)),
            pl.BlockSpec((1, m, width), lambda i: (i, 0, 0)),
            pl.BlockSpec((2, 1, CMP_LEN * HEAD_DIM), lambda i: (0, 0, 0)),
            pl.BlockSpec((2, CMP_LEN * HEAD_DIM, CMP_HIDDEN), lambda i: (0, 0, 0)),
            pl.BlockSpec((2, CMP_HIDDEN, HEAD_DIM), lambda i: (0, 0, 0)),
        ],
        out_specs=[out_spec, out_spec],
        out_shape=[out_shape, out_shape],
        compiler_params=_cparams(("parallel",)),
        name="nsa_compress",
    )(gk, gv, pos, w1, w2)


NSA_TQ = 256
NSA_CK = 512


NSA_ROWS = N_HEADS * NSA_TQ
SEL_OFF = 1e30


def _per_head(x, fn):
    return jnp.concatenate([fn(x[h * NSA_TQ:(h + 1) * NSA_TQ]) for h in range(N_HEADS)], axis=0)


def _nsa_kernel(q_ref, kx_ref, vx_ref, kw_ref, vw_ref, kb_ref, kc_ref, vc_ref, ovt_ref, o_ref, m_ref, acc_ref):
    seq = kx_ref.shape[0]
    n_sel = seq // SEL_LEN
    n_cmp = kc_ref.shape[1]
    q0 = pl.program_id(1) * NSA_TQ
    NT = (((1,), (1,)), ((), ()))
    lo_half = lax.broadcasted_iota(jnp.int32, (NSA_TQ, LANES), 1) < HEAD_DIM
    qpos_c = q0 + lax.broadcasted_iota(jnp.int32, (NSA_TQ, 1), 0)

    qs = []
    for pair in range(N_HEADS // 2):
        x = q_ref[:, pair * LANES:(pair + 1) * LANES].astype(F32)
        qs += [jnp.where(lo_half, x, 0.0), jnp.where(lo_half, pltpu.roll(x, HEAD_DIM, 1), 0.0)]
    q4 = jnp.concatenate(qs, axis=0).astype(BF16)

    ones_half = lax.broadcasted_iota(jnp.int32, (1, LANES), 1) >= HEAD_DIM

    def masked_exp(s, mask):
        s = _per_head(s, lambda t: jnp.where(mask, t, NEG))
        return jnp.exp2(s - jnp.max(s, axis=-1, keepdims=True))

    def attend(p, vals):
        return jnp.dot(p.astype(BF16), jnp.where(ones_half, 1.0, vals).astype(BF16), preferred_element_type=F32)

    def normalized(r):
        return r / jnp.where(ones_half, 1.0, pltpu.roll(r, HEAD_DIM, 1))

    cmp_end = lax.broadcasted_iota(jnp.int32, (NSA_TQ, n_cmp), 1) * CMP_STRIDE + (CMP_LEN - 1)
    p_cmp = masked_exp(lax.dot_general(q4, kc_ref[0], NT, preferred_element_type=F32), cmp_end <= qpos_c)
    has_cmp = jnp.concatenate([qpos_c >= CMP_LEN - 1] * N_HEADS, axis=0)
    p_cmp = p_cmp * jnp.where(has_cmp, 1.0 / jnp.sum(p_cmp, axis=-1, keepdims=True), 0.0)
    r_cmp = jnp.dot(p_cmp.astype(BF16), vc_ref[0], preferred_element_type=F32)
    p_sum = sum(p_cmp[h * NSA_TQ:(h + 1) * NSA_TQ] for h in range(N_HEADS))

    hi = p_sum.astype(BF16)
    lo = (p_sum - hi.astype(F32)).astype(BF16)
    imp = lax.dot_general(ovt_ref[...], jnp.concatenate([hi, lo], axis=1), NT, preferred_element_type=F32)
    blk = lax.broadcasted_iota(jnp.int32, (n_sel, NSA_TQ), 0)
    qpos_r = q0 + lax.broadcasted_iota(jnp.int32, (n_sel, NSA_TQ), 1)
    valid = blk * SEL_LEN <= qpos_r
    forced = (blk == 0) | (blk == jnp.right_shift(qpos_r, 6))
    score = jnp.where(valid, imp + jnp.where(forced, FORCE_SCORE, 0.0), -jnp.inf)
    groups = [score[8 * g:8 * g + 8] for g in range(n_sel // 8)]
    ranks = [jnp.zeros((8, NSA_TQ), F32) for _ in groups]
    row8 = lax.broadcasted_iota(jnp.int32, (8, NSA_TQ), 0)
    for i in range(n_sel):
        ci = jnp.broadcast_to(score[i:i + 1], (8, NSA_TQ))
        for g, sg in enumerate(groups):
            if 8 * g > i:
                beats = ci >= sg
            elif 8 * g + 7 < i:
                beats = ci > sg
            else:
                beats = (ci > sg) | ((ci == sg) & (row8 + 8 * g > i))
            ranks[g] = ranks[g] + jnp.where(beats, 1.0, 0.0)
    sel = jnp.where(valid & (jnp.concatenate(ranks, axis=0) < SEL_TOPN), 0.0, -SEL_OFF)
    sel = jnp.concatenate([sel, jnp.full((LANES - n_sel, NSA_TQ), -SEL_OFF, F32)], axis=0).T
    aug = lambda t: jnp.concatenate([q4, jnp.concatenate([t.astype(BF16)] * N_HEADS, axis=0)], axis=1)
    q_aug = aug(sel)
    blk_lane = lax.broadcasted_iota(jnp.int32, (NSA_TQ, LANES), 1)
    q_aug_past = aug(jnp.where(blk_lane * SEL_LEN >= q0, -SEL_OFF, sel))

    w_start = pl.multiple_of(jnp.maximum(q0 - WINDOW, 0), NSA_TQ)
    kwpos = w_start + lax.broadcasted_iota(jnp.int32, (NSA_TQ, WINDOW + NSA_TQ), 1)
    s_win = lax.dot_general(q4, kw_ref[pl.ds(w_start, WINDOW + NSA_TQ), :], NT, preferred_element_type=F32)
    p_win = masked_exp(s_win, (kwpos <= qpos_c) & (kwpos > qpos_c - WINDOW))
    r_win = normalized(attend(p_win, vw_ref[pl.ds(w_start, WINDOW + NSA_TQ), :]))

    def slab_stats(q, start, size, causal):
        keys = jnp.concatenate([kx_ref[pl.ds(start, size), :], kb_ref[pl.ds(start, size), :]], axis=1)
        s = lax.dot_general(q, keys, NT, preferred_element_type=F32)
        if causal:
            ok = start + lax.broadcasted_iota(jnp.int32, (NSA_TQ, size), 1) <= qpos_c
            s = _per_head(s, lambda t: jnp.where(ok, t, NEG))
        m_c = jnp.max(s, axis=-1, keepdims=True)
        return m_c, attend(jnp.exp2(s - m_c), vx_ref[pl.ds(start, size), :])

    m_d, a_d = slab_stats(q_aug, pl.multiple_of(q0, NSA_TQ), NSA_TQ, True)
    m_ref[...] = m_d
    acc_ref[...] = a_d

    def pair_body(i, carry):
        m0, a0 = slab_stats(q_aug_past, pl.multiple_of(2 * i * NSA_CK, NSA_CK), NSA_CK, False)
        m1, a1 = slab_stats(q_aug_past, pl.multiple_of((2 * i + 1) * NSA_CK, NSA_CK), NSA_CK, False)
        m_old = m_ref[...]
        m_new = jnp.maximum(m_old, jnp.maximum(m0, m1))
        acc_ref[...] = (jnp.exp2(m_old - m_new) * acc_ref[...] + jnp.exp2(m0 - m_new) * a0
                        + jnp.exp2(m1 - m_new) * a1)
        m_ref[...] = m_new
        return carry

    lax.fori_loop(0, (q0 + 2 * NSA_CK - 1) // (2 * NSA_CK), pair_body, 0)

    r_slc = normalized(acc_ref[...])
    gates = _sigmoid(vw_ref[pl.ds(pl.multiple_of(q0, NSA_TQ), NSA_TQ), :].astype(F32))
    g = [jnp.concatenate([gates[:, HEAD_DIM + 3 * h + t:HEAD_DIM + 3 * h + t + 1] for h in range(N_HEADS)], axis=0)
         for t in range(N_BRANCH)]
    y = g[0] * r_cmp + g[1] * r_slc + g[2] * r_win
    for pair in range(N_HEADS // 2):
        even = y[2 * pair * NSA_TQ:(2 * pair + 1) * NSA_TQ]
        odd = y[(2 * pair + 1) * NSA_TQ:(2 * pair + 2) * NSA_TQ]
        o_ref[:, pair * LANES:(pair + 1) * LANES] = jnp.where(lo_half, even, pltpu.roll(odd, HEAD_DIM, 1)).astype(o_ref.dtype)


def _nsa(proj, kc, vc, batch, seq):
    n = proj.shape[0]
    nqb = seq // NSA_TQ
    n_sel = seq // SEL_LEN
    n_cmp = kc.shape[1]
    assert n_sel % 8 == 0 and n_sel <= LANES and seq % (2 * NSA_CK) == 0
    cs = (np.arange(n_cmp) * CMP_STRIDE)[None, :]
    ss = (np.arange(n_sel) * SEL_LEN)[:, None]
    ovt = np.clip(np.minimum(cs + CMP_LEN, ss + SEL_LEN) - np.maximum(cs, ss), 0, None).astype(np.float32) / CMP_LEN
    ovt = jnp.asarray(np.concatenate([ovt, ovt], axis=1), BF16)
    kb = jnp.asarray((np.arange(seq)[:, None] // SEL_LEN == np.arange(LANES)[None, :]).astype(np.float32), BF16)
    seq_spec = lambda col: pl.BlockSpec((seq, LANES), lambda b, i: (b, col))
    cmp_spec = pl.BlockSpec((1, n_cmp, LANES), lambda b, i: (b, 0, 0))
    return pl.pallas_call(
        _nsa_kernel,
        grid=(batch, nqb),
        in_specs=[
            pl.BlockSpec((NSA_TQ, MIX_WIDTH), lambda b, i: (b * nqb + i, COL_CQ // MIX_WIDTH)),
            seq_spec(COL_KX // LANES),
            seq_spec(COL_VX // LANES),
            seq_spec(COL_KX // LANES + 1),
            seq_spec(COL_VX // LANES + 1),
            pl.BlockSpec((seq, LANES), lambda b, i: (0, 0)),
            cmp_spec, cmp_spec,
            pl.BlockSpec((n_sel, 2 * n_cmp), lambda b, i: (0, 0)),
        ],
        out_specs=pl.BlockSpec((NSA_TQ, MIX_WIDTH), lambda b, i: (b * nqb + i, 0)),
        out_shape=jax.ShapeDtypeStruct((n, MIX_WIDTH), BF16),
        scratch_shapes=[pltpu.VMEM((NSA_ROWS, 1), F32), pltpu.VMEM((NSA_ROWS, LANES), F32)],
        compiler_params=_cparams(("parallel", "arbitrary")),
        name="nsa",
    )(proj, proj, proj, proj, proj, kb, kc, vc, ovt)


def _merge_kernel(ya_ref, yb_ref, yc_ref, g0_ref, g1_ref, g2_ref, wb_ref, wo_ref, h_ref, o_ref):
    merged = None
    for y_ref, g_ref, t in ((ya_ref, g0_ref, 0), (yb_ref, g1_ref, 1), (yc_ref, g2_ref, 2)):
        term = _sigmoid(g_ref[...].astype(F32)) * jnp.dot(y_ref[...], wb_ref[t], preferred_element_type=F32)
        merged = term if merged is None else merged + term
    o_ref[...] = h_ref[...] + jnp.dot(merged.astype(BF16), wo_ref[...], preferred_element_type=F32)


def _merge(ya, yb, yc, proj, wb, wo, h, tm=512):
    n = h.shape[0]
    y_spec = pl.BlockSpec((tm, MIX_WIDTH), lambda i: (i, 0))
    g_spec = lambda t: pl.BlockSpec((tm, D_MODEL), lambda i: (i, COL_BRG // D_MODEL + t))
    return pl.pallas_call(
        _merge_kernel,
        grid=(n // tm,),
        in_specs=[y_spec, y_spec, y_spec, g_spec(0), g_spec(1), g_spec(2),
                  pl.BlockSpec((N_BRANCH, MIX_WIDTH, D_MODEL), lambda i: (0, 0, 0)),
                  pl.BlockSpec((D_MODEL, D_MODEL), lambda i: (0, 0)),
                  pl.BlockSpec((tm, D_MODEL), lambda i: (i, 0))],
        out_specs=pl.BlockSpec((tm, D_MODEL), lambda i: (i, 0)),
        out_shape=jax.ShapeDtypeStruct((n, D_MODEL), F32),
        compiler_params=_cparams(("parallel",)),
        name="merge",
    )(ya, yb, yc, proj, proj, proj, wb, wo, h)


def _ffn_kernel(h_ref, g_ref, wg_ref, wu_ref, wd_ref, o_ref, xn_ref, acc_ref):
    f = pl.program_id(1)

    @pl.when(f == 0)
    def _():
        xn_ref[...] = _rms(h_ref[...], g_ref[...]).astype(BF16)
        acc_ref[...] = h_ref[...]

    xn = xn_ref[...]
    a = jnp.dot(xn, wg_ref[...], preferred_element_type=F32)
    u = jnp.dot(xn, wu_ref[...], preferred_element_type=F32)
    hid = (a * _sigmoid(a) * u).astype(BF16)
    acc_ref[...] += jnp.dot(hid, wd_ref[...], preferred_element_type=F32)

    @pl.when(f == pl.num_programs(1) - 1)
    def _():
        o_ref[...] = acc_ref[...]


def _ffn(h, g, wg, wu, wd, tm=1024, tf=256):
    n = h.shape[0]
    dff = wg.shape[1]
    return pl.pallas_call(
        _ffn_kernel,
        grid=(n // tm, dff // tf),
        in_specs=[
            pl.BlockSpec((tm, D_MODEL), lambda i, f: (i, 0)),
            pl.BlockSpec((1, D_MODEL), lambda i, f: (0, 0)),
            pl.BlockSpec((D_MODEL, tf), lambda i, f: (0, f)),
            pl.BlockSpec((D_MODEL, tf), lambda i, f: (0, f)),
            pl.BlockSpec((tf, D_MODEL), lambda i, f: (f, 0)),
        ],
        out_specs=pl.BlockSpec((tm, D_MODEL), lambda i, f: (i, 0)),
        out_shape=jax.ShapeDtypeStruct((n, D_MODEL), F32),
        scratch_shapes=[pltpu.VMEM((tm, D_MODEL), BF16), pltpu.VMEM((tm, D_MODEL), F32)],
        compiler_params=_cparams(("parallel", "arbitrary")),
        name="ffn",
    )(h, g, wg, wu, wd)


RT_TM = 1024
RT_TF = 896
RT_TD = 512


def _gate_kernel(h_ref, g_ref, rw_ref, gate_ref):
    tm = h_ref.shape[0]
    lane = lax.broadcasted_iota(jnp.int32, (tm, LANES), 1)
    xn = _rms(h_ref[...], g_ref[...])
    logits = jnp.dot(xn, rw_ref[...], preferred_element_type=F32, precision=lax.Precision.HIGHEST)
    logits = jnp.where(lane < N_EXPERTS, logits, -jnp.inf)
    m1 = jnp.max(logits, axis=-1, keepdims=True)
    i1 = jnp.min(jnp.where(logits == m1, lane, LANES), axis=-1, keepdims=True)
    rest = jnp.where(lane == i1, -jnp.inf, logits)
    m2 = jnp.max(rest, axis=-1, keepdims=True)
    i2 = jnp.min(jnp.where(rest == m2, lane, LANES), axis=-1, keepdims=True)
    e2 = jnp.exp(m2 - m1)
    gate_ref[...] = jnp.where(lane == i1, 1.0 / (1.0 + e2), 0.0) + jnp.where(lane == i2, e2 / (1.0 + e2), 0.0)


def _gate(h, g, rw, tm=1024):
    n = h.shape[0]
    return pl.pallas_call(
        _gate_kernel,
        grid=(n // tm,),
        in_specs=[
            pl.BlockSpec((tm, D_MODEL), lambda i: (i, 0)),
            pl.BlockSpec((1, D_MODEL), lambda i: (0, 0)),
            pl.BlockSpec((D_MODEL, LANES), lambda i: (0, 0)),
        ],
        out_specs=pl.BlockSpec((tm, LANES), lambda i: (i, 0)),
        out_shape=jax.ShapeDtypeStruct((n, LANES), F32),
        compiler_params=_cparams(("parallel",)),
        name="moe_gate",
    )(h, g, rw)


def _row_copy(src_ref, src_row, dst_ref, dst_row, sem):
    return pltpu.make_async_copy(src_ref.at[pl.ds(src_row, 1), :], dst_ref.at[pl.ds(dst_row, 1), :], sem)


def _dispatch_kernel(pad_ref, pa_ref, pb_ref, h_ref, xs_ref, zero_ref, sems):
    i = pl.program_id(0)

    def send(t, carry):
        _row_copy(h_ref, t, xs_ref, pa_ref[0, 0, t], sems.at[0]).start()
        _row_copy(h_ref, t, xs_ref, pb_ref[0, 0, t], sems.at[1]).start(priority=1)
        return carry

    lax.fori_loop(0, RT_TD, send, 0, unroll=4)

    @pl.when(i == 0)
    def _():
        zero_ref[...] = jnp.zeros_like(zero_ref)
        for e in range(N_EXPERTS + 1):
            lo, hi = pad_ref[2 * e], pad_ref[2 * e + 1]
            lax.fori_loop(lo, hi, lambda r, c: (_row_copy(zero_ref, 0, xs_ref, r, sems.at[2]).start(), c)[1], 0)
        for e in range(N_EXPERTS + 1):
            lo, hi = pad_ref[2 * e], pad_ref[2 * e + 1]
            lax.fori_loop(lo, hi, lambda r, c: (_row_copy(zero_ref, 0, xs_ref, r, sems.at[2]).wait(), c)[1], 0)

    for s in range(2):
        pltpu.make_async_copy(h_ref, xs_ref.at[pl.ds(0, RT_TD), :], sems.at[s]).wait()


def _dispatch(pad_bounds, pa, pb, h, n_rows):
    n = h.shape[0]
    idx_spec = pl.BlockSpec((1, 1, RT_TD), lambda i, *_: (i, 0, 0), memory_space=pltpu.SMEM)
    grid_spec = pltpu.PrefetchScalarGridSpec(
        num_scalar_prefetch=1,
        grid=(n // RT_TD,),
        in_specs=[idx_spec, idx_spec, pl.BlockSpec((RT_TD, D_MODEL), lambda i, *_: (i, 0))],
        out_specs=pl.BlockSpec(memory_space=pl.ANY),
        scratch_shapes=[pltpu.VMEM((8, D_MODEL), F32), pltpu.SemaphoreType.DMA((3,))],
    )
    return pl.pallas_call(
        _dispatch_kernel,
        grid_spec=grid_spec,
        out_shape=jax.ShapeDtypeStruct((n_rows, D_MODEL), F32),
        compiler_params=_cparams(("arbitrary",)),
        name="moe_dispatch",
    )(pad_bounds, pa.reshape(-1, 1, RT_TD), pb.reshape(-1, 1, RT_TD), h)


def _sorted_experts_kernel(te_ref, tx_ref, live_ref, xs_ref, g_ref, wg_ref, wu_ref, wd_ref, y_ref, xb_ref):
    i, f = pl.program_id(0), pl.program_id(1)

    @pl.when(f == 0)
    def _():
        xb_ref[...] = _rms(xs_ref[...], g_ref[...]).astype(BF16)
        y_ref[...] = jnp.zeros_like(y_ref)

    @pl.when(live_ref[i] > 0)
    def _():
        x = xb_ref[...]
        a = jnp.dot(x, wg_ref[0], preferred_element_type=F32)
        u = jnp.dot(x, wu_ref[0], preferred_element_type=F32)
        y_ref[...] += jnp.dot((a * _sigmoid(a) * u).astype(BF16), wd_ref[0], preferred_element_type=F32)


def _sorted_experts(tile_expert, tile_src, tile_live, xs, g, wg, wu, wd):
    n_rows = xs.shape[0]
    ne, _, dff = wg.shape
    grid_spec = pltpu.PrefetchScalarGridSpec(
        num_scalar_prefetch=3,
        grid=(n_rows // RT_TM, dff // RT_TF),
        in_specs=[
            pl.BlockSpec((RT_TM, D_MODEL), lambda i, f, te, tx, lv: (tx[i], 0)),
            pl.BlockSpec((1, D_MODEL), lambda i, f, *_: (0, 0)),
            pl.BlockSpec((1, D_MODEL, RT_TF), lambda i, f, te, tx, lv: (te[i], 0, f)),
            pl.BlockSpec((1, D_MODEL, RT_TF), lambda i, f, te, tx, lv: (te[i], 0, f)),
            pl.BlockSpec((1, RT_TF, D_MODEL), lambda i, f, te, tx, lv: (te[i], f, 0)),
        ],
        out_specs=pl.BlockSpec((RT_TM, D_MODEL), lambda i, f, *_: (i, 0)),
        scratch_shapes=[pltpu.VMEM((RT_TM, D_MODEL), BF16)],
    )
    return pl.pallas_call(
        _sorted_experts_kernel,
        grid_spec=grid_spec,
        out_shape=jax.ShapeDtypeStruct((n_rows, D_MODEL), F32),
        compiler_params=_cparams(("parallel", "arbitrary")),
        name="moe_experts",
    )(tile_expert, tile_src, tile_live, xs, g, wg, wu, wd)


def _combine_kernel(pa_ref, pb_ref, pa_next_ref, pb_next_ref, ga_ref, gb_ref, h_ref, y_ref, o_ref, ya_ref, yb_ref, sems):
    i, n_steps = pl.program_id(0), pl.num_programs(0)
    slot = i % 2

    def fetch(pa, pb, s):
        def body(t, carry):
            _row_copy(y_ref, pa[0, 0, t], ya_ref.at[s], t, sems.at[0, s]).start()
            _row_copy(y_ref, pb[0, 0, t], yb_ref.at[s], t, sems.at[1, s]).start(priority=1)
            return carry

        lax.fori_loop(0, RT_TD, body, 0, unroll=4)

    @pl.when(i == 0)
    def _():
        fetch(pa_ref, pb_ref, 0)

    @pl.when(i + 1 < n_steps)
    def _():
        fetch(pa_next_ref, pb_next_ref, 1 - slot)

    pltpu.make_async_copy(y_ref.at[pl.ds(0, RT_TD), :], ya_ref.at[slot], sems.at[0, slot]).wait()
    pltpu.make_async_copy(y_ref.at[pl.ds(0, RT_TD), :], yb_ref.at[slot], sems.at[1, slot]).wait()
    o_ref[...] = h_ref[...] + ga_ref[...] * ya_ref[slot] + gb_ref[...] * yb_ref[slot]


def _combine(pa, pb, ga, gb, h, y):
    n = h.shape[0]
    n_steps = n // RT_TD
    idx_spec = pl.BlockSpec((1, 1, RT_TD), lambda i: (i, 0, 0), memory_space=pltpu.SMEM)
    next_spec = pl.BlockSpec((1, 1, RT_TD), lambda i: (jnp.minimum(i + 1, n_steps - 1), 0, 0), memory_space=pltpu.SMEM)
    col_spec = pl.BlockSpec((RT_TD, 1), lambda i: (i, 0))
    row_spec = pl.BlockSpec((RT_TD, D_MODEL), lambda i: (i, 0))
    pa, pb = pa.reshape(-1, 1, RT_TD), pb.reshape(-1, 1, RT_TD)
    return pl.pallas_call(
        _combine_kernel,
        grid=(n_steps,),
        in_specs=[idx_spec, idx_spec, next_spec, next_spec, col_spec, col_spec, row_spec,
                  pl.BlockSpec(memory_space=pl.ANY)],
        out_specs=row_spec,
        out_shape=jax.ShapeDtypeStruct((n, D_MODEL), F32),
        scratch_shapes=[pltpu.VMEM((2, RT_TD, D_MODEL), F32), pltpu.VMEM((2, RT_TD, D_MODEL), F32),
                        pltpu.SemaphoreType.DMA((2, 2))],
        compiler_params=_cparams(("arbitrary",)),
        name="moe_combine",
    )(pa, pb, pa, pb, ga, gb, h, y)


def _moe_sorted(h, g, rw, wg, wu, wd):
    n = h.shape[0]
    n_tiles = 2 * n // RT_TM + N_EXPERTS
    gate = _gate(h, g, rw)[:, :N_EXPERTS]
    routed = gate > 0
    r = routed.astype(jnp.int32)
    incl = jnp.cumsum(r, axis=0)
    count = incl[-1]
    padded = (count + RT_TM - 1) // RT_TM * RT_TM
    ends = jnp.cumsum(padded)
    pos = (ends - padded)[None, :] + incl - 1
    order = jnp.cumsum(r, axis=1)
    first, second = routed & (order == 1), routed & (order == 2)
    pick = lambda m, v: jnp.sum(jnp.where(m, v, 0), axis=1)
    pa = pick(first, pos)
    pb = jnp.where(jnp.any(second, axis=1), pick(second, pos), pa)
    ga, gb = pick(first, gate)[:, None], pick(second, gate)[:, None]
    pad_lo = jnp.concatenate([ends - padded + count, ends[-1:]])
    pad_hi = jnp.concatenate([ends, jnp.full((1,), n_tiles * RT_TM, ends.dtype)])
    pad_bounds = jnp.stack([pad_lo, pad_hi], axis=1).reshape(-1).astype(jnp.int32)
    tile_start = jnp.arange(n_tiles, dtype=jnp.int32) * RT_TM
    tile_live = (tile_start < ends[-1]).astype(jnp.int32)
    tile_expert = jnp.minimum(jnp.sum(tile_start[:, None] >= ends[None, :], axis=1), N_EXPERTS - 1).astype(jnp.int32)
    tile_src = jnp.minimum(jnp.arange(n_tiles, dtype=jnp.int32), ends[-1] // RT_TM - 1)
    xs = _dispatch(pad_bounds, pa.astype(jnp.int32), pb.astype(jnp.int32), h, n_tiles * RT_TM)
    y = _sorted_experts(tile_expert, tile_src, tile_live, xs, g, wg, wu, wd)
    return _combine(pa.astype(jnp.int32), pb.astype(jnp.int32), ga, gb, h, y)


def _final_norm_kernel(h_ref, g_ref, o_ref):
    o_ref[...] = _rms(h_ref[...], g_ref[...])


def _final_norm(h, g, tm=1024):
    n = h.shape[0]
    return pl.pallas_call(
        _final_norm_kernel,
        grid=(n // tm,),
        in_specs=[pl.BlockSpec((tm, D_MODEL), lambda i: (i, 0)), pl.BlockSpec((1, D_MODEL), lambda i: (0, 0))],
        out_specs=pl.BlockSpec((tm, D_MODEL), lambda i: (i, 0)),
        out_shape=jax.ShapeDtypeStruct((n, D_MODEL), F32),
        compiler_params=_cparams(("parallel",)),
        name="final_norm",
    )(h, g)


def _rope_tables(seq):
    half = ROPE_DIMS // 2
    inv_freq = ROPE_THETA ** (-jnp.arange(0, ROPE_DIMS, 2, dtype=F32) / ROPE_DIMS)
    ang = jnp.arange(seq, dtype=F32)[:, None] * inv_freq[None, :]
    cos, sin = jnp.cos(ang), jnp.sin(ang)
    ones = jnp.ones((seq, HEAD_DIM - ROPE_DIMS), F32)
    zeros_h = jnp.zeros((seq, half), F32)
    zeros_r = jnp.zeros((seq, HEAD_DIM - ROPE_DIMS), F32)
    c = jnp.concatenate([cos, cos, ones], axis=1)
    s1 = jnp.concatenate([-sin, zeros_h, zeros_r], axis=1)
    s2 = jnp.concatenate([zeros_h, sin, zeros_r], axis=1)
    rep = LANES // HEAD_DIM
    return tuple(jnp.tile(t, (1, rep)) for t in (c, s1, s2))


def _layout_w_in(w):
    scale = HEAD_DIM ** -0.5 * LOG2E
    a_in, sb_q, sb_k, sb_v, c_q, c_kv, c_g, br_g = jnp.split(
        w, np.cumsum([2 * MIX_WIDTH, MIX_WIDTH, MIX_WIDTH, MIX_WIDTH, MIX_WIDTH, 6 * HEAD_DIM, 3 * N_HEADS])[:].tolist(),
        axis=-1)
    k_cmp, v_cmp, k_slc, v_slc, k_win, v_win = jnp.split(c_kv, 6, axis=-1)
    zeros = lambda width: jnp.zeros((w.shape[0], width), w.dtype)
    cols = [a_in, sb_q * scale, sb_k, sb_v,
            v_slc, v_cmp, v_win, c_g, zeros(HEAD_DIM - 3 * N_HEADS),
            c_q * scale, k_slc, k_cmp, k_win, zeros(HEAD_DIM),
            br_g]
    out = jnp.concatenate(cols, axis=-1)
    assert out.shape[-1] == PROJ_W
    return out.astype(BF16)


def kernel(x, norm1_g, w_in, sgu_norm_g, sgu_w, sgu_b, cmp_pos, cmp_w1, cmp_w2, w_branch, w_out, norm2_g,
           ffn_w_gate, ffn_w_up, ffn_w_down, router_w, moe_w_gate, moe_w_up, moe_w_down, final_norm_g):
    batch, seq, _ = x.shape
    n = batch * seq
    depth = norm1_g.shape[0]
    rope_c, rope_s1, rope_s2 = _rope_tables(seq)
    dff_pad = -(-D_FF // 256) * 256
    h = x.reshape(n, D_MODEL)
    for layer in range(depth):
        proj = _norm_proj(h, norm1_g[layer][None, :], _layout_w_in(w_in[layer]), rope_c, rope_s1, rope_s2, seq)
        sgu_bias = jnp.repeat(sgu_b[layer].T, HEAD_DIM, axis=1)
        y_a = _sgu(proj, sgu_norm_g[layer][None, :], sgu_w[layer], sgu_bias)
        y_b = _stick_breaking(proj, batch, seq)
        groups = seq // CMP_STRIDE
        gk = proj[:, COL_KX + HEAD_DIM:COL_KX + 2 * HEAD_DIM].reshape(batch, groups, CMP_STRIDE * HEAD_DIM)
        gv = proj[:, COL_VX + HEAD_DIM:COL_VX + 2 * HEAD_DIM].reshape(batch, groups, CMP_STRIDE * HEAD_DIM)
        kc, vc = _compress(gk, gv, cmp_pos[layer].reshape(2, 1, CMP_LEN * HEAD_DIM),
                           cmp_w1[layer].astype(BF16), cmp_w2[layer].astype(BF16))
        y_c = _nsa(proj, kc, vc, batch, seq)
        h = _merge(y_a, y_b, y_c, proj, w_branch[layer].astype(BF16), w_out[layer].astype(BF16), h)
        j = layer // 2
        g2 = norm2_g[layer][None, :]
        if layer % 2 == 0:
            pad = dff_pad - D_FF
            wg = jnp.pad(ffn_w_gate[j], ((0, 0), (0, pad))).astype(BF16)
            wu = jnp.pad(ffn_w_up[j], ((0, 0), (0, pad))).astype(BF16)
            wd = jnp.pad(ffn_w_down[j], ((0, pad), (0, 0))).astype(BF16)
            h = _ffn(h, g2, wg, wu, wd)
        else:
            rw = jnp.pad(router_w[j], ((0, 0), (0, LANES - N_EXPERTS)))
            h = _moe_sorted(h, g2, rw, moe_w_gate[j].astype(BF16), moe_w_up[j].astype(BF16), moe_w_down[j].astype(BF16))
    return _final_norm(h, final_norm_g[None, :]).reshape(batch, seq, D_MODEL)
```

```python
import functools

import numpy as np
import jax
import jax.numpy as jnp
from jax import lax
from jax.experimental import pallas as pl
from jax.experimental.pallas import tpu as pltpu

F32 = jnp.float32
BF16 = jnp.bfloat16

D_MODEL = 1024
HEAD_DIM = 64
N_HEADS = 4
MIX_WIDTH = N_HEADS * HEAD_DIM
ROPE_DIMS = HEAD_DIM // 4
ROPE_THETA = 500000.0
EPS = 1e-6
SGU_CHUNK = 128
CMP_LEN = 32
CMP_STRIDE = 16
CMP_HIDDEN = 256
SEL_LEN = 64
SEL_TOPN = 16
WINDOW = 512
FORCE_SCORE = 1e4
N_BRANCH = 3
D_FF = 2752
N_EXPERTS = 8
D_FF_EXPERT = 3584
NEG = -1e30
LOG2E = 1.4426950408889634

LANES = 128
VMEM_LIMIT = 56 * 1024 * 1024

PROJ_TILE = 1024
COL_A = 0
COL_SBQ, COL_SBK, COL_SBV = 512, 768, 1024
COL_VX = 1280
COL_CQ = 1536
COL_KX = 1792
COL_BRG = 2048
PROJ_W = COL_BRG + N_BRANCH * D_MODEL
ROPE_TILE = COL_CQ // PROJ_TILE
ROPE_START = COL_CQ % PROJ_TILE
assert COL_BRG == (ROPE_TILE + 1) * PROJ_TILE


def _cparams(sem):
    return pltpu.CompilerParams(dimension_semantics=sem, vmem_limit_bytes=VMEM_LIMIT)


def _sigmoid(x):
    return 1.0 / (1.0 + jnp.exp(-x))


def _rms(x, g):
    return x * lax.rsqrt(jnp.mean(x * x, axis=-1, keepdims=True) + EPS) * g


def _norm_proj_kernel(h_ref, g_ref, w_ref, c_ref, s1_ref, s2_ref, o_ref, xn_ref):
    j = pl.program_id(1)

    @pl.when(j == 0)
    def _():
        xn_ref[...] = _rms(h_ref[...], g_ref[...]).astype(BF16)

    acc = jnp.dot(xn_ref[...], w_ref[...], preferred_element_type=F32)

    @pl.when(j != ROPE_TILE)
    def _():
        o_ref[...] = acc.astype(o_ref.dtype)

    @pl.when(j == ROPE_TILE)
    def _():
        c, s1, s2 = c_ref[...], s1_ref[...], s2_ref[...]
        o_ref[:, :ROPE_START] = acc[:, :ROPE_START].astype(o_ref.dtype)
        for g in range(ROPE_START // LANES, PROJ_TILE // LANES):
            x = acc[:, g * LANES:(g + 1) * LANES]
            y = x * c + pltpu.roll(x, LANES - 8, 1) * s1 + pltpu.roll(x, 8, 1) * s2
            o_ref[:, g * LANES:(g + 1) * LANES] = y.astype(o_ref.dtype)


def _norm_proj(h, g, w, rope_c, rope_s1, rope_s2, seq, tm=1024):
    n = h.shape[0]
    nseq = seq // tm
    return pl.pallas_call(
        _norm_proj_kernel,
        grid=(n // tm, PROJ_W // PROJ_TILE),
        in_specs=[
            pl.BlockSpec((tm, D_MODEL), lambda i, j: (i, 0)),
            pl.BlockSpec((1, D_MODEL), lambda i, j: (0, 0)),
            pl.BlockSpec((D_MODEL, PROJ_TILE), lambda i, j: (0, j)),
            pl.BlockSpec((tm, LANES), lambda i, j: (i % nseq, 0)),
            pl.BlockSpec((tm, LANES), lambda i, j: (i % nseq, 0)),
            pl.BlockSpec((tm, LANES), lambda i, j: (i % nseq, 0)),
        ],
        out_specs=pl.BlockSpec((tm, PROJ_TILE), lambda i, j: (i, j)),
        out_shape=jax.ShapeDtypeStruct((n, PROJ_W), BF16),
        scratch_shapes=[pltpu.VMEM((tm, D_MODEL), BF16)],
        compiler_params=_cparams(("parallel", "arbitrary")),
        name="norm_proj",
    )(h, g, w, rope_c, rope_s1, rope_s2)


def _sgu_kernel(z_ref, g_ref, w_ref, b_ref, o_ref):
    tm = z_ref.shape[0]
    a = jax.nn.gelu(z_ref[...].astype(F32))
    u = a[:, :MIX_WIDTH]
    v = _rms(a[:, MIX_WIDTH:], g_ref[...]).astype(BF16)
    row = lax.broadcasted_iota(jnp.int32, (SGU_CHUNK, SGU_CHUNK), 0)
    col = lax.broadcasted_iota(jnp.int32, (SGU_CHUNK, SGU_CHUNK), 1)
    ws = [jnp.where(row >= col, w_ref[gi], 0.0).astype(BF16) for gi in range(N_HEADS)]
    bias = b_ref[...]
    for c in range(tm // SGU_CHUNK):
        rows = slice(c * SGU_CHUNK, (c + 1) * SGU_CHUNK)
        mix = jnp.concatenate(
            [jnp.dot(ws[gi], v[rows, gi * HEAD_DIM:(gi + 1) * HEAD_DIM], preferred_element_type=F32)
             for gi in range(N_HEADS)], axis=1)
        o_ref[rows, :] = (u[rows, :] * (mix + bias)).astype(o_ref.dtype)


def _sgu(proj, g, w, bias, tm=512):
    n = proj.shape[0]
    return pl.pallas_call(
        _sgu_kernel,
        grid=(n // tm,),
        in_specs=[
            pl.BlockSpec((tm, 2 * MIX_WIDTH), lambda i: (i, COL_A // (2 * MIX_WIDTH))),
            pl.BlockSpec((1, MIX_WIDTH), lambda i: (0, 0)),
            pl.BlockSpec((N_HEADS, SGU_CHUNK, SGU_CHUNK), lambda i: (0, 0, 0)),
            pl.BlockSpec((SGU_CHUNK, MIX_WIDTH), lambda i: (0, 0)),
        ],
        out_specs=pl.BlockSpec((tm, MIX_WIDTH), lambda i: (i, 0)),
        out_shape=jax.ShapeDtypeStruct((n, MIX_WIDTH), BF16),
        compiler_params=_cparams(("parallel",)),
        name="sgu",
    )(proj, g, w, bias)


SB_TQ = 1024
SB_TK = 256
SB_R = SB_TQ // SB_TK


def _sb_tile(q_ref, k_ref, v_ref, u_ref, acc_ref, carry_ref, k_off):
    diag = k_off is not None
    rows = slice(k_off if diag else 0, SB_TQ)
    n_rows = rows.stop - rows.start
    q, k, v = q_ref[rows, :], k_ref[...], v_ref[...]
    u = u_ref[...]
    if diag:
        mask = (lax.broadcasted_iota(jnp.int32, (n_rows, SB_TK), 1)
                < lax.broadcasted_iota(jnp.int32, (n_rows, SB_TK), 0))
    pvs = []
    for h in range(N_HEADS):
        hs = slice(h * HEAD_DIM, (h + 1) * HEAD_DIM)
        z = lax.dot_general(q[:, hs], k[:, hs], (((1,), (1,)), ((), ())), preferred_element_type=F32)
        log_beta = jnp.minimum(z, 0.0) - jnp.log(1.0 + jnp.exp2(-jnp.abs(z))) * LOG2E
        log_1m = log_beta - z
        if diag:
            log_1m = jnp.where(mask, log_1m, 0.0)
        carry = carry_ref[h, rows, :]
        terms = log_1m.astype(BF16)
        r = jnp.dot(terms, u, preferred_element_type=F32)
        a = jnp.exp2(log_beta + r + carry)
        if diag:
            a = jnp.where(mask, a, 0.0)
        carry_ref[h, rows, :] = carry + r[:, 0:1] + terms[:, 0:1].astype(F32)
        pvs.append(jnp.dot(a.astype(BF16), v[:, hs], preferred_element_type=F32))
    acc_ref[rows, :] += jnp.concatenate(pvs, axis=1)


def _sb_kernel(qi_ref, ki_ref, q_ref, k_ref, v_ref, u_ref, o_ref, acc_ref, carry_ref):
    step = pl.program_id(1)
    qi = qi_ref[step]
    ki = ki_ref[step]

    @pl.when(ki == SB_R * qi + SB_R - 1)
    def _():
        acc_ref[...] = jnp.zeros_like(acc_ref)
        carry_ref[...] = jnp.zeros_like(carry_ref)

    for d in range(SB_R):
        @pl.when(ki == SB_R * qi + d)
        def _(d=d):
            _sb_tile(q_ref, k_ref, v_ref, u_ref, acc_ref, carry_ref, d * SB_TK)

    @pl.when(ki < SB_R * qi)
    def _():
        _sb_tile(q_ref, k_ref, v_ref, u_ref, acc_ref, carry_ref, None)

    @pl.when(ki == 0)
    def _():
        o_ref[...] = acc_ref[...].astype(o_ref.dtype)


def _stick_breaking(proj, batch, seq):
    n = proj.shape[0]
    nq, nk = seq // SB_TQ, seq // SB_TK
    qi_tab = np.array([qi for qi in range(nq) for _ in range(SB_R * (qi + 1))], np.int32)
    ki_tab = np.array([ki for qi in range(nq) for ki in range(SB_R * (qi + 1) - 1, -1, -1)], np.int32)
    u = jnp.asarray((np.arange(SB_TK)[:, None] > np.arange(SB_TK)[None, :]).astype(np.float32), BF16)
    qb, kb, vb = COL_SBQ // MIX_WIDTH, COL_SBK // MIX_WIDTH, COL_SBV // MIX_WIDTH
    grid_spec = pltpu.PrefetchScalarGridSpec(
        num_scalar_prefetch=2,
        grid=(batch, len(qi_tab)),
        in_specs=[
            pl.BlockSpec((SB_TQ, MIX_WIDTH), lambda b, s, qt, kt: (b * nq + qt[s], qb)),
            pl.BlockSpec((SB_TK, MIX_WIDTH), lambda b, s, qt, kt: (b * nk + kt[s], kb)),
            pl.BlockSpec((SB_TK, MIX_WIDTH), lambda b, s, qt, kt: (b * nk + kt[s], vb)),
            pl.BlockSpec((SB_TK, SB_TK), lambda b, s, qt, kt: (0, 0)),
        ],
        out_specs=pl.BlockSpec((SB_TQ, MIX_WIDTH), lambda b, s, qt, kt: (b * nq + qt[s], 0)),
        scratch_shapes=[pltpu.VMEM((SB_TQ, MIX_WIDTH), F32), pltpu.VMEM((N_HEADS, SB_TQ, 1), F32)],
    )
    return pl.pallas_call(
        _sb_kernel,
        grid_spec=grid_spec,
        out_shape=jax.ShapeDtypeStruct((n, MIX_WIDTH), BF16),
        compiler_params=_cparams(("parallel", "arbitrary")),
        name="stick_breaking",
    )(jnp.asarray(qi_tab), jnp.asarray(ki_tab), proj, proj, proj, u)


def _compress_kernel(gk_ref, gv_ref, pos_ref, w1_ref, w2_ref, kc_ref, vc_ref):
    half = CMP_STRIDE * HEAD_DIM
    for t, (g_ref, o_ref) in enumerate(((gk_ref, kc_ref), (gv_ref, vc_ref))):
        g = g_ref[0].astype(F32)
        top = (g + pos_ref[t, :, :half]).astype(BF16)
        bot = (g + pos_ref[t, :, half:]).astype(BF16)
        a = jnp.dot(top, w1_ref[t, :half, :], preferred_element_type=F32)
        b = jnp.dot(bot, w1_ref[t, half:, :], preferred_element_type=F32)
        hid = jax.nn.gelu(a + pltpu.roll(b, b.shape[0] - 1, 0))
        out = jnp.dot(hid.astype(BF16), w2_ref[t], preferred_element_type=F32)
        o_ref[0] = jnp.concatenate([out, jnp.zeros_like(out)], axis=1).astype(o_ref.dtype)


def _compress(gk, gv, pos, w1, w2):
    b, m, width = gk.shape
    out_spec = pl.BlockSpec((1, m, LANES), lambda i: (i, 0, 0))
    out_shape = jax.ShapeDtypeStruct((b, m, LANES), BF16)
    return pl.pallas_call(
        _compress_kernel,
        grid=(b,),
        in_specs=[
            pl.BlockSpec((1, m, width), lambda i: (i, 0, 0)),
            pl.BlockSpec((1, m, width), lambda i: (i, 0, 0)),
            pl.BlockSpec((2, 1, CMP_LEN * HEAD_DIM), lambda i: (0, 0, 0)),
            pl.BlockSpec((2, CMP_LEN * HEAD_DIM, CMP_HIDDEN), lambda i: (0, 0, 0)),
            pl.BlockSpec((2, CMP_HIDDEN, HEAD_DIM), lambda i: (0, 0, 0)),
        ],
        out_specs=[out_spec, out_spec],
        out_shape=[out_shape, out_shape],
        compiler_params=_cparams(("parallel",)),
        name="nsa_compress",
    )(gk, gv, pos, w1, w2)


NSA_TQ = 256
NSA_CK = 512


NSA_ROWS = N_HEADS * NSA_TQ
SEL_OFF = 1e30


def _per_head(x, fn):
    return jnp.concatenate([fn(x[h * NSA_TQ:(h + 1) * NSA_TQ]) for h in range(N_HEADS)], axis=0)


def _nsa_kernel(q_ref, kx_ref, vx_ref, kw_ref, vw_ref, kb_ref, kc_ref, vc_ref, ovt_ref, o_ref, m_ref, acc_ref):
    seq = kx_ref.shape[0]
    n_sel = seq // SEL_LEN
    n_cmp = kc_ref.shape[1]
    q0 = pl.program_id(1) * NSA_TQ
    NT = (((1,), (1,)), ((), ()))
    lo_half = lax.broadcasted_iota(jnp.int32, (NSA_TQ, LANES), 1) < HEAD_DIM
    qpos_c = q0 + lax.broadcasted_iota(jnp.int32, (NSA_TQ, 1), 0)

    qs = []
    for pair in range(N_HEADS // 2):
        x = q_ref[:, pair * LANES:(pair + 1) * LANES].astype(F32)
        qs += [jnp.where(lo_half, x, 0.0), jnp.where(lo_half, pltpu.roll(x, HEAD_DIM, 1), 0.0)]
    q4 = jnp.concatenate(qs, axis=0).astype(BF16)

    ones_half = lax.broadcasted_iota(jnp.int32, (1, LANES), 1) >= HEAD_DIM

    def masked_exp(s, mask):
        s = _per_head(s, lambda t: jnp.where(mask, t, NEG))
        return jnp.exp2(s - jnp.max(s, axis=-1, keepdims=True))

    def attend(p, vals):
        return jnp.dot(p.astype(BF16), jnp.where(ones_half, 1.0, vals).astype(BF16), preferred_element_type=F32)

    def normalized(r):
        return r / jnp.where(ones_half, 1.0, pltpu.roll(r, HEAD_DIM, 1))

    cmp_end = lax.broadcasted_iota(jnp.int32, (NSA_TQ, n_cmp), 1) * CMP_STRIDE + (CMP_LEN - 1)
    p_cmp = masked_exp(lax.dot_general(q4, kc_ref[0], NT, preferred_element_type=F32), cmp_end <= qpos_c)
    has_cmp = jnp.concatenate([qpos_c >= CMP_LEN - 1] * N_HEADS, axis=0)
    p_cmp = p_cmp * jnp.where(has_cmp, 1.0 / jnp.sum(p_cmp, axis=-1, keepdims=True), 0.0)
    r_cmp = jnp.dot(p_cmp.astype(BF16), vc_ref[0], preferred_element_type=F32)
    p_sum = sum(p_cmp[h * NSA_TQ:(h + 1) * NSA_TQ] for h in range(N_HEADS))

    hi = p_sum.astype(BF16)
    lo = (p_sum - hi.astype(F32)).astype(BF16)
    imp = lax.dot_general(ovt_ref[...], jnp.concatenate([hi, lo], axis=1), NT, preferred_element_type=F32)
    blk = lax.broadcasted_iota(jnp.int32, (n_sel, NSA_TQ), 0)
    qpos_r = q0 + lax.broadcasted_iota(jnp.int32, (n_sel, NSA_TQ), 1)
    valid = blk * SEL_LEN <= qpos_r
    forced = (blk == 0) | (blk == jnp.right_shift(qpos_r, 6))
    score = jnp.where(valid, imp + jnp.where(forced, FORCE_SCORE, 0.0), -jnp.inf)
    groups = [score[8 * g:8 * g + 8] for g in range(n_sel // 8)]
    ranks = [jnp.zeros((8, NSA_TQ), F32) for _ in groups]
    row8 = lax.broadcasted_iota(jnp.int32, (8, NSA_TQ), 0)
    for i in range(n_sel):
        ci = jnp.broadcast_to(score[i:i + 1], (8, NSA_TQ))
        for g, sg in enumerate(groups):
            if 8 * g > i:
                beats = ci >= sg
            elif 8 * g + 7 < i:
                beats = ci > sg
            else:
                beats = (ci > sg) | ((ci == sg) & (row8 + 8 * g > i))
            ranks[g] = ranks[g] + jnp.where(beats, 1.0, 0.0)
    sel = jnp.where(valid & (jnp.concatenate(ranks, axis=0) < SEL_TOPN), 0.0, -SEL_OFF)
    sel = jnp.concatenate([sel, jnp.full((LANES - n_sel, NSA_TQ), -SEL_OFF, F32)], axis=0).T
    aug = lambda t: jnp.concatenate([q4, jnp.concatenate([t.astype(BF16)] * N_HEADS, axis=0)], axis=1)
    q_aug = aug(sel)
    blk_lane = lax.broadcasted_iota(jnp.int32, (NSA_TQ, LANES), 1)
    q_aug_past = aug(jnp.where(blk_lane * SEL_LEN >= q0, -SEL_OFF, sel))

    w_start = pl.multiple_of(jnp.maximum(q0 - WINDOW, 0), NSA_TQ)
    kwpos = w_start + lax.broadcasted_iota(jnp.int32, (NSA_TQ, WINDOW + NSA_TQ), 1)
    s_win = lax.dot_general(q4, kw_ref[pl.ds(w_start, WINDOW + NSA_TQ), :], NT, preferred_element_type=F32)
    p_win = masked_exp(s_win, (kwpos <= qpos_c) & (kwpos > qpos_c - WINDOW))
    r_win = normalized(attend(p_win, vw_ref[pl.ds(w_start, WINDOW + NSA_TQ), :]))

    def slab_stats(q, start, size, causal):
        keys = jnp.concatenate([kx_ref[pl.ds(start, size), :], kb_ref[pl.ds(start, size), :]], axis=1)
        s = lax.dot_general(q, keys, NT, preferred_element_type=F32)
        if causal:
            ok = start + lax.broadcasted_iota(jnp.int32, (NSA_TQ, size), 1) <= qpos_c
            s = _per_head(s, lambda t: jnp.where(ok, t, NEG))
        m_c = jnp.max(s, axis=-1, keepdims=True)
        return m_c, attend(jnp.exp2(s - m_c), vx_ref[pl.ds(start, size), :])

    m_d, a_d = slab_stats(q_aug, pl.multiple_of(q0, NSA_TQ), NSA_TQ, True)
    m_ref[...] = m_d
    acc_ref[...] = a_d

    def pair_body(i, carry):
        m0, a0 = slab_stats(q_aug_past, pl.multiple_of(2 * i * NSA_CK, NSA_CK), NSA_CK, False)
        m1, a1 = slab_stats(q_aug_past, pl.multiple_of((2 * i + 1) * NSA_CK, NSA_CK), NSA_CK, False)
        m_old = m_ref[...]
        m_new = jnp.maximum(m_old, jnp.maximum(m0, m1))
        acc_ref[...] = (jnp.exp2(m_old - m_new) * acc_ref[...] + jnp.exp2(m0 - m_new) * a0
                        + jnp.exp2(m1 - m_new) * a1)
        m_ref[...] = m_new
        return carry

    lax.fori_loop(0, (q0 + 2 * NSA_CK - 1) // (2 * NSA_CK), pair_body, 0)

    r_slc = normalized(acc_ref[...])
    gates = _sigmoid(vw_ref[pl.ds(pl.multiple_of(q0, NSA_TQ), NSA_TQ), :].astype(F32))
    g = [jnp.concatenate([gates[:, HEAD_DIM + 3 * h + t:HEAD_DIM + 3 * h + t + 1] for h in range(N_HEADS)], axis=0)
         for t in range(N_BRANCH)]
    y = g[0] * r_cmp + g[1] * r_slc + g[2] * r_win
    for pair in range(N_HEADS // 2):
        even = y[2 * pair * NSA_TQ:(2 * pair + 1) * NSA_TQ]
        odd = y[(2 * pair + 1) * NSA_TQ:(2 * pair + 2) * NSA_TQ]
        o_ref[:, pair * LANES:(pair + 1) * LANES] = jnp.where(lo_half, even, pltpu.roll(odd, HEAD_DIM, 1)).astype(o_ref.dtype)


def _nsa(proj, kc, vc, batch, seq):
    n = proj.shape[0]
    nqb = seq // NSA_TQ
    n_sel = seq // SEL_LEN
    n_cmp = kc.shape[1]
    assert n_sel % 8 == 0 and n_sel <= LANES and seq % (2 * NSA_CK) == 0
    cs = (np.arange(n_cmp) * CMP_STRIDE)[None, :]
    ss = (np.arange(n_sel) * SEL_LEN)[:, None]
    ovt = np.clip(np.minimum(cs + CMP_LEN, ss + SEL_LEN) - np.maximum(cs, ss), 0, None).astype(np.float32) / CMP_LEN
    ovt = jnp.asarray(np.concatenate([ovt, ovt], axis=1), BF16)
    kb = jnp.asarray((np.arange(seq)[:, None] // SEL_LEN == np.arange(LANES)[None, :]).astype(np.float32), BF16)
    seq_spec = lambda col: pl.BlockSpec((seq, LANES), lambda b, i: (b, col))
    cmp_spec = pl.BlockSpec((1, n_cmp, LANES), lambda b, i: (b, 0, 0))
    return pl.pallas_call(
        _nsa_kernel,
        grid=(batch, nqb),
        in_specs=[
            pl.BlockSpec((NSA_TQ, MIX_WIDTH), lambda b, i: (b * nqb + i, COL_CQ // MIX_WIDTH)),
            seq_spec(COL_KX // LANES),
            seq_spec(COL_VX // LANES),
            seq_spec(COL_KX // LANES + 1),
            seq_spec(COL_VX // LANES + 1),
            pl.BlockSpec((seq, LANES), lambda b, i: (0, 0)),
            cmp_spec, cmp_spec,
            pl.BlockSpec((n_sel, 2 * n_cmp), lambda b, i: (0, 0)),
        ],
        out_specs=pl.BlockSpec((NSA_TQ, MIX_WIDTH), lambda b, i: (b * nqb + i, 0)),
        out_shape=jax.ShapeDtypeStruct((n, MIX_WIDTH), BF16),
        scratch_shapes=[pltpu.VMEM((NSA_ROWS, 1), F32), pltpu.VMEM((NSA_ROWS, LANES), F32)],
        compiler_params=_cparams(("parallel", "arbitrary")),
        name="nsa",
    )(proj, proj, proj, proj, proj, kb, kc, vc, ovt)


def _merge_kernel(ya_ref, yb_ref, yc_ref, g0_ref, g1_ref, g2_ref, wb_ref, wo_ref, h_ref, o_ref):
    merged = None
    for y_ref, g_ref, t in ((ya_ref, g0_ref, 0), (yb_ref, g1_ref, 1), (yc_ref, g2_ref, 2)):
        term = _sigmoid(g_ref[...].astype(F32)) * jnp.dot(y_ref[...], wb_ref[t], preferred_element_type=F32)
        merged = term if merged is None else merged + term
    o_ref[...] = h_ref[...] + jnp.dot(merged.astype(BF16), wo_ref[...], preferred_element_type=F32)


def _merge(ya, yb, yc, proj, wb, wo, h, tm=512):
    n = h.shape[0]
    y_spec = pl.BlockSpec((tm, MIX_WIDTH), lambda i: (i, 0))
    g_spec = lambda t: pl.BlockSpec((tm, D_MODEL), lambda i: (i, COL_BRG // D_MODEL + t))
    return pl.pallas_call(
        _merge_kernel,
        grid=(n // tm,),
        in_specs=[y_spec, y_spec, y_spec, g_spec(0), g_spec(1), g_spec(2),
                  pl.BlockSpec((N_BRANCH, MIX_WIDTH, D_MODEL), lambda i: (0, 0, 0)),
                  pl.BlockSpec((D_MODEL, D_MODEL), lambda i: (0, 0)),
                  pl.BlockSpec((tm, D_MODEL), lambda i: (i, 0))],
        out_specs=pl.BlockSpec((tm, D_MODEL), lambda i: (i, 0)),
        out_shape=jax.ShapeDtypeStruct((n, D_MODEL), F32),
        compiler_params=_cparams(("parallel",)),
        name="merge",
    )(ya, yb, yc, proj, proj, proj, wb, wo, h)


def _ffn_kernel(h_ref, g_ref, wg_ref, wu_ref, wd_ref, o_ref, xn_ref, acc_ref):
    f = pl.program_id(1)

    @pl.when(f == 0)
    def _():
        xn_ref[...] = _rms(h_ref[...], g_ref[...]).astype(BF16)
        acc_ref[...] = h_ref[...]

    xn = xn_ref[...]
    a = jnp.dot(xn, wg_ref[...], preferred_element_type=F32)
    u = jnp.dot(xn, wu_ref[...], preferred_element_type=F32)
    hid = (a * _sigmoid(a) * u).astype(BF16)
    acc_ref[...] += jnp.dot(hid, wd_ref[...], preferred_element_type=F32)

    @pl.when(f == pl.num_programs(1) - 1)
    def _():
        o_ref[...] = acc_ref[...]


def _ffn(h, g, wg, wu, wd, tm=1024, tf=256):
    n = h.shape[0]
    dff = wg.shape[1]
    return pl.pallas_call(
        _ffn_kernel,
        grid=(n // tm, dff // tf),
        in_specs=[
            pl.BlockSpec((tm, D_MODEL), lambda i, f: (i, 0)),
            pl.BlockSpec((1, D_MODEL), lambda i, f: (0, 0)),
            pl.BlockSpec((D_MODEL, tf), lambda i, f: (0, f)),
            pl.BlockSpec((D_MODEL, tf), lambda i, f: (0, f)),
            pl.BlockSpec((tf, D_MODEL), lambda i, f: (f, 0)),
        ],
        out_specs=pl.BlockSpec((tm, D_MODEL), lambda i, f: (i, 0)),
        out_shape=jax.ShapeDtypeStruct((n, D_MODEL), F32),
        scratch_shapes=[pltpu.VMEM((tm, D_MODEL), BF16), pltpu.VMEM((tm, D_MODEL), F32)],
        compiler_params=_cparams(("parallel", "arbitrary")),
        name="ffn",
    )(h, g, wg, wu, wd)


RT_TM = 1024
RT_TF = 896
RT_TD = 512


def _gate_kernel(h_ref, g_ref, rw_ref, gate_ref):
    tm = h_ref.shape[0]
    lane = lax.broadcasted_iota(jnp.int32, (tm, LANES), 1)
    xn = _rms(h_ref[...], g_ref[...])
    logits = jnp.dot(xn, rw_ref[...], preferred_element_type=F32, precision=lax.Precision.HIGHEST)
    logits = jnp.where(lane < N_EXPERTS, logits, -jnp.inf)
    m1 = jnp.max(logits, axis=-1, keepdims=True)
    i1 = jnp.min(jnp.where(logits == m1, lane, LANES), axis=-1, keepdims=True)
    rest = jnp.where(lane == i1, -jnp.inf, logits)
    m2 = jnp.max(rest, axis=-1, keepdims=True)
    i2 = jnp.min(jnp.where(rest == m2, lane, LANES), axis=-1, keepdims=True)
    e2 = jnp.exp(m2 - m1)
    gate_ref[...] = jnp.where(lane == i1, 1.0 / (1.0 + e2), 0.0) + jnp.where(lane == i2, e2 / (1.0 + e2), 0.0)


def _gate(h, g, rw, tm=1024):
    n = h.shape[0]
    return pl.pallas_call(
        _gate_kernel,
        grid=(n // tm,),
        in_specs=[
            pl.BlockSpec((tm, D_MODEL), lambda i: (i, 0)),
            pl.BlockSpec((1, D_MODEL), lambda i: (0, 0)),
            pl.BlockSpec((D_MODEL, LANES), lambda i: (0, 0)),
        ],
        out_specs=pl.BlockSpec((tm, LANES), lambda i: (i, 0)),
        out_shape=jax.ShapeDtypeStruct((n, LANES), F32),
        compiler_params=_cparams(("parallel",)),
        name="moe_gate",
    )(h, g, rw)


def _row_copy(src_ref, src_row, dst_ref, dst_row, sem):
    return pltpu.make_async_copy(src_ref.at[pl.ds(src_row, 1), :], dst_ref.at[pl.ds(dst_row, 1), :], sem)


def _dispatch_kernel(pad_ref, pa_ref, pb_ref, h_ref, xs_ref, zero_ref, sems):
    i = pl.program_id(0)

    def send(t, carry):
        _row_copy(h_ref, t, xs_ref, pa_ref[0, 0, t], sems.at[0]).start()
        _row_copy(h_ref, t, xs_ref, pb_ref[0, 0, t], sems.at[1]).start(priority=1)
        return carry

    lax.fori_loop(0, RT_TD, send, 0, unroll=4)

    @pl.when(i == 0)
    def _():
        zero_ref[...] = jnp.zeros_like(zero_ref)
        for e in range(N_EXPERTS + 1):
            lo, hi = pad_ref[2 * e], pad_ref[2 * e + 1]
            lax.fori_loop(lo, hi, lambda r, c: (_row_copy(zero_ref, 0, xs_ref, r, sems.at[2]).start(), c)[1], 0)
        for e in range(N_EXPERTS + 1):
            lo, hi = pad_ref[2 * e], pad_ref[2 * e + 1]
            lax.fori_loop(lo, hi, lambda r, c: (_row_copy(zero_ref, 0, xs_ref, r, sems.at[2]).wait(), c)[1], 0)

    for s in range(2):
        pltpu.make_async_copy(h_ref, xs_ref.at[pl.ds(0, RT_TD), :], sems.at[s]).wait()


def _dispatch(pad_bounds, pa, pb, h, n_rows):
    n = h.shape[0]
    idx_spec = pl.BlockSpec((1, 1, RT_TD), lambda i, *_: (i, 0, 0), memory_space=pltpu.SMEM)
    grid_spec = pltpu.PrefetchScalarGridSpec(
        num_scalar_prefetch=1,
        grid=(n // RT_TD,),
        in_specs=[idx_spec, idx_spec, pl.BlockSpec((RT_TD, D_MODEL), lambda i, *_: (i, 0))],
        out_specs=pl.BlockSpec(memory_space=pl.ANY),
        scratch_shapes=[pltpu.VMEM((8, D_MODEL), F32), pltpu.SemaphoreType.DMA((3,))],
    )
    return pl.pallas_call(
        _dispatch_kernel,
        grid_spec=grid_spec,
        out_shape=jax.ShapeDtypeStruct((n_rows, D_MODEL), F32),
        compiler_params=_cparams(("arbitrary",)),
        name="moe_dispatch",
    )(pad_bounds, pa.reshape(-1, 1, RT_TD), pb.reshape(-1, 1, RT_TD), h)


def _sorted_experts_kernel(te_ref, tx_ref, live_ref, xs_ref, g_ref, wg_ref, wu_ref, wd_ref, y_ref, xb_ref):
    i, f = pl.program_id(0), pl.program_id(1)

    @pl.when(f == 0)
    def _():
        xb_ref[...] = _rms(xs_ref[...], g_ref[...]).astype(BF16)
        y_ref[...] = jnp.zeros_like(y_ref)

    @pl.when(live_ref[i] > 0)
    def _():
        x = xb_ref[...]
        a = jnp.dot(x, wg_ref[0], preferred_element_type=F32)
        u = jnp.dot(x, wu_ref[0], preferred_element_type=F32)
        y_ref[...] += jnp.dot((a * _sigmoid(a) * u).astype(BF16), wd_ref[0], preferred_element_type=F32)


def _sorted_experts(tile_expert, tile_src, tile_live, xs, g, wg, wu, wd):
    n_rows = xs.shape[0]
    ne, _, dff = wg.shape
    grid_spec = pltpu.PrefetchScalarGridSpec(
        num_scalar_prefetch=3,
        grid=(n_rows // RT_TM, dff // RT_TF),
        in_specs=[
            pl.BlockSpec((RT_TM, D_MODEL), lambda i, f, te, tx, lv: (tx[i], 0)),
            pl.BlockSpec((1, D_MODEL), lambda i, f, *_: (0, 0)),
            pl.BlockSpec((1, D_MODEL, RT_TF), lambda i, f, te, tx, lv: (te[i], 0, f)),
            pl.BlockSpec((1, D_MODEL, RT_TF), lambda i, f, te, tx, lv: (te[i], 0, f)),
            pl.BlockSpec((1, RT_TF, D_MODEL), lambda i, f, te, tx, lv: (te[i], f, 0)),
        ],
        out_specs=pl.BlockSpec((RT_TM, D_MODEL), lambda i, f, *_: (i, 0)),
        scratch_shapes=[pltpu.VMEM((RT_TM, D_MODEL), BF16)],
    )
    return pl.pallas_call(
        _sorted_experts_kernel,
        grid_spec=grid_spec,
        out_shape=jax.ShapeDtypeStruct((n_rows, D_MODEL), F32),
        compiler_params=_cparams(("parallel", "arbitrary")),
        name="moe_experts",
    )(tile_expert, tile_src, tile_live, xs, g, wg, wu, wd)


def _combine_kernel(pa_ref, pb_ref, pa_next_ref, pb_next_ref, ga_ref, gb_ref, h_ref, y_ref, o_ref, ya_ref, yb_ref, sems):
    i, n_steps = pl.program_id(0), pl.num_programs(0)
    slot = i % 2

    def fetch(pa, pb, s):
        def body(t, carry):
            _row_copy(y_ref, pa[0, 0, t], ya_ref.at[s], t, sems.at[0, s]).start()
            _row_copy(y_ref, pb[0, 0, t], yb_ref.at[s], t, sems.at[1, s]).start(priority=1)
            return carry

        lax.fori_loop(0, RT_TD, body, 0, unroll=4)

    @pl.when(i == 0)
    def _():
        fetch(pa_ref, pb_ref, 0)

    @pl.when(i + 1 < n_steps)
    def _():
        fetch(pa_next_ref, pb_next_ref, 1 - slot)

    pltpu.make_async_copy(y_ref.at[pl.ds(0, RT_TD), :], ya_ref.at[slot], sems.at[0, slot]).wait()
    pltpu.make_async_copy(y_ref.at[pl.ds(0, RT_TD), :], yb_ref.at[slot], sems.at[1, slot]).wait()
    o_ref[...] = h_ref[...] + ga_ref[...] * ya_ref[slot] + gb_ref[...] * yb_ref[slot]


def _combine(pa, pb, ga, gb, h, y):
    n = h.shape[0]
    n_steps = n // RT_TD
    idx_spec = pl.BlockSpec((1, 1, RT_TD), lambda i: (i, 0, 0), memory_space=pltpu.SMEM)
    next_spec = pl.BlockSpec((1, 1, RT_TD), lambda i: (jnp.minimum(i + 1, n_steps - 1), 0, 0), memory_space=pltpu.SMEM)
    col_spec = pl.BlockSpec((RT_TD, 1), lambda i: (i, 0))
    row_spec = pl.BlockSpec((RT_TD, D_MODEL), lambda i: (i, 0))
    pa, pb = pa.reshape(-1, 1, RT_TD), pb.reshape(-1, 1, RT_TD)
    return pl.pallas_call(
        _combine_kernel,
        grid=(n_steps,),
        in_specs=[idx_spec, idx_spec, next_spec, next_spec, col_spec, col_spec, row_spec,
                  pl.BlockSpec(memory_space=pl.ANY)],
        out_specs=row_spec,
        out_shape=jax.ShapeDtypeStruct((n, D_MODEL), F32),
        scratch_shapes=[pltpu.VMEM((2, RT_TD, D_MODEL), F32), pltpu.VMEM((2, RT_TD, D_MODEL), F32),
                        pltpu.SemaphoreType.DMA((2, 2))],
        compiler_params=_cparams(("arbitrary",)),
        name="moe_combine",
    )(pa, pb, pa, pb, ga, gb, h, y)


def _moe_sorted(h, g, rw, wg, wu, wd):
    n = h.shape[0]
    n_tiles = 2 * n // RT_TM + N_EXPERTS
    gate = _gate(h, g, rw)[:, :N_EXPERTS]
    routed = gate > 0
    r = routed.astype(jnp.int32)
    incl = jnp.cumsum(r, axis=0)
    count = incl[-1]
    padded = (count + RT_TM - 1) // RT_TM * RT_TM
    ends = jnp.cumsum(padded)
    pos = (ends - padded)[None, :] + incl - 1
    order = jnp.cumsum(r, axis=1)
    first, second = routed & (order == 1), routed & (order == 2)
    pick = lambda m, v: jnp.sum(jnp.where(m, v, 0), axis=1)
    pa = pick(first, pos)
    pb = jnp.where(jnp.any(second, axis=1), pick(second, pos), pa)
    ga, gb = pick(first, gate)[:, None], pick(second, gate)[:, None]
    pad_lo = jnp.concatenate([ends - padded + count, ends[-1:]])
    pad_hi = jnp.concatenate([ends, jnp.full((1,), n_tiles * RT_TM, ends.dtype)])
    pad_bounds = jnp.stack([pad_lo, pad_hi], axis=1).reshape(-1).astype(jnp.int32)
    tile_start = jnp.arange(n_tiles, dtype=jnp.int32) * RT_TM
    tile_live = (tile_start < ends[-1]).astype(jnp.int32)
    tile_expert = jnp.minimum(jnp.sum(tile_start[:, None] >= ends[None, :], axis=1), N_EXPERTS - 1).astype(jnp.int32)
    tile_src = jnp.minimum(jnp.arange(n_tiles, dtype=jnp.int32), ends[-1] // RT_TM - 1)
    xs = _dispatch(pad_bounds, pa.astype(jnp.int32), pb.astype(jnp.int32), h, n_tiles * RT_TM)
    y = _sorted_experts(tile_expert, tile_src, tile_live, xs, g, wg, wu, wd)
    return _combine(pa.astype(jnp.int32), pb.astype(jnp.int32), ga, gb, h, y)


def _final_norm_kernel(h_ref, g_ref, o_ref):
    o_ref[...] = _rms(h_ref[...], g_ref[...])


def _final_norm(h, g, tm=1024):
    n = h.shape[0]
    return pl.pallas_call(
        _final_norm_kernel,
        grid=(n // tm,),
        in_specs=[pl.BlockSpec((tm, D_MODEL), lambda i: (i, 0)), pl.BlockSpec((1, D_MODEL), lambda i: (0, 0))],
        out_specs=pl.BlockSpec((tm, D_MODEL), lambda i: (i, 0)),
        out_shape=jax.ShapeDtypeStruct((n, D_MODEL), F32),
        compiler_params=_cparams(("parallel",)),
        name="final_norm",
    )(h, g)


def _rope_tables(seq):
    half = ROPE_DIMS // 2
    inv_freq = ROPE_THETA ** (-jnp.arange(0, ROPE_DIMS, 2, dtype=F32) / ROPE_DIMS)
    ang = jnp.arange(seq, dtype=F32)[:, None] * inv_freq[None, :]
    cos, sin = jnp.cos(ang), jnp.sin(ang)
    ones = jnp.ones((seq, HEAD_DIM - ROPE_DIMS), F32)
    zeros_h = jnp.zeros((seq, half), F32)
    zeros_r = jnp.zeros((seq, HEAD_DIM - ROPE_DIMS), F32)
    c = jnp.concatenate([cos, cos, ones], axis=1)
    s1 = jnp.concatenate([-sin, zeros_h, zeros_r], axis=1)
    s2 = jnp.concatenate([zeros_h, sin, zeros_r], axis=1)
    rep = LANES // HEAD_DIM
    return tuple(jnp.tile(t, (1, rep)) for t in (c, s1, s2))


def _layout_w_in(w):
    scale = HEAD_DIM ** -0.5 * LOG2E
    a_in, sb_q, sb_k, sb_v, c_q, c_kv, c_g, br_g = jnp.split(
        w, np.cumsum([2 * MIX_WIDTH, MIX_WIDTH, MIX_WIDTH, MIX_WIDTH, MIX_WIDTH, 6 * HEAD_DIM, 3 * N_HEADS])[:].tolist(),
        axis=-1)
    k_cmp, v_cmp, k_slc, v_slc, k_win, v_win = jnp.split(c_kv, 6, axis=-1)
    zeros = lambda width: jnp.zeros((w.shape[0], width), w.dtype)
    cols = [a_in, sb_q * scale, sb_k, sb_v,
            v_slc, v_cmp, v_win, c_g, zeros(HEAD_DIM - 3 * N_HEADS),
            c_q * scale, k_slc, k_cmp, k_win, zeros(HEAD_DIM),
            br_g]
    out = jnp.concatenate(cols, axis=-1)
    assert out.shape[-1] == PROJ_W
    return out.astype(BF16)


def kernel(x, norm1_g, w_in, sgu_norm_g, sgu_w, sgu_b, cmp_pos, cmp_w1, cmp_w2, w_branch, w_out, norm2_g,
           ffn_w_gate, ffn_w_up, ffn_w_down, router_w, moe_w_gate, moe_w_up, moe_w_down, final_norm_g):
    batch, seq, _ = x.shape
    n = batch * seq
    depth = norm1_g.shape[0]
    rope_c, rope_s1, rope_s2 = _rope_tables(seq)
    dff_pad = -(-D_FF // 256) * 256
    h = x.reshape(n, D_MODEL)
    for layer in range(depth):
        proj = _norm_proj(h, norm1_g[layer][None, :], _layout_w_in(w_in[layer]), rope_c, rope_s1, rope_s2, seq)
        sgu_bias = jnp.repeat(sgu_b[layer].T, HEAD_DIM, axis=1)
        y_a = _sgu(proj, sgu_norm_g[layer][None, :], sgu_w[layer], sgu_bias)
        y_b = _stick_breaking(proj, batch, seq)
        groups = seq // CMP_STRIDE
        gk = proj[:, COL_KX + HEAD_DIM:COL_KX + 2 * HEAD_DIM].reshape(batch, groups, CMP_STRIDE * HEAD_DIM)
        gv = proj[:, COL_VX + HEAD_DIM:COL_VX + 2 * HEAD_DIM].reshape(batch, groups, CMP_STRIDE * HEAD_DIM)
        kc, vc = _compress(gk, gv, cmp_pos[layer].reshape(2, 1, CMP_LEN * HEAD_DIM),
                           cmp_w1[layer].astype(BF16), cmp_w2[layer].astype(BF16))
        y_c = _nsa(proj, kc, vc, batch, seq)
        h = _merge(y_a, y_b, y_c, proj, w_branch[layer].astype(BF16), w_out[layer].astype(BF16), h)
        j = layer // 2
        g2 = norm2_g[layer][None, :]
        if layer % 2 == 0:
            pad = dff_pad - D_FF
            wg = jnp.pad(ffn_w_gate[j], ((0, 0), (0, pad))).astype(BF16)
            wu = jnp.pad(ffn_w_up[j], ((0, 0), (0, pad))).astype(BF16)
            wd = jnp.pad(ffn_w_down[j], ((0, pad), (0, 0))).astype(BF16)
            h = _ffn(h, g2, wg, wu, wd)
        else:
            rw = jnp.pad(router_w[j], ((0, 0), (0, LANES - N_EXPERTS)))
            h = _moe_sorted(h, g2, rw, moe_w_gate[j].astype(BF16), moe_w_up[j].astype(BF16), moe_w_down[j].astype(BF16))
    return _final_norm(h, final_norm_g[None, :]).reshape(batch, seq, D_MODEL)
```

```python
import functools

import numpy as np
import jax
import jax.numpy as jnp
from jax import lax
from jax.experimental import pallas as pl
from jax.experimental.pallas import tpu as pltpu

F32 = jnp.float32
BF16 = jnp.bfloat16

D_MODEL = 1024
HEAD_DIM = 64
N_HEADS = 4
MIX_WIDTH = N_HEADS * HEAD_DIM
ROPE_DIMS = HEAD_DIM // 4
ROPE_THETA = 500000.0
EPS = 1e-6
SGU_CHUNK = 128
CMP_LEN = 32
CMP_STRIDE = 16
CMP_HIDDEN = 256
SEL_LEN = 64
SEL_TOPN = 16
WINDOW = 512
FORCE_SCORE = 1e4
N_BRANCH = 3
D_FF = 2752
N_EXPERTS = 8
D_FF_EXPERT = 3584
NEG = -1e30
LOG2E = 1.4426950408889634

LANES = 128
VMEM_LIMIT = 56 * 1024 * 1024

PROJ_TILE = 1024
COL_A = 0
COL_SBQ, COL_SBK, COL_SBV = 512, 768, 1024
COL_VX = 1280
COL_CQ = 1536
COL_KX = 1792
COL_BRG = 2048
PROJ_W = COL_BRG + N_BRANCH * D_MODEL
ROPE_TILE = COL_CQ // PROJ_TILE
ROPE_START = COL_CQ % PROJ_TILE
assert COL_BRG == (ROPE_TILE + 1) * PROJ_TILE


def _cparams(sem):
    return pltpu.CompilerParams(dimension_semantics=sem, vmem_limit_bytes=VMEM_LIMIT)


def _sigmoid(x):
    return 1.0 / (1.0 + jnp.exp(-x))


def _rms(x, g):
    return x * lax.rsqrt(jnp.mean(x * x, axis=-1, keepdims=True) + EPS) * g


def _norm_proj_kernel(h_ref, g_ref, w_ref, c_ref, s1_ref, s2_ref, o_ref, xn_ref):
    j = pl.program_id(1)

    @pl.when(j == 0)
    def _():
        xn_ref[...] = _rms(h_ref[...], g_ref[...]).astype(BF16)

    acc = jnp.dot(xn_ref[...], w_ref[...], preferred_element_type=F32)

    @pl.when(j != ROPE_TILE)
    def _():
        o_ref[...] = acc.astype(o_ref.dtype)

    @pl.when(j == ROPE_TILE)
    def _():
        c, s1, s2 = c_ref[...], s1_ref[...], s2_ref[...]
        o_ref[:, :ROPE_START] = acc[:, :ROPE_START].astype(o_ref.dtype)
        for g in range(ROPE_START // LANES, PROJ_TILE // LANES):
            x = acc[:, g * LANES:(g + 1) * LANES]
            y = x * c + pltpu.roll(x, LANES - 8, 1) * s1 + pltpu.roll(x, 8, 1) * s2
            o_ref[:, g * LANES:(g + 1) * LANES] = y.astype(o_ref.dtype)


def _norm_proj(h, g, w, rope_c, rope_s1, rope_s2, seq, tm=2048):
    n = h.shape[0]
    assert seq % tm == 0, "a projection tile must not straddle two sequences (rotary tables are per position)"
    nseq = seq // tm
    return pl.pallas_call(
        _norm_proj_kernel,
        grid=(n // tm, PROJ_W // PROJ_TILE),
        in_specs=[
            pl.BlockSpec((tm, D_MODEL), lambda i, j: (i, 0)),
            pl.BlockSpec((1, D_MODEL), lambda i, j: (0, 0)),
            pl.BlockSpec((D_MODEL, PROJ_TILE), lambda i, j: (0, j)),
            pl.BlockSpec((tm, LANES), lambda i, j: (i % nseq, 0)),
            pl.BlockSpec((tm, LANES), lambda i, j: (i % nseq, 0)),
            pl.BlockSpec((tm, LANES), lambda i, j: (i % nseq, 0)),
        ],
        out_specs=pl.BlockSpec((tm, PROJ_TILE), lambda i, j: (i, j)),
        out_shape=jax.ShapeDtypeStruct((n, PROJ_W), BF16),
        scratch_shapes=[pltpu.VMEM((tm, D_MODEL), BF16)],
        compiler_params=_cparams(("parallel", "arbitrary")),
        name="norm_proj",
    )(h, g, w, rope_c, rope_s1, rope_s2)


def _sgu_kernel(z_ref, g_ref, w_ref, b_ref, o_ref):
    tm = z_ref.shape[0]
    a = jax.nn.gelu(z_ref[...].astype(F32))
    u = a[:, :MIX_WIDTH]
    v = _rms(a[:, MIX_WIDTH:], g_ref[...]).astype(BF16)
    row = lax.broadcasted_iota(jnp.int32, (SGU_CHUNK, SGU_CHUNK), 0)
    col = lax.broadcasted_iota(jnp.int32, (SGU_CHUNK, SGU_CHUNK), 1)
    ws = [jnp.where(row >= col, w_ref[gi], 0.0).astype(BF16) for gi in range(N_HEADS)]
    bias = b_ref[...]
    for c in range(tm // SGU_CHUNK):
        rows = slice(c * SGU_CHUNK, (c + 1) * SGU_CHUNK)
        mix = jnp.concatenate(
            [jnp.dot(ws[gi], v[rows, gi * HEAD_DIM:(gi + 1) * HEAD_DIM], preferred_element_type=F32)
             for gi in range(N_HEADS)], axis=1)
        o_ref[rows, :] = (u[rows, :] * (mix + bias)).astype(o_ref.dtype)


def _sgu(proj, g, w, bias, tm=512):
    n = proj.shape[0]
    return pl.pallas_call(
        _sgu_kernel,
        grid=(n // tm,),
        in_specs=[
            pl.BlockSpec((tm, 2 * MIX_WIDTH), lambda i: (i, COL_A // (2 * MIX_WIDTH))),
            pl.BlockSpec((1, MIX_WIDTH), lambda i: (0, 0)),
            pl.BlockSpec((N_HEADS, SGU_CHUNK, SGU_CHUNK), lambda i: (0, 0, 0)),
            pl.BlockSpec((SGU_CHUNK, MIX_WIDTH), lambda i: (0, 0)),
        ],
        out_specs=pl.BlockSpec((tm, MIX_WIDTH), lambda i: (i, 0)),
        out_shape=jax.ShapeDtypeStruct((n, MIX_WIDTH), BF16),
        compiler_params=_cparams(("parallel",)),
        name="sgu",
    )(proj, g, w, bias)


SB_TQ = 1024
SB_TK = 256
SB_R = SB_TQ // SB_TK


def _sb_tile(q_ref, k_ref, v_ref, u_ref, acc_ref, carry_ref, k_off):
    diag = k_off is not None
    rows = slice(k_off if diag else 0, SB_TQ)
    n_rows = rows.stop - rows.start
    q, k, v = q_ref[rows, :], k_ref[...], v_ref[...]
    u = u_ref[...]
    if diag:
        mask = (lax.broadcasted_iota(jnp.int32, (n_rows, SB_TK), 1)
                < lax.broadcasted_iota(jnp.int32, (n_rows, SB_TK), 0))
    pvs = []
    for h in range(N_HEADS):
        hs = slice(h * HEAD_DIM, (h + 1) * HEAD_DIM)
        z = lax.dot_general(q[:, hs], k[:, hs], (((1,), (1,)), ((), ())), preferred_element_type=F32)
        log_beta = jnp.minimum(z, 0.0) - jnp.log(1.0 + jnp.exp2(-jnp.abs(z))) * LOG2E
        log_1m = log_beta - z
        if diag:
            log_1m = jnp.where(mask, log_1m, 0.0)
        carry = carry_ref[h, rows, :]
        terms = log_1m.astype(BF16)
        r = jnp.dot(terms, u, preferred_element_type=F32)
        a = jnp.exp2(log_beta + r + carry)
        if diag:
            a = jnp.where(mask, a, 0.0)
        carry_ref[h, rows, :] = carry + r[:, 0:1] + terms[:, 0:1].astype(F32)
        pvs.append(jnp.dot(a.astype(BF16), v[:, hs], preferred_element_type=F32))
    acc_ref[rows, :] += jnp.concatenate(pvs, axis=1)


def _sb_kernel(qi_ref, ki_ref, q_ref, k_ref, v_ref, u_ref, o_ref, acc_ref, carry_ref):
    step = pl.program_id(1)
    qi = qi_ref[step]
    ki = ki_ref[step]

    @pl.when(ki == SB_R * qi + SB_R - 1)
    def _():
        acc_ref[...] = jnp.zeros_like(acc_ref)
        carry_ref[...] = jnp.zeros_like(carry_ref)

    for d in range(SB_R):
        @pl.when(ki == SB_R * qi + d)
        def _(d=d):
            _sb_tile(q_ref, k_ref, v_ref, u_ref, acc_ref, carry_ref, d * SB_TK)

    @pl.when(ki < SB_R * qi)
    def _():
        _sb_tile(q_ref, k_ref, v_ref, u_ref, acc_ref, carry_ref, None)

    @pl.when(ki == 0)
    def _():
        o_ref[...] = acc_ref[...].astype(o_ref.dtype)


def _stick_breaking(proj, batch, seq):
    n = proj.shape[0]
    nq, nk = seq // SB_TQ, seq // SB_TK
    qi_tab = np.array([qi for qi in range(nq) for _ in range(SB_R * (qi + 1))], np.int32)
    ki_tab = np.array([ki for qi in range(nq) for ki in range(SB_R * (qi + 1) - 1, -1, -1)], np.int32)
    u = jnp.asarray((np.arange(SB_TK)[:, None] > np.arange(SB_TK)[None, :]).astype(np.float32), BF16)
    qb, kb, vb = COL_SBQ // MIX_WIDTH, COL_SBK // MIX_WIDTH, COL_SBV // MIX_WIDTH
    grid_spec = pltpu.PrefetchScalarGridSpec(
        num_scalar_prefetch=2,
        grid=(batch, len(qi_tab)),
        in_specs=[
            pl.BlockSpec((SB_TQ, MIX_WIDTH), lambda b, s, qt, kt: (b * nq + qt[s], qb)),
            pl.BlockSpec((SB_TK, MIX_WIDTH), lambda b, s, qt, kt: (b * nk + kt[s], kb)),
            pl.BlockSpec((SB_TK, MIX_WIDTH), lambda b, s, qt, kt: (b * nk + kt[s], vb)),
            pl.BlockSpec((SB_TK, SB_TK), lambda b, s, qt, kt: (0, 0)),
        ],
        out_specs=pl.BlockSpec((SB_TQ, MIX_WIDTH), lambda b, s, qt, kt: (b * nq + qt[s], 0)),
        scratch_shapes=[pltpu.VMEM((SB_TQ, MIX_WIDTH), F32), pltpu.VMEM((N_HEADS, SB_TQ, 1), F32)],
    )
    return pl.pallas_call(
        _sb_kernel,
        grid_spec=grid_spec,
        out_shape=jax.ShapeDtypeStruct((n, MIX_WIDTH), BF16),
        compiler_params=_cparams(("parallel", "arbitrary")),
        name="stick_breaking",
    )(jnp.asarray(qi_tab), jnp.asarray(ki_tab), proj, proj, proj, u)


def _compress_kernel(gk_ref, gv_ref, pos_ref, w1_ref, w2_ref, kc_ref, vc_ref):
    half = CMP_STRIDE * HEAD_DIM
    for t, (g_ref, o_ref) in enumerate(((gk_ref, kc_ref), (gv_ref, vc_ref))):
        g = g_ref[0].astype(F32)
        top = (g + pos_ref[t, :, :half]).astype(BF16)
        bot = (g + pos_ref[t, :, half:]).astype(BF16)
        a = jnp.dot(top, w1_ref[t, :half, :], preferred_element_type=F32)
        b = jnp.dot(bot, w1_ref[t, half:, :], preferred_element_type=F32)
        hid = jax.nn.gelu(a + pltpu.roll(b, b.shape[0] - 1, 0))
        out = jnp.dot(hid.astype(BF16), w2_ref[t], preferred_element_type=F32)
        o_ref[0] = jnp.concatenate([out, jnp.zeros_like(out)], axis=1).astype(o_ref.dtype)


def _compress(gk, gv, pos, w1, w2):
    b, m, width = gk.shape
    out_spec = pl.BlockSpec((1, m, LANES), lambda i: (i, 0, 0))
    out_shape = jax.ShapeDtypeStruct((b, m, LANES), BF16)
    return pl.pallas_call(
        _compress_kernel,
        grid=(b,),
        in_specs=[
            pl.BlockSpec((1, m, width), lambda i: (i, 0, 0)),
            pl.BlockSpec((1, m, width), lambda i: (i, 0, 0)),
            pl.BlockSpec((2, 1, CMP_LEN * HEAD_DIM), lambda i: (0, 0, 0)),
            pl.BlockSpec((2, CMP_LEN * HEAD_DIM, CMP_HIDDEN), lambda i: (0, 0, 0)),
            pl.BlockSpec((2, CMP_HIDDEN, HEAD_DIM), lambda i: (0, 0, 0)),
        ],
        out_specs=[out_spec, out_spec],
        out_shape=[out_shape, out_shape],
        compiler_params=_cparams(("parallel",)),
        name="nsa_compress",
    )(gk, gv, pos, w1, w2)


NSA_TQ = 256
NSA_CK = 512


NSA_ROWS = N_HEADS * NSA_TQ
SEL_OFF = 1e30


def _per_head(x, fn):
    return jnp.concatenate([fn(x[h * NSA_TQ:(h + 1) * NSA_TQ]) for h in range(N_HEADS)], axis=0)


def _nsa_kernel(q_ref, kx_ref, vx_ref, kw_ref, vw_ref, kb_ref, kc_ref, vc_ref, ovt_ref, o_ref, m_ref, acc_ref):
    seq = kx_ref.shape[0]
    n_sel = seq // SEL_LEN
    n_cmp = kc_ref.shape[1]
    q0 = pl.program_id(1) * NSA_TQ
    NT = (((1,), (1,)), ((), ()))
    lo_half = lax.broadcasted_iota(jnp.int32, (NSA_TQ, LANES), 1) < HEAD_DIM
    qpos_c = q0 + lax.broadcasted_iota(jnp.int32, (NSA_TQ, 1), 0)

    qs = []
    for pair in range(N_HEADS // 2):
        x = q_ref[:, pair * LANES:(pair + 1) * LANES].astype(F32)
        qs += [jnp.where(lo_half, x, 0.0), jnp.where(lo_half, pltpu.roll(x, HEAD_DIM, 1), 0.0)]
    q4 = jnp.concatenate(qs, axis=0).astype(BF16)

    ones_half = lax.broadcasted_iota(jnp.int32, (1, LANES), 1) >= HEAD_DIM

    def masked_exp(s, mask):
        bias = jnp.where(mask, 0.0, NEG)
        s = _per_head(s, lambda t: t + bias)
        return jnp.exp2(s - jnp.max(s, axis=-1, keepdims=True))

    def attend(p, vals):
        return jnp.dot(p.astype(BF16), jnp.where(ones_half, 1.0, vals).astype(BF16), preferred_element_type=F32)

    def normalized(r):
        return r / jnp.where(ones_half, 1.0, pltpu.roll(r, HEAD_DIM, 1))

    cmp_end = lax.broadcasted_iota(jnp.int32, (NSA_TQ, n_cmp), 1) * CMP_STRIDE + (CMP_LEN - 1)
    p_cmp = masked_exp(lax.dot_general(q4, kc_ref[0], NT, preferred_element_type=F32), cmp_end <= qpos_c)
    has_cmp = jnp.concatenate([qpos_c >= CMP_LEN - 1] * N_HEADS, axis=0)
    p_cmp = p_cmp * jnp.where(has_cmp, 1.0 / jnp.sum(p_cmp, axis=-1, keepdims=True), 0.0)
    r_cmp = jnp.dot(p_cmp.astype(BF16), vc_ref[0], preferred_element_type=F32)
    p_sum = sum(p_cmp[h * NSA_TQ:(h + 1) * NSA_TQ] for h in range(N_HEADS))

    hi = p_sum.astype(BF16)
    lo = (p_sum - hi.astype(F32)).astype(BF16)
    imp = lax.dot_general(ovt_ref[...], jnp.concatenate([hi, lo], axis=1), NT, preferred_element_type=F32)
    blk = lax.broadcasted_iota(jnp.int32, (n_sel, NSA_TQ), 0)
    qpos_r = q0 + lax.broadcasted_iota(jnp.int32, (n_sel, NSA_TQ), 1)
    valid = blk * SEL_LEN <= qpos_r
    forced = (blk == 0) | (blk == jnp.right_shift(qpos_r, 6))
    score = jnp.where(valid, imp + jnp.where(forced, FORCE_SCORE, 0.0), -jnp.inf)
    groups = [score[8 * g:8 * g + 8] for g in range(n_sel // 8)]
    ranks = [jnp.zeros((8, NSA_TQ), F32) for _ in groups]
    row8 = lax.broadcasted_iota(jnp.int32, (8, NSA_TQ), 0)
    for i in range(n_sel):
        ci = jnp.broadcast_to(score[i:i + 1], (8, NSA_TQ))
        for g, sg in enumerate(groups):
            if 8 * g > i:
                beats = ci >= sg
            elif 8 * g + 7 < i:
                beats = ci > sg
            else:
                beats = (ci > sg) | ((ci == sg) & (row8 + 8 * g > i))
            ranks[g] = ranks[g] + jnp.where(beats, 1.0, 0.0)
    sel = jnp.where(valid & (jnp.concatenate(ranks, axis=0) < SEL_TOPN), 0.0, -SEL_OFF)
    sel = jnp.concatenate([sel, jnp.full((LANES - n_sel, NSA_TQ), -SEL_OFF, F32)], axis=0).T
    aug = lambda t: jnp.concatenate([q4, jnp.concatenate([t.astype(BF16)] * N_HEADS, axis=0)], axis=1)
    q_aug = aug(sel)
    blk_lane = lax.broadcasted_iota(jnp.int32, (NSA_TQ, LANES), 1)
    q_aug_past = aug(jnp.where(blk_lane * SEL_LEN >= q0, -SEL_OFF, sel))

    w_start = pl.multiple_of(jnp.maximum(q0 - WINDOW, 0), NSA_TQ)
    kwpos = w_start + lax.broadcasted_iota(jnp.int32, (NSA_TQ, WINDOW + NSA_TQ), 1)
    s_win = lax.dot_general(q4, kw_ref[pl.ds(w_start, WINDOW + NSA_TQ), :], NT, preferred_element_type=F32)
    p_win = masked_exp(s_win, (kwpos <= qpos_c) & (kwpos > qpos_c - WINDOW))
    r_win = normalized(attend(p_win, vw_ref[pl.ds(w_start, WINDOW + NSA_TQ), :]))

    def slab_stats(q, start, size, causal):
        keys = jnp.concatenate([kx_ref[pl.ds(start, size), :], kb_ref[pl.ds(start, size), :]], axis=1)
        s = lax.dot_general(q, keys, NT, preferred_element_type=F32)
        if causal:
            bias = jnp.where(start + lax.broadcasted_iota(jnp.int32, (NSA_TQ, size), 1) <= qpos_c, 0.0, NEG)
            s = _per_head(s, lambda t: t + bias)
        m_c = jnp.max(s, axis=-1, keepdims=True)
        return m_c, attend(jnp.exp2(s - m_c), vx_ref[pl.ds(start, size), :])

    m_d, a_d = slab_stats(q_aug, pl.multiple_of(q0, NSA_TQ), NSA_TQ, True)
    m_ref[...] = m_d
    acc_ref[...] = a_d

    def pair_body(i, carry):
        m0, a0 = slab_stats(q_aug_past, pl.multiple_of(2 * i * NSA_CK, NSA_CK), NSA_CK, False)
        m1, a1 = slab_stats(q_aug_past, pl.multiple_of((2 * i + 1) * NSA_CK, NSA_CK), NSA_CK, False)
        m_old = m_ref[...]
        m_new = jnp.maximum(m_old, jnp.maximum(m0, m1))
        acc_ref[...] = (jnp.exp2(m_old - m_new) * acc_ref[...] + jnp.exp2(m0 - m_new) * a0
                        + jnp.exp2(m1 - m_new) * a1)
        m_ref[...] = m_new
        return carry

    lax.fori_loop(0, (q0 + 2 * NSA_CK - 1) // (2 * NSA_CK), pair_body, 0)

    r_slc = normalized(acc_ref[...])
    gates = _sigmoid(vw_ref[pl.ds(pl.multiple_of(q0, NSA_TQ), NSA_TQ), :].astype(F32))
    g = [jnp.concatenate([gates[:, HEAD_DIM + 3 * h + t:HEAD_DIM + 3 * h + t + 1] for h in range(N_HEADS)], axis=0)
         for t in range(N_BRANCH)]
    y = g[0] * r_cmp + g[1] * r_slc + g[2] * r_win
    for pair in range(N_HEADS // 2):
        even = y[2 * pair * NSA_TQ:(2 * pair + 1) * NSA_TQ]
        odd = y[(2 * pair + 1) * NSA_TQ:(2 * pair + 2) * NSA_TQ]
        o_ref[:, pair * LANES:(pair + 1) * LANES] = jnp.where(lo_half, even, pltpu.roll(odd, HEAD_DIM, 1)).astype(o_ref.dtype)


def _nsa(proj, kc, vc, batch, seq):
    n = proj.shape[0]
    nqb = seq // NSA_TQ
    n_sel = seq // SEL_LEN
    n_cmp = kc.shape[1]
    assert n_sel % 8 == 0 and n_sel <= LANES and seq % (2 * NSA_CK) == 0
    cs = (np.arange(n_cmp) * CMP_STRIDE)[None, :]
    ss = (np.arange(n_sel) * SEL_LEN)[:, None]
    ovt = np.clip(np.minimum(cs + CMP_LEN, ss + SEL_LEN) - np.maximum(cs, ss), 0, None).astype(np.float32) / CMP_LEN
    ovt = jnp.asarray(np.concatenate([ovt, ovt], axis=1), BF16)
    kb = jnp.asarray((np.arange(seq)[:, None] // SEL_LEN == np.arange(LANES)[None, :]).astype(np.float32), BF16)
    seq_spec = lambda col: pl.BlockSpec((seq, LANES), lambda b, i: (b, col))
    cmp_spec = pl.BlockSpec((1, n_cmp, LANES), lambda b, i: (b, 0, 0))
    return pl.pallas_call(
        _nsa_kernel,
        grid=(batch, nqb),
        in_specs=[
            pl.BlockSpec((NSA_TQ, MIX_WIDTH), lambda b, i: (b * nqb + i, COL_CQ // MIX_WIDTH)),
            seq_spec(COL_KX // LANES),
            seq_spec(COL_VX // LANES),
            seq_spec(COL_KX // LANES + 1),
            seq_spec(COL_VX // LANES + 1),
            pl.BlockSpec((seq, LANES), lambda b, i: (0, 0)),
            cmp_spec, cmp_spec,
            pl.BlockSpec((n_sel, 2 * n_cmp), lambda b, i: (0, 0)),
        ],
        out_specs=pl.BlockSpec((NSA_TQ, MIX_WIDTH), lambda b, i: (b * nqb + i, 0)),
        out_shape=jax.ShapeDtypeStruct((n, MIX_WIDTH), BF16),
        scratch_shapes=[pltpu.VMEM((NSA_ROWS, 1), F32), pltpu.VMEM((NSA_ROWS, LANES), F32)],
        compiler_params=_cparams(("parallel", "arbitrary")),
        name="nsa",
    )(proj, proj, proj, proj, proj, kb, kc, vc, ovt)


def _merge_kernel(ya_ref, yb_ref, yc_ref, g0_ref, g1_ref, g2_ref, wb_ref, wo_ref, h_ref, o_ref):
    merged = None
    for y_ref, g_ref, t in ((ya_ref, g0_ref, 0), (yb_ref, g1_ref, 1), (yc_ref, g2_ref, 2)):
        term = _sigmoid(g_ref[...].astype(F32)) * jnp.dot(y_ref[...], wb_ref[t], preferred_element_type=F32)
        merged = term if merged is None else merged + term
    o_ref[...] = h_ref[...] + jnp.dot(merged.astype(BF16), wo_ref[...], preferred_element_type=F32)


def _merge(ya, yb, yc, proj, wb, wo, h, tm=512):
    n = h.shape[0]
    y_spec = pl.BlockSpec((tm, MIX_WIDTH), lambda i: (i, 0))
    g_spec = lambda t: pl.BlockSpec((tm, D_MODEL), lambda i: (i, COL_BRG // D_MODEL + t))
    return pl.pallas_call(
        _merge_kernel,
        grid=(n // tm,),
        in_specs=[y_spec, y_spec, y_spec, g_spec(0), g_spec(1), g_spec(2),
                  pl.BlockSpec((N_BRANCH, MIX_WIDTH, D_MODEL), lambda i: (0, 0, 0)),
                  pl.BlockSpec((D_MODEL, D_MODEL), lambda i: (0, 0)),
                  pl.BlockSpec((tm, D_MODEL), lambda i: (i, 0))],
        out_specs=pl.BlockSpec((tm, D_MODEL), lambda i: (i, 0)),
        out_shape=jax.ShapeDtypeStruct((n, D_MODEL), F32),
        compiler_params=_cparams(("parallel",)),
        name="merge",
    )(ya, yb, yc, proj, proj, proj, wb, wo, h)


def _ffn_kernel(h_ref, g_ref, wg_ref, wu_ref, wd_ref, o_ref, xn_ref, acc_ref):
    f = pl.program_id(1)

    @pl.when(f == 0)
    def _():
        xn_ref[...] = _rms(h_ref[...], g_ref[...]).astype(BF16)
        acc_ref[...] = h_ref[...]

    xn = xn_ref[...]
    a = jnp.dot(xn, wg_ref[...], preferred_element_type=F32)
    u = jnp.dot(xn, wu_ref[...], preferred_element_type=F32)
    hid = (a * _sigmoid(a) * u).astype(BF16)
    acc_ref[...] += jnp.dot(hid, wd_ref[...], preferred_element_type=F32)

    @pl.when(f == pl.num_programs(1) - 1)
    def _():
        o_ref[...] = acc_ref[...]


def _ffn(h, g, wg, wu, wd, tm=1024, tf=256):
    n = h.shape[0]
    dff = wg.shape[1]
    return pl.pallas_call(
        _ffn_kernel,
        grid=(n // tm, dff // tf),
        in_specs=[
            pl.BlockSpec((tm, D_MODEL), lambda i, f: (i, 0)),
            pl.BlockSpec((1, D_MODEL), lambda i, f: (0, 0)),
            pl.BlockSpec((D_MODEL, tf), lambda i, f: (0, f)),
            pl.BlockSpec((D_MODEL, tf), lambda i, f: (0, f)),
            pl.BlockSpec((tf, D_MODEL), lambda i, f: (f, 0)),
        ],
        out_specs=pl.BlockSpec((tm, D_MODEL), lambda i, f: (i, 0)),
        out_shape=jax.ShapeDtypeStruct((n, D_MODEL), F32),
        scratch_shapes=[pltpu.VMEM((tm, D_MODEL), BF16), pltpu.VMEM((tm, D_MODEL), F32)],
        compiler_params=_cparams(("parallel", "arbitrary")),
        name="ffn",
    )(h, g, wg, wu, wd)


RT_TM = 1024
RT_TF = 896
RT_TD = 512


def _gate_kernel(h_ref, g_ref, rw_ref, gate_ref):
    tm = h_ref.shape[0]
    lane = lax.broadcasted_iota(jnp.int32, (tm, LANES), 1)
    xn = _rms(h_ref[...], g_ref[...])
    logits = jnp.dot(xn, rw_ref[...], preferred_element_type=F32, precision=lax.Precision.HIGHEST)
    logits = jnp.where(lane < N_EXPERTS, logits, -jnp.inf)
    m1 = jnp.max(logits, axis=-1, keepdims=True)
    i1 = jnp.min(jnp.where(logits == m1, lane, LANES), axis=-1, keepdims=True)
    rest = jnp.where(lane == i1, -jnp.inf, logits)
    m2 = jnp.max(rest, axis=-1, keepdims=True)
    i2 = jnp.min(jnp.where(rest == m2, lane, LANES), axis=-1, keepdims=True)
    e2 = jnp.exp(m2 - m1)
    gate_ref[...] = jnp.where(lane == i1, 1.0 / (1.0 + e2), 0.0) + jnp.where(lane == i2, e2 / (1.0 + e2), 0.0)


def _gate(h, g, rw, tm=1024):
    n = h.shape[0]
    return pl.pallas_call(
        _gate_kernel,
        grid=(n // tm,),
        in_specs=[
            pl.BlockSpec((tm, D_MODEL), lambda i: (i, 0)),
            pl.BlockSpec((1, D_MODEL), lambda i: (0, 0)),
            pl.BlockSpec((D_MODEL, LANES), lambda i: (0, 0)),
        ],
        out_specs=pl.BlockSpec((tm, LANES), lambda i: (i, 0)),
        out_shape=jax.ShapeDtypeStruct((n, LANES), F32),
        compiler_params=_cparams(("parallel",)),
        name="moe_gate",
    )(h, g, rw)


def _row_copy(src_ref, src_row, dst_ref, dst_row, sem):
    return pltpu.make_async_copy(src_ref.at[pl.ds(src_row, 1), :], dst_ref.at[pl.ds(dst_row, 1), :], sem)


def _dispatch_kernel(pad_ref, pa_ref, pb_ref, h_ref, xs_ref, hbuf_ref, zero_ref, fetch_sems, row_sems, sems):
    i, n_steps = pl.program_id(0), pl.num_programs(0)
    slot, parity = i % 3, i % 2

    def fetch(block, s):
        return pltpu.make_async_copy(h_ref.at[pl.ds(block * RT_TD, RT_TD), :], hbuf_ref.at[s], fetch_sems.at[s])

    @pl.when(i == 0)
    def _():
        fetch(0, 0).start()

    @pl.when(i + 1 < n_steps)
    def _():
        fetch(i + 1, (i + 1) % 3).start()

    fetch(i, slot).wait()
    src = hbuf_ref.at[slot]

    def send(t, carry):
        _row_copy(src, t, xs_ref, pa_ref[0, 0, t], row_sems.at[0, parity]).start()
        _row_copy(src, t, xs_ref, pb_ref[0, 0, t], row_sems.at[1, parity]).start(priority=1)
        return carry

    lax.fori_loop(0, RT_TD, send, 0, unroll=4)

    @pl.when(i == 0)
    def _():
        zero_ref[...] = jnp.zeros_like(zero_ref)
        for e in range(N_EXPERTS + 1):
            lo, hi = pad_ref[2 * e], pad_ref[2 * e + 1]
            lax.fori_loop(lo, hi, lambda r, c: (_row_copy(zero_ref, 0, xs_ref, r, sems.at[0]).start(), c)[1], 0)
        for e in range(N_EXPERTS + 1):
            lo, hi = pad_ref[2 * e], pad_ref[2 * e + 1]
            lax.fori_loop(lo, hi, lambda r, c: (_row_copy(zero_ref, 0, xs_ref, r, sems.at[0]).wait(), c)[1], 0)

    def drain(p):
        for s in range(2):
            pltpu.make_async_copy(hbuf_ref.at[0], xs_ref.at[pl.ds(0, RT_TD), :], row_sems.at[s, p]).wait()

    @pl.when(i > 0)
    def _():
        drain(1 - parity)

    @pl.when(i == n_steps - 1)
    def _():
        drain(parity)


def _dispatch(pad_bounds, pa, pb, h, n_rows):
    n = h.shape[0]
    idx_spec = pl.BlockSpec((1, 1, RT_TD), lambda i, *_: (i, 0, 0), memory_space=pltpu.SMEM)
    grid_spec = pltpu.PrefetchScalarGridSpec(
        num_scalar_prefetch=1,
        grid=(n // RT_TD,),
        in_specs=[idx_spec, idx_spec, pl.BlockSpec(memory_space=pl.ANY)],
        out_specs=pl.BlockSpec(memory_space=pl.ANY),
        scratch_shapes=[pltpu.VMEM((3, RT_TD, D_MODEL), F32), pltpu.VMEM((8, D_MODEL), F32),
                        pltpu.SemaphoreType.DMA((3,)), pltpu.SemaphoreType.DMA((2, 2)),
                        pltpu.SemaphoreType.DMA((1,))],
    )
    return pl.pallas_call(
        _dispatch_kernel,
        grid_spec=grid_spec,
        out_shape=jax.ShapeDtypeStruct((n_rows, D_MODEL), F32),
        compiler_params=_cparams(("arbitrary",)),
        name="moe_dispatch",
    )(pad_bounds, pa.reshape(-1, 1, RT_TD), pb.reshape(-1, 1, RT_TD), h)


def _sorted_experts_kernel(te_ref, tx_ref, live_ref, xs_ref, g_ref, wg_ref, wu_ref, wd_ref, y_ref, xb_ref):
    i, f = pl.program_id(0), pl.program_id(1)

    @pl.when(f == 0)
    def _():
        xb_ref[...] = _rms(xs_ref[...], g_ref[...]).astype(BF16)
        y_ref[...] = jnp.zeros_like(y_ref)

    @pl.when(live_ref[i] > 0)
    def _():
        x = xb_ref[...]
        a = jnp.dot(x, wg_ref[0], preferred_element_type=F32)
        u = jnp.dot(x, wu_ref[0], preferred_element_type=F32)
        y_ref[...] += jnp.dot((a * _sigmoid(a) * u).astype(BF16), wd_ref[0], preferred_element_type=F32)


def _sorted_experts(tile_expert, tile_src, tile_live, xs, g, wg, wu, wd):
    n_rows = xs.shape[0]
    ne, _, dff = wg.shape
    grid_spec = pltpu.PrefetchScalarGridSpec(
        num_scalar_prefetch=3,
        grid=(n_rows // RT_TM, dff // RT_TF),
        in_specs=[
            pl.BlockSpec((RT_TM, D_MODEL), lambda i, f, te, tx, lv: (tx[i], 0)),
            pl.BlockSpec((1, D_MODEL), lambda i, f, *_: (0, 0)),
            pl.BlockSpec((1, D_MODEL, RT_TF), lambda i, f, te, tx, lv: (te[i], 0, f)),
            pl.BlockSpec((1, D_MODEL, RT_TF), lambda i, f, te, tx, lv: (te[i], 0, f)),
            pl.BlockSpec((1, RT_TF, D_MODEL), lambda i, f, te, tx, lv: (te[i], f, 0)),
        ],
        out_specs=pl.BlockSpec((RT_TM, D_MODEL), lambda i, f, *_: (i, 0)),
        scratch_shapes=[pltpu.VMEM((RT_TM, D_MODEL), BF16)],
    )
    return pl.pallas_call(
        _sorted_experts_kernel,
        grid_spec=grid_spec,
        out_shape=jax.ShapeDtypeStruct((n_rows, D_MODEL), F32),
        compiler_params=_cparams(("parallel", "arbitrary")),
        name="moe_experts",
    )(tile_expert, tile_src, tile_live, xs, g, wg, wu, wd)


def _combine_kernel(pa_ref, pb_ref, pa_next_ref, pb_next_ref, ga_ref, gb_ref, h_ref, y_ref, o_ref, ya_ref, yb_ref, sems):
    i, n_steps = pl.program_id(0), pl.num_programs(0)
    slot = i % 2

    def fetch(pa, pb, s):
        def body(t, carry):
            _row_copy(y_ref, pa[0, 0, t], ya_ref.at[s], t, sems.at[0, s]).start()
            _row_copy(y_ref, pb[0, 0, t], yb_ref.at[s], t, sems.at[1, s]).start(priority=1)
            return carry

        lax.fori_loop(0, RT_TD, body, 0, unroll=4)

    @pl.when(i == 0)
    def _():
        fetch(pa_ref, pb_ref, 0)

    @pl.when(i + 1 < n_steps)
    def _():
        fetch(pa_next_ref, pb_next_ref, 1 - slot)

    pltpu.make_async_copy(y_ref.at[pl.ds(0, RT_TD), :], ya_ref.at[slot], sems.at[0, slot]).wait()
    pltpu.make_async_copy(y_ref.at[pl.ds(0, RT_TD), :], yb_ref.at[slot], sems.at[1, slot]).wait()
    o_ref[...] = h_ref[...] + ga_ref[...] * ya_ref[slot] + gb_ref[...] * yb_ref[slot]


def _combine(pa, pb, ga, gb, h, y):
    n = h.shape[0]
    n_steps = n // RT_TD
    idx_spec = pl.BlockSpec((1, 1, RT_TD), lambda i: (i, 0, 0), memory_space=pltpu.SMEM)
    next_spec = pl.BlockSpec((1, 1, RT_TD), lambda i: (jnp.minimum(i + 1, n_steps - 1), 0, 0), memory_space=pltpu.SMEM)
    col_spec = pl.BlockSpec((RT_TD, 1), lambda i: (i, 0))
    row_spec = pl.BlockSpec((RT_TD, D_MODEL), lambda i: (i, 0))
    pa, pb = pa.reshape(-1, 1, RT_TD), pb.reshape(-1, 1, RT_TD)
    return pl.pallas_call(
        _combine_kernel,
        grid=(n_steps,),
        in_specs=[idx_spec, idx_spec, next_spec, next_spec, col_spec, col_spec, row_spec,
                  pl.BlockSpec(memory_space=pl.ANY)],
        out_specs=row_spec,
        out_shape=jax.ShapeDtypeStruct((n, D_MODEL), F32),
        scratch_shapes=[pltpu.VMEM((2, RT_TD, D_MODEL), F32), pltpu.VMEM((2, RT_TD, D_MODEL), F32),
                        pltpu.SemaphoreType.DMA((2, 2))],
        compiler_params=_cparams(("arbitrary",)),
        name="moe_combine",
    )(pa, pb, pa, pb, ga, gb, h, y)


def _moe_sorted(h, g, rw, wg, wu, wd):
    n = h.shape[0]
    n_tiles = 2 * n // RT_TM + N_EXPERTS
    gate = _gate(h, g, rw)[:, :N_EXPERTS]
    routed = gate > 0
    r = routed.astype(jnp.int32)
    incl = jnp.cumsum(r, axis=0)
    count = incl[-1]
    padded = (count + RT_TM - 1) // RT_TM * RT_TM
    ends = jnp.cumsum(padded)
    pos = (ends - padded)[None, :] + incl - 1
    order = jnp.cumsum(r, axis=1)
    first, second = routed & (order == 1), routed & (order == 2)
    pick = lambda m, v: jnp.sum(jnp.where(m, v, 0), axis=1)
    pa = pick(first, pos)
    pb = jnp.where(jnp.any(second, axis=1), pick(second, pos), pa)
    ga, gb = pick(first, gate)[:, None], pick(second, gate)[:, None]
    pad_lo = jnp.concatenate([ends - padded + count, ends[-1:]])
    pad_hi = jnp.concatenate([ends, jnp.full((1,), n_tiles * RT_TM, ends.dtype)])
    pad_bounds = jnp.stack([pad_lo, pad_hi], axis=1).reshape(-1).astype(jnp.int32)
    tile_start = jnp.arange(n_tiles, dtype=jnp.int32) * RT_TM
    tile_live = (tile_start < ends[-1]).astype(jnp.int32)
    tile_expert = jnp.minimum(jnp.sum(tile_start[:, None] >= ends[None, :], axis=1), N_EXPERTS - 1).astype(jnp.int32)
    tile_src = jnp.minimum(jnp.arange(n_tiles, dtype=jnp.int32), ends[-1] // RT_TM - 1)
    xs = _dispatch(pad_bounds, pa.astype(jnp.int32), pb.astype(jnp.int32), h, n_tiles * RT_TM)
    y = _sorted_experts(tile_expert, tile_src, tile_live, xs, g, wg, wu, wd)
    return _combine(pa.astype(jnp.int32), pb.astype(jnp.int32), ga, gb, h, y)


def _final_norm_kernel(h_ref, g_ref, o_ref):
    o_ref[...] = _rms(h_ref[...], g_ref[...])


def _final_norm(h, g, tm=1024):
    n = h.shape[0]
    return pl.pallas_call(
        _final_norm_kernel,
        grid=(n // tm,),
        in_specs=[pl.BlockSpec((tm, D_MODEL), lambda i: (i, 0)), pl.BlockSpec((1, D_MODEL), lambda i: (0, 0))],
        out_specs=pl.BlockSpec((tm, D_MODEL), lambda i: (i, 0)),
        out_shape=jax.ShapeDtypeStruct((n, D_MODEL), F32),
        compiler_params=_cparams(("parallel",)),
        name="final_norm",
    )(h, g)


def _rope_tables(seq):
    half = ROPE_DIMS // 2
    inv_freq = ROPE_THETA ** (-jnp.arange(0, ROPE_DIMS, 2, dtype=F32) / ROPE_DIMS)
    ang = jnp.arange(seq, dtype=F32)[:, None] * inv_freq[None, :]
    cos, sin = jnp.cos(ang), jnp.sin(ang)
    ones = jnp.ones((seq, HEAD_DIM - ROPE_DIMS), F32)
    zeros_h = jnp.zeros((seq, half), F32)
    zeros_r = jnp.zeros((seq, HEAD_DIM - ROPE_DIMS), F32)
    c = jnp.concatenate([cos, cos, ones], axis=1)
    s1 = jnp.concatenate([-sin, zeros_h, zeros_r], axis=1)
    s2 = jnp.concatenate([zeros_h, sin, zeros_r], axis=1)
    rep = LANES // HEAD_DIM
    return tuple(jnp.tile(t, (1, rep)) for t in (c, s1, s2))


def _layout_w_in(w):
    scale = HEAD_DIM ** -0.5 * LOG2E
    a_in, sb_q, sb_k, sb_v, c_q, c_kv, c_g, br_g = jnp.split(
        w, np.cumsum([2 * MIX_WIDTH, MIX_WIDTH, MIX_WIDTH, MIX_WIDTH, MIX_WIDTH, 6 * HEAD_DIM, 3 * N_HEADS])[:].tolist(),
        axis=-1)
    k_cmp, v_cmp, k_slc, v_slc, k_win, v_win = jnp.split(c_kv, 6, axis=-1)
    zeros = lambda width: jnp.zeros((w.shape[0], width), w.dtype)
    cols = [a_in, sb_q * scale, sb_k, sb_v,
            v_slc, v_cmp, v_win, c_g, zeros(HEAD_DIM - 3 * N_HEADS),
            c_q * scale, k_slc, k_cmp, k_win, zeros(HEAD_DIM),
            br_g]
    out = jnp.concatenate(cols, axis=-1)
    assert out.shape[-1] == PROJ_W
    return out.astype(BF16)


def kernel(x, norm1_g, w_in, sgu_norm_g, sgu_w, sgu_b, cmp_pos, cmp_w1, cmp_w2, w_branch, w_out, norm2_g,
           ffn_w_gate, ffn_w_up, ffn_w_down, router_w, moe_w_gate, moe_w_up, moe_w_down, final_norm_g):
    batch, seq, _ = x.shape
    n = batch * seq
    depth = norm1_g.shape[0]
    rope_c, rope_s1, rope_s2 = _rope_tables(seq)
    dff_pad = -(-D_FF // 256) * 256
    h = x.reshape(n, D_MODEL)
    for layer in range(depth):
        proj = _norm_proj(h, norm1_g[layer][None, :], _layout_w_in(w_in[layer]), rope_c, rope_s1, rope_s2, seq)
        sgu_bias = jnp.repeat(sgu_b[layer].T, HEAD_DIM, axis=1)
        y_a = _sgu(proj, sgu_norm_g[layer][None, :], sgu_w[layer], sgu_bias)
        y_b = _stick_breaking(proj, batch, seq)
        groups = seq // CMP_STRIDE
        gk = proj[:, COL_KX + HEAD_DIM:COL_KX + 2 * HEAD_DIM].reshape(batch, groups, CMP_STRIDE * HEAD_DIM)
        gv = proj[:, COL_VX + HEAD_DIM:COL_VX + 2 * HEAD_DIM].reshape(batch, groups, CMP_STRIDE * HEAD_DIM)
        kc, vc = _compress(gk, gv, cmp_pos[layer].reshape(2, 1, CMP_LEN * HEAD_DIM),
                           cmp_w1[layer].astype(BF16), cmp_w2[layer].astype(BF16))
        y_c = _nsa(proj, kc, vc, batch, seq)
        h = _merge(y_a, y_b, y_c, proj, w_branch[layer].astype(BF16), w_out[layer].astype(BF16), h)
        j = layer // 2
        g2 = norm2_g[layer][None, :]
        if layer % 2 == 0:
            pad = dff_pad - D_FF
            wg = jnp.pad(ffn_w_gate[j], ((0, 0), (0, pad))).astype(BF16)
            wu = jnp.pad(ffn_w_up[j], ((0, 0), (0, pad))).astype(BF16)
            wd = jnp.pad(ffn_w_down[j], ((0, pad), (0, 0))).astype(BF16)
            h = _ffn(h, g2, wg, wu, wd)
        else:
            rw = jnp.pad(router_w[j], ((0, 0), (0, LANES - N_EXPERTS)))
            h = _moe_sorted(h, g2, rw, moe_w_gate[j].astype(BF16), moe_w_up[j].astype(BF16), moe_w_down[j].astype(BF16))
    return _final_norm(h, final_norm_g[None, :]).reshape(batch, seq, D_MODEL)
```

```python
import numpy as np
import jax
import jax.numpy as jnp
from jax import lax
from jax.experimental import pallas as pl
from jax.experimental.pallas import tpu as pltpu

F32 = jnp.float32
BF16 = jnp.bfloat16

D_MODEL = 1024
HEAD_DIM = 64
N_HEADS = 4
MIX_WIDTH = N_HEADS * HEAD_DIM
ROPE_DIMS = HEAD_DIM // 4
ROPE_THETA = 500000.0
EPS = 1e-6
SGU_CHUNK = 128
CMP_LEN = 32
CMP_STRIDE = 16
CMP_HIDDEN = 256
SEL_LEN = 64
SEL_TOPN = 16
WINDOW = 512
FORCE_SCORE = 1e4
N_BRANCH = 3
D_FF = 2752
N_EXPERTS = 8
D_FF_EXPERT = 3584
NEG = -1e30
LOG2E = 1.4426950408889634

LANES = 128
VMEM_LIMIT = 56 * 1024 * 1024

PROJ_TILE = 1024
COL_A = 0
COL_SBQ, COL_SBK, COL_SBV = 512, 768, 1024
COL_VX = 1280
COL_CQ = 1536
COL_KX = 1792
COL_BRG = 2048
PROJ_W = COL_BRG + N_BRANCH * D_MODEL
ROPE_TILE = COL_CQ // PROJ_TILE
ROPE_START = COL_CQ % PROJ_TILE
assert COL_BRG == (ROPE_TILE + 1) * PROJ_TILE


def _cparams(sem):
    return pltpu.CompilerParams(dimension_semantics=sem, vmem_limit_bytes=VMEM_LIMIT)


def _sigmoid(x):
    return 1.0 / (1.0 + jnp.exp(-x))


def _rms(x, g):
    return x * lax.rsqrt(jnp.mean(x * x, axis=-1, keepdims=True) + EPS) * g


def _norm_proj_kernel(h_ref, g_ref, w_ref, c_ref, s1_ref, s2_ref, o_ref, xn_ref):
    j = pl.program_id(1)

    @pl.when(j == 0)
    def _():
        xn_ref[...] = _rms(h_ref[...], g_ref[...]).astype(BF16)

    acc = jnp.dot(xn_ref[...], w_ref[...], preferred_element_type=F32)

    @pl.when(j != ROPE_TILE)
    def _():
        o_ref[...] = acc.astype(o_ref.dtype)

    @pl.when(j == ROPE_TILE)
    def _():
        c, s1, s2 = c_ref[...], s1_ref[...], s2_ref[...]
        o_ref[:, :ROPE_START] = acc[:, :ROPE_START].astype(o_ref.dtype)
        for g in range(ROPE_START // LANES, PROJ_TILE // LANES):
            x = acc[:, g * LANES:(g + 1) * LANES]
            y = x * c + pltpu.roll(x, LANES - 8, 1) * s1 + pltpu.roll(x, 8, 1) * s2
            o_ref[:, g * LANES:(g + 1) * LANES] = y.astype(o_ref.dtype)


def _norm_proj(h, g, w, rope_c, rope_s1, rope_s2, seq, tm=2048):
    n = h.shape[0]
    assert seq % tm == 0, "a projection tile must not straddle two sequences (rotary tables are per position)"
    nseq = seq // tm
    return pl.pallas_call(
        _norm_proj_kernel,
        grid=(n // tm, PROJ_W // PROJ_TILE),
        in_specs=[
            pl.BlockSpec((tm, D_MODEL), lambda i, j: (i, 0)),
            pl.BlockSpec((1, D_MODEL), lambda i, j: (0, 0)),
            pl.BlockSpec((D_MODEL, PROJ_TILE), lambda i, j: (0, j)),
            pl.BlockSpec((tm, LANES), lambda i, j: (i % nseq, 0)),
            pl.BlockSpec((tm, LANES), lambda i, j: (i % nseq, 0)),
            pl.BlockSpec((tm, LANES), lambda i, j: (i % nseq, 0)),
        ],
        out_specs=pl.BlockSpec((tm, PROJ_TILE), lambda i, j: (i, j)),
        out_shape=jax.ShapeDtypeStruct((n, PROJ_W), BF16),
        scratch_shapes=[pltpu.VMEM((tm, D_MODEL), BF16)],
        compiler_params=_cparams(("parallel", "arbitrary")),
        name="norm_proj",
    )(h, g, w, rope_c, rope_s1, rope_s2)


def _sgu_kernel(z_ref, g_ref, w_ref, b_ref, o_ref):
    tm = z_ref.shape[0]
    a = jax.nn.gelu(z_ref[...].astype(F32))
    u = a[:, :MIX_WIDTH]
    v = _rms(a[:, MIX_WIDTH:], g_ref[...]).astype(BF16)
    row = lax.broadcasted_iota(jnp.int32, (SGU_CHUNK, SGU_CHUNK), 0)
    col = lax.broadcasted_iota(jnp.int32, (SGU_CHUNK, SGU_CHUNK), 1)
    ws = [jnp.where(row >= col, w_ref[gi], 0.0).astype(BF16) for gi in range(N_HEADS)]
    bias = b_ref[...]
    for c in range(tm // SGU_CHUNK):
        rows = slice(c * SGU_CHUNK, (c + 1) * SGU_CHUNK)
        mix = jnp.concatenate(
            [jnp.dot(ws[gi], v[rows, gi * HEAD_DIM:(gi + 1) * HEAD_DIM], preferred_element_type=F32)
             for gi in range(N_HEADS)], axis=1)
        o_ref[rows, :] = (u[rows, :] * (mix + bias)).astype(o_ref.dtype)


def _sgu(proj, g, w, bias, tm=512):
    n = proj.shape[0]
    return pl.pallas_call(
        _sgu_kernel,
        grid=(n // tm,),
        in_specs=[
            pl.BlockSpec((tm, 2 * MIX_WIDTH), lambda i: (i, COL_A // (2 * MIX_WIDTH))),
            pl.BlockSpec((1, MIX_WIDTH), lambda i: (0, 0)),
            pl.BlockSpec((N_HEADS, SGU_CHUNK, SGU_CHUNK), lambda i: (0, 0, 0)),
            pl.BlockSpec((SGU_CHUNK, MIX_WIDTH), lambda i: (0, 0)),
        ],
        out_specs=pl.BlockSpec((tm, MIX_WIDTH), lambda i: (i, 0)),
        out_shape=jax.ShapeDtypeStruct((n, MIX_WIDTH), BF16),
        compiler_params=_cparams(("parallel",)),
        name="sgu",
    )(proj, g, w, bias)


SB_TQ = 1024
SB_TK = 512
SB_SUB = 256
SB_R = SB_TQ // SB_TK


def _sb_tile(q_ref, k_ref, v_ref, u_ref, acc_ref, carry_ref, k_off):
    for j in reversed(range(SB_TK // SB_SUB)):
        keys = slice(j * SB_SUB, (j + 1) * SB_SUB)
        _sb_sub_tile(q_ref, k_ref[keys, :], v_ref[keys, :], u_ref[...], acc_ref, carry_ref,
                     None if k_off is None else k_off + j * SB_SUB)


def _sb_sub_tile(q_ref, k, v, u, acc_ref, carry_ref, k_off):
    diag = k_off is not None
    rows = slice(k_off if diag else 0, SB_TQ)
    n_rows = rows.stop - rows.start
    q = q_ref[rows, :]
    if diag:
        mask = (lax.broadcasted_iota(jnp.int32, (n_rows, SB_SUB), 1)
                < lax.broadcasted_iota(jnp.int32, (n_rows, SB_SUB), 0))
    pvs = []
    for h in range(N_HEADS):
        hs = slice(h * HEAD_DIM, (h + 1) * HEAD_DIM)
        z = lax.dot_general(q[:, hs], k[:, hs], (((1,), (1,)), ((), ())), preferred_element_type=F32)
        log_beta = jnp.minimum(z, 0.0) - jnp.log(1.0 + jnp.exp2(-jnp.abs(z))) * LOG2E
        log_1m = log_beta - z
        if diag:
            log_1m = jnp.where(mask, log_1m, 0.0)
        carry = carry_ref[h, rows, :]
        terms = log_1m.astype(BF16)
        r = jnp.dot(terms, u, preferred_element_type=F32)
        a = jnp.exp2(log_beta + r + carry)
        if diag:
            a = jnp.where(mask, a, 0.0)
        carry_ref[h, rows, :] = carry + r[:, 0:1] + terms[:, 0:1].astype(F32)
        pvs.append(jnp.dot(a.astype(BF16), v[:, hs], preferred_element_type=F32))
    acc_ref[rows, :] += jnp.concatenate(pvs, axis=1)


def _sb_kernel(qi_ref, ki_ref, q_ref, k_ref, v_ref, u_ref, o_ref, acc_ref, carry_ref):
    step = pl.program_id(1)
    qi = qi_ref[step]
    ki = ki_ref[step]

    @pl.when(ki == SB_R * qi + SB_R - 1)
    def _():
        acc_ref[...] = jnp.zeros_like(acc_ref)
        carry_ref[...] = jnp.zeros_like(carry_ref)

    for d in range(SB_R):
        @pl.when(ki == SB_R * qi + d)
        def _(d=d):
            _sb_tile(q_ref, k_ref, v_ref, u_ref, acc_ref, carry_ref, d * SB_TK)

    @pl.when(ki < SB_R * qi)
    def _():
        _sb_tile(q_ref, k_ref, v_ref, u_ref, acc_ref, carry_ref, None)

    @pl.when(ki == 0)
    def _():
        o_ref[...] = acc_ref[...].astype(o_ref.dtype)


def _stick_breaking(proj, batch, seq):
    n = proj.shape[0]
    nq, nk = seq // SB_TQ, seq // SB_TK
    qi_tab = np.array([qi for qi in range(nq) for _ in range(SB_R * (qi + 1))], np.int32)
    ki_tab = np.array([ki for qi in range(nq) for ki in range(SB_R * (qi + 1) - 1, -1, -1)], np.int32)
    u = jnp.asarray((np.arange(SB_SUB)[:, None] > np.arange(SB_SUB)[None, :]).astype(np.float32), BF16)
    qb, kb, vb = COL_SBQ // MIX_WIDTH, COL_SBK // MIX_WIDTH, COL_SBV // MIX_WIDTH
    grid_spec = pltpu.PrefetchScalarGridSpec(
        num_scalar_prefetch=2,
        grid=(batch, len(qi_tab)),
        in_specs=[
            pl.BlockSpec((SB_TQ, MIX_WIDTH), lambda b, s, qt, kt: (b * nq + qt[s], qb)),
            pl.BlockSpec((SB_TK, MIX_WIDTH), lambda b, s, qt, kt: (b * nk + kt[s], kb)),
            pl.BlockSpec((SB_TK, MIX_WIDTH), lambda b, s, qt, kt: (b * nk + kt[s], vb)),
            pl.BlockSpec((SB_SUB, SB_SUB), lambda b, s, qt, kt: (0, 0)),
        ],
        out_specs=pl.BlockSpec((SB_TQ, MIX_WIDTH), lambda b, s, qt, kt: (b * nq + qt[s], 0)),
        scratch_shapes=[pltpu.VMEM((SB_TQ, MIX_WIDTH), F32), pltpu.VMEM((N_HEADS, SB_TQ, 1), F32)],
    )
    return pl.pallas_call(
        _sb_kernel,
        grid_spec=grid_spec,
        out_shape=jax.ShapeDtypeStruct((n, MIX_WIDTH), BF16),
        compiler_params=_cparams(("parallel", "arbitrary")),
        name="stick_breaking",
    )(jnp.asarray(qi_tab), jnp.asarray(ki_tab), proj, proj, proj, u)


def _compress_kernel(gk_ref, gv_ref, pos_ref, w1_ref, w2_ref, kc_ref, vc_ref):
    half = CMP_STRIDE * HEAD_DIM
    for t, (g_ref, o_ref) in enumerate(((gk_ref, kc_ref), (gv_ref, vc_ref))):
        g = g_ref[0].astype(F32)
        top = (g + pos_ref[t, :, :half]).astype(BF16)
        bot = (g + pos_ref[t, :, half:]).astype(BF16)
        a = jnp.dot(top, w1_ref[t, :half, :], preferred_element_type=F32)
        b = jnp.dot(bot, w1_ref[t, half:, :], preferred_element_type=F32)
        hid = jax.nn.gelu(a + pltpu.roll(b, b.shape[0] - 1, 0))
        out = jnp.dot(hid.astype(BF16), w2_ref[t], preferred_element_type=F32)
        o_ref[0] = jnp.concatenate([out, jnp.zeros_like(out)], axis=1).astype(o_ref.dtype)


def _compress(gk, gv, pos, w1, w2):
    b, m, width = gk.shape
    out_spec = pl.BlockSpec((1, m, LANES), lambda i: (i, 0, 0))
    out_shape = jax.ShapeDtypeStruct((b, m, LANES), BF16)
    return pl.pallas_call(
        _compress_kernel,
        grid=(b,),
        in_specs=[
            pl.BlockSpec((1, m, width), lambda i: (i, 0, 0)),
            pl.BlockSpec((1, m, width), lambda i: (i, 0, 0)),
            pl.BlockSpec((2, 1, CMP_LEN * HEAD_DIM), lambda i: (0, 0, 0)),
            pl.BlockSpec((2, CMP_LEN * HEAD_DIM, CMP_HIDDEN), lambda i: (0, 0, 0)),
            pl.BlockSpec((2, CMP_HIDDEN, HEAD_DIM), lambda i: (0, 0, 0)),
        ],
        out_specs=[out_spec, out_spec],
        out_shape=[out_shape, out_shape],
        compiler_params=_cparams(("parallel",)),
        name="nsa_compress",
    )(gk, gv, pos, w1, w2)


NSA_TQ = 256
NSA_CK = 512


NSA_ROWS = N_HEADS * NSA_TQ
SEL_OFF = 1e30


def _per_head(x, fn):
    return jnp.concatenate([fn(x[h * NSA_TQ:(h + 1) * NSA_TQ]) for h in range(N_HEADS)], axis=0)


def _nsa_kernel(q_ref, kx_ref, vx_ref, kw_ref, vw_ref, kb_ref, kc_ref, vc_ref, ovt_ref, o_ref, m_ref, acc_ref):
    seq = kx_ref.shape[0]
    n_sel = seq // SEL_LEN
    n_cmp = kc_ref.shape[1]
    q0 = pl.program_id(1) * NSA_TQ
    NT = (((1,), (1,)), ((), ()))
    lo_half = lax.broadcasted_iota(jnp.int32, (NSA_TQ, LANES), 1) < HEAD_DIM
    qpos_c = q0 + lax.broadcasted_iota(jnp.int32, (NSA_TQ, 1), 0)

    qs = []
    for pair in range(N_HEADS // 2):
        x = q_ref[:, pair * LANES:(pair + 1) * LANES].astype(F32)
        qs += [jnp.where(lo_half, x, 0.0), jnp.where(lo_half, pltpu.roll(x, HEAD_DIM, 1), 0.0)]
    q4 = jnp.concatenate(qs, axis=0).astype(BF16)

    ones_half = lax.broadcasted_iota(jnp.int32, (1, LANES), 1) >= HEAD_DIM

    def masked_exp(s, mask):
        bias = jnp.where(mask, 0.0, NEG)
        s = _per_head(s, lambda t: t + bias)
        return jnp.exp2(s - jnp.max(s, axis=-1, keepdims=True))

    def attend(p, vals):
        return jnp.dot(p.astype(BF16), jnp.where(ones_half, 1.0, vals).astype(BF16), preferred_element_type=F32)

    def normalized(r):
        return r / jnp.where(ones_half, 1.0, pltpu.roll(r, HEAD_DIM, 1))

    cmp_end = lax.broadcasted_iota(jnp.int32, (NSA_TQ, n_cmp), 1) * CMP_STRIDE + (CMP_LEN - 1)
    p_cmp = masked_exp(lax.dot_general(q4, kc_ref[0], NT, preferred_element_type=F32), cmp_end <= qpos_c)
    has_cmp = jnp.concatenate([qpos_c >= CMP_LEN - 1] * N_HEADS, axis=0)
    p_cmp = p_cmp * jnp.where(has_cmp, 1.0 / jnp.sum(p_cmp, axis=-1, keepdims=True), 0.0)
    r_cmp = jnp.dot(p_cmp.astype(BF16), vc_ref[0], preferred_element_type=F32)
    p_sum = sum(p_cmp[h * NSA_TQ:(h + 1) * NSA_TQ] for h in range(N_HEADS))

    hi = p_sum.astype(BF16)
    lo = (p_sum - hi.astype(F32)).astype(BF16)
    imp = lax.dot_general(ovt_ref[...], jnp.concatenate([hi, lo], axis=1), NT, preferred_element_type=F32)
    blk = lax.broadcasted_iota(jnp.int32, (n_sel, NSA_TQ), 0)
    qpos_r = q0 + lax.broadcasted_iota(jnp.int32, (n_sel, NSA_TQ), 1)
    valid = blk * SEL_LEN <= qpos_r
    forced = (blk == 0) | (blk == jnp.right_shift(qpos_r, 6))
    score = jnp.where(valid, imp + jnp.where(forced, FORCE_SCORE, 0.0), -jnp.inf)
    groups = [score[8 * g:8 * g + 8] for g in range(n_sel // 8)]
    ranks = [jnp.zeros((8, NSA_TQ), F32) for _ in groups]
    row8 = lax.broadcasted_iota(jnp.int32, (8, NSA_TQ), 0)
    for i in range(n_sel):
        ci = jnp.broadcast_to(score[i:i + 1], (8, NSA_TQ))
        for g, sg in enumerate(groups):
            if 8 * g > i:
                beats = ci >= sg
            elif 8 * g + 7 < i:
                beats = ci > sg
            else:
                beats = (ci > sg) | ((ci == sg) & (row8 + 8 * g > i))
            ranks[g] = ranks[g] + jnp.where(beats, 1.0, 0.0)
    sel = jnp.where(valid & (jnp.concatenate(ranks, axis=0) < SEL_TOPN), 0.0, -SEL_OFF)
    sel = jnp.concatenate([sel, jnp.full((LANES - n_sel, NSA_TQ), -SEL_OFF, F32)], axis=0).T
    aug = lambda t: jnp.concatenate([q4, jnp.concatenate([t.astype(BF16)] * N_HEADS, axis=0)], axis=1)
    q_aug = aug(sel)
    blk_lane = lax.broadcasted_iota(jnp.int32, (NSA_TQ, LANES), 1)
    q_aug_past = aug(jnp.where(blk_lane * SEL_LEN >= q0, -SEL_OFF, sel))

    w_start = pl.multiple_of(jnp.maximum(q0 - WINDOW, 0), NSA_TQ)
    kwpos = w_start + lax.broadcasted_iota(jnp.int32, (NSA_TQ, WINDOW + NSA_TQ), 1)
    s_win = lax.dot_general(q4, kw_ref[pl.ds(w_start, WINDOW + NSA_TQ), :], NT, preferred_element_type=F32)
    p_win = masked_exp(s_win, (kwpos <= qpos_c) & (kwpos > qpos_c - WINDOW))
    r_win = normalized(attend(p_win, vw_ref[pl.ds(w_start, WINDOW + NSA_TQ), :]))

    def slab_stats(q, start, size, causal):
        keys = jnp.concatenate([kx_ref[pl.ds(start, size), :], kb_ref[pl.ds(start, size), :]], axis=1)
        s = lax.dot_general(q, keys, NT, preferred_element_type=F32)
        if causal:
            bias = jnp.where(start + lax.broadcasted_iota(jnp.int32, (NSA_TQ, size), 1) <= qpos_c, 0.0, NEG)
            s = _per_head(s, lambda t: t + bias)
        m_c = jnp.max(s, axis=-1, keepdims=True)
        return m_c, attend(jnp.exp2(s - m_c), vx_ref[pl.ds(start, size), :])

    m_d, a_d = slab_stats(q_aug, pl.multiple_of(q0, NSA_TQ), NSA_TQ, True)
    m_ref[...] = m_d
    acc_ref[...] = a_d

    def pair_body(i, carry):
        m0, a0 = slab_stats(q_aug_past, pl.multiple_of(2 * i * NSA_CK, NSA_CK), NSA_CK, False)
        m1, a1 = slab_stats(q_aug_past, pl.multiple_of((2 * i + 1) * NSA_CK, NSA_CK), NSA_CK, False)
        m_old = m_ref[...]
        m_new = jnp.maximum(m_old, jnp.maximum(m0, m1))
        acc_ref[...] = (jnp.exp2(m_old - m_new) * acc_ref[...] + jnp.exp2(m0 - m_new) * a0
                        + jnp.exp2(m1 - m_new) * a1)
        m_ref[...] = m_new
        return carry

    lax.fori_loop(0, (q0 + 2 * NSA_CK - 1) // (2 * NSA_CK), pair_body, 0)

    r_slc = normalized(acc_ref[...])
    gates = _sigmoid(vw_ref[pl.ds(pl.multiple_of(q0, NSA_TQ), NSA_TQ), :].astype(F32))
    g = [jnp.concatenate([gates[:, HEAD_DIM + 3 * h + t:HEAD_DIM + 3 * h + t + 1] for h in range(N_HEADS)], axis=0)
         for t in range(N_BRANCH)]
    y = g[0] * r_cmp + g[1] * r_slc + g[2] * r_win
    for pair in range(N_HEADS // 2):
        even = y[2 * pair * NSA_TQ:(2 * pair + 1) * NSA_TQ]
        odd = y[(2 * pair + 1) * NSA_TQ:(2 * pair + 2) * NSA_TQ]
        o_ref[:, pair * LANES:(pair + 1) * LANES] = jnp.where(lo_half, even, pltpu.roll(odd, HEAD_DIM, 1)).astype(o_ref.dtype)


def _nsa(proj, kc, vc, batch, seq):
    n = proj.shape[0]
    nqb = seq // NSA_TQ
    n_sel = seq // SEL_LEN
    n_cmp = kc.shape[1]
    assert n_sel % 8 == 0 and n_sel <= LANES and seq % (2 * NSA_CK) == 0
    cs = (np.arange(n_cmp) * CMP_STRIDE)[None, :]
    ss = (np.arange(n_sel) * SEL_LEN)[:, None]
    ovt = np.clip(np.minimum(cs + CMP_LEN, ss + SEL_LEN) - np.maximum(cs, ss), 0, None).astype(np.float32) / CMP_LEN
    ovt = jnp.asarray(np.concatenate([ovt, ovt], axis=1), BF16)
    kb = jnp.asarray((np.arange(seq)[:, None] // SEL_LEN == np.arange(LANES)[None, :]).astype(np.float32), BF16)
    seq_spec = lambda col: pl.BlockSpec((seq, LANES), lambda b, i: (b, col))
    cmp_spec = pl.BlockSpec((1, n_cmp, LANES), lambda b, i: (b, 0, 0))
    return pl.pallas_call(
        _nsa_kernel,
        grid=(batch, nqb),
        in_specs=[
            pl.BlockSpec((NSA_TQ, MIX_WIDTH), lambda b, i: (b * nqb + i, COL_CQ // MIX_WIDTH)),
            seq_spec(COL_KX // LANES),
            seq_spec(COL_VX // LANES),
            seq_spec(COL_KX // LANES + 1),
            seq_spec(COL_VX // LANES + 1),
            pl.BlockSpec((seq, LANES), lambda b, i: (0, 0)),
            cmp_spec, cmp_spec,
            pl.BlockSpec((n_sel, 2 * n_cmp), lambda b, i: (0, 0)),
        ],
        out_specs=pl.BlockSpec((NSA_TQ, MIX_WIDTH), lambda b, i: (b * nqb + i, 0)),
        out_shape=jax.ShapeDtypeStruct((n, MIX_WIDTH), BF16),
        scratch_shapes=[pltpu.VMEM((NSA_ROWS, 1), F32), pltpu.VMEM((NSA_ROWS, LANES), F32)],
        compiler_params=_cparams(("parallel", "arbitrary")),
        name="nsa",
    )(proj, proj, proj, proj, proj, kb, kc, vc, ovt)


def _merge_kernel(ya_ref, yb_ref, yc_ref, g0_ref, g1_ref, g2_ref, wb_ref, wo_ref, h_ref, o_ref):
    merged = None
    for y_ref, g_ref, t in ((ya_ref, g0_ref, 0), (yb_ref, g1_ref, 1), (yc_ref, g2_ref, 2)):
        term = _sigmoid(g_ref[...].astype(F32)) * jnp.dot(y_ref[...], wb_ref[t], preferred_element_type=F32)
        merged = term if merged is None else merged + term
    o_ref[...] = h_ref[...] + jnp.dot(merged.astype(BF16), wo_ref[...], preferred_element_type=F32)


def _merge(ya, yb, yc, proj, wb, wo, h, tm=512):
    n = h.shape[0]
    y_spec = pl.BlockSpec((tm, MIX_WIDTH), lambda i: (i, 0))
    g_spec = lambda t: pl.BlockSpec((tm, D_MODEL), lambda i: (i, COL_BRG // D_MODEL + t))
    return pl.pallas_call(
        _merge_kernel,
        grid=(n // tm,),
        in_specs=[y_spec, y_spec, y_spec, g_spec(0), g_spec(1), g_spec(2),
                  pl.BlockSpec((N_BRANCH, MIX_WIDTH, D_MODEL), lambda i: (0, 0, 0)),
                  pl.BlockSpec((D_MODEL, D_MODEL), lambda i: (0, 0)),
                  pl.BlockSpec((tm, D_MODEL), lambda i: (i, 0))],
        out_specs=pl.BlockSpec((tm, D_MODEL), lambda i: (i, 0)),
        out_shape=jax.ShapeDtypeStruct((n, D_MODEL), F32),
        compiler_params=_cparams(("parallel",)),
        name="merge",
    )(ya, yb, yc, proj, proj, proj, wb, wo, h)


def _ffn_kernel(h_ref, g_ref, wg_ref, wu_ref, wd_ref, o_ref, xn_ref, acc_ref):
    f = pl.program_id(1)

    @pl.when(f == 0)
    def _():
        xn_ref[...] = _rms(h_ref[...], g_ref[...]).astype(BF16)
        acc_ref[...] = h_ref[...]

    xn = xn_ref[...]
    a = jnp.dot(xn, wg_ref[...], preferred_element_type=F32)
    u = jnp.dot(xn, wu_ref[...], preferred_element_type=F32)
    hid = (a * _sigmoid(a) * u).astype(BF16)
    acc_ref[...] += jnp.dot(hid, wd_ref[...], preferred_element_type=F32)

    @pl.when(f == pl.num_programs(1) - 1)
    def _():
        o_ref[...] = acc_ref[...]


def _ffn(h, g, wg, wu, wd, tm=1024, tf=256):
    n = h.shape[0]
    dff = wg.shape[1]
    return pl.pallas_call(
        _ffn_kernel,
        grid=(n // tm, dff // tf),
        in_specs=[
            pl.BlockSpec((tm, D_MODEL), lambda i, f: (i, 0)),
            pl.BlockSpec((1, D_MODEL), lambda i, f: (0, 0)),
            pl.BlockSpec((D_MODEL, tf), lambda i, f: (0, f)),
            pl.BlockSpec((D_MODEL, tf), lambda i, f: (0, f)),
            pl.BlockSpec((tf, D_MODEL), lambda i, f: (f, 0)),
        ],
        out_specs=pl.BlockSpec((tm, D_MODEL), lambda i, f: (i, 0)),
        out_shape=jax.ShapeDtypeStruct((n, D_MODEL), F32),
        scratch_shapes=[pltpu.VMEM((tm, D_MODEL), BF16), pltpu.VMEM((tm, D_MODEL), F32)],
        compiler_params=_cparams(("parallel", "arbitrary")),
        name="ffn",
    )(h, g, wg, wu, wd)


RT_TM = 1024
RT_TF = 896
RT_TD = 512


def _gate_kernel(h_ref, g_ref, rw_ref, gate_ref):
    tm = h_ref.shape[0]
    lane = lax.broadcasted_iota(jnp.int32, (tm, LANES), 1)
    xn = _rms(h_ref[...], g_ref[...])
    logits = jnp.dot(xn, rw_ref[...], preferred_element_type=F32, precision=lax.Precision.HIGHEST)
    logits = jnp.where(lane < N_EXPERTS, logits, -jnp.inf)
    m1 = jnp.max(logits, axis=-1, keepdims=True)
    i1 = jnp.min(jnp.where(logits == m1, lane, LANES), axis=-1, keepdims=True)
    rest = jnp.where(lane == i1, -jnp.inf, logits)
    m2 = jnp.max(rest, axis=-1, keepdims=True)
    i2 = jnp.min(jnp.where(rest == m2, lane, LANES), axis=-1, keepdims=True)
    e2 = jnp.exp(m2 - m1)
    gate_ref[...] = jnp.where(lane == i1, 1.0 / (1.0 + e2), 0.0) + jnp.where(lane == i2, e2 / (1.0 + e2), 0.0)


def _gate(h, g, rw, tm=1024):
    n = h.shape[0]
    return pl.pallas_call(
        _gate_kernel,
        grid=(n // tm,),
        in_specs=[
            pl.BlockSpec((tm, D_MODEL), lambda i: (i, 0)),
            pl.BlockSpec((1, D_MODEL), lambda i: (0, 0)),
            pl.BlockSpec((D_MODEL, LANES), lambda i: (0, 0)),
        ],
        out_specs=pl.BlockSpec((tm, LANES), lambda i: (i, 0)),
        out_shape=jax.ShapeDtypeStruct((n, LANES), F32),
        compiler_params=_cparams(("parallel",)),
        name="moe_gate",
    )(h, g, rw)


def _row_copy(src_ref, src_row, dst_ref, dst_row, sem):
    return pltpu.make_async_copy(src_ref.at[pl.ds(src_row, 1), :], dst_ref.at[pl.ds(dst_row, 1), :], sem)


def _dispatch_kernel(pad_ref, pa_ref, pb_ref, h_ref, xs_ref, hbuf_ref, zero_ref, fetch_sems, row_sems, sems):
    i, n_steps = pl.program_id(0), pl.num_programs(0)
    slot, parity = i % 3, i % 2

    def fetch(block, s):
        return pltpu.make_async_copy(h_ref.at[pl.ds(block * RT_TD, RT_TD), :], hbuf_ref.at[s], fetch_sems.at[s])

    @pl.when(i == 0)
    def _():
        fetch(0, 0).start()

    @pl.when(i + 1 < n_steps)
    def _():
        fetch(i + 1, (i + 1) % 3).start()

    fetch(i, slot).wait()
    src = hbuf_ref.at[slot]

    def send(t, carry):
        _row_copy(src, t, xs_ref, pa_ref[0, 0, t], row_sems.at[0, parity]).start()
        _row_copy(src, t, xs_ref, pb_ref[0, 0, t], row_sems.at[1, parity]).start(priority=1)
        return carry

    lax.fori_loop(0, RT_TD, send, 0, unroll=4)

    @pl.when(i == 0)
    def _():
        zero_ref[...] = jnp.zeros_like(zero_ref)
        for e in range(N_EXPERTS + 1):
            lo, hi = pad_ref[2 * e], pad_ref[2 * e + 1]
            lax.fori_loop(lo, hi, lambda r, c: (_row_copy(zero_ref, 0, xs_ref, r, sems.at[0]).start(), c)[1], 0)
        for e in range(N_EXPERTS + 1):
            lo, hi = pad_ref[2 * e], pad_ref[2 * e + 1]
            lax.fori_loop(lo, hi, lambda r, c: (_row_copy(zero_ref, 0, xs_ref, r, sems.at[0]).wait(), c)[1], 0)

    def drain(p):
        for s in range(2):
            pltpu.make_async_copy(hbuf_ref.at[0], xs_ref.at[pl.ds(0, RT_TD), :], row_sems.at[s, p]).wait()

    @pl.when(i > 0)
    def _():
        drain(1 - parity)

    @pl.when(i == n_steps - 1)
    def _():
        drain(parity)


def _dispatch(pad_bounds, pa, pb, h, n_rows):
    n = h.shape[0]
    idx_spec = pl.BlockSpec((1, 1, RT_TD), lambda i, *_: (i, 0, 0), memory_space=pltpu.SMEM)
    grid_spec = pltpu.PrefetchScalarGridSpec(
        num_scalar_prefetch=1,
        grid=(n // RT_TD,),
        in_specs=[idx_spec, idx_spec, pl.BlockSpec(memory_space=pl.ANY)],
        out_specs=pl.BlockSpec(memory_space=pl.ANY),
        scratch_shapes=[pltpu.VMEM((3, RT_TD, D_MODEL), F32), pltpu.VMEM((8, D_MODEL), F32),
                        pltpu.SemaphoreType.DMA((3,)), pltpu.SemaphoreType.DMA((2, 2)),
                        pltpu.SemaphoreType.DMA((1,))],
    )
    return pl.pallas_call(
        _dispatch_kernel,
        grid_spec=grid_spec,
        out_shape=jax.ShapeDtypeStruct((n_rows, D_MODEL), F32),
        compiler_params=_cparams(("arbitrary",)),
        name="moe_dispatch",
    )(pad_bounds, pa.reshape(-1, 1, RT_TD), pb.reshape(-1, 1, RT_TD), h)


def _sorted_experts_kernel(te_ref, tx_ref, live_ref, xs_ref, g_ref, wg_ref, wu_ref, wd_ref, y_ref, xb_ref):
    i, f = pl.program_id(0), pl.program_id(1)

    @pl.when(f == 0)
    def _():
        xb_ref[...] = _rms(xs_ref[...], g_ref[...]).astype(BF16)
        y_ref[...] = jnp.zeros_like(y_ref)

    @pl.when(live_ref[i] > 0)
    def _():
        x = xb_ref[...]
        a = jnp.dot(x, wg_ref[0], preferred_element_type=F32)
        u = jnp.dot(x, wu_ref[0], preferred_element_type=F32)
        y_ref[...] += jnp.dot((a * _sigmoid(a) * u).astype(BF16), wd_ref[0], preferred_element_type=F32)


def _sorted_experts(tile_expert, tile_src, tile_live, xs, g, wg, wu, wd):
    n_rows = xs.shape[0]
    ne, _, dff = wg.shape
    grid_spec = pltpu.PrefetchScalarGridSpec(
        num_scalar_prefetch=3,
        grid=(n_rows // RT_TM, dff // RT_TF),
        in_specs=[
            pl.BlockSpec((RT_TM, D_MODEL), lambda i, f, te, tx, lv: (tx[i], 0)),
            pl.BlockSpec((1, D_MODEL), lambda i, f, *_: (0, 0)),
            pl.BlockSpec((1, D_MODEL, RT_TF), lambda i, f, te, tx, lv: (te[i], 0, f)),
            pl.BlockSpec((1, D_MODEL, RT_TF), lambda i, f, te, tx, lv: (te[i], 0, f)),
            pl.BlockSpec((1, RT_TF, D_MODEL), lambda i, f, te, tx, lv: (te[i], f, 0)),
        ],
        out_specs=pl.BlockSpec((RT_TM, D_MODEL), lambda i, f, *_: (i, 0)),
        scratch_shapes=[pltpu.VMEM((RT_TM, D_MODEL), BF16)],
    )
    return pl.pallas_call(
        _sorted_experts_kernel,
        grid_spec=grid_spec,
        out_shape=jax.ShapeDtypeStruct((n_rows, D_MODEL), F32),
        compiler_params=_cparams(("parallel", "arbitrary")),
        name="moe_experts",
    )(tile_expert, tile_src, tile_live, xs, g, wg, wu, wd)


def _combine_kernel(pa_ref, pb_ref, pa_next_ref, pb_next_ref, ga_ref, gb_ref, h_ref, y_ref, o_ref, ya_ref, yb_ref, sems):
    i, n_steps = pl.program_id(0), pl.num_programs(0)
    slot = i % 2

    def fetch(pa, pb, s):
        def body(t, carry):
            _row_copy(y_ref, pa[0, 0, t], ya_ref.at[s], t, sems.at[0, s]).start()
            _row_copy(y_ref, pb[0, 0, t], yb_ref.at[s], t, sems.at[1, s]).start(priority=1)
            return carry

        lax.fori_loop(0, RT_TD, body, 0, unroll=4)

    @pl.when(i == 0)
    def _():
        fetch(pa_ref, pb_ref, 0)

    @pl.when(i + 1 < n_steps)
    def _():
        fetch(pa_next_ref, pb_next_ref, 1 - slot)

    pltpu.make_async_copy(y_ref.at[pl.ds(0, RT_TD), :], ya_ref.at[slot], sems.at[0, slot]).wait()
    pltpu.make_async_copy(y_ref.at[pl.ds(0, RT_TD), :], yb_ref.at[slot], sems.at[1, slot]).wait()
    o_ref[...] = h_ref[...] + ga_ref[...] * ya_ref[slot] + gb_ref[...] * yb_ref[slot]


def _combine(pa, pb, ga, gb, h, y):
    n = h.shape[0]
    n_steps = n // RT_TD
    idx_spec = pl.BlockSpec((1, 1, RT_TD), lambda i: (i, 0, 0), memory_space=pltpu.SMEM)
    next_spec = pl.BlockSpec((1, 1, RT_TD), lambda i: (jnp.minimum(i + 1, n_steps - 1), 0, 0), memory_space=pltpu.SMEM)
    col_spec = pl.BlockSpec((RT_TD, 1), lambda i: (i, 0))
    row_spec = pl.BlockSpec((RT_TD, D_MODEL), lambda i: (i, 0))
    pa, pb = pa.reshape(-1, 1, RT_TD), pb.reshape(-1, 1, RT_TD)
    return pl.pallas_call(
        _combine_kernel,
        grid=(n_steps,),
        in_specs=[idx_spec, idx_spec, next_spec, next_spec, col_spec, col_spec, row_spec,
                  pl.BlockSpec(memory_space=pl.ANY)],
        out_specs=row_spec,
        out_shape=jax.ShapeDtypeStruct((n, D_MODEL), F32),
        scratch_shapes=[pltpu.VMEM((2, RT_TD, D_MODEL), F32), pltpu.VMEM((2, RT_TD, D_MODEL), F32),
                        pltpu.SemaphoreType.DMA((2, 2))],
        compiler_params=_cparams(("arbitrary",)),
        name="moe_combine",
    )(pa, pb, pa, pb, ga, gb, h, y)


def _moe_sorted(h, g, rw, wg, wu, wd):
    n = h.shape[0]
    n_tiles = 2 * n // RT_TM + N_EXPERTS
    gate = _gate(h, g, rw)[:, :N_EXPERTS]
    routed = gate > 0
    r = routed.astype(jnp.int32)
    incl = jnp.cumsum(r, axis=0)
    count = incl[-1]
    padded = (count + RT_TM - 1) // RT_TM * RT_TM
    ends = jnp.cumsum(padded)
    pos = (ends - padded)[None, :] + incl - 1
    order = jnp.cumsum(r, axis=1)
    first, second = routed & (order == 1), routed & (order == 2)
    pick = lambda m, v: jnp.sum(jnp.where(m, v, 0), axis=1)
    pa = pick(first, pos)
    pb = jnp.where(jnp.any(second, axis=1), pick(second, pos), pa)
    ga, gb = pick(first, gate)[:, None], pick(second, gate)[:, None]
    pad_lo = jnp.concatenate([ends - padded + count, ends[-1:]])
    pad_hi = jnp.concatenate([ends, jnp.full((1,), n_tiles * RT_TM, ends.dtype)])
    pad_bounds = jnp.stack([pad_lo, pad_hi], axis=1).reshape(-1).astype(jnp.int32)
    tile_start = jnp.arange(n_tiles, dtype=jnp.int32) * RT_TM
    tile_live = (tile_start < ends[-1]).astype(jnp.int32)
    tile_expert = jnp.minimum(jnp.sum(tile_start[:, None] >= ends[None, :], axis=1), N_EXPERTS - 1).astype(jnp.int32)
    tile_src = jnp.minimum(jnp.arange(n_tiles, dtype=jnp.int32), ends[-1] // RT_TM - 1)
    xs = _dispatch(pad_bounds, pa.astype(jnp.int32), pb.astype(jnp.int32), h, n_tiles * RT_TM)
    y = _sorted_experts(tile_expert, tile_src, tile_live, xs, g, wg, wu, wd)
    return _combine(pa.astype(jnp.int32), pb.astype(jnp.int32), ga, gb, h, y)


def _final_norm_kernel(h_ref, g_ref, o_ref):
    o_ref[...] = _rms(h_ref[...], g_ref[...])


def _final_norm(h, g, tm=1024):
    n = h.shape[0]
    return pl.pallas_call(
        _final_norm_kernel,
        grid=(n // tm,),
        in_specs=[pl.BlockSpec((tm, D_MODEL), lambda i: (i, 0)), pl.BlockSpec((1, D_MODEL), lambda i: (0, 0))],
        out_specs=pl.BlockSpec((tm, D_MODEL), lambda i: (i, 0)),
        out_shape=jax.ShapeDtypeStruct((n, D_MODEL), F32),
        compiler_params=_cparams(("parallel",)),
        name="final_norm",
    )(h, g)


def _rope_tables(seq):
    half = ROPE_DIMS // 2
    inv_freq = ROPE_THETA ** (-jnp.arange(0, ROPE_DIMS, 2, dtype=F32) / ROPE_DIMS)
    ang = jnp.arange(seq, dtype=F32)[:, None] * inv_freq[None, :]
    cos, sin = jnp.cos(ang), jnp.sin(ang)
    ones = jnp.ones((seq, HEAD_DIM - ROPE_DIMS), F32)
    zeros_h = jnp.zeros((seq, half), F32)
    zeros_r = jnp.zeros((seq, HEAD_DIM - ROPE_DIMS), F32)
    c = jnp.concatenate([cos, cos, ones], axis=1)
    s1 = jnp.concatenate([-sin, zeros_h, zeros_r], axis=1)
    s2 = jnp.concatenate([zeros_h, sin, zeros_r], axis=1)
    rep = LANES // HEAD_DIM
    return tuple(jnp.tile(t, (1, rep)) for t in (c, s1, s2))


def _layout_w_in(w):
    scale = HEAD_DIM ** -0.5 * LOG2E
    a_in, sb_q, sb_k, sb_v, c_q, c_kv, c_g, br_g = jnp.split(
        w, np.cumsum([2 * MIX_WIDTH, MIX_WIDTH, MIX_WIDTH, MIX_WIDTH, MIX_WIDTH, 6 * HEAD_DIM, 3 * N_HEADS])[:].tolist(),
        axis=-1)
    k_cmp, v_cmp, k_slc, v_slc, k_win, v_win = jnp.split(c_kv, 6, axis=-1)
    zeros = lambda width: jnp.zeros((w.shape[0], width), w.dtype)
    cols = [a_in, sb_q * scale, sb_k, sb_v,
            v_slc, v_cmp, v_win, c_g, zeros(HEAD_DIM - 3 * N_HEADS),
            c_q * scale, k_slc, k_cmp, k_win, zeros(HEAD_DIM),
            br_g]
    out = jnp.concatenate(cols, axis=-1)
    assert out.shape[-1] == PROJ_W
    return out.astype(BF16)


def kernel(x, norm1_g, w_in, sgu_norm_g, sgu_w, sgu_b, cmp_pos, cmp_w1, cmp_w2, w_branch, w_out, norm2_g,
           ffn_w_gate, ffn_w_up, ffn_w_down, router_w, moe_w_gate, moe_w_up, moe_w_down, final_norm_g):
    batch, seq, _ = x.shape
    n = batch * seq
    depth = norm1_g.shape[0]
    rope_c, rope_s1, rope_s2 = _rope_tables(seq)
    dff_pad = -(-D_FF // 256) * 256
    h = x.reshape(n, D_MODEL)
    for layer in range(depth):
        proj = _norm_proj(h, norm1_g[layer][None, :], _layout_w_in(w_in[layer]), rope_c, rope_s1, rope_s2, seq)
        sgu_bias = jnp.repeat(sgu_b[layer].T, HEAD_DIM, axis=1)
        y_a = _sgu(proj, sgu_norm_g[layer][None, :], sgu_w[layer], sgu_bias)
        y_b = _stick_breaking(proj, batch, seq)
        groups = seq // CMP_STRIDE
        gk = proj[:, COL_KX + HEAD_DIM:COL_KX + 2 * HEAD_DIM].reshape(batch, groups, CMP_STRIDE * HEAD_DIM)
        gv = proj[:, COL_VX + HEAD_DIM:COL_VX + 2 * HEAD_DIM].reshape(batch, groups, CMP_STRIDE * HEAD_DIM)
        kc, vc = _compress(gk, gv, cmp_pos[layer].reshape(2, 1, CMP_LEN * HEAD_DIM),
                           cmp_w1[layer].astype(BF16), cmp_w2[layer].astype(BF16))
        y_c = _nsa(proj, kc, vc, batch, seq)
        h = _merge(y_a, y_b, y_c, proj, w_branch[layer].astype(BF16), w_out[layer].astype(BF16), h)
        j = layer // 2
        g2 = norm2_g[layer][None, :]
        if layer % 2 == 0:
            pad = dff_pad - D_FF
            wg = jnp.pad(ffn_w_gate[j], ((0, 0), (0, pad))).astype(BF16)
            wu = jnp.pad(ffn_w_up[j], ((0, 0), (0, pad))).astype(BF16)
            wd = jnp.pad(ffn_w_down[j], ((0, pad), (0, 0))).astype(BF16)
            h = _ffn(h, g2, wg, wu, wd)
        else:
            rw = jnp.pad(router_w[j], ((0, 0), (0, LANES - N_EXPERTS)))
            h = _moe_sorted(h, g2, rw, moe_w_gate[j].astype(BF16), moe_w_up[j].astype(BF16), moe_w_down[j].astype(BF16))
    return _final_norm(h, final_norm_g[None, :]).reshape(batch, seq, D_MODEL)
```

```python
import numpy as np
import jax
import jax.numpy as jnp
from jax import lax
from jax.experimental import pallas as pl
from jax.experimental.pallas import tpu as pltpu

F32 = jnp.float32
BF16 = jnp.bfloat16

D_MODEL = 1024
HEAD_DIM = 64
N_HEADS = 4
MIX_WIDTH = N_HEADS * HEAD_DIM
ROPE_DIMS = HEAD_DIM // 4
ROPE_THETA = 500000.0
EPS = 1e-6
SGU_CHUNK = 128
CMP_LEN = 32
CMP_STRIDE = 16
CMP_HIDDEN = 256
SEL_LEN = 64
SEL_TOPN = 16
WINDOW = 512
FORCE_SCORE = 1e4
N_BRANCH = 3
D_FF = 2752
N_EXPERTS = 8
D_FF_EXPERT = 3584
NEG = -1e30
LOG2E = 1.4426950408889634

LANES = 128
VMEM_LIMIT = 56 * 1024 * 1024

PROJ_TILE = 1024
COL_A = 0
COL_SBQ, COL_SBK, COL_SBV = 512, 768, 1024
COL_VX = 1280
COL_CQ = 1536
COL_KX = 1792
COL_BRG = 2048
PROJ_W = COL_BRG + N_BRANCH * D_MODEL
ROPE_TILE = COL_CQ // PROJ_TILE
ROPE_START = COL_CQ % PROJ_TILE
assert COL_BRG == (ROPE_TILE + 1) * PROJ_TILE


def _cparams(sem):
    return pltpu.CompilerParams(dimension_semantics=sem, vmem_limit_bytes=VMEM_LIMIT)


def _sigmoid(x):
    return 1.0 / (1.0 + jnp.exp(-x))


def _rms(x, g):
    return x * lax.rsqrt(jnp.mean(x * x, axis=-1, keepdims=True) + EPS) * g


def _norm_proj_kernel(h_ref, g_ref, w_ref, c_ref, s1_ref, s2_ref, o_ref, xn_ref):
    j = pl.program_id(1)

    @pl.when(j == 0)
    def _():
        xn_ref[...] = _rms(h_ref[...], g_ref[...]).astype(BF16)

    acc = jnp.dot(xn_ref[...], w_ref[...], preferred_element_type=F32)

    @pl.when(j != ROPE_TILE)
    def _():
        o_ref[...] = acc.astype(o_ref.dtype)

    @pl.when(j == ROPE_TILE)
    def _():
        c, s1, s2 = c_ref[...], s1_ref[...], s2_ref[...]
        o_ref[:, :ROPE_START] = acc[:, :ROPE_START].astype(o_ref.dtype)
        for g in range(ROPE_START // LANES, PROJ_TILE // LANES):
            x = acc[:, g * LANES:(g + 1) * LANES]
            y = x * c + pltpu.roll(x, LANES - 8, 1) * s1 + pltpu.roll(x, 8, 1) * s2
            o_ref[:, g * LANES:(g + 1) * LANES] = y.astype(o_ref.dtype)


def _norm_proj(h, g, w, rope_c, rope_s1, rope_s2, seq, tm=2048):
    n = h.shape[0]
    assert seq % tm == 0, "a projection tile must not straddle two sequences (rotary tables are per position)"
    nseq = seq // tm
    return pl.pallas_call(
        _norm_proj_kernel,
        grid=(n // tm, PROJ_W // PROJ_TILE),
        in_specs=[
            pl.BlockSpec((tm, D_MODEL), lambda i, j: (i, 0)),
            pl.BlockSpec((1, D_MODEL), lambda i, j: (0, 0)),
            pl.BlockSpec((D_MODEL, PROJ_TILE), lambda i, j: (0, j)),
            pl.BlockSpec((tm, LANES), lambda i, j: (i % nseq, 0)),
            pl.BlockSpec((tm, LANES), lambda i, j: (i % nseq, 0)),
            pl.BlockSpec((tm, LANES), lambda i, j: (i % nseq, 0)),
        ],
        out_specs=pl.BlockSpec((tm, PROJ_TILE), lambda i, j: (i, j)),
        out_shape=jax.ShapeDtypeStruct((n, PROJ_W), BF16),
        scratch_shapes=[pltpu.VMEM((tm, D_MODEL), BF16)],
        compiler_params=_cparams(("parallel", "arbitrary")),
        name="norm_proj",
    )(h, g, w, rope_c, rope_s1, rope_s2)


def _sgu_kernel(z_ref, g_ref, w_ref, b_ref, o_ref):
    tm = z_ref.shape[0]
    a = jax.nn.gelu(z_ref[...].astype(F32))
    u = a[:, :MIX_WIDTH]
    v = _rms(a[:, MIX_WIDTH:], g_ref[...]).astype(BF16)
    row = lax.broadcasted_iota(jnp.int32, (SGU_CHUNK, SGU_CHUNK), 0)
    col = lax.broadcasted_iota(jnp.int32, (SGU_CHUNK, SGU_CHUNK), 1)
    ws = [jnp.where(row >= col, w_ref[gi], 0.0).astype(BF16) for gi in range(N_HEADS)]
    bias = b_ref[...]
    for c in range(tm // SGU_CHUNK):
        rows = slice(c * SGU_CHUNK, (c + 1) * SGU_CHUNK)
        mix = jnp.concatenate(
            [jnp.dot(ws[gi], v[rows, gi * HEAD_DIM:(gi + 1) * HEAD_DIM], preferred_element_type=F32)
             for gi in range(N_HEADS)], axis=1)
        o_ref[rows, :] = (u[rows, :] * (mix + bias)).astype(o_ref.dtype)


def _sgu(proj, g, w, bias, tm=512):
    n = proj.shape[0]
    return pl.pallas_call(
        _sgu_kernel,
        grid=(n // tm,),
        in_specs=[
            pl.BlockSpec((tm, 2 * MIX_WIDTH), lambda i: (i, COL_A // (2 * MIX_WIDTH))),
            pl.BlockSpec((1, MIX_WIDTH), lambda i: (0, 0)),
            pl.BlockSpec((N_HEADS, SGU_CHUNK, SGU_CHUNK), lambda i: (0, 0, 0)),
            pl.BlockSpec((SGU_CHUNK, MIX_WIDTH), lambda i: (0, 0)),
        ],
        out_specs=pl.BlockSpec((tm, MIX_WIDTH), lambda i: (i, 0)),
        out_shape=jax.ShapeDtypeStruct((n, MIX_WIDTH), BF16),
        compiler_params=_cparams(("parallel",)),
        name="sgu",
    )(proj, g, w, bias)


SB_TQ = 1024
SB_TK = 1024
SB_SUB = 256
SB_R = SB_TQ // SB_TK


def _sb_tile(q_ref, k_ref, v_ref, u_ref, acc_ref, carry_ref, k_off):
    for j in reversed(range(SB_TK // SB_SUB)):
        keys = slice(j * SB_SUB, (j + 1) * SB_SUB)
        _sb_sub_tile(q_ref, k_ref[keys, :], v_ref[keys, :], u_ref[...], acc_ref, carry_ref,
                     None if k_off is None else k_off + j * SB_SUB)


def _sb_sub_tile(q_ref, k, v, u, acc_ref, carry_ref, k_off):
    diag = k_off is not None
    rows = slice(k_off if diag else 0, SB_TQ)
    n_rows = rows.stop - rows.start
    q = q_ref[rows, :]
    if diag:
        mask = (lax.broadcasted_iota(jnp.int32, (n_rows, SB_SUB), 1)
                < lax.broadcasted_iota(jnp.int32, (n_rows, SB_SUB), 0))
    pvs = []
    for h in range(N_HEADS):
        hs = slice(h * HEAD_DIM, (h + 1) * HEAD_DIM)
        z = lax.dot_general(q[:, hs], k[:, hs], (((1,), (1,)), ((), ())), preferred_element_type=F32)
        log_beta = jnp.minimum(z, 0.0) - jnp.log(1.0 + jnp.exp2(-jnp.abs(z))) * LOG2E
        log_1m = log_beta - z
        if diag:
            log_1m = jnp.where(mask, log_1m, 0.0)
        carry = carry_ref[h, rows, :]
        terms = log_1m.astype(BF16)
        r = jnp.dot(terms, u, preferred_element_type=F32)
        a = jnp.exp2(log_beta + r + carry)
        if diag:
            a = jnp.where(mask, a, 0.0)
        carry_ref[h, rows, :] = carry + r[:, 0:1] + terms[:, 0:1].astype(F32)
        pvs.append(jnp.dot(a.astype(BF16), v[:, hs], preferred_element_type=F32))
    acc_ref[rows, :] += jnp.concatenate(pvs, axis=1)


def _sb_kernel(qi_ref, ki_ref, q_ref, k_ref, v_ref, u_ref, o_ref, acc_ref, carry_ref):
    step = pl.program_id(1)
    qi = qi_ref[step]
    ki = ki_ref[step]

    @pl.when(ki == SB_R * qi + SB_R - 1)
    def _():
        acc_ref[...] = jnp.zeros_like(acc_ref)
        carry_ref[...] = jnp.zeros_like(carry_ref)

    for d in range(SB_R):
        @pl.when(ki == SB_R * qi + d)
        def _(d=d):
            _sb_tile(q_ref, k_ref, v_ref, u_ref, acc_ref, carry_ref, d * SB_TK)

    @pl.when(ki < SB_R * qi)
    def _():
        _sb_tile(q_ref, k_ref, v_ref, u_ref, acc_ref, carry_ref, None)

    @pl.when(ki == 0)
    def _():
        o_ref[...] = acc_ref[...].astype(o_ref.dtype)


def _stick_breaking(proj, batch, seq):
    n = proj.shape[0]
    nq, nk = seq // SB_TQ, seq // SB_TK
    qi_tab = np.array([qi for qi in range(nq) for _ in range(SB_R * (qi + 1))], np.int32)
    ki_tab = np.array([ki for qi in range(nq) for ki in range(SB_R * (qi + 1) - 1, -1, -1)], np.int32)
    u = jnp.asarray((np.arange(SB_SUB)[:, None] > np.arange(SB_SUB)[None, :]).astype(np.float32), BF16)
    qb, kb, vb = COL_SBQ // MIX_WIDTH, COL_SBK // MIX_WIDTH, COL_SBV // MIX_WIDTH
    grid_spec = pltpu.PrefetchScalarGridSpec(
        num_scalar_prefetch=2,
        grid=(batch, len(qi_tab)),
        in_specs=[
            pl.BlockSpec((SB_TQ, MIX_WIDTH), lambda b, s, qt, kt: (b * nq + qt[s], qb)),
            pl.BlockSpec((SB_TK, MIX_WIDTH), lambda b, s, qt, kt: (b * nk + kt[s], kb)),
            pl.BlockSpec((SB_TK, MIX_WIDTH), lambda b, s, qt, kt: (b * nk + kt[s], vb)),
            pl.BlockSpec((SB_SUB, SB_SUB), lambda b, s, qt, kt: (0, 0)),
        ],
        out_specs=pl.BlockSpec((SB_TQ, MIX_WIDTH), lambda b, s, qt, kt: (b * nq + qt[s], 0)),
        scratch_shapes=[pltpu.VMEM((SB_TQ, MIX_WIDTH), F32), pltpu.VMEM((N_HEADS, SB_TQ, 1), F32)],
    )
    return pl.pallas_call(
        _sb_kernel,
        grid_spec=grid_spec,
        out_shape=jax.ShapeDtypeStruct((n, MIX_WIDTH), BF16),
        compiler_params=_cparams(("parallel", "arbitrary")),
        name="stick_breaking",
    )(jnp.asarray(qi_tab), jnp.asarray(ki_tab), proj, proj, proj, u)


def _compress_kernel(gk_ref, gv_ref, pos_ref, w1_ref, w2_ref, kc_ref, vc_ref):
    half = CMP_STRIDE * HEAD_DIM
    for t, (g_ref, o_ref) in enumerate(((gk_ref, kc_ref), (gv_ref, vc_ref))):
        g = g_ref[0].astype(F32)
        top = (g + pos_ref[t, :, :half]).astype(BF16)
        bot = (g + pos_ref[t, :, half:]).astype(BF16)
        a = jnp.dot(top, w1_ref[t, :half, :], preferred_element_type=F32)
        b = jnp.dot(bot, w1_ref[t, half:, :], preferred_element_type=F32)
        hid = jax.nn.gelu(a + pltpu.roll(b, b.shape[0] - 1, 0))
        out = jnp.dot(hid.astype(BF16), w2_ref[t], preferred_element_type=F32)
        o_ref[0] = jnp.concatenate([out, jnp.zeros_like(out)], axis=1).astype(o_ref.dtype)


def _compress(gk, gv, pos, w1, w2):
    b, m, width = gk.shape
    out_spec = pl.BlockSpec((1, m, LANES), lambda i: (i, 0, 0))
    out_shape = jax.ShapeDtypeStruct((b, m, LANES), BF16)
    return pl.pallas_call(
        _compress_kernel,
        grid=(b,),
        in_specs=[
            pl.BlockSpec((1, m, width), lambda i: (i, 0, 0)),
            pl.BlockSpec((1, m, width), lambda i: (i, 0, 0)),
            pl.BlockSpec((2, 1, CMP_LEN * HEAD_DIM), lambda i: (0, 0, 0)),
            pl.BlockSpec((2, CMP_LEN * HEAD_DIM, CMP_HIDDEN), lambda i: (0, 0, 0)),
            pl.BlockSpec((2, CMP_HIDDEN, HEAD_DIM), lambda i: (0, 0, 0)),
        ],
        out_specs=[out_spec, out_spec],
        out_shape=[out_shape, out_shape],
        compiler_params=_cparams(("parallel",)),
        name="nsa_compress",
    )(gk, gv, pos, w1, w2)


NSA_TQ = 256
NSA_CK = 512


NSA_ROWS = N_HEADS * NSA_TQ
SEL_OFF = 1e30


def _per_head(x, fn):
    return jnp.concatenate([fn(x[h * NSA_TQ:(h + 1) * NSA_TQ]) for h in range(N_HEADS)], axis=0)


def _nsa_kernel(q_ref, kx_ref, vx_ref, kw_ref, vw_ref, kb_ref, kc_ref, vc_ref, ovt_ref, o_ref, m_ref, acc_ref):
    seq = kx_ref.shape[0]
    n_sel = seq // SEL_LEN
    n_cmp = kc_ref.shape[1]
    q0 = pl.program_id(1) * NSA_TQ
    NT = (((1,), (1,)), ((), ()))
    lo_half = lax.broadcasted_iota(jnp.int32, (NSA_TQ, LANES), 1) < HEAD_DIM
    qpos_c = q0 + lax.broadcasted_iota(jnp.int32, (NSA_TQ, 1), 0)

    qs = []
    for pair in range(N_HEADS // 2):
        x = q_ref[:, pair * LANES:(pair + 1) * LANES].astype(F32)
        qs += [jnp.where(lo_half, x, 0.0), jnp.where(lo_half, pltpu.roll(x, HEAD_DIM, 1), 0.0)]
    q4 = jnp.concatenate(qs, axis=0).astype(BF16)

    ones_half = lax.broadcasted_iota(jnp.int32, (1, LANES), 1) >= HEAD_DIM

    def masked_exp(s, mask):
        bias = jnp.where(mask, 0.0, NEG)
        s = _per_head(s, lambda t: t + bias)
        return jnp.exp2(s - jnp.max(s, axis=-1, keepdims=True))

    def attend(p, vals):
        return jnp.dot(p.astype(BF16), jnp.where(ones_half, 1.0, vals).astype(BF16), preferred_element_type=F32)

    def normalized(r):
        return r / jnp.where(ones_half, 1.0, pltpu.roll(r, HEAD_DIM, 1))

    cmp_end = lax.broadcasted_iota(jnp.int32, (NSA_TQ, n_cmp), 1) * CMP_STRIDE + (CMP_LEN - 1)
    p_cmp = masked_exp(lax.dot_general(q4, kc_ref[0], NT, preferred_element_type=F32), cmp_end <= qpos_c)
    has_cmp = jnp.concatenate([qpos_c >= CMP_LEN - 1] * N_HEADS, axis=0)
    p_cmp = p_cmp * jnp.where(has_cmp, 1.0 / jnp.sum(p_cmp, axis=-1, keepdims=True), 0.0)
    r_cmp = jnp.dot(p_cmp.astype(BF16), vc_ref[0], preferred_element_type=F32)
    p_sum = sum(p_cmp[h * NSA_TQ:(h + 1) * NSA_TQ] for h in range(N_HEADS))

    hi = p_sum.astype(BF16)
    lo = (p_sum - hi.astype(F32)).astype(BF16)
    imp = lax.dot_general(ovt_ref[...], jnp.concatenate([hi, lo], axis=1), NT, preferred_element_type=F32)
    blk = lax.broadcasted_iota(jnp.int32, (n_sel, NSA_TQ), 0)
    qpos_r = q0 + lax.broadcasted_iota(jnp.int32, (n_sel, NSA_TQ), 1)
    valid = blk * SEL_LEN <= qpos_r
    forced = (blk == 0) | (blk == jnp.right_shift(qpos_r, 6))
    score = jnp.where(valid, imp + jnp.where(forced, FORCE_SCORE, 0.0), -jnp.inf)
    groups = [score[8 * g:8 * g + 8] for g in range(n_sel // 8)]
    ranks = [jnp.zeros((8, NSA_TQ), F32) for _ in groups]
    row8 = lax.broadcasted_iota(jnp.int32, (8, NSA_TQ), 0)
    for i in range(n_sel):
        ci = jnp.broadcast_to(score[i:i + 1], (8, NSA_TQ))
        for g, sg in enumerate(groups):
            if 8 * g > i:
                beats = ci >= sg
            elif 8 * g + 7 < i:
                beats = ci > sg
            else:
                beats = (ci > sg) | ((ci == sg) & (row8 + 8 * g > i))
            ranks[g] = ranks[g] + jnp.where(beats, 1.0, 0.0)
    sel = jnp.where(valid & (jnp.concatenate(ranks, axis=0) < SEL_TOPN), 0.0, -SEL_OFF)
    sel = jnp.concatenate([sel, jnp.full((LANES - n_sel, NSA_TQ), -SEL_OFF, F32)], axis=0).T
    aug = lambda t: jnp.concatenate([q4, jnp.concatenate([t.astype(BF16)] * N_HEADS, axis=0)], axis=1)
    q_aug = aug(sel)
    blk_lane = lax.broadcasted_iota(jnp.int32, (NSA_TQ, LANES), 1)
    q_aug_past = aug(jnp.where(blk_lane * SEL_LEN >= q0, -SEL_OFF, sel))

    w_start = pl.multiple_of(jnp.maximum(q0 - WINDOW, 0), NSA_TQ)
    kwpos = w_start + lax.broadcasted_iota(jnp.int32, (NSA_TQ, WINDOW + NSA_TQ), 1)
    s_win = lax.dot_general(q4, kw_ref[pl.ds(w_start, WINDOW + NSA_TQ), :], NT, preferred_element_type=F32)
    p_win = masked_exp(s_win, (kwpos <= qpos_c) & (kwpos > qpos_c - WINDOW))
    r_win = normalized(attend(p_win, vw_ref[pl.ds(w_start, WINDOW + NSA_TQ), :]))

    def slab_stats(q, start, size, causal):
        keys = jnp.concatenate([kx_ref[pl.ds(start, size), :], kb_ref[pl.ds(start, size), :]], axis=1)
        s = lax.dot_general(q, keys, NT, preferred_element_type=F32)
        if causal:
            bias = jnp.where(start + lax.broadcasted_iota(jnp.int32, (NSA_TQ, size), 1) <= qpos_c, 0.0, NEG)
            s = _per_head(s, lambda t: t + bias)
        m_c = jnp.max(s, axis=-1, keepdims=True)
        return m_c, attend(jnp.exp2(s - m_c), vx_ref[pl.ds(start, size), :])

    m_d, a_d = slab_stats(q_aug, pl.multiple_of(q0, NSA_TQ), NSA_TQ, True)
    m_ref[...] = m_d
    acc_ref[...] = a_d

    def pair_body(i, carry):
        m0, a0 = slab_stats(q_aug_past, pl.multiple_of(2 * i * NSA_CK, NSA_CK), NSA_CK, False)
        m1, a1 = slab_stats(q_aug_past, pl.multiple_of((2 * i + 1) * NSA_CK, NSA_CK), NSA_CK, False)
        m_old = m_ref[...]
        m_new = jnp.maximum(m_old, jnp.maximum(m0, m1))
        acc_ref[...] = (jnp.exp2(m_old - m_new) * acc_ref[...] + jnp.exp2(m0 - m_new) * a0
                        + jnp.exp2(m1 - m_new) * a1)
        m_ref[...] = m_new
        return carry

    lax.fori_loop(0, (q0 + 2 * NSA_CK - 1) // (2 * NSA_CK), pair_body, 0)

    r_slc = normalized(acc_ref[...])
    gates = _sigmoid(vw_ref[pl.ds(pl.multiple_of(q0, NSA_TQ), NSA_TQ), :].astype(F32))
    g = [jnp.concatenate([gates[:, HEAD_DIM + 3 * h + t:HEAD_DIM + 3 * h + t + 1] for h in range(N_HEADS)], axis=0)
         for t in range(N_BRANCH)]
    y = g[0] * r_cmp + g[1] * r_slc + g[2] * r_win
    for pair in range(N_HEADS // 2):
        even = y[2 * pair * NSA_TQ:(2 * pair + 1) * NSA_TQ]
        odd = y[(2 * pair + 1) * NSA_TQ:(2 * pair + 2) * NSA_TQ]
        o_ref[:, pair * LANES:(pair + 1) * LANES] = jnp.where(lo_half, even, pltpu.roll(odd, HEAD_DIM, 1)).astype(o_ref.dtype)


def _nsa(proj, kc, vc, batch, seq):
    n = proj.shape[0]
    nqb = seq // NSA_TQ
    n_sel = seq // SEL_LEN
    n_cmp = kc.shape[1]
    assert n_sel % 8 == 0 and n_sel <= LANES and seq % (2 * NSA_CK) == 0
    cs = (np.arange(n_cmp) * CMP_STRIDE)[None, :]
    ss = (np.arange(n_sel) * SEL_LEN)[:, None]
    ovt = np.clip(np.minimum(cs + CMP_LEN, ss + SEL_LEN) - np.maximum(cs, ss), 0, None).astype(np.float32) / CMP_LEN
    ovt = jnp.asarray(np.concatenate([ovt, ovt], axis=1), BF16)
    kb = jnp.asarray((np.arange(seq)[:, None] // SEL_LEN == np.arange(LANES)[None, :]).astype(np.float32), BF16)
    seq_spec = lambda col: pl.BlockSpec((seq, LANES), lambda b, i: (b, col))
    cmp_spec = pl.BlockSpec((1, n_cmp, LANES), lambda b, i: (b, 0, 0))
    return pl.pallas_call(
        _nsa_kernel,
        grid=(batch, nqb),
        in_specs=[
            pl.BlockSpec((NSA_TQ, MIX_WIDTH), lambda b, i: (b * nqb + i, COL_CQ // MIX_WIDTH)),
            seq_spec(COL_KX // LANES),
            seq_spec(COL_VX // LANES),
            seq_spec(COL_KX // LANES + 1),
            seq_spec(COL_VX // LANES + 1),
            pl.BlockSpec((seq, LANES), lambda b, i: (0, 0)),
            cmp_spec, cmp_spec,
            pl.BlockSpec((n_sel, 2 * n_cmp), lambda b, i: (0, 0)),
        ],
        out_specs=pl.BlockSpec((NSA_TQ, MIX_WIDTH), lambda b, i: (b * nqb + i, 0)),
        out_shape=jax.ShapeDtypeStruct((n, MIX_WIDTH), BF16),
        scratch_shapes=[pltpu.VMEM((NSA_ROWS, 1), F32), pltpu.VMEM((NSA_ROWS, LANES), F32)],
        compiler_params=_cparams(("parallel", "arbitrary")),
        name="nsa",
    )(proj, proj, proj, proj, proj, kb, kc, vc, ovt)


def _merge_kernel(ya_ref, yb_ref, yc_ref, g0_ref, g1_ref, g2_ref, wb_ref, wo_ref, h_ref, o_ref):
    merged = None
    for y_ref, g_ref, t in ((ya_ref, g0_ref, 0), (yb_ref, g1_ref, 1), (yc_ref, g2_ref, 2)):
        term = _sigmoid(g_ref[...].astype(F32)) * jnp.dot(y_ref[...], wb_ref[t], preferred_element_type=F32)
        merged = term if merged is None else merged + term
    o_ref[...] = h_ref[...] + jnp.dot(merged.astype(BF16), wo_ref[...], preferred_element_type=F32)


def _merge(ya, yb, yc, proj, wb, wo, h, tm=512):
    n = h.shape[0]
    y_spec = pl.BlockSpec((tm, MIX_WIDTH), lambda i: (i, 0))
    g_spec = lambda t: pl.BlockSpec((tm, D_MODEL), lambda i: (i, COL_BRG // D_MODEL + t))
    return pl.pallas_call(
        _merge_kernel,
        grid=(n // tm,),
        in_specs=[y_spec, y_spec, y_spec, g_spec(0), g_spec(1), g_spec(2),
                  pl.BlockSpec((N_BRANCH, MIX_WIDTH, D_MODEL), lambda i: (0, 0, 0)),
                  pl.BlockSpec((D_MODEL, D_MODEL), lambda i: (0, 0)),
                  pl.BlockSpec((tm, D_MODEL), lambda i: (i, 0))],
        out_specs=pl.BlockSpec((tm, D_MODEL), lambda i: (i, 0)),
        out_shape=jax.ShapeDtypeStruct((n, D_MODEL), F32),
        compiler_params=_cparams(("parallel",)),
        name="merge",
    )(ya, yb, yc, proj, proj, proj, wb, wo, h)


def _ffn_kernel(h_ref, g_ref, wg_ref, wu_ref, wd_ref, o_ref, xn_ref, acc_ref):
    f = pl.program_id(1)

    @pl.when(f == 0)
    def _():
        xn_ref[...] = _rms(h_ref[...], g_ref[...]).astype(BF16)
        acc_ref[...] = h_ref[...]

    xn = xn_ref[...]
    a = jnp.dot(xn, wg_ref[...], preferred_element_type=F32)
    u = jnp.dot(xn, wu_ref[...], preferred_element_type=F32)
    hid = (a * _sigmoid(a) * u).astype(BF16)
    acc_ref[...] += jnp.dot(hid, wd_ref[...], preferred_element_type=F32)

    @pl.when(f == pl.num_programs(1) - 1)
    def _():
        o_ref[...] = acc_ref[...]


def _ffn(h, g, wg, wu, wd, tm=1024, tf=256):
    n = h.shape[0]
    dff = wg.shape[1]
    return pl.pallas_call(
        _ffn_kernel,
        grid=(n // tm, dff // tf),
        in_specs=[
            pl.BlockSpec((tm, D_MODEL), lambda i, f: (i, 0)),
            pl.BlockSpec((1, D_MODEL), lambda i, f: (0, 0)),
            pl.BlockSpec((D_MODEL, tf), lambda i, f: (0, f)),
            pl.BlockSpec((D_MODEL, tf), lambda i, f: (0, f)),
            pl.BlockSpec((tf, D_MODEL), lambda i, f: (f, 0)),
        ],
        out_specs=pl.BlockSpec((tm, D_MODEL), lambda i, f: (i, 0)),
        out_shape=jax.ShapeDtypeStruct((n, D_MODEL), F32),
        scratch_shapes=[pltpu.VMEM((tm, D_MODEL), BF16), pltpu.VMEM((tm, D_MODEL), F32)],
        compiler_params=_cparams(("parallel", "arbitrary")),
        name="ffn",
    )(h, g, wg, wu, wd)


RT_TM = 1024
RT_TF = 896
RT_TD = 512


def _gate_kernel(h_ref, g_ref, rw_ref, gate_ref):
    tm = h_ref.shape[0]
    lane = lax.broadcasted_iota(jnp.int32, (tm, LANES), 1)
    xn = _rms(h_ref[...], g_ref[...])
    logits = jnp.dot(xn, rw_ref[...], preferred_element_type=F32, precision=lax.Precision.HIGHEST)
    logits = jnp.where(lane < N_EXPERTS, logits, -jnp.inf)
    m1 = jnp.max(logits, axis=-1, keepdims=True)
    i1 = jnp.min(jnp.where(logits == m1, lane, LANES), axis=-1, keepdims=True)
    rest = jnp.where(lane == i1, -jnp.inf, logits)
    m2 = jnp.max(rest, axis=-1, keepdims=True)
    i2 = jnp.min(jnp.where(rest == m2, lane, LANES), axis=-1, keepdims=True)
    e2 = jnp.exp(m2 - m1)
    gate_ref[...] = jnp.where(lane == i1, 1.0 / (1.0 + e2), 0.0) + jnp.where(lane == i2, e2 / (1.0 + e2), 0.0)


def _gate(h, g, rw, tm=1024):
    n = h.shape[0]
    return pl.pallas_call(
        _gate_kernel,
        grid=(n // tm,),
        in_specs=[
            pl.BlockSpec((tm, D_MODEL), lambda i: (i, 0)),
            pl.BlockSpec((1, D_MODEL), lambda i: (0, 0)),
            pl.BlockSpec((D_MODEL, LANES), lambda i: (0, 0)),
        ],
        out_specs=pl.BlockSpec((tm, LANES), lambda i: (i, 0)),
        out_shape=jax.ShapeDtypeStruct((n, LANES), F32),
        compiler_params=_cparams(("parallel",)),
        name="moe_gate",
    )(h, g, rw)


def _row_copy(src_ref, src_row, dst_ref, dst_row, sem):
    return pltpu.make_async_copy(src_ref.at[pl.ds(src_row, 1), :], dst_ref.at[pl.ds(dst_row, 1), :], sem)


def _dispatch_kernel(pad_ref, pa_ref, pb_ref, h_ref, xs_ref, hbuf_ref, zero_ref, fetch_sems, row_sems, sems):
    i, n_steps = pl.program_id(0), pl.num_programs(0)
    slot, parity = i % 3, i % 2

    def fetch(block, s):
        return pltpu.make_async_copy(h_ref.at[pl.ds(block * RT_TD, RT_TD), :], hbuf_ref.at[s], fetch_sems.at[s])

    @pl.when(i == 0)
    def _():
        fetch(0, 0).start()

    @pl.when(i + 1 < n_steps)
    def _():
        fetch(i + 1, (i + 1) % 3).start()

    fetch(i, slot).wait()
    src = hbuf_ref.at[slot]

    def send(t, carry):
        _row_copy(src, t, xs_ref, pa_ref[0, 0, t], row_sems.at[0, parity]).start()
        _row_copy(src, t, xs_ref, pb_ref[0, 0, t], row_sems.at[1, parity]).start(priority=1)
        return carry

    lax.fori_loop(0, RT_TD, send, 0, unroll=4)

    @pl.when(i == 0)
    def _():
        zero_ref[...] = jnp.zeros_like(zero_ref)
        for e in range(N_EXPERTS + 1):
            lo, hi = pad_ref[2 * e], pad_ref[2 * e + 1]
            lax.fori_loop(lo, hi, lambda r, c: (_row_copy(zero_ref, 0, xs_ref, r, sems.at[0]).start(), c)[1], 0)
        for e in range(N_EXPERTS + 1):
            lo, hi = pad_ref[2 * e], pad_ref[2 * e + 1]
            lax.fori_loop(lo, hi, lambda r, c: (_row_copy(zero_ref, 0, xs_ref, r, sems.at[0]).wait(), c)[1], 0)

    def drain(p):
        for s in range(2):
            pltpu.make_async_copy(hbuf_ref.at[0], xs_ref.at[pl.ds(0, RT_TD), :], row_sems.at[s, p]).wait()

    @pl.when(i > 0)
    def _():
        drain(1 - parity)

    @pl.when(i == n_steps - 1)
    def _():
        drain(parity)


def _dispatch(pad_bounds, pa, pb, h, n_rows):
    n = h.shape[0]
    idx_spec = pl.BlockSpec((1, 1, RT_TD), lambda i, *_: (i, 0, 0), memory_space=pltpu.SMEM)
    grid_spec = pltpu.PrefetchScalarGridSpec(
        num_scalar_prefetch=1,
        grid=(n // RT_TD,),
        in_specs=[idx_spec, idx_spec, pl.BlockSpec(memory_space=pl.ANY)],
        out_specs=pl.BlockSpec(memory_space=pl.ANY),
        scratch_shapes=[pltpu.VMEM((3, RT_TD, D_MODEL), F32), pltpu.VMEM((8, D_MODEL), F32),
                        pltpu.SemaphoreType.DMA((3,)), pltpu.SemaphoreType.DMA((2, 2)),
                        pltpu.SemaphoreType.DMA((1,))],
    )
    return pl.pallas_call(
        _dispatch_kernel,
        grid_spec=grid_spec,
        out_shape=jax.ShapeDtypeStruct((n_rows, D_MODEL), F32),
        compiler_params=_cparams(("arbitrary",)),
        name="moe_dispatch",
    )(pad_bounds, pa.reshape(-1, 1, RT_TD), pb.reshape(-1, 1, RT_TD), h)


def _sorted_experts_kernel(te_ref, tx_ref, live_ref, xs_ref, g_ref, wg_ref, wu_ref, wd_ref, y_ref, xb_ref):
    i, f = pl.program_id(0), pl.program_id(1)

    @pl.when(f == 0)
    def _():
        xb_ref[...] = _rms(xs_ref[...], g_ref[...]).astype(BF16)
        y_ref[...] = jnp.zeros_like(y_ref)

    @pl.when(live_ref[i] > 0)
    def _():
        x = xb_ref[...]
        a = jnp.dot(x, wg_ref[0], preferred_element_type=F32)
        u = jnp.dot(x, wu_ref[0], preferred_element_type=F32)
        y_ref[...] += jnp.dot((a * _sigmoid(a) * u).astype(BF16), wd_ref[0], preferred_element_type=F32)


def _sorted_experts(tile_expert, tile_src, tile_live, xs, g, wg, wu, wd):
    n_rows = xs.shape[0]
    ne, _, dff = wg.shape
    grid_spec = pltpu.PrefetchScalarGridSpec(
        num_scalar_prefetch=3,
        grid=(n_rows // RT_TM, dff // RT_TF),
        in_specs=[
            pl.BlockSpec((RT_TM, D_MODEL), lambda i, f, te, tx, lv: (tx[i], 0)),
            pl.BlockSpec((1, D_MODEL), lambda i, f, *_: (0, 0)),
            pl.BlockSpec((1, D_MODEL, RT_TF), lambda i, f, te, tx, lv: (te[i], 0, f)),
            pl.BlockSpec((1, D_MODEL, RT_TF), lambda i, f, te, tx, lv: (te[i], 0, f)),
            pl.BlockSpec((1, RT_TF, D_MODEL), lambda i, f, te, tx, lv: (te[i], f, 0)),
        ],
        out_specs=pl.BlockSpec((RT_TM, D_MODEL), lambda i, f, *_: (i, 0)),
        scratch_shapes=[pltpu.VMEM((RT_TM, D_MODEL), BF16)],
    )
    return pl.pallas_call(
        _sorted_experts_kernel,
        grid_spec=grid_spec,
        out_shape=jax.ShapeDtypeStruct((n_rows, D_MODEL), F32),
        compiler_params=_cparams(("parallel", "arbitrary")),
        name="moe_experts",
    )(tile_expert, tile_src, tile_live, xs, g, wg, wu, wd)


def _combine_kernel(pa_ref, pb_ref, pa_next_ref, pb_next_ref, ga_ref, gb_ref, h_ref, y_ref, o_ref, ya_ref, yb_ref, sems):
    i, n_steps = pl.program_id(0), pl.num_programs(0)
    slot = i % 2

    def fetch(pa, pb, s):
        def body(t, carry):
            _row_copy(y_ref, pa[0, 0, t], ya_ref.at[s], t, sems.at[0, s]).start()
            _row_copy(y_ref, pb[0, 0, t], yb_ref.at[s], t, sems.at[1, s]).start(priority=1)
            return carry

        lax.fori_loop(0, RT_TD, body, 0, unroll=4)

    @pl.when(i == 0)
    def _():
        fetch(pa_ref, pb_ref, 0)

    @pl.when(i + 1 < n_steps)
    def _():
        fetch(pa_next_ref, pb_next_ref, 1 - slot)

    pltpu.make_async_copy(y_ref.at[pl.ds(0, RT_TD), :], ya_ref.at[slot], sems.at[0, slot]).wait()
    pltpu.make_async_copy(y_ref.at[pl.ds(0, RT_TD), :], yb_ref.at[slot], sems.at[1, slot]).wait()
    o_ref[...] = h_ref[...] + ga_ref[...] * ya_ref[slot] + gb_ref[...] * yb_ref[slot]


def _combine(pa, pb, ga, gb, h, y):
    n = h.shape[0]
    n_steps = n // RT_TD
    idx_spec = pl.BlockSpec((1, 1, RT_TD), lambda i: (i, 0, 0), memory_space=pltpu.SMEM)
    next_spec = pl.BlockSpec((1, 1, RT_TD), lambda i: (jnp.minimum(i + 1, n_steps - 1), 0, 0), memory_space=pltpu.SMEM)
    col_spec = pl.BlockSpec((RT_TD, 1), lambda i: (i, 0))
    row_spec = pl.BlockSpec((RT_TD, D_MODEL), lambda i: (i, 0))
    pa, pb = pa.reshape(-1, 1, RT_TD), pb.reshape(-1, 1, RT_TD)
    return pl.pallas_call(
        _combine_kernel,
        grid=(n_steps,),
        in_specs=[idx_spec, idx_spec, next_spec, next_spec, col_spec, col_spec, row_spec,
                  pl.BlockSpec(memory_space=pl.ANY)],
        out_specs=row_spec,
        out_shape=jax.ShapeDtypeStruct((n, D_MODEL), F32),
        scratch_shapes=[pltpu.VMEM((2, RT_TD, D_MODEL), F32), pltpu.VMEM((2, RT_TD, D_MODEL), F32),
                        pltpu.SemaphoreType.DMA((2, 2))],
        compiler_params=_cparams(("arbitrary",)),
        name="moe_combine",
    )(pa, pb, pa, pb, ga, gb, h, y)


def _moe_sorted(h, g, rw, wg, wu, wd):
    n = h.shape[0]
    n_tiles = 2 * n // RT_TM + N_EXPERTS
    gate = _gate(h, g, rw)[:, :N_EXPERTS]
    routed = gate > 0
    r = routed.astype(jnp.int32)
    incl = jnp.cumsum(r, axis=0)
    count = incl[-1]
    padded = (count + RT_TM - 1) // RT_TM * RT_TM
    ends = jnp.cumsum(padded)
    pos = (ends - padded)[None, :] + incl - 1
    order = jnp.cumsum(r, axis=1)
    first, second = routed & (order == 1), routed & (order == 2)
    pick = lambda m, v: jnp.sum(jnp.where(m, v, 0), axis=1)
    pa = pick(first, pos)
    pb = jnp.where(jnp.any(second, axis=1), pick(second, pos), pa)
    ga, gb = pick(first, gate)[:, None], pick(second, gate)[:, None]
    pad_lo = jnp.concatenate([ends - padded + count, ends[-1:]])
    pad_hi = jnp.concatenate([ends, jnp.full((1,), n_tiles * RT_TM, ends.dtype)])
    pad_bounds = jnp.stack([pad_lo, pad_hi], axis=1).reshape(-1).astype(jnp.int32)
    tile_start = jnp.arange(n_tiles, dtype=jnp.int32) * RT_TM
    tile_live = (tile_start < ends[-1]).astype(jnp.int32)
    tile_expert = jnp.minimum(jnp.sum(tile_start[:, None] >= ends[None, :], axis=1), N_EXPERTS - 1).astype(jnp.int32)
    tile_src = jnp.minimum(jnp.arange(n_tiles, dtype=jnp.int32), ends[-1] // RT_TM - 1)
    xs = _dispatch(pad_bounds, pa.astype(jnp.int32), pb.astype(jnp.int32), h, n_tiles * RT_TM)
    y = _sorted_experts(tile_expert, tile_src, tile_live, xs, g, wg, wu, wd)
    return _combine(pa.astype(jnp.int32), pb.astype(jnp.int32), ga, gb, h, y)


def _final_norm_kernel(h_ref, g_ref, o_ref):
    o_ref[...] = _rms(h_ref[...], g_ref[...])


def _final_norm(h, g, tm=1024):
    n = h.shape[0]
    return pl.pallas_call(
        _final_norm_kernel,
        grid=(n // tm,),
        in_specs=[pl.BlockSpec((tm, D_MODEL), lambda i: (i, 0)), pl.BlockSpec((1, D_MODEL), lambda i: (0, 0))],
        out_specs=pl.BlockSpec((tm, D_MODEL), lambda i: (i, 0)),
        out_shape=jax.ShapeDtypeStruct((n, D_MODEL), F32),
        compiler_params=_cparams(("parallel",)),
        name="final_norm",
    )(h, g)


def _rope_tables(seq):
    half = ROPE_DIMS // 2
    inv_freq = ROPE_THETA ** (-jnp.arange(0, ROPE_DIMS, 2, dtype=F32) / ROPE_DIMS)
    ang = jnp.arange(seq, dtype=F32)[:, None] * inv_freq[None, :]
    cos, sin = jnp.cos(ang), jnp.sin(ang)
    ones = jnp.ones((seq, HEAD_DIM - ROPE_DIMS), F32)
    zeros_h = jnp.zeros((seq, half), F32)
    zeros_r = jnp.zeros((seq, HEAD_DIM - ROPE_DIMS), F32)
    c = jnp.concatenate([cos, cos, ones], axis=1)
    s1 = jnp.concatenate([-sin, zeros_h, zeros_r], axis=1)
    s2 = jnp.concatenate([zeros_h, sin, zeros_r], axis=1)
    rep = LANES // HEAD_DIM
    return tuple(jnp.tile(t, (1, rep)) for t in (c, s1, s2))


def _layout_w_in(w):
    scale = HEAD_DIM ** -0.5 * LOG2E
    a_in, sb_q, sb_k, sb_v, c_q, c_kv, c_g, br_g = jnp.split(
        w, np.cumsum([2 * MIX_WIDTH, MIX_WIDTH, MIX_WIDTH, MIX_WIDTH, MIX_WIDTH, 6 * HEAD_DIM, 3 * N_HEADS])[:].tolist(),
        axis=-1)
    k_cmp, v_cmp, k_slc, v_slc, k_win, v_win = jnp.split(c_kv, 6, axis=-1)
    zeros = lambda width: jnp.zeros((w.shape[0], width), w.dtype)
    cols = [a_in, sb_q * scale, sb_k, sb_v,
            v_slc, v_cmp, v_win, c_g, zeros(HEAD_DIM - 3 * N_HEADS),
            c_q * scale, k_slc, k_cmp, k_win, zeros(HEAD_DIM),
            br_g]
    out = jnp.concatenate(cols, axis=-1)
    assert out.shape[-1] == PROJ_W
    return out.astype(BF16)


def kernel(x, norm1_g, w_in, sgu_norm_g, sgu_w, sgu_b, cmp_pos, cmp_w1, cmp_w2, w_branch, w_out, norm2_g,
           ffn_w_gate, ffn_w_up, ffn_w_down, router_w, moe_w_gate, moe_w_up, moe_w_down, final_norm_g):
    batch, seq, _ = x.shape
    n = batch * seq
    depth = norm1_g.shape[0]
    rope_c, rope_s1, rope_s2 = _rope_tables(seq)
    dff_pad = -(-D_FF // 256) * 256
    h = x.reshape(n, D_MODEL)
    for layer in range(depth):
        proj = _norm_proj(h, norm1_g[layer][None, :], _layout_w_in(w_in[layer]), rope_c, rope_s1, rope_s2, seq)
        sgu_bias = jnp.repeat(sgu_b[layer].T, HEAD_DIM, axis=1)
        y_a = _sgu(proj, sgu_norm_g[layer][None, :], sgu_w[layer], sgu_bias)
        y_b = _stick_breaking(proj, batch, seq)
        groups = seq // CMP_STRIDE
        gk = proj[:, COL_KX + HEAD_DIM:COL_KX + 2 * HEAD_DIM].reshape(batch, groups, CMP_STRIDE * HEAD_DIM)
        gv = proj[:, COL_VX + HEAD_DIM:COL_VX + 2 * HEAD_DIM].reshape(batch, groups, CMP_STRIDE * HEAD_DIM)
        kc, vc = _compress(gk, gv, cmp_pos[layer].reshape(2, 1, CMP_LEN * HEAD_DIM),
                           cmp_w1[layer].astype(BF16), cmp_w2[layer].astype(BF16))
        y_c = _nsa(proj, kc, vc, batch, seq)
        h = _merge(y_a, y_b, y_c, proj, w_branch[layer].astype(BF16), w_out[layer].astype(BF16), h)
        j = layer // 2
        g2 = norm2_g[layer][None, :]
        if layer % 2 == 0:
            pad = dff_pad - D_FF
            wg = jnp.pad(ffn_w_gate[j], ((0, 0), (0, pad))).astype(BF16)
            wu = jnp.pad(ffn_w_up[j], ((0, 0), (0, pad))).astype(BF16)
            wd = jnp.pad(ffn_w_down[j], ((0, pad), (0, 0))).astype(BF16)
            h = _ffn(h, g2, wg, wu, wd)
        else:
            rw = jnp.pad(router_w[j], ((0, 0), (0, LANES - N_EXPERTS)))
            h = _moe_sorted(h, g2, rw, moe_w_gate[j].astype(BF16), moe_w_up[j].astype(BF16), moe_w_down[j].astype(BF16))
    return _final_norm(h, final_norm_g[None, :]).reshape(batch, seq, D_MODEL)
```

```python
import numpy as np
import jax
import jax.numpy as jnp
from jax import lax
from jax.experimental import pallas as pl
from jax.experimental.pallas import tpu as pltpu

F32 = jnp.float32
BF16 = jnp.bfloat16

D_MODEL = 1024
HEAD_DIM = 64
N_HEADS = 4
MIX_WIDTH = N_HEADS * HEAD_DIM
ROPE_DIMS = HEAD_DIM // 4
ROPE_THETA = 500000.0
EPS = 1e-6
SGU_CHUNK = 128
CMP_LEN = 32
CMP_STRIDE = 16
CMP_HIDDEN = 256
SEL_LEN = 64
SEL_TOPN = 16
WINDOW = 512
FORCE_SCORE = 1e4
N_BRANCH = 3
D_FF = 2752
N_EXPERTS = 8
D_FF_EXPERT = 3584
NEG = -1e30
LOG2E = 1.4426950408889634

LANES = 128
VMEM_LIMIT = 56 * 1024 * 1024

PROJ_TILE = 1024
COL_A = 0
COL_SBQ, COL_SBK, COL_SBV = 512, 768, 1024
COL_VX = 1280
COL_CQ = 1536
COL_KX = 1792
COL_BRG = 2048
PROJ_W = COL_BRG + N_BRANCH * D_MODEL
ROPE_TILE = COL_CQ // PROJ_TILE
ROPE_START = COL_CQ % PROJ_TILE
assert COL_BRG == (ROPE_TILE + 1) * PROJ_TILE


def _cparams(sem):
    return pltpu.CompilerParams(dimension_semantics=sem, vmem_limit_bytes=VMEM_LIMIT)


def _sigmoid(x):
    return 1.0 / (1.0 + jnp.exp(-x))


def _rms(x, g):
    return x * lax.rsqrt(jnp.mean(x * x, axis=-1, keepdims=True) + EPS) * g


def _norm_proj_kernel(h_ref, g_ref, w_ref, c_ref, s1_ref, s2_ref, o_ref, xn_ref):
    j = pl.program_id(1)

    @pl.when(j == 0)
    def _():
        xn_ref[...] = _rms(h_ref[...], g_ref[...]).astype(BF16)

    acc = jnp.dot(xn_ref[...], w_ref[...], preferred_element_type=F32)

    @pl.when(j != ROPE_TILE)
    def _():
        o_ref[...] = acc.astype(o_ref.dtype)

    @pl.when(j == ROPE_TILE)
    def _():
        c, s1, s2 = c_ref[...], s1_ref[...], s2_ref[...]
        o_ref[:, :ROPE_START] = acc[:, :ROPE_START].astype(o_ref.dtype)
        for g in range(ROPE_START // LANES, PROJ_TILE // LANES):
            x = acc[:, g * LANES:(g + 1) * LANES]
            y = x * c + pltpu.roll(x, LANES - 8, 1) * s1 + pltpu.roll(x, 8, 1) * s2
            o_ref[:, g * LANES:(g + 1) * LANES] = y.astype(o_ref.dtype)


def _norm_proj(h, g, w, rope_c, rope_s1, rope_s2, seq, tm=2048):
    n = h.shape[0]
    assert seq % tm == 0, "a projection tile must not straddle two sequences (rotary tables are per position)"
    nseq = seq // tm
    return pl.pallas_call(
        _norm_proj_kernel,
        grid=(n // tm, PROJ_W // PROJ_TILE),
        in_specs=[
            pl.BlockSpec((tm, D_MODEL), lambda i, j: (i, 0)),
            pl.BlockSpec((1, D_MODEL), lambda i, j: (0, 0)),
            pl.BlockSpec((D_MODEL, PROJ_TILE), lambda i, j: (0, j)),
            pl.BlockSpec((tm, LANES), lambda i, j: (i % nseq, 0)),
            pl.BlockSpec((tm, LANES), lambda i, j: (i % nseq, 0)),
            pl.BlockSpec((tm, LANES), lambda i, j: (i % nseq, 0)),
        ],
        out_specs=pl.BlockSpec((tm, PROJ_TILE), lambda i, j: (i, j)),
        out_shape=jax.ShapeDtypeStruct((n, PROJ_W), BF16),
        scratch_shapes=[pltpu.VMEM((tm, D_MODEL), BF16)],
        compiler_params=_cparams(("parallel", "arbitrary")),
        name="norm_proj",
    )(h, g, w, rope_c, rope_s1, rope_s2)


def _sgu_kernel(z_ref, g_ref, w_ref, b_ref, o_ref):
    tm = z_ref.shape[0]
    a = jax.nn.gelu(z_ref[...].astype(F32))
    u = a[:, :MIX_WIDTH]
    v = _rms(a[:, MIX_WIDTH:], g_ref[...]).astype(BF16)
    row = lax.broadcasted_iota(jnp.int32, (SGU_CHUNK, SGU_CHUNK), 0)
    col = lax.broadcasted_iota(jnp.int32, (SGU_CHUNK, SGU_CHUNK), 1)
    ws = [jnp.where(row >= col, w_ref[gi], 0.0).astype(BF16) for gi in range(N_HEADS)]
    bias = b_ref[...]
    for c in range(tm // SGU_CHUNK):
        rows = slice(c * SGU_CHUNK, (c + 1) * SGU_CHUNK)
        mix = jnp.concatenate(
            [jnp.dot(ws[gi], v[rows, gi * HEAD_DIM:(gi + 1) * HEAD_DIM], preferred_element_type=F32)
             for gi in range(N_HEADS)], axis=1)
        o_ref[rows, :] = (u[rows, :] * (mix + bias)).astype(o_ref.dtype)


def _sgu(proj, g, w, bias, tm=512):
    n = proj.shape[0]
    return pl.pallas_call(
        _sgu_kernel,
        grid=(n // tm,),
        in_specs=[
            pl.BlockSpec((tm, 2 * MIX_WIDTH), lambda i: (i, COL_A // (2 * MIX_WIDTH))),
            pl.BlockSpec((1, MIX_WIDTH), lambda i: (0, 0)),
            pl.BlockSpec((N_HEADS, SGU_CHUNK, SGU_CHUNK), lambda i: (0, 0, 0)),
            pl.BlockSpec((SGU_CHUNK, MIX_WIDTH), lambda i: (0, 0)),
        ],
        out_specs=pl.BlockSpec((tm, MIX_WIDTH), lambda i: (i, 0)),
        out_shape=jax.ShapeDtypeStruct((n, MIX_WIDTH), BF16),
        compiler_params=_cparams(("parallel",)),
        name="sgu",
    )(proj, g, w, bias)


SB_TQ = 1024
SB_TK = 512
SB_SUB = 256
SB_R = SB_TQ // SB_TK


def _sb_tile(q_ref, k_ref, v_ref, u_ref, acc_ref, carry_ref, k_off):
    for j in reversed(range(SB_TK // SB_SUB)):
        keys = slice(j * SB_SUB, (j + 1) * SB_SUB)
        _sb_sub_tile(q_ref, k_ref[keys, :], v_ref[keys, :], u_ref[...], acc_ref, carry_ref,
                     None if k_off is None else k_off + j * SB_SUB)


def _sb_sub_tile(q_ref, k, v, u, acc_ref, carry_ref, k_off):
    diag = k_off is not None
    rows = slice(k_off if diag else 0, SB_TQ)
    n_rows = rows.stop - rows.start
    q = q_ref[rows, :]
    if diag:
        mask = (lax.broadcasted_iota(jnp.int32, (n_rows, SB_SUB), 1)
                < lax.broadcasted_iota(jnp.int32, (n_rows, SB_SUB), 0))
    pvs = []
    for h in range(N_HEADS):
        hs = slice(h * HEAD_DIM, (h + 1) * HEAD_DIM)
        z = lax.dot_general(q[:, hs], k[:, hs], (((1,), (1,)), ((), ())), preferred_element_type=F32)
        log_beta = jnp.minimum(z, 0.0) - jnp.log(1.0 + jnp.exp2(-jnp.abs(z))) * LOG2E
        log_1m = log_beta - z
        if diag:
            log_1m = jnp.where(mask, log_1m, 0.0)
        carry = carry_ref[h, rows, :]
        terms = log_1m.astype(BF16)
        r = jnp.dot(terms, u, preferred_element_type=F32)
        a = jnp.exp2(log_beta + r + carry)
        if diag:
            a = jnp.where(mask, a, 0.0)
        carry_ref[h, rows, :] = carry + r[:, 0:1] + terms[:, 0:1].astype(F32)
        pvs.append(jnp.dot(a.astype(BF16), v[:, hs], preferred_element_type=F32))
    acc_ref[rows, :] += jnp.concatenate(pvs, axis=1)


def _sb_kernel(qi_ref, ki_ref, q_ref, k_ref, v_ref, u_ref, o_ref, acc_ref, carry_ref):
    step = pl.program_id(1)
    qi = qi_ref[step]
    ki = ki_ref[step]

    @pl.when(ki == SB_R * qi + SB_R - 1)
    def _():
        acc_ref[...] = jnp.zeros_like(acc_ref)
        carry_ref[...] = jnp.zeros_like(carry_ref)

    for d in range(SB_R):
        @pl.when(ki == SB_R * qi + d)
        def _(d=d):
            _sb_tile(q_ref, k_ref, v_ref, u_ref, acc_ref, carry_ref, d * SB_TK)

    @pl.when(ki < SB_R * qi)
    def _():
        _sb_tile(q_ref, k_ref, v_ref, u_ref, acc_ref, carry_ref, None)

    @pl.when(ki == 0)
    def _():
        o_ref[...] = acc_ref[...].astype(o_ref.dtype)


def _stick_breaking(proj, batch, seq):
    n = proj.shape[0]
    nq, nk = seq // SB_TQ, seq // SB_TK
    qi_tab = np.array([qi for qi in range(nq) for _ in range(SB_R * (qi + 1))], np.int32)
    ki_tab = np.array([ki for qi in range(nq) for ki in range(SB_R * (qi + 1) - 1, -1, -1)], np.int32)
    u = jnp.asarray((np.arange(SB_SUB)[:, None] > np.arange(SB_SUB)[None, :]).astype(np.float32), BF16)
    qb, kb, vb = COL_SBQ // MIX_WIDTH, COL_SBK // MIX_WIDTH, COL_SBV // MIX_WIDTH
    grid_spec = pltpu.PrefetchScalarGridSpec(
        num_scalar_prefetch=2,
        grid=(batch, len(qi_tab)),
        in_specs=[
            pl.BlockSpec((SB_TQ, MIX_WIDTH), lambda b, s, qt, kt: (b * nq + qt[s], qb)),
            pl.BlockSpec((SB_TK, MIX_WIDTH), lambda b, s, qt, kt: (b * nk + kt[s], kb)),
            pl.BlockSpec((SB_TK, MIX_WIDTH), lambda b, s, qt, kt: (b * nk + kt[s], vb)),
            pl.BlockSpec((SB_SUB, SB_SUB), lambda b, s, qt, kt: (0, 0)),
        ],
        out_specs=pl.BlockSpec((SB_TQ, MIX_WIDTH), lambda b, s, qt, kt: (b * nq + qt[s], 0)),
        scratch_shapes=[pltpu.VMEM((SB_TQ, MIX_WIDTH), F32), pltpu.VMEM((N_HEADS, SB_TQ, 1), F32)],
    )
    return pl.pallas_call(
        _sb_kernel,
        grid_spec=grid_spec,
        out_shape=jax.ShapeDtypeStruct((n, MIX_WIDTH), BF16),
        compiler_params=_cparams(("parallel", "arbitrary")),
        name="stick_breaking",
    )(jnp.asarray(qi_tab), jnp.asarray(ki_tab), proj, proj, proj, u)


def _compress_kernel(gk_ref, gv_ref, pos_ref, w1_ref, w2_ref, kc_ref, vc_ref):
    half = CMP_STRIDE * HEAD_DIM
    for t, (g_ref, o_ref) in enumerate(((gk_ref, kc_ref), (gv_ref, vc_ref))):
        g = g_ref[0].astype(F32)
        top = (g + pos_ref[t, :, :half]).astype(BF16)
        bot = (g + pos_ref[t, :, half:]).astype(BF16)
        a = jnp.dot(top, w1_ref[t, :half, :], preferred_element_type=F32)
        b = jnp.dot(bot, w1_ref[t, half:, :], preferred_element_type=F32)
        hid = jax.nn.gelu(a + pltpu.roll(b, b.shape[0] - 1, 0))
        out = jnp.dot(hid.astype(BF16), w2_ref[t], preferred_element_type=F32)
        o_ref[0] = jnp.concatenate([out, jnp.zeros_like(out)], axis=1).astype(o_ref.dtype)


def _compress(gk, gv, pos, w1, w2):
    b, m, width = gk.shape
    out_spec = pl.BlockSpec((1, m, LANES), lambda i: (i, 0, 0))
    out_shape = jax.ShapeDtypeStruct((b, m, LANES), BF16)
    return pl.pallas_call(
        _compress_kernel,
        grid=(b,),
        in_specs=[
            pl.BlockSpec((1, m, width), lambda i: (i, 0, 0)),
            pl.BlockSpec((1, m, width), lambda i: (i, 0, 0)),
            pl.BlockSpec((2, 1, CMP_LEN * HEAD_DIM), lambda i: (0, 0, 0)),
            pl.BlockSpec((2, CMP_LEN * HEAD_DIM, CMP_HIDDEN), lambda i: (0, 0, 0)),
            pl.BlockSpec((2, CMP_HIDDEN, HEAD_DIM), lambda i: (0, 0, 0)),
        ],
        out_specs=[out_spec, out_spec],
        out_shape=[out_shape, out_shape],
        compiler_params=_cparams(("parallel",)),
        name="nsa_compress",
    )(gk, gv, pos, w1, w2)


NSA_TQ = 256
NSA_CK = 512


NSA_ROWS = N_HEADS * NSA_TQ
SEL_OFF = 1e30


def _per_head(x, fn):
    return jnp.concatenate([fn(x[h * NSA_TQ:(h + 1) * NSA_TQ]) for h in range(N_HEADS)], axis=0)


def _nsa_kernel(q_ref, kx_ref, vx_ref, kw_ref, vw_ref, kb_ref, kc_ref, vc_ref, ovt_ref, o_ref, m_ref, acc_ref):
    seq = kx_ref.shape[0]
    n_sel = seq // SEL_LEN
    n_cmp = kc_ref.shape[1]
    q0 = pl.program_id(1) * NSA_TQ
    NT = (((1,), (1,)), ((), ()))
    lo_half = lax.broadcasted_iota(jnp.int32, (NSA_TQ, LANES), 1) < HEAD_DIM
    qpos_c = q0 + lax.broadcasted_iota(jnp.int32, (NSA_TQ, 1), 0)

    qs = []
    for pair in range(N_HEADS // 2):
        x = q_ref[:, pair * LANES:(pair + 1) * LANES].astype(F32)
        qs += [jnp.where(lo_half, x, 0.0), jnp.where(lo_half, pltpu.roll(x, HEAD_DIM, 1), 0.0)]
    q4 = jnp.concatenate(qs, axis=0).astype(BF16)

    ones_half = lax.broadcasted_iota(jnp.int32, (1, LANES), 1) >= HEAD_DIM

    def masked_exp(s, mask):
        bias = jnp.where(mask, 0.0, NEG)
        s = _per_head(s, lambda t: t + bias)
        return jnp.exp2(s - jnp.max(s, axis=-1, keepdims=True))

    def attend(p, vals):
        return jnp.dot(p.astype(BF16), jnp.where(ones_half, 1.0, vals).astype(BF16), preferred_element_type=F32)

    def normalized(r):
        return r / jnp.where(ones_half, 1.0, pltpu.roll(r, HEAD_DIM, 1))

    cmp_end = lax.broadcasted_iota(jnp.int32, (NSA_TQ, n_cmp), 1) * CMP_STRIDE + (CMP_LEN - 1)
    p_cmp = masked_exp(lax.dot_general(q4, kc_ref[0], NT, preferred_element_type=F32), cmp_end <= qpos_c)
    has_cmp = jnp.concatenate([qpos_c >= CMP_LEN - 1] * N_HEADS, axis=0)
    p_cmp = p_cmp * jnp.where(has_cmp, 1.0 / jnp.sum(p_cmp, axis=-1, keepdims=True), 0.0)
    r_cmp = jnp.dot(p_cmp.astype(BF16), vc_ref[0], preferred_element_type=F32)
    p_sum = sum(p_cmp[h * NSA_TQ:(h + 1) * NSA_TQ] for h in range(N_HEADS))

    hi = p_sum.astype(BF16)
    lo = (p_sum - hi.astype(F32)).astype(BF16)
    imp = lax.dot_general(ovt_ref[...], jnp.concatenate([hi, lo], axis=1), NT, preferred_element_type=F32)
    blk = lax.broadcasted_iota(jnp.int32, (n_sel, NSA_TQ), 0)
    qpos_r = q0 + lax.broadcasted_iota(jnp.int32, (n_sel, NSA_TQ), 1)
    valid = blk * SEL_LEN <= qpos_r
    forced = (blk == 0) | (blk == jnp.right_shift(qpos_r, 6))
    score = jnp.where(valid, imp + jnp.where(forced, FORCE_SCORE, 0.0), -jnp.inf)
    groups = [score[8 * g:8 * g + 8] for g in range(n_sel // 8)]
    ranks = [jnp.zeros((8, NSA_TQ), F32) for _ in groups]
    row8 = lax.broadcasted_iota(jnp.int32, (8, NSA_TQ), 0)
    for i in range(n_sel):
        ci = jnp.broadcast_to(score[i:i + 1], (8, NSA_TQ))
        for g, sg in enumerate(groups):
            if 8 * g > i:
                beats = ci >= sg
            elif 8 * g + 7 < i:
                beats = ci > sg
            else:
                beats = (ci > sg) | ((ci == sg) & (row8 + 8 * g > i))
            ranks[g] = ranks[g] + jnp.where(beats, 1.0, 0.0)
    sel = jnp.where(valid & (jnp.concatenate(ranks, axis=0) < SEL_TOPN), 0.0, -SEL_OFF)
    sel = jnp.concatenate([sel, jnp.full((LANES - n_sel, NSA_TQ), -SEL_OFF, F32)], axis=0).T
    aug = lambda t: jnp.concatenate([q4, jnp.concatenate([t.astype(BF16)] * N_HEADS, axis=0)], axis=1)
    q_aug = aug(sel)
    blk_lane = lax.broadcasted_iota(jnp.int32, (NSA_TQ, LANES), 1)
    q_aug_past = aug(jnp.where(blk_lane * SEL_LEN >= q0, -SEL_OFF, sel))

    w_start = pl.multiple_of(jnp.maximum(q0 - WINDOW, 0), NSA_TQ)
    kwpos = w_start + lax.broadcasted_iota(jnp.int32, (NSA_TQ, WINDOW + NSA_TQ), 1)
    s_win = lax.dot_general(q4, kw_ref[pl.ds(w_start, WINDOW + NSA_TQ), :], NT, preferred_element_type=F32)
    p_win = masked_exp(s_win, (kwpos <= qpos_c) & (kwpos > qpos_c - WINDOW))
    r_win = normalized(attend(p_win, vw_ref[pl.ds(w_start, WINDOW + NSA_TQ), :]))

    def slab_stats(q, start, size, causal):
        keys = jnp.concatenate([kx_ref[pl.ds(start, size), :], kb_ref[pl.ds(start, size), :]], axis=1)
        s = lax.dot_general(q, keys, NT, preferred_element_type=F32)
        if causal:
            bias = jnp.where(start + lax.broadcasted_iota(jnp.int32, (NSA_TQ, size), 1) <= qpos_c, 0.0, NEG)
            s = _per_head(s, lambda t: t + bias)
        m_c = jnp.max(s, axis=-1, keepdims=True)
        return m_c, attend(jnp.exp2(s - m_c), vx_ref[pl.ds(start, size), :])

    m_d, a_d = slab_stats(q_aug, pl.multiple_of(q0, NSA_TQ), NSA_TQ, True)
    m_ref[...] = m_d
    acc_ref[...] = a_d

    def pair_body(i, carry):
        m0, a0 = slab_stats(q_aug_past, pl.multiple_of(2 * i * NSA_CK, NSA_CK), NSA_CK, False)
        m1, a1 = slab_stats(q_aug_past, pl.multiple_of((2 * i + 1) * NSA_CK, NSA_CK), NSA_CK, False)
        m_old = m_ref[...]
        m_new = jnp.maximum(m_old, jnp.maximum(m0, m1))
        acc_ref[...] = (jnp.exp2(m_old - m_new) * acc_ref[...] + jnp.exp2(m0 - m_new) * a0
                        + jnp.exp2(m1 - m_new) * a1)
        m_ref[...] = m_new
        return carry

    lax.fori_loop(0, (q0 + 2 * NSA_CK - 1) // (2 * NSA_CK), pair_body, 0)

    r_slc = normalized(acc_ref[...])
    gates = _sigmoid(vw_ref[pl.ds(pl.multiple_of(q0, NSA_TQ), NSA_TQ), :].astype(F32))
    g = [jnp.concatenate([gates[:, HEAD_DIM + 3 * h + t:HEAD_DIM + 3 * h + t + 1] for h in range(N_HEADS)], axis=0)
         for t in range(N_BRANCH)]
    y = g[0] * r_cmp + g[1] * r_slc + g[2] * r_win
    for pair in range(N_HEADS // 2):
        even = y[2 * pair * NSA_TQ:(2 * pair + 1) * NSA_TQ]
        odd = y[(2 * pair + 1) * NSA_TQ:(2 * pair + 2) * NSA_TQ]
        o_ref[:, pair * LANES:(pair + 1) * LANES] = jnp.where(lo_half, even, pltpu.roll(odd, HEAD_DIM, 1)).astype(o_ref.dtype)


def _nsa(proj, kc, vc, batch, seq):
    n = proj.shape[0]
    nqb = seq // NSA_TQ
    n_sel = seq // SEL_LEN
    n_cmp = kc.shape[1]
    assert n_sel % 8 == 0 and n_sel <= LANES and seq % (2 * NSA_CK) == 0
    cs = (np.arange(n_cmp) * CMP_STRIDE)[None, :]
    ss = (np.arange(n_sel) * SEL_LEN)[:, None]
    ovt = np.clip(np.minimum(cs + CMP_LEN, ss + SEL_LEN) - np.maximum(cs, ss), 0, None).astype(np.float32) / CMP_LEN
    ovt = jnp.asarray(np.concatenate([ovt, ovt], axis=1), BF16)
    kb = jnp.asarray((np.arange(seq)[:, None] // SEL_LEN == np.arange(LANES)[None, :]).astype(np.float32), BF16)
    seq_spec = lambda col: pl.BlockSpec((seq, LANES), lambda b, i: (b, col))
    cmp_spec = pl.BlockSpec((1, n_cmp, LANES), lambda b, i: (b, 0, 0))
    return pl.pallas_call(
        _nsa_kernel,
        grid=(batch, nqb),
        in_specs=[
            pl.BlockSpec((NSA_TQ, MIX_WIDTH), lambda b, i: (b * nqb + i, COL_CQ // MIX_WIDTH)),
            seq_spec(COL_KX // LANES),
            seq_spec(COL_VX // LANES),
            seq_spec(COL_KX // LANES + 1),
            seq_spec(COL_VX // LANES + 1),
            pl.BlockSpec((seq, LANES), lambda b, i: (0, 0)),
            cmp_spec, cmp_spec,
            pl.BlockSpec((n_sel, 2 * n_cmp), lambda b, i: (0, 0)),
        ],
        out_specs=pl.BlockSpec((NSA_TQ, MIX_WIDTH), lambda b, i: (b * nqb + i, 0)),
        out_shape=jax.ShapeDtypeStruct((n, MIX_WIDTH), BF16),
        scratch_shapes=[pltpu.VMEM((NSA_ROWS, 1), F32), pltpu.VMEM((NSA_ROWS, LANES), F32)],
        compiler_params=_cparams(("parallel", "arbitrary")),
        name="nsa",
    )(proj, proj, proj, proj, proj, kb, kc, vc, ovt)


def _merge_kernel(ya_ref, yb_ref, yc_ref, g0_ref, g1_ref, g2_ref, wb_ref, wo_ref, h_ref, o_ref):
    merged = None
    for y_ref, g_ref, t in ((ya_ref, g0_ref, 0), (yb_ref, g1_ref, 1), (yc_ref, g2_ref, 2)):
        term = _sigmoid(g_ref[...].astype(F32)) * jnp.dot(y_ref[...], wb_ref[t], preferred_element_type=F32)
        merged = term if merged is None else merged + term
    o_ref[...] = h_ref[...] + jnp.dot(merged.astype(BF16), wo_ref[...], preferred_element_type=F32)


def _merge(ya, yb, yc, proj, wb, wo, h, tm=512):
    n = h.shape[0]
    y_spec = pl.BlockSpec((tm, MIX_WIDTH), lambda i: (i, 0))
    g_spec = lambda t: pl.BlockSpec((tm, D_MODEL), lambda i: (i, COL_BRG // D_MODEL + t))
    return pl.pallas_call(
        _merge_kernel,
        grid=(n // tm,),
        in_specs=[y_spec, y_spec, y_spec, g_spec(0), g_spec(1), g_spec(2),
                  pl.BlockSpec((N_BRANCH, MIX_WIDTH, D_MODEL), lambda i: (0, 0, 0)),
                  pl.BlockSpec((D_MODEL, D_MODEL), lambda i: (0, 0)),
                  pl.BlockSpec((tm, D_MODEL), lambda i: (i, 0))],
        out_specs=pl.BlockSpec((tm, D_MODEL), lambda i: (i, 0)),
        out_shape=jax.ShapeDtypeStruct((n, D_MODEL), F32),
        compiler_params=_cparams(("parallel",)),
        name="merge",
    )(ya, yb, yc, proj, proj, proj, wb, wo, h)


def _ffn_kernel(h_ref, g_ref, wg_ref, wu_ref, wd_ref, o_ref, xn_ref, acc_ref):
    f = pl.program_id(1)

    @pl.when(f == 0)
    def _():
        xn_ref[...] = _rms(h_ref[...], g_ref[...]).astype(BF16)
        acc_ref[...] = h_ref[...]

    xn = xn_ref[...]
    a = jnp.dot(xn, wg_ref[...], preferred_element_type=F32)
    u = jnp.dot(xn, wu_ref[...], preferred_element_type=F32)
    hid = (a * _sigmoid(a) * u).astype(BF16)
    acc_ref[...] += jnp.dot(hid, wd_ref[...], preferred_element_type=F32)

    @pl.when(f == pl.num_programs(1) - 1)
    def _():
        o_ref[...] = acc_ref[...]


def _ffn(h, g, wg, wu, wd, tm=512, tf=1408):
    n = h.shape[0]
    dff = wg.shape[1]
    return pl.pallas_call(
        _ffn_kernel,
        grid=(n // tm, dff // tf),
        in_specs=[
            pl.BlockSpec((tm, D_MODEL), lambda i, f: (i, 0)),
            pl.BlockSpec((1, D_MODEL), lambda i, f: (0, 0)),
            pl.BlockSpec((D_MODEL, tf), lambda i, f: (0, f)),
            pl.BlockSpec((D_MODEL, tf), lambda i, f: (0, f)),
            pl.BlockSpec((tf, D_MODEL), lambda i, f: (f, 0)),
        ],
        out_specs=pl.BlockSpec((tm, D_MODEL), lambda i, f: (i, 0)),
        out_shape=jax.ShapeDtypeStruct((n, D_MODEL), F32),
        scratch_shapes=[pltpu.VMEM((tm, D_MODEL), BF16), pltpu.VMEM((tm, D_MODEL), F32)],
        compiler_params=_cparams(("parallel", "arbitrary")),
        name="ffn",
    )(h, g, wg, wu, wd)


RT_TM = 1024
RT_TF = 896
RT_TD = 512


def _gate_kernel(h_ref, g_ref, rw_ref, gate_ref):
    tm = h_ref.shape[0]
    lane = lax.broadcasted_iota(jnp.int32, (tm, LANES), 1)
    xn = _rms(h_ref[...], g_ref[...])
    logits = jnp.dot(xn, rw_ref[...], preferred_element_type=F32, precision=lax.Precision.HIGHEST)
    logits = jnp.where(lane < N_EXPERTS, logits, -jnp.inf)
    m1 = jnp.max(logits, axis=-1, keepdims=True)
    i1 = jnp.min(jnp.where(logits == m1, lane, LANES), axis=-1, keepdims=True)
    rest = jnp.where(lane == i1, -jnp.inf, logits)
    m2 = jnp.max(rest, axis=-1, keepdims=True)
    i2 = jnp.min(jnp.where(rest == m2, lane, LANES), axis=-1, keepdims=True)
    e2 = jnp.exp(m2 - m1)
    gate_ref[...] = jnp.where(lane == i1, 1.0 / (1.0 + e2), 0.0) + jnp.where(lane == i2, e2 / (1.0 + e2), 0.0)


def _gate(h, g, rw, tm=1024):
    n = h.shape[0]
    return pl.pallas_call(
        _gate_kernel,
        grid=(n // tm,),
        in_specs=[
            pl.BlockSpec((tm, D_MODEL), lambda i: (i, 0)),
            pl.BlockSpec((1, D_MODEL), lambda i: (0, 0)),
            pl.BlockSpec((D_MODEL, LANES), lambda i: (0, 0)),
        ],
        out_specs=pl.BlockSpec((tm, LANES), lambda i: (i, 0)),
        out_shape=jax.ShapeDtypeStruct((n, LANES), F32),
        compiler_params=_cparams(("parallel",)),
        name="moe_gate",
    )(h, g, rw)


def _row_copy(src_ref, src_row, dst_ref, dst_row, sem):
    return pltpu.make_async_copy(src_ref.at[pl.ds(src_row, 1), :], dst_ref.at[pl.ds(dst_row, 1), :], sem)


def _dispatch_kernel(pad_ref, pa_ref, pb_ref, h_ref, xs_ref, hbuf_ref, zero_ref, fetch_sems, row_sems, sems):
    i, n_steps = pl.program_id(0), pl.num_programs(0)
    slot, parity = i % 3, i % 2

    def fetch(block, s):
        return pltpu.make_async_copy(h_ref.at[pl.ds(block * RT_TD, RT_TD), :], hbuf_ref.at[s], fetch_sems.at[s])

    @pl.when(i == 0)
    def _():
        fetch(0, 0).start()

    @pl.when(i + 1 < n_steps)
    def _():
        fetch(i + 1, (i + 1) % 3).start()

    fetch(i, slot).wait()
    src = hbuf_ref.at[slot]

    def send(t, carry):
        _row_copy(src, t, xs_ref, pa_ref[0, 0, t], row_sems.at[0, parity]).start()
        _row_copy(src, t, xs_ref, pb_ref[0, 0, t], row_sems.at[1, parity]).start(priority=1)
        return carry

    lax.fori_loop(0, RT_TD, send, 0, unroll=4)

    @pl.when(i == 0)
    def _():
        zero_ref[...] = jnp.zeros_like(zero_ref)
        for e in range(N_EXPERTS + 1):
            lo, hi = pad_ref[2 * e], pad_ref[2 * e + 1]
            lax.fori_loop(lo, hi, lambda r, c: (_row_copy(zero_ref, 0, xs_ref, r, sems.at[0]).start(), c)[1], 0)
        for e in range(N_EXPERTS + 1):
            lo, hi = pad_ref[2 * e], pad_ref[2 * e + 1]
            lax.fori_loop(lo, hi, lambda r, c: (_row_copy(zero_ref, 0, xs_ref, r, sems.at[0]).wait(), c)[1], 0)

    def drain(p):
        for s in range(2):
            pltpu.make_async_copy(hbuf_ref.at[0], xs_ref.at[pl.ds(0, RT_TD), :], row_sems.at[s, p]).wait()

    @pl.when(i > 0)
    def _():
        drain(1 - parity)

    @pl.when(i == n_steps - 1)
    def _():
        drain(parity)


def _dispatch(pad_bounds, pa, pb, h, n_rows):
    n = h.shape[0]
    idx_spec = pl.BlockSpec((1, 1, RT_TD), lambda i, *_: (i, 0, 0), memory_space=pltpu.SMEM)
    grid_spec = pltpu.PrefetchScalarGridSpec(
        num_scalar_prefetch=1,
        grid=(n // RT_TD,),
        in_specs=[idx_spec, idx_spec, pl.BlockSpec(memory_space=pl.ANY)],
        out_specs=pl.BlockSpec(memory_space=pl.ANY),
        scratch_shapes=[pltpu.VMEM((3, RT_TD, D_MODEL), F32), pltpu.VMEM((8, D_MODEL), F32),
                        pltpu.SemaphoreType.DMA((3,)), pltpu.SemaphoreType.DMA((2, 2)),
                        pltpu.SemaphoreType.DMA((1,))],
    )
    return pl.pallas_call(
        _dispatch_kernel,
        grid_spec=grid_spec,
        out_shape=jax.ShapeDtypeStruct((n_rows, D_MODEL), F32),
        compiler_params=_cparams(("arbitrary",)),
        name="moe_dispatch",
    )(pad_bounds, pa.reshape(-1, 1, RT_TD), pb.reshape(-1, 1, RT_TD), h)


def _sorted_experts_kernel(te_ref, tx_ref, live_ref, xs_ref, g_ref, wg_ref, wu_ref, wd_ref, y_ref, xb_ref):
    i, f = pl.program_id(0), pl.program_id(1)

    @pl.when(f == 0)
    def _():
        xb_ref[...] = _rms(xs_ref[...], g_ref[...]).astype(BF16)
        y_ref[...] = jnp.zeros_like(y_ref)

    @pl.when(live_ref[i] > 0)
    def _():
        x = xb_ref[...]
        a = jnp.dot(x, wg_ref[0], preferred_element_type=F32)
        u = jnp.dot(x, wu_ref[0], preferred_element_type=F32)
        y_ref[...] += jnp.dot((a * _sigmoid(a) * u).astype(BF16), wd_ref[0], preferred_element_type=F32)


def _sorted_experts(tile_expert, tile_src, tile_live, xs, g, wg, wu, wd):
    n_rows = xs.shape[0]
    ne, _, dff = wg.shape
    grid_spec = pltpu.PrefetchScalarGridSpec(
        num_scalar_prefetch=3,
        grid=(n_rows // RT_TM, dff // RT_TF),
        in_specs=[
            pl.BlockSpec((RT_TM, D_MODEL), lambda i, f, te, tx, lv: (tx[i], 0)),
            pl.BlockSpec((1, D_MODEL), lambda i, f, *_: (0, 0)),
            pl.BlockSpec((1, D_MODEL, RT_TF), lambda i, f, te, tx, lv: (te[i], 0, f)),
            pl.BlockSpec((1, D_MODEL, RT_TF), lambda i, f, te, tx, lv: (te[i], 0, f)),
            pl.BlockSpec((1, RT_TF, D_MODEL), lambda i, f, te, tx, lv: (te[i], f, 0)),
        ],
        out_specs=pl.BlockSpec((RT_TM, D_MODEL), lambda i, f, *_: (i, 0)),
        scratch_shapes=[pltpu.VMEM((RT_TM, D_MODEL), BF16)],
    )
    return pl.pallas_call(
        _sorted_experts_kernel,
        grid_spec=grid_spec,
        out_shape=jax.ShapeDtypeStruct((n_rows, D_MODEL), F32),
        compiler_params=_cparams(("parallel", "arbitrary")),
        name="moe_experts",
    )(tile_expert, tile_src, tile_live, xs, g, wg, wu, wd)


def _combine_kernel(pa_ref, pb_ref, pa_next_ref, pb_next_ref, ga_ref, gb_ref, h_ref, y_ref, o_ref, ya_ref, yb_ref, sems):
    i, n_steps = pl.program_id(0), pl.num_programs(0)
    slot = i % 2

    def fetch(pa, pb, s):
        def body(t, carry):
            _row_copy(y_ref, pa[0, 0, t], ya_ref.at[s], t, sems.at[0, s]).start()
            _row_copy(y_ref, pb[0, 0, t], yb_ref.at[s], t, sems.at[1, s]).start(priority=1)
            return carry

        lax.fori_loop(0, RT_TD, body, 0, unroll=4)

    @pl.when(i == 0)
    def _():
        fetch(pa_ref, pb_ref, 0)

    @pl.when(i + 1 < n_steps)
    def _():
        fetch(pa_next_ref, pb_next_ref, 1 - slot)

    pltpu.make_async_copy(y_ref.at[pl.ds(0, RT_TD), :], ya_ref.at[slot], sems.at[0, slot]).wait()
    pltpu.make_async_copy(y_ref.at[pl.ds(0, RT_TD), :], yb_ref.at[slot], sems.at[1, slot]).wait()
    o_ref[...] = h_ref[...] + ga_ref[...] * ya_ref[slot] + gb_ref[...] * yb_ref[slot]


def _combine(pa, pb, ga, gb, h, y):
    n = h.shape[0]
    n_steps = n // RT_TD
    idx_spec = pl.BlockSpec((1, 1, RT_TD), lambda i: (i, 0, 0), memory_space=pltpu.SMEM)
    next_spec = pl.BlockSpec((1, 1, RT_TD), lambda i: (jnp.minimum(i + 1, n_steps - 1), 0, 0), memory_space=pltpu.SMEM)
    col_spec = pl.BlockSpec((RT_TD, 1), lambda i: (i, 0))
    row_spec = pl.BlockSpec((RT_TD, D_MODEL), lambda i: (i, 0))
    pa, pb = pa.reshape(-1, 1, RT_TD), pb.reshape(-1, 1, RT_TD)
    return pl.pallas_call(
        _combine_kernel,
        grid=(n_steps,),
        in_specs=[idx_spec, idx_spec, next_spec, next_spec, col_spec, col_spec, row_spec,
                  pl.BlockSpec(memory_space=pl.ANY)],
        out_specs=row_spec,
        out_shape=jax.ShapeDtypeStruct((n, D_MODEL), F32),
        scratch_shapes=[pltpu.VMEM((2, RT_TD, D_MODEL), F32), pltpu.VMEM((2, RT_TD, D_MODEL), F32),
                        pltpu.SemaphoreType.DMA((2, 2))],
        compiler_params=_cparams(("arbitrary",)),
        name="moe_combine",
    )(pa, pb, pa, pb, ga, gb, h, y)


def _moe_sorted(h, g, rw, wg, wu, wd):
    n = h.shape[0]
    n_tiles = 2 * n // RT_TM + N_EXPERTS
    gate = _gate(h, g, rw)[:, :N_EXPERTS]
    routed = gate > 0
    r = routed.astype(jnp.int32)
    incl = jnp.cumsum(r, axis=0)
    count = incl[-1]
    padded = (count + RT_TM - 1) // RT_TM * RT_TM
    ends = jnp.cumsum(padded)
    pos = (ends - padded)[None, :] + incl - 1
    order = jnp.cumsum(r, axis=1)
    first, second = routed & (order == 1), routed & (order == 2)
    pick = lambda m, v: jnp.sum(jnp.where(m, v, 0), axis=1)
    pa = pick(first, pos)
    pb = jnp.where(jnp.any(second, axis=1), pick(second, pos), pa)
    ga, gb = pick(first, gate)[:, None], pick(second, gate)[:, None]
    pad_lo = jnp.concatenate([ends - padded + count, ends[-1:]])
    pad_hi = jnp.concatenate([ends, jnp.full((1,), n_tiles * RT_TM, ends.dtype)])
    pad_bounds = jnp.stack([pad_lo, pad_hi], axis=1).reshape(-1).astype(jnp.int32)
    tile_start = jnp.arange(n_tiles, dtype=jnp.int32) * RT_TM
    tile_live = (tile_start < ends[-1]).astype(jnp.int32)
    tile_expert = jnp.minimum(jnp.sum(tile_start[:, None] >= ends[None, :], axis=1), N_EXPERTS - 1).astype(jnp.int32)
    tile_src = jnp.minimum(jnp.arange(n_tiles, dtype=jnp.int32), ends[-1] // RT_TM - 1)
    xs = _dispatch(pad_bounds, pa.astype(jnp.int32), pb.astype(jnp.int32), h, n_tiles * RT_TM)
    y = _sorted_experts(tile_expert, tile_src, tile_live, xs, g, wg, wu, wd)
    return _combine(pa.astype(jnp.int32), pb.astype(jnp.int32), ga, gb, h, y)


def _final_norm_kernel(h_ref, g_ref, o_ref):
    o_ref[...] = _rms(h_ref[...], g_ref[...])


def _final_norm(h, g, tm=1024):
    n = h.shape[0]
    return pl.pallas_call(
        _final_norm_kernel,
        grid=(n // tm,),
        in_specs=[pl.BlockSpec((tm, D_MODEL), lambda i: (i, 0)), pl.BlockSpec((1, D_MODEL), lambda i: (0, 0))],
        out_specs=pl.BlockSpec((tm, D_MODEL), lambda i: (i, 0)),
        out_shape=jax.ShapeDtypeStruct((n, D_MODEL), F32),
        compiler_params=_cparams(("parallel",)),
        name="final_norm",
    )(h, g)


def _rope_tables(seq):
    half = ROPE_DIMS // 2
    inv_freq = ROPE_THETA ** (-jnp.arange(0, ROPE_DIMS, 2, dtype=F32) / ROPE_DIMS)
    ang = jnp.arange(seq, dtype=F32)[:, None] * inv_freq[None, :]
    cos, sin = jnp.cos(ang), jnp.sin(ang)
    ones = jnp.ones((seq, HEAD_DIM - ROPE_DIMS), F32)
    zeros_h = jnp.zeros((seq, half), F32)
    zeros_r = jnp.zeros((seq, HEAD_DIM - ROPE_DIMS), F32)
    c = jnp.concatenate([cos, cos, ones], axis=1)
    s1 = jnp.concatenate([-sin, zeros_h, zeros_r], axis=1)
    s2 = jnp.concatenate([zeros_h, sin, zeros_r], axis=1)
    rep = LANES // HEAD_DIM
    return tuple(jnp.tile(t, (1, rep)) for t in (c, s1, s2))


def _layout_w_in(w):
    scale = HEAD_DIM ** -0.5 * LOG2E
    a_in, sb_q, sb_k, sb_v, c_q, c_kv, c_g, br_g = jnp.split(
        w, np.cumsum([2 * MIX_WIDTH, MIX_WIDTH, MIX_WIDTH, MIX_WIDTH, MIX_WIDTH, 6 * HEAD_DIM, 3 * N_HEADS])[:].tolist(),
        axis=-1)
    k_cmp, v_cmp, k_slc, v_slc, k_win, v_win = jnp.split(c_kv, 6, axis=-1)
    zeros = lambda width: jnp.zeros((w.shape[0], width), w.dtype)
    cols = [a_in, sb_q * scale, sb_k, sb_v,
            v_slc, v_cmp, v_win, c_g, zeros(HEAD_DIM - 3 * N_HEADS),
            c_q * scale, k_slc, k_cmp, k_win, zeros(HEAD_DIM),
            br_g]
    out = jnp.concatenate(cols, axis=-1)
    assert out.shape[-1] == PROJ_W
    return out.astype(BF16)


def kernel(x, norm1_g, w_in, sgu_norm_g, sgu_w, sgu_b, cmp_pos, cmp_w1, cmp_w2, w_branch, w_out, norm2_g,
           ffn_w_gate, ffn_w_up, ffn_w_down, router_w, moe_w_gate, moe_w_up, moe_w_down, final_norm_g):
    batch, seq, _ = x.shape
    n = batch * seq
    depth = norm1_g.shape[0]
    rope_c, rope_s1, rope_s2 = _rope_tables(seq)
    dff_pad = -(-D_FF // 256) * 256
    h = x.reshape(n, D_MODEL)
    for layer in range(depth):
        proj = _norm_proj(h, norm1_g[layer][None, :], _layout_w_in(w_in[layer]), rope_c, rope_s1, rope_s2, seq)
        sgu_bias = jnp.repeat(sgu_b[layer].T, HEAD_DIM, axis=1)
        y_a = _sgu(proj, sgu_norm_g[layer][None, :], sgu_w[layer], sgu_bias)
        y_b = _stick_breaking(proj, batch, seq)
        groups = seq // CMP_STRIDE
        gk = proj[:, COL_KX + HEAD_DIM:COL_KX + 2 * HEAD_DIM].reshape(batch, groups, CMP_STRIDE * HEAD_DIM)
        gv = proj[:, COL_VX + HEAD_DIM:COL_VX + 2 * HEAD_DIM].reshape(batch, groups, CMP_STRIDE * HEAD_DIM)
        kc, vc = _compress(gk, gv, cmp_pos[layer].reshape(2, 1, CMP_LEN * HEAD_DIM),
                           cmp_w1[layer].astype(BF16), cmp_w2[layer].astype(BF16))
        y_c = _nsa(proj, kc, vc, batch, seq)
        h = _merge(y_a, y_b, y_c, proj, w_branch[layer].astype(BF16), w_out[layer].astype(BF16), h)
        j = layer // 2
        g2 = norm2_g[layer][None, :]
        if layer % 2 == 0:
            pad = dff_pad - D_FF
            wg = jnp.pad(ffn_w_gate[j], ((0, 0), (0, pad))).astype(BF16)
            wu = jnp.pad(ffn_w_up[j], ((0, 0), (0, pad))).astype(BF16)
            wd = jnp.pad(ffn_w_down[j], ((0, pad), (0, 0))).astype(BF16)
            h = _ffn(h, g2, wg, wu, wd)
        else:
            rw = jnp.pad(router_w[j], ((0, 0), (0, LANES - N_EXPERTS)))
            h = _moe_sorted(h, g2, rw, moe_w_gate[j].astype(BF16), moe_w_up[j].astype(BF16), moe_w_down[j].astype(BF16))
    return _final_norm(h, final_norm_g[None, :]).reshape(batch, seq, D_MODEL)
```

```python
import numpy as np
import jax
import jax.numpy as jnp
from jax import lax
from jax.experimental import pallas as pl
from jax.experimental.pallas import tpu as pltpu

F32 = jnp.float32
BF16 = jnp.bfloat16

D_MODEL = 1024
HEAD_DIM = 64
N_HEADS = 4
MIX_WIDTH = N_HEADS * HEAD_DIM
ROPE_DIMS = HEAD_DIM // 4
ROPE_THETA = 500000.0
EPS = 1e-6
SGU_CHUNK = 128
CMP_LEN = 32
CMP_STRIDE = 16
CMP_HIDDEN = 256
SEL_LEN = 64
SEL_TOPN = 16
WINDOW = 512
FORCE_SCORE = 1e4
N_BRANCH = 3
D_FF = 2752
N_EXPERTS = 8
D_FF_EXPERT = 3584
NEG = -1e30
LOG2E = 1.4426950408889634

LANES = 128
VMEM_LIMIT = 56 * 1024 * 1024

PROJ_TILE = 1024
COL_A = 0
COL_SBQ, COL_SBK, COL_SBV = 512, 768, 1024
COL_VX = 1280
COL_CQ = 1536
COL_KX = 1792
COL_BRG = 2048
PROJ_W = COL_BRG + N_BRANCH * D_MODEL
ROPE_TILE = COL_CQ // PROJ_TILE
ROPE_START = COL_CQ % PROJ_TILE
assert COL_BRG == (ROPE_TILE + 1) * PROJ_TILE


def _cparams(sem):
    return pltpu.CompilerParams(dimension_semantics=sem, vmem_limit_bytes=VMEM_LIMIT)


def _sigmoid(x):
    return 1.0 / (1.0 + jnp.exp(-x))


def _rms(x, g):
    return x * lax.rsqrt(jnp.mean(x * x, axis=-1, keepdims=True) + EPS) * g


def _norm_proj_kernel(h_ref, g_ref, w_ref, c_ref, s1_ref, s2_ref, o_ref, xn_ref):
    j = pl.program_id(1)

    @pl.when(j == 0)
    def _():
        xn_ref[...] = _rms(h_ref[...], g_ref[...]).astype(BF16)

    acc = jnp.dot(xn_ref[...], w_ref[...], preferred_element_type=F32)

    @pl.when(j != ROPE_TILE)
    def _():
        o_ref[...] = acc.astype(o_ref.dtype)

    @pl.when(j == ROPE_TILE)
    def _():
        c, s1, s2 = c_ref[...], s1_ref[...], s2_ref[...]
        o_ref[:, :ROPE_START] = acc[:, :ROPE_START].astype(o_ref.dtype)
        for g in range(ROPE_START // LANES, PROJ_TILE // LANES):
            x = acc[:, g * LANES:(g + 1) * LANES]
            y = x * c + pltpu.roll(x, LANES - 8, 1) * s1 + pltpu.roll(x, 8, 1) * s2
            o_ref[:, g * LANES:(g + 1) * LANES] = y.astype(o_ref.dtype)


def _norm_proj(h, g, w, rope_c, rope_s1, rope_s2, seq, tm=2048):
    n = h.shape[0]
    assert seq % tm == 0, "a projection tile must not straddle two sequences (rotary tables are per position)"
    nseq = seq // tm
    return pl.pallas_call(
        _norm_proj_kernel,
        grid=(n // tm, PROJ_W // PROJ_TILE),
        in_specs=[
            pl.BlockSpec((tm, D_MODEL), lambda i, j: (i, 0)),
            pl.BlockSpec((1, D_MODEL), lambda i, j: (0, 0)),
            pl.BlockSpec((D_MODEL, PROJ_TILE), lambda i, j: (0, j)),
            pl.BlockSpec((tm, LANES), lambda i, j: (i % nseq, 0)),
            pl.BlockSpec((tm, LANES), lambda i, j: (i % nseq, 0)),
            pl.BlockSpec((tm, LANES), lambda i, j: (i % nseq, 0)),
        ],
        out_specs=pl.BlockSpec((tm, PROJ_TILE), lambda i, j: (i, j)),
        out_shape=jax.ShapeDtypeStruct((n, PROJ_W), BF16),
        scratch_shapes=[pltpu.VMEM((tm, D_MODEL), BF16)],
        compiler_params=_cparams(("parallel", "arbitrary")),
        name="norm_proj",
    )(h, g, w, rope_c, rope_s1, rope_s2)


def _sgu_kernel(z_ref, g_ref, w_ref, b_ref, o_ref):
    tm = z_ref.shape[0]
    a = jax.nn.gelu(z_ref[...].astype(F32))
    u = a[:, :MIX_WIDTH]
    v = _rms(a[:, MIX_WIDTH:], g_ref[...]).astype(BF16)
    row = lax.broadcasted_iota(jnp.int32, (SGU_CHUNK, SGU_CHUNK), 0)
    col = lax.broadcasted_iota(jnp.int32, (SGU_CHUNK, SGU_CHUNK), 1)
    ws = [jnp.where(row >= col, w_ref[gi], 0.0).astype(BF16) for gi in range(N_HEADS)]
    bias = b_ref[...]
    for c in range(tm // SGU_CHUNK):
        rows = slice(c * SGU_CHUNK, (c + 1) * SGU_CHUNK)
        mix = jnp.concatenate(
            [jnp.dot(ws[gi], v[rows, gi * HEAD_DIM:(gi + 1) * HEAD_DIM], preferred_element_type=F32)
             for gi in range(N_HEADS)], axis=1)
        o_ref[rows, :] = (u[rows, :] * (mix + bias)).astype(o_ref.dtype)


def _sgu(proj, g, w, bias, tm=512):
    n = proj.shape[0]
    return pl.pallas_call(
        _sgu_kernel,
        grid=(n // tm,),
        in_specs=[
            pl.BlockSpec((tm, 2 * MIX_WIDTH), lambda i: (i, COL_A // (2 * MIX_WIDTH))),
            pl.BlockSpec((1, MIX_WIDTH), lambda i: (0, 0)),
            pl.BlockSpec((N_HEADS, SGU_CHUNK, SGU_CHUNK), lambda i: (0, 0, 0)),
            pl.BlockSpec((SGU_CHUNK, MIX_WIDTH), lambda i: (0, 0)),
        ],
        out_specs=pl.BlockSpec((tm, MIX_WIDTH), lambda i: (i, 0)),
        out_shape=jax.ShapeDtypeStruct((n, MIX_WIDTH), BF16),
        compiler_params=_cparams(("parallel",)),
        name="sgu",
    )(proj, g, w, bias)


SB_TQ = 1024
SB_TK = 512
SB_SUB = 256
SB_R = SB_TQ // SB_TK


def _sb_tile(q_ref, k_ref, v_ref, u_ref, acc_ref, carry_ref, k_off):
    for j in reversed(range(SB_TK // SB_SUB)):
        keys = slice(j * SB_SUB, (j + 1) * SB_SUB)
        _sb_sub_tile(q_ref, k_ref[keys, :], v_ref[keys, :], u_ref[...], acc_ref, carry_ref,
                     None if k_off is None else k_off + j * SB_SUB)


def _sb_sub_tile(q_ref, k, v, u, acc_ref, carry_ref, k_off):
    diag = k_off is not None
    rows = slice(k_off if diag else 0, SB_TQ)
    n_rows = rows.stop - rows.start
    q = q_ref[rows, :]
    if diag:
        mask = (lax.broadcasted_iota(jnp.int32, (n_rows, SB_SUB), 1)
                < lax.broadcasted_iota(jnp.int32, (n_rows, SB_SUB), 0))
    pvs = []
    for h in range(N_HEADS):
        hs = slice(h * HEAD_DIM, (h + 1) * HEAD_DIM)
        z = lax.dot_general(q[:, hs], k[:, hs], (((1,), (1,)), ((), ())), preferred_element_type=F32)
        log_beta = jnp.minimum(z, 0.0) - jnp.log(1.0 + jnp.exp2(-jnp.abs(z))) * LOG2E
        log_1m = log_beta - z
        if diag:
            log_1m = jnp.where(mask, log_1m, 0.0)
        carry = carry_ref[h, rows, :]
        terms = log_1m.astype(BF16)
        r = jnp.dot(terms, u, preferred_element_type=F32)
        a = jnp.exp2(log_beta + r + carry)
        if diag:
            a = jnp.where(mask, a, 0.0)
        carry_ref[h, rows, :] = carry + r[:, 0:1] + terms[:, 0:1].astype(F32)
        pvs.append(jnp.dot(a.astype(BF16), v[:, hs], preferred_element_type=F32))
    acc_ref[rows, :] += jnp.concatenate(pvs, axis=1)


def _sb_kernel(qi_ref, ki_ref, q_ref, k_ref, v_ref, u_ref, o_ref, acc_ref, carry_ref):
    step = pl.program_id(1)
    qi = qi_ref[step]
    ki = ki_ref[step]

    @pl.when(ki == SB_R * qi + SB_R - 1)
    def _():
        acc_ref[...] = jnp.zeros_like(acc_ref)
        carry_ref[...] = jnp.zeros_like(carry_ref)

    for d in range(SB_R):
        @pl.when(ki == SB_R * qi + d)
        def _(d=d):
            _sb_tile(q_ref, k_ref, v_ref, u_ref, acc_ref, carry_ref, d * SB_TK)

    @pl.when(ki < SB_R * qi)
    def _():
        _sb_tile(q_ref, k_ref, v_ref, u_ref, acc_ref, carry_ref, None)

    @pl.when(ki == 0)
    def _():
        o_ref[...] = acc_ref[...].astype(o_ref.dtype)


def _stick_breaking(proj, batch, seq):
    n = proj.shape[0]
    nq, nk = seq // SB_TQ, seq // SB_TK
    qi_tab = np.array([qi for qi in range(nq) for _ in range(SB_R * (qi + 1))], np.int32)
    ki_tab = np.array([ki for qi in range(nq) for ki in range(SB_R * (qi + 1) - 1, -1, -1)], np.int32)
    u = jnp.asarray((np.arange(SB_SUB)[:, None] > np.arange(SB_SUB)[None, :]).astype(np.float32), BF16)
    qb, kb, vb = COL_SBQ // MIX_WIDTH, COL_SBK // MIX_WIDTH, COL_SBV // MIX_WIDTH
    grid_spec = pltpu.PrefetchScalarGridSpec(
        num_scalar_prefetch=2,
        grid=(batch, len(qi_tab)),
        in_specs=[
            pl.BlockSpec((SB_TQ, MIX_WIDTH), lambda b, s, qt, kt: (b * nq + qt[s], qb)),
            pl.BlockSpec((SB_TK, MIX_WIDTH), lambda b, s, qt, kt: (b * nk + kt[s], kb)),
            pl.BlockSpec((SB_TK, MIX_WIDTH), lambda b, s, qt, kt: (b * nk + kt[s], vb)),
            pl.BlockSpec((SB_SUB, SB_SUB), lambda b, s, qt, kt: (0, 0)),
        ],
        out_specs=pl.BlockSpec((SB_TQ, MIX_WIDTH), lambda b, s, qt, kt: (b * nq + qt[s], 0)),
        scratch_shapes=[pltpu.VMEM((SB_TQ, MIX_WIDTH), F32), pltpu.VMEM((N_HEADS, SB_TQ, 1), F32)],
    )
    return pl.pallas_call(
        _sb_kernel,
        grid_spec=grid_spec,
        out_shape=jax.ShapeDtypeStruct((n, MIX_WIDTH), BF16),
        compiler_params=_cparams(("parallel", "arbitrary")),
        name="stick_breaking",
    )(jnp.asarray(qi_tab), jnp.asarray(ki_tab), proj, proj, proj, u)


def _compress_kernel(gk_ref, gv_ref, pos_ref, w1_ref, w2_ref, kc_ref, vc_ref):
    half = CMP_STRIDE * HEAD_DIM
    for t, (g_ref, o_ref) in enumerate(((gk_ref, kc_ref), (gv_ref, vc_ref))):
        g = g_ref[0].astype(F32)
        top = (g + pos_ref[t, :, :half]).astype(BF16)
        bot = (g + pos_ref[t, :, half:]).astype(BF16)
        a = jnp.dot(top, w1_ref[t, :half, :], preferred_element_type=F32)
        b = jnp.dot(bot, w1_ref[t, half:, :], preferred_element_type=F32)
        hid = jax.nn.gelu(a + pltpu.roll(b, b.shape[0] - 1, 0))
        out = jnp.dot(hid.astype(BF16), w2_ref[t], preferred_element_type=F32)
        o_ref[0] = jnp.concatenate([out, jnp.zeros_like(out)], axis=1).astype(o_ref.dtype)


def _compress(gk, gv, pos, w1, w2):
    b, m, width = gk.shape
    out_spec = pl.BlockSpec((1, m, LANES), lambda i: (i, 0, 0))
    out_shape = jax.ShapeDtypeStruct((b, m, LANES), BF16)
    return pl.pallas_call(
        _compress_kernel,
        grid=(b,),
        in_specs=[
            pl.BlockSpec((1, m, width), lambda i: (i, 0, 0)),
            pl.BlockSpec((1, m, width), lambda i: (i, 0, 0)),
            pl.BlockSpec((2, 1, CMP_LEN * HEAD_DIM), lambda i: (0, 0, 0)),
            pl.BlockSpec((2, CMP_LEN * HEAD_DIM, CMP_HIDDEN), lambda i: (0, 0, 0)),
            pl.BlockSpec((2, CMP_HIDDEN, HEAD_DIM), lambda i: (0, 0, 0)),
        ],
        out_specs=[out_spec, out_spec],
        out_shape=[out_shape, out_shape],
        compiler_params=_cparams(("parallel",)),
        name="nsa_compress",
    )(gk, gv, pos, w1, w2)


NSA_TQ = 256
NSA_CK = 512


NSA_ROWS = N_HEADS * NSA_TQ
SEL_OFF = 1e30


def _per_head(x, fn):
    return jnp.concatenate([fn(x[h * NSA_TQ:(h + 1) * NSA_TQ]) for h in range(N_HEADS)], axis=0)


def _nsa_kernel(q_ref, kx_ref, vx_ref, kw_ref, vw_ref, kb_ref, kc_ref, vc_ref, ovt_ref, o_ref, m_ref, acc_ref):
    seq = kx_ref.shape[0]
    n_sel = seq // SEL_LEN
    n_cmp = kc_ref.shape[1]
    q0 = pl.program_id(1) * NSA_TQ
    NT = (((1,), (1,)), ((), ()))
    lo_half = lax.broadcasted_iota(jnp.int32, (NSA_TQ, LANES), 1) < HEAD_DIM
    qpos_c = q0 + lax.broadcasted_iota(jnp.int32, (NSA_TQ, 1), 0)

    qs = []
    for pair in range(N_HEADS // 2):
        x = q_ref[:, pair * LANES:(pair + 1) * LANES].astype(F32)
        qs += [jnp.where(lo_half, x, 0.0), jnp.where(lo_half, pltpu.roll(x, HEAD_DIM, 1), 0.0)]
    q4 = jnp.concatenate(qs, axis=0).astype(BF16)

    ones_half = lax.broadcasted_iota(jnp.int32, (1, LANES), 1) >= HEAD_DIM

    def masked_exp(s, mask):
        bias = jnp.where(mask, 0.0, NEG)
        s = _per_head(s, lambda t: t + bias)
        return jnp.exp2(s - jnp.max(s, axis=-1, keepdims=True))

    def attend(p, vals):
        return jnp.dot(p.astype(BF16), jnp.where(ones_half, 1.0, vals).astype(BF16), preferred_element_type=F32)

    def normalized(r):
        return r / jnp.where(ones_half, 1.0, pltpu.roll(r, HEAD_DIM, 1))

    cmp_end = lax.broadcasted_iota(jnp.int32, (NSA_TQ, n_cmp), 1) * CMP_STRIDE + (CMP_LEN - 1)
    p_cmp = masked_exp(lax.dot_general(q4, kc_ref[0], NT, preferred_element_type=F32), cmp_end <= qpos_c)
    has_cmp = jnp.concatenate([qpos_c >= CMP_LEN - 1] * N_HEADS, axis=0)
    p_cmp = p_cmp * jnp.where(has_cmp, 1.0 / jnp.sum(p_cmp, axis=-1, keepdims=True), 0.0)
    r_cmp = jnp.dot(p_cmp.astype(BF16), vc_ref[0], preferred_element_type=F32)
    p_sum = sum(p_cmp[h * NSA_TQ:(h + 1) * NSA_TQ] for h in range(N_HEADS))

    hi = p_sum.astype(BF16)
    lo = (p_sum - hi.astype(F32)).astype(BF16)
    imp = lax.dot_general(ovt_ref[...], jnp.concatenate([hi, lo], axis=1), NT, preferred_element_type=F32)
    blk = lax.broadcasted_iota(jnp.int32, (n_sel, NSA_TQ), 0)
    qpos_r = q0 + lax.broadcasted_iota(jnp.int32, (n_sel, NSA_TQ), 1)
    valid = blk * SEL_LEN <= qpos_r
    forced = (blk == 0) | (blk == jnp.right_shift(qpos_r, 6))
    score = jnp.where(valid, imp + jnp.where(forced, FORCE_SCORE, 0.0), -jnp.inf)
    groups = [score[8 * g:8 * g + 8] for g in range(n_sel // 8)]
    ranks = [jnp.zeros((8, NSA_TQ), F32) for _ in groups]
    row8 = lax.broadcasted_iota(jnp.int32, (8, NSA_TQ), 0)
    for i in range(n_sel):
        ci = jnp.broadcast_to(score[i:i + 1], (8, NSA_TQ))
        for g, sg in enumerate(groups):
            if 8 * g > i:
                beats = ci >= sg
            elif 8 * g + 7 < i:
                beats = ci > sg
            else:
                beats = (ci > sg) | ((ci == sg) & (row8 + 8 * g > i))
            ranks[g] = ranks[g] + jnp.where(beats, 1.0, 0.0)
    sel = jnp.where(valid & (jnp.concatenate(ranks, axis=0) < SEL_TOPN), 0.0, -SEL_OFF)
    sel = jnp.concatenate([sel, jnp.full((LANES - n_sel, NSA_TQ), -SEL_OFF, F32)], axis=0).T
    aug = lambda t: jnp.concatenate([q4, jnp.concatenate([t.astype(BF16)] * N_HEADS, axis=0)], axis=1)
    q_aug = aug(sel)
    blk_lane = lax.broadcasted_iota(jnp.int32, (NSA_TQ, LANES), 1)
    q_aug_past = aug(jnp.where(blk_lane * SEL_LEN >= q0, -SEL_OFF, sel))

    w_start = pl.multiple_of(jnp.maximum(q0 - WINDOW, 0), NSA_TQ)
    kwpos = w_start + lax.broadcasted_iota(jnp.int32, (NSA_TQ, WINDOW + NSA_TQ), 1)
    s_win = lax.dot_general(q4, kw_ref[pl.ds(w_start, WINDOW + NSA_TQ), :], NT, preferred_element_type=F32)
    p_win = masked_exp(s_win, (kwpos <= qpos_c) & (kwpos > qpos_c - WINDOW))
    r_win = normalized(attend(p_win, vw_ref[pl.ds(w_start, WINDOW + NSA_TQ), :]))

    def slab_stats(q, start, size, causal):
        keys = jnp.concatenate([kx_ref[pl.ds(start, size), :], kb_ref[pl.ds(start, size), :]], axis=1)
        s = lax.dot_general(q, keys, NT, preferred_element_type=F32)
        if causal:
            bias = jnp.where(start + lax.broadcasted_iota(jnp.int32, (NSA_TQ, size), 1) <= qpos_c, 0.0, NEG)
            s = _per_head(s, lambda t: t + bias)
        m_c = jnp.max(s, axis=-1, keepdims=True)
        return m_c, attend(jnp.exp2(s - m_c), vx_ref[pl.ds(start, size), :])

    m_d, a_d = slab_stats(q_aug, pl.multiple_of(q0, NSA_TQ), NSA_TQ, True)
    m_ref[...] = m_d
    acc_ref[...] = a_d

    def pair_body(i, carry):
        m0, a0 = slab_stats(q_aug_past, pl.multiple_of(2 * i * NSA_CK, NSA_CK), NSA_CK, False)
        m1, a1 = slab_stats(q_aug_past, pl.multiple_of((2 * i + 1) * NSA_CK, NSA_CK), NSA_CK, False)
        m_old = m_ref[...]
        m_new = jnp.maximum(m_old, jnp.maximum(m0, m1))
        acc_ref[...] = (jnp.exp2(m_old - m_new) * acc_ref[...] + jnp.exp2(m0 - m_new) * a0
                        + jnp.exp2(m1 - m_new) * a1)
        m_ref[...] = m_new
        return carry

    lax.fori_loop(0, (q0 + 2 * NSA_CK - 1) // (2 * NSA_CK), pair_body, 0)

    r_slc = normalized(acc_ref[...])
    gates = _sigmoid(vw_ref[pl.ds(pl.multiple_of(q0, NSA_TQ), NSA_TQ), :].astype(F32))
    g = [jnp.concatenate([gates[:, HEAD_DIM + 3 * h + t:HEAD_DIM + 3 * h + t + 1] for h in range(N_HEADS)], axis=0)
         for t in range(N_BRANCH)]
    y = g[0] * r_cmp + g[1] * r_slc + g[2] * r_win
    for pair in range(N_HEADS // 2):
        even = y[2 * pair * NSA_TQ:(2 * pair + 1) * NSA_TQ]
        odd = y[(2 * pair + 1) * NSA_TQ:(2 * pair + 2) * NSA_TQ]
        o_ref[:, pair * LANES:(pair + 1) * LANES] = jnp.where(lo_half, even, pltpu.roll(odd, HEAD_DIM, 1)).astype(o_ref.dtype)


def _nsa(proj, kc, vc, batch, seq):
    n = proj.shape[0]
    nqb = seq // NSA_TQ
    n_sel = seq // SEL_LEN
    n_cmp = kc.shape[1]
    assert n_sel % 8 == 0 and n_sel <= LANES and seq % (2 * NSA_CK) == 0
    cs = (np.arange(n_cmp) * CMP_STRIDE)[None, :]
    ss = (np.arange(n_sel) * SEL_LEN)[:, None]
    ovt = np.clip(np.minimum(cs + CMP_LEN, ss + SEL_LEN) - np.maximum(cs, ss), 0, None).astype(np.float32) / CMP_LEN
    ovt = jnp.asarray(np.concatenate([ovt, ovt], axis=1), BF16)
    kb = jnp.asarray((np.arange(seq)[:, None] // SEL_LEN == np.arange(LANES)[None, :]).astype(np.float32), BF16)
    seq_spec = lambda col: pl.BlockSpec((seq, LANES), lambda b, i: (b, col))
    cmp_spec = pl.BlockSpec((1, n_cmp, LANES), lambda b, i: (b, 0, 0))
    return pl.pallas_call(
        _nsa_kernel,
        grid=(batch, nqb),
        in_specs=[
            pl.BlockSpec((NSA_TQ, MIX_WIDTH), lambda b, i: (b * nqb + i, COL_CQ // MIX_WIDTH)),
            seq_spec(COL_KX // LANES),
            seq_spec(COL_VX // LANES),
            seq_spec(COL_KX // LANES + 1),
            seq_spec(COL_VX // LANES + 1),
            pl.BlockSpec((seq, LANES), lambda b, i: (0, 0)),
            cmp_spec, cmp_spec,
            pl.BlockSpec((n_sel, 2 * n_cmp), lambda b, i: (0, 0)),
        ],
        out_specs=pl.BlockSpec((NSA_TQ, MIX_WIDTH), lambda b, i: (b * nqb + i, 0)),
        out_shape=jax.ShapeDtypeStruct((n, MIX_WIDTH), BF16),
        scratch_shapes=[pltpu.VMEM((NSA_ROWS, 1), F32), pltpu.VMEM((NSA_ROWS, LANES), F32)],
        compiler_params=_cparams(("parallel", "arbitrary")),
        name="nsa",
    )(proj, proj, proj, proj, proj, kb, kc, vc, ovt)


def _merge_kernel(ya_ref, yb_ref, yc_ref, g0_ref, g1_ref, g2_ref, wb_ref, wo_ref, h_ref, o_ref):
    merged = None
    for y_ref, g_ref, t in ((ya_ref, g0_ref, 0), (yb_ref, g1_ref, 1), (yc_ref, g2_ref, 2)):
        term = _sigmoid(g_ref[...].astype(F32)) * jnp.dot(y_ref[...], wb_ref[t], preferred_element_type=F32)
        merged = term if merged is None else merged + term
    o_ref[...] = h_ref[...] + jnp.dot(merged.astype(BF16), wo_ref[...], preferred_element_type=F32)


def _merge(ya, yb, yc, proj, wb, wo, h, tm=1024):
    n = h.shape[0]
    y_spec = pl.BlockSpec((tm, MIX_WIDTH), lambda i: (i, 0))
    g_spec = lambda t: pl.BlockSpec((tm, D_MODEL), lambda i: (i, COL_BRG // D_MODEL + t))
    return pl.pallas_call(
        _merge_kernel,
        grid=(n // tm,),
        in_specs=[y_spec, y_spec, y_spec, g_spec(0), g_spec(1), g_spec(2),
                  pl.BlockSpec((N_BRANCH, MIX_WIDTH, D_MODEL), lambda i: (0, 0, 0)),
                  pl.BlockSpec((D_MODEL, D_MODEL), lambda i: (0, 0)),
                  pl.BlockSpec((tm, D_MODEL), lambda i: (i, 0))],
        out_specs=pl.BlockSpec((tm, D_MODEL), lambda i: (i, 0)),
        out_shape=jax.ShapeDtypeStruct((n, D_MODEL), F32),
        compiler_params=_cparams(("parallel",)),
        name="merge",
    )(ya, yb, yc, proj, proj, proj, wb, wo, h)


def _ffn_kernel(h_ref, g_ref, wg_ref, wu_ref, wd_ref, o_ref, xn_ref, acc_ref):
    f = pl.program_id(1)

    @pl.when(f == 0)
    def _():
        xn_ref[...] = _rms(h_ref[...], g_ref[...]).astype(BF16)
        acc_ref[...] = h_ref[...]

    xn = xn_ref[...]
    a = jnp.dot(xn, wg_ref[...], preferred_element_type=F32)
    u = jnp.dot(xn, wu_ref[...], preferred_element_type=F32)
    hid = (a * _sigmoid(a) * u).astype(BF16)
    acc_ref[...] += jnp.dot(hid, wd_ref[...], preferred_element_type=F32)

    @pl.when(f == pl.num_programs(1) - 1)
    def _():
        o_ref[...] = acc_ref[...]


def _ffn(h, g, wg, wu, wd, tm=512, tf=1408):
    n = h.shape[0]
    dff = wg.shape[1]
    return pl.pallas_call(
        _ffn_kernel,
        grid=(n // tm, dff // tf),
        in_specs=[
            pl.BlockSpec((tm, D_MODEL), lambda i, f: (i, 0)),
            pl.BlockSpec((1, D_MODEL), lambda i, f: (0, 0)),
            pl.BlockSpec((D_MODEL, tf), lambda i, f: (0, f)),
            pl.BlockSpec((D_MODEL, tf), lambda i, f: (0, f)),
            pl.BlockSpec((tf, D_MODEL), lambda i, f: (f, 0)),
        ],
        out_specs=pl.BlockSpec((tm, D_MODEL), lambda i, f: (i, 0)),
        out_shape=jax.ShapeDtypeStruct((n, D_MODEL), F32),
        scratch_shapes=[pltpu.VMEM((tm, D_MODEL), BF16), pltpu.VMEM((tm, D_MODEL), F32)],
        compiler_params=_cparams(("parallel", "arbitrary")),
        name="ffn",
    )(h, g, wg, wu, wd)


RT_TM = 1024
RT_TF = 896
RT_TD = 512


def _gate_kernel(h_ref, g_ref, rw_ref, gate_ref):
    tm = h_ref.shape[0]
    lane = lax.broadcasted_iota(jnp.int32, (tm, LANES), 1)
    xn = _rms(h_ref[...], g_ref[...])
    logits = jnp.dot(xn, rw_ref[...], preferred_element_type=F32, precision=lax.Precision.HIGHEST)
    logits = jnp.where(lane < N_EXPERTS, logits, -jnp.inf)
    m1 = jnp.max(logits, axis=-1, keepdims=True)
    i1 = jnp.min(jnp.where(logits == m1, lane, LANES), axis=-1, keepdims=True)
    rest = jnp.where(lane == i1, -jnp.inf, logits)
    m2 = jnp.max(rest, axis=-1, keepdims=True)
    i2 = jnp.min(jnp.where(rest == m2, lane, LANES), axis=-1, keepdims=True)
    e2 = jnp.exp(m2 - m1)
    gate_ref[...] = jnp.where(lane == i1, 1.0 / (1.0 + e2), 0.0) + jnp.where(lane == i2, e2 / (1.0 + e2), 0.0)


def _gate(h, g, rw, tm=2048):
    n = h.shape[0]
    return pl.pallas_call(
        _gate_kernel,
        grid=(n // tm,),
        in_specs=[
            pl.BlockSpec((tm, D_MODEL), lambda i: (i, 0)),
            pl.BlockSpec((1, D_MODEL), lambda i: (0, 0)),
            pl.BlockSpec((D_MODEL, LANES), lambda i: (0, 0)),
        ],
        out_specs=pl.BlockSpec((tm, LANES), lambda i: (i, 0)),
        out_shape=jax.ShapeDtypeStruct((n, LANES), F32),
        compiler_params=_cparams(("parallel",)),
        name="moe_gate",
    )(h, g, rw)


def _row_copy(src_ref, src_row, dst_ref, dst_row, sem):
    return pltpu.make_async_copy(src_ref.at[pl.ds(src_row, 1), :], dst_ref.at[pl.ds(dst_row, 1), :], sem)


def _dispatch_kernel(pad_ref, pa_ref, pb_ref, h_ref, xs_ref, hbuf_ref, zero_ref, fetch_sems, row_sems, sems):
    i, n_steps = pl.program_id(0), pl.num_programs(0)
    slot, parity = i % 3, i % 2

    def fetch(block, s):
        return pltpu.make_async_copy(h_ref.at[pl.ds(block * RT_TD, RT_TD), :], hbuf_ref.at[s], fetch_sems.at[s])

    @pl.when(i == 0)
    def _():
        fetch(0, 0).start()

    @pl.when(i + 1 < n_steps)
    def _():
        fetch(i + 1, (i + 1) % 3).start()

    fetch(i, slot).wait()
    src = hbuf_ref.at[slot]

    def send(t, carry):
        _row_copy(src, t, xs_ref, pa_ref[0, 0, t], row_sems.at[0, parity]).start()
        _row_copy(src, t, xs_ref, pb_ref[0, 0, t], row_sems.at[1, parity]).start(priority=1)
        return carry

    lax.fori_loop(0, RT_TD, send, 0, unroll=4)

    @pl.when(i == 0)
    def _():
        zero_ref[...] = jnp.zeros_like(zero_ref)
        for e in range(N_EXPERTS + 1):
            lo, hi = pad_ref[2 * e], pad_ref[2 * e + 1]
            lax.fori_loop(lo, hi, lambda r, c: (_row_copy(zero_ref, 0, xs_ref, r, sems.at[0]).start(), c)[1], 0)
        for e in range(N_EXPERTS + 1):
            lo, hi = pad_ref[2 * e], pad_ref[2 * e + 1]
            lax.fori_loop(lo, hi, lambda r, c: (_row_copy(zero_ref, 0, xs_ref, r, sems.at[0]).wait(), c)[1], 0)

    def drain(p):
        for s in range(2):
            pltpu.make_async_copy(hbuf_ref.at[0], xs_ref.at[pl.ds(0, RT_TD), :], row_sems.at[s, p]).wait()

    @pl.when(i > 0)
    def _():
        drain(1 - parity)

    @pl.when(i == n_steps - 1)
    def _():
        drain(parity)


def _dispatch(pad_bounds, pa, pb, h, n_rows):
    n = h.shape[0]
    idx_spec = pl.BlockSpec((1, 1, RT_TD), lambda i, *_: (i, 0, 0), memory_space=pltpu.SMEM)
    grid_spec = pltpu.PrefetchScalarGridSpec(
        num_scalar_prefetch=1,
        grid=(n // RT_TD,),
        in_specs=[idx_spec, idx_spec, pl.BlockSpec(memory_space=pl.ANY)],
        out_specs=pl.BlockSpec(memory_space=pl.ANY),
        scratch_shapes=[pltpu.VMEM((3, RT_TD, D_MODEL), F32), pltpu.VMEM((8, D_MODEL), F32),
                        pltpu.SemaphoreType.DMA((3,)), pltpu.SemaphoreType.DMA((2, 2)),
                        pltpu.SemaphoreType.DMA((1,))],
    )
    return pl.pallas_call(
        _dispatch_kernel,
        grid_spec=grid_spec,
        out_shape=jax.ShapeDtypeStruct((n_rows, D_MODEL), F32),
        compiler_params=_cparams(("arbitrary",)),
        name="moe_dispatch",
    )(pad_bounds, pa.reshape(-1, 1, RT_TD), pb.reshape(-1, 1, RT_TD), h)


def _sorted_experts_kernel(te_ref, tx_ref, live_ref, xs_ref, g_ref, wg_ref, wu_ref, wd_ref, y_ref, xb_ref):
    i, f = pl.program_id(0), pl.program_id(1)

    @pl.when(f == 0)
    def _():
        xb_ref[...] = _rms(xs_ref[...], g_ref[...]).astype(BF16)
        y_ref[...] = jnp.zeros_like(y_ref)

    @pl.when(live_ref[i] > 0)
    def _():
        x = xb_ref[...]
        a = jnp.dot(x, wg_ref[0], preferred_element_type=F32)
        u = jnp.dot(x, wu_ref[0], preferred_element_type=F32)
        y_ref[...] += jnp.dot((a * _sigmoid(a) * u).astype(BF16), wd_ref[0], preferred_element_type=F32)


def _sorted_experts(tile_expert, tile_src, tile_live, xs, g, wg, wu, wd):
    n_rows = xs.shape[0]
    ne, _, dff = wg.shape
    grid_spec = pltpu.PrefetchScalarGridSpec(
        num_scalar_prefetch=3,
        grid=(n_rows // RT_TM, dff // RT_TF),
        in_specs=[
            pl.BlockSpec((RT_TM, D_MODEL), lambda i, f, te, tx, lv: (tx[i], 0)),
            pl.BlockSpec((1, D_MODEL), lambda i, f, *_: (0, 0)),
            pl.BlockSpec((1, D_MODEL, RT_TF), lambda i, f, te, tx, lv: (te[i], 0, f)),
            pl.BlockSpec((1, D_MODEL, RT_TF), lambda i, f, te, tx, lv: (te[i], 0, f)),
            pl.BlockSpec((1, RT_TF, D_MODEL), lambda i, f, te, tx, lv: (te[i], f, 0)),
        ],
        out_specs=pl.BlockSpec((RT_TM, D_MODEL), lambda i, f, *_: (i, 0)),
        scratch_shapes=[pltpu.VMEM((RT_TM, D_MODEL), BF16)],
    )
    return pl.pallas_call(
        _sorted_experts_kernel,
        grid_spec=grid_spec,
        out_shape=jax.ShapeDtypeStruct((n_rows, D_MODEL), F32),
        compiler_params=_cparams(("parallel", "arbitrary")),
        name="moe_experts",
    )(tile_expert, tile_src, tile_live, xs, g, wg, wu, wd)


def _combine_kernel(pa_ref, pb_ref, pa_next_ref, pb_next_ref, ga_ref, gb_ref, h_ref, y_ref, o_ref, ya_ref, yb_ref, sems):
    i, n_steps = pl.program_id(0), pl.num_programs(0)
    slot = i % 2

    def fetch(pa, pb, s):
        def body(t, carry):
            _row_copy(y_ref, pa[0, 0, t], ya_ref.at[s], t, sems.at[0, s]).start()
            _row_copy(y_ref, pb[0, 0, t], yb_ref.at[s], t, sems.at[1, s]).start(priority=1)
            return carry

        lax.fori_loop(0, RT_TD, body, 0, unroll=4)

    @pl.when(i == 0)
    def _():
        fetch(pa_ref, pb_ref, 0)

    @pl.when(i + 1 < n_steps)
    def _():
        fetch(pa_next_ref, pb_next_ref, 1 - slot)

    pltpu.make_async_copy(y_ref.at[pl.ds(0, RT_TD), :], ya_ref.at[slot], sems.at[0, slot]).wait()
    pltpu.make_async_copy(y_ref.at[pl.ds(0, RT_TD), :], yb_ref.at[slot], sems.at[1, slot]).wait()
    o_ref[...] = h_ref[...] + ga_ref[...] * ya_ref[slot] + gb_ref[...] * yb_ref[slot]


def _combine(pa, pb, ga, gb, h, y):
    n = h.shape[0]
    n_steps = n // RT_TD
    idx_spec = pl.BlockSpec((1, 1, RT_TD), lambda i: (i, 0, 0), memory_space=pltpu.SMEM)
    next_spec = pl.BlockSpec((1, 1, RT_TD), lambda i: (jnp.minimum(i + 1, n_steps - 1), 0, 0), memory_space=pltpu.SMEM)
    col_spec = pl.BlockSpec((RT_TD, 1), lambda i: (i, 0))
    row_spec = pl.BlockSpec((RT_TD, D_MODEL), lambda i: (i, 0))
    pa, pb = pa.reshape(-1, 1, RT_TD), pb.reshape(-1, 1, RT_TD)
    return pl.pallas_call(
        _combine_kernel,
        grid=(n_steps,),
        in_specs=[idx_spec, idx_spec, next_spec, next_spec, col_spec, col_spec, row_spec,
                  pl.BlockSpec(memory_space=pl.ANY)],
        out_specs=row_spec,
        out_shape=jax.ShapeDtypeStruct((n, D_MODEL), F32),
        scratch_shapes=[pltpu.VMEM((2, RT_TD, D_MODEL), F32), pltpu.VMEM((2, RT_TD, D_MODEL), F32),
                        pltpu.SemaphoreType.DMA((2, 2))],
        compiler_params=_cparams(("arbitrary",)),
        name="moe_combine",
    )(pa, pb, pa, pb, ga, gb, h, y)


def _moe_sorted(h, g, rw, wg, wu, wd):
    n = h.shape[0]
    n_tiles = 2 * n // RT_TM + N_EXPERTS
    gate = _gate(h, g, rw)[:, :N_EXPERTS]
    routed = gate > 0
    r = routed.astype(jnp.int32)
    incl = jnp.cumsum(r, axis=0)
    count = incl[-1]
    padded = (count + RT_TM - 1) // RT_TM * RT_TM
    ends = jnp.cumsum(padded)
    pos = (ends - padded)[None, :] + incl - 1
    order = jnp.cumsum(r, axis=1)
    first, second = routed & (order == 1), routed & (order == 2)
    pick = lambda m, v: jnp.sum(jnp.where(m, v, 0), axis=1)
    pa = pick(first, pos)
    pb = jnp.where(jnp.any(second, axis=1), pick(second, pos), pa)
    ga, gb = pick(first, gate)[:, None], pick(second, gate)[:, None]
    pad_lo = jnp.concatenate([ends - padded + count, ends[-1:]])
    pad_hi = jnp.concatenate([ends, jnp.full((1,), n_tiles * RT_TM, ends.dtype)])
    pad_bounds = jnp.stack([pad_lo, pad_hi], axis=1).reshape(-1).astype(jnp.int32)
    tile_start = jnp.arange(n_tiles, dtype=jnp.int32) * RT_TM
    tile_live = (tile_start < ends[-1]).astype(jnp.int32)
    tile_expert = jnp.minimum(jnp.sum(tile_start[:, None] >= ends[None, :], axis=1), N_EXPERTS - 1).astype(jnp.int32)
    tile_src = jnp.minimum(jnp.arange(n_tiles, dtype=jnp.int32), ends[-1] // RT_TM - 1)
    xs = _dispatch(pad_bounds, pa.astype(jnp.int32), pb.astype(jnp.int32), h, n_tiles * RT_TM)
    y = _sorted_experts(tile_expert, tile_src, tile_live, xs, g, wg, wu, wd)
    return _combine(pa.astype(jnp.int32), pb.astype(jnp.int32), ga, gb, h, y)


def _final_norm_kernel(h_ref, g_ref, o_ref):
    o_ref[...] = _rms(h_ref[...], g_ref[...])


def _final_norm(h, g, tm=2048):
    n = h.shape[0]
    return pl.pallas_call(
        _final_norm_kernel,
        grid=(n // tm,),
        in_specs=[pl.BlockSpec((tm, D_MODEL), lambda i: (i, 0)), pl.BlockSpec((1, D_MODEL), lambda i: (0, 0))],
        out_specs=pl.BlockSpec((tm, D_MODEL), lambda i: (i, 0)),
        out_shape=jax.ShapeDtypeStruct((n, D_MODEL), F32),
        compiler_params=_cparams(("parallel",)),
        name="final_norm",
    )(h, g)


def _rope_tables(seq):
    half = ROPE_DIMS // 2
    inv_freq = ROPE_THETA ** (-jnp.arange(0, ROPE_DIMS, 2, dtype=F32) / ROPE_DIMS)
    ang = jnp.arange(seq, dtype=F32)[:, None] * inv_freq[None, :]
    cos, sin = jnp.cos(ang), jnp.sin(ang)
    ones = jnp.ones((seq, HEAD_DIM - ROPE_DIMS), F32)
    zeros_h = jnp.zeros((seq, half), F32)
    zeros_r = jnp.zeros((seq, HEAD_DIM - ROPE_DIMS), F32)
    c = jnp.concatenate([cos, cos, ones], axis=1)
    s1 = jnp.concatenate([-sin, zeros_h, zeros_r], axis=1)
    s2 = jnp.concatenate([zeros_h, sin, zeros_r], axis=1)
    rep = LANES // HEAD_DIM
    return tuple(jnp.tile(t, (1, rep)) for t in (c, s1, s2))


def _layout_w_in(w):
    scale = HEAD_DIM ** -0.5 * LOG2E
    a_in, sb_q, sb_k, sb_v, c_q, c_kv, c_g, br_g = jnp.split(
        w, np.cumsum([2 * MIX_WIDTH, MIX_WIDTH, MIX_WIDTH, MIX_WIDTH, MIX_WIDTH, 6 * HEAD_DIM, 3 * N_HEADS])[:].tolist(),
        axis=-1)
    k_cmp, v_cmp, k_slc, v_slc, k_win, v_win = jnp.split(c_kv, 6, axis=-1)
    zeros = lambda width: jnp.zeros((w.shape[0], width), w.dtype)
    cols = [a_in, sb_q * scale, sb_k, sb_v,
            v_slc, v_cmp, v_win, c_g, zeros(HEAD_DIM - 3 * N_HEADS),
            c_q * scale, k_slc, k_cmp, k_win, zeros(HEAD_DIM),
            br_g]
    out = jnp.concatenate(cols, axis=-1)
    assert out.shape[-1] == PROJ_W
    return out.astype(BF16)


def kernel(x, norm1_g, w_in, sgu_norm_g, sgu_w, sgu_b, cmp_pos, cmp_w1, cmp_w2, w_branch, w_out, norm2_g,
           ffn_w_gate, ffn_w_up, ffn_w_down, router_w, moe_w_gate, moe_w_up, moe_w_down, final_norm_g):
    batch, seq, _ = x.shape
    n = batch * seq
    depth = norm1_g.shape[0]
    rope_c, rope_s1, rope_s2 = _rope_tables(seq)
    dff_pad = -(-D_FF // 256) * 256
    h = x.reshape(n, D_MODEL)
    for layer in range(depth):
        proj = _norm_proj(h, norm1_g[layer][None, :], _layout_w_in(w_in[layer]), rope_c, rope_s1, rope_s2, seq)
        sgu_bias = jnp.repeat(sgu_b[layer].T, HEAD_DIM, axis=1)
        y_a = _sgu(proj, sgu_norm_g[layer][None, :], sgu_w[layer], sgu_bias)
        y_b = _stick_breaking(proj, batch, seq)
        groups = seq // CMP_STRIDE
        gk = proj[:, COL_KX + HEAD_DIM:COL_KX + 2 * HEAD_DIM].reshape(batch, groups, CMP_STRIDE * HEAD_DIM)
        gv = proj[:, COL_VX + HEAD_DIM:COL_VX + 2 * HEAD_DIM].reshape(batch, groups, CMP_STRIDE * HEAD_DIM)
        kc, vc = _compress(gk, gv, cmp_pos[layer].reshape(2, 1, CMP_LEN * HEAD_DIM),
                           cmp_w1[layer].astype(BF16), cmp_w2[layer].astype(BF16))
        y_c = _nsa(proj, kc, vc, batch, seq)
        h = _merge(y_a, y_b, y_c, proj, w_branch[layer].astype(BF16), w_out[layer].astype(BF16), h)
        j = layer // 2
        g2 = norm2_g[layer][None, :]
        if layer % 2 == 0:
            pad = dff_pad - D_FF
            wg = jnp.pad(ffn_w_gate[j], ((0, 0), (0, pad))).astype(BF16)
            wu = jnp.pad(ffn_w_up[j], ((0, 0), (0, pad))).astype(BF16)
            wd = jnp.pad(ffn_w_down[j], ((0, pad), (0, 0))).astype(BF16)
            h = _ffn(h, g2, wg, wu, wd)
        else:
            rw = jnp.pad(router_w[j], ((0, 0), (0, LANES - N_EXPERTS)))
            h = _moe_sorted(h, g2, rw, moe_w_gate[j].astype(BF16), moe_w_up[j].astype(BF16), moe_w_down[j].astype(BF16))
    return _final_norm(h, final_norm_g[None, :]).reshape(batch, seq, D_MODEL)
```

```python
import numpy as np
import jax
import jax.numpy as jnp
from jax import lax
from jax.experimental import pallas as pl
from jax.experimental.pallas import tpu as pltpu

F32 = jnp.float32
BF16 = jnp.bfloat16

D_MODEL = 1024
HEAD_DIM = 64
N_HEADS = 4
MIX_WIDTH = N_HEADS * HEAD_DIM
ROPE_DIMS = HEAD_DIM // 4
ROPE_THETA = 500000.0
EPS = 1e-6
SGU_CHUNK = 128
CMP_LEN = 32
CMP_STRIDE = 16
CMP_HIDDEN = 256
SEL_LEN = 64
SEL_TOPN = 16
WINDOW = 512
FORCE_SCORE = 1e4
N_BRANCH = 3
D_FF = 2752
N_EXPERTS = 8
D_FF_EXPERT = 3584
NEG = -1e30
LOG2E = 1.4426950408889634

LANES = 128
VMEM_LIMIT = 56 * 1024 * 1024

PROJ_TILE = 1024
COL_A = 0
COL_SBQ, COL_SBK, COL_SBV = 512, 768, 1024
COL_VX = 1280
COL_CQ = 1536
COL_KX = 1792
COL_BRG = 2048
PROJ_W = COL_BRG + N_BRANCH * D_MODEL
ROPE_TILE = COL_CQ // PROJ_TILE
ROPE_START = COL_CQ % PROJ_TILE
assert COL_BRG == (ROPE_TILE + 1) * PROJ_TILE


def _cparams(sem):
    return pltpu.CompilerParams(dimension_semantics=sem, vmem_limit_bytes=VMEM_LIMIT)


def _sigmoid(x):
    return 1.0 / (1.0 + jnp.exp(-x))


def _rms(x, g):
    return x * lax.rsqrt(jnp.mean(x * x, axis=-1, keepdims=True) + EPS) * g


def _norm_proj_kernel(h_ref, g_ref, w_ref, c_ref, s1_ref, s2_ref, o_ref, xn_ref):
    j = pl.program_id(1)

    @pl.when(j == 0)
    def _():
        xn_ref[...] = _rms(h_ref[...], g_ref[...]).astype(BF16)

    acc = jnp.dot(xn_ref[...], w_ref[...], preferred_element_type=F32)

    @pl.when(j != ROPE_TILE)
    def _():
        o_ref[...] = acc.astype(o_ref.dtype)

    @pl.when(j == ROPE_TILE)
    def _():
        c, s1, s2 = c_ref[...], s1_ref[...], s2_ref[...]
        o_ref[:, :ROPE_START] = acc[:, :ROPE_START].astype(o_ref.dtype)
        for g in range(ROPE_START // LANES, PROJ_TILE // LANES):
            x = acc[:, g * LANES:(g + 1) * LANES]
            y = x * c + pltpu.roll(x, LANES - 8, 1) * s1 + pltpu.roll(x, 8, 1) * s2
            o_ref[:, g * LANES:(g + 1) * LANES] = y.astype(o_ref.dtype)


def _norm_proj(h, g, w, rope_c, rope_s1, rope_s2, seq, tm=2048):
    n = h.shape[0]
    assert seq % tm == 0, "a projection tile must not straddle two sequences (rotary tables are per position)"
    nseq = seq // tm
    return pl.pallas_call(
        _norm_proj_kernel,
        grid=(n // tm, PROJ_W // PROJ_TILE),
        in_specs=[
            pl.BlockSpec((tm, D_MODEL), lambda i, j: (i, 0)),
            pl.BlockSpec((1, D_MODEL), lambda i, j: (0, 0)),
            pl.BlockSpec((D_MODEL, PROJ_TILE), lambda i, j: (0, j)),
            pl.BlockSpec((tm, LANES), lambda i, j: (i % nseq, 0)),
            pl.BlockSpec((tm, LANES), lambda i, j: (i % nseq, 0)),
            pl.BlockSpec((tm, LANES), lambda i, j: (i % nseq, 0)),
        ],
        out_specs=pl.BlockSpec((tm, PROJ_TILE), lambda i, j: (i, j)),
        out_shape=jax.ShapeDtypeStruct((n, PROJ_W), BF16),
        scratch_shapes=[pltpu.VMEM((tm, D_MODEL), BF16)],
        compiler_params=_cparams(("parallel", "arbitrary")),
        name="norm_proj",
    )(h, g, w, rope_c, rope_s1, rope_s2)


def _sgu_kernel(z_ref, g_ref, w_ref, b_ref, o_ref):
    tm = z_ref.shape[0]
    a = jax.nn.gelu(z_ref[...].astype(F32))
    u = a[:, :MIX_WIDTH]
    v = _rms(a[:, MIX_WIDTH:], g_ref[...]).astype(BF16)
    row = lax.broadcasted_iota(jnp.int32, (SGU_CHUNK, SGU_CHUNK), 0)
    col = lax.broadcasted_iota(jnp.int32, (SGU_CHUNK, SGU_CHUNK), 1)
    ws = [jnp.where(row >= col, w_ref[gi], 0.0).astype(BF16) for gi in range(N_HEADS)]
    bias = b_ref[...]
    for c in range(tm // SGU_CHUNK):
        rows = slice(c * SGU_CHUNK, (c + 1) * SGU_CHUNK)
        mix = jnp.concatenate(
            [jnp.dot(ws[gi], v[rows, gi * HEAD_DIM:(gi + 1) * HEAD_DIM], preferred_element_type=F32)
             for gi in range(N_HEADS)], axis=1)
        o_ref[rows, :] = (u[rows, :] * (mix + bias)).astype(o_ref.dtype)


def _sgu(proj, g, w, bias, tm=512):
    n = proj.shape[0]
    return pl.pallas_call(
        _sgu_kernel,
        grid=(n // tm,),
        in_specs=[
            pl.BlockSpec((tm, 2 * MIX_WIDTH), lambda i: (i, COL_A // (2 * MIX_WIDTH))),
            pl.BlockSpec((1, MIX_WIDTH), lambda i: (0, 0)),
            pl.BlockSpec((N_HEADS, SGU_CHUNK, SGU_CHUNK), lambda i: (0, 0, 0)),
            pl.BlockSpec((SGU_CHUNK, MIX_WIDTH), lambda i: (0, 0)),
        ],
        out_specs=pl.BlockSpec((tm, MIX_WIDTH), lambda i: (i, 0)),
        out_shape=jax.ShapeDtypeStruct((n, MIX_WIDTH), BF16),
        compiler_params=_cparams(("parallel",)),
        name="sgu",
    )(proj, g, w, bias)


SB_TQ = 1024
SB_TK = 512
SB_SUB = 256
SB_R = SB_TQ // SB_TK


def _sb_tile(q_ref, k_ref, v_ref, u_ref, acc_ref, carry_ref, k_off):
    for j in reversed(range(SB_TK // SB_SUB)):
        keys = slice(j * SB_SUB, (j + 1) * SB_SUB)
        _sb_sub_tile(q_ref, k_ref[keys, :], v_ref[keys, :], u_ref[...], acc_ref, carry_ref,
                     None if k_off is None else k_off + j * SB_SUB)


def _sb_sub_tile(q_ref, k, v, u, acc_ref, carry_ref, k_off):
    diag = k_off is not None
    rows = slice(k_off if diag else 0, SB_TQ)
    n_rows = rows.stop - rows.start
    q = q_ref[rows, :]
    if diag:
        mask = (lax.broadcasted_iota(jnp.int32, (n_rows, SB_SUB), 1)
                < lax.broadcasted_iota(jnp.int32, (n_rows, SB_SUB), 0))
    pvs = []
    for h in range(N_HEADS):
        hs = slice(h * HEAD_DIM, (h + 1) * HEAD_DIM)
        z = lax.dot_general(q[:, hs], k[:, hs], (((1,), (1,)), ((), ())), preferred_element_type=F32)
        log_beta = jnp.minimum(z, 0.0) - jnp.log(1.0 + jnp.exp2(-jnp.abs(z))) * LOG2E
        log_1m = log_beta - z
        if diag:
            log_1m = jnp.where(mask, log_1m, 0.0)
        carry = carry_ref[h, rows, :]
        terms = log_1m.astype(BF16)
        r = jnp.dot(terms, u, preferred_element_type=F32)
        a = jnp.exp2(log_beta + r + carry)
        if diag:
            a = jnp.where(mask, a, 0.0)
        carry_ref[h, rows, :] = carry + r[:, 0:1] + terms[:, 0:1].astype(F32)
        pvs.append(jnp.dot(a.astype(BF16), v[:, hs], preferred_element_type=F32))
    acc_ref[rows, :] += jnp.concatenate(pvs, axis=1)


def _sb_kernel(qi_ref, ki_ref, q_ref, k_ref, v_ref, u_ref, o_ref, acc_ref, carry_ref):
    step = pl.program_id(1)
    qi = qi_ref[step]
    ki = ki_ref[step]

    @pl.when(ki == SB_R * qi + SB_R - 1)
    def _():
        acc_ref[...] = jnp.zeros_like(acc_ref)
        carry_ref[...] = jnp.zeros_like(carry_ref)

    for d in range(SB_R):
        @pl.when(ki == SB_R * qi + d)
        def _(d=d):
            _sb_tile(q_ref, k_ref, v_ref, u_ref, acc_ref, carry_ref, d * SB_TK)

    @pl.when(ki < SB_R * qi)
    def _():
        _sb_tile(q_ref, k_ref, v_ref, u_ref, acc_ref, carry_ref, None)

    @pl.when(ki == 0)
    def _():
        o_ref[...] = acc_ref[...].astype(o_ref.dtype)


def _stick_breaking(proj, batch, seq):
    n = proj.shape[0]
    nq, nk = seq // SB_TQ, seq // SB_TK
    qi_tab = np.array([qi for qi in range(nq) for _ in range(SB_R * (qi + 1))], np.int32)
    ki_tab = np.array([ki for qi in range(nq) for ki in range(SB_R * (qi + 1) - 1, -1, -1)], np.int32)
    u = jnp.asarray((np.arange(SB_SUB)[:, None] > np.arange(SB_SUB)[None, :]).astype(np.float32), BF16)
    qb, kb, vb = COL_SBQ // MIX_WIDTH, COL_SBK // MIX_WIDTH, COL_SBV // MIX_WIDTH
    grid_spec = pltpu.PrefetchScalarGridSpec(
        num_scalar_prefetch=2,
        grid=(batch, len(qi_tab)),
        in_specs=[
            pl.BlockSpec((SB_TQ, MIX_WIDTH), lambda b, s, qt, kt: (b * nq + qt[s], qb)),
            pl.BlockSpec((SB_TK, MIX_WIDTH), lambda b, s, qt, kt: (b * nk + kt[s], kb)),
            pl.BlockSpec((SB_TK, MIX_WIDTH), lambda b, s, qt, kt: (b * nk + kt[s], vb)),
            pl.BlockSpec((SB_SUB, SB_SUB), lambda b, s, qt, kt: (0, 0)),
        ],
        out_specs=pl.BlockSpec((SB_TQ, MIX_WIDTH), lambda b, s, qt, kt: (b * nq + qt[s], 0)),
        scratch_shapes=[pltpu.VMEM((SB_TQ, MIX_WIDTH), F32), pltpu.VMEM((N_HEADS, SB_TQ, 1), F32)],
    )
    return pl.pallas_call(
        _sb_kernel,
        grid_spec=grid_spec,
        out_shape=jax.ShapeDtypeStruct((n, MIX_WIDTH), BF16),
        compiler_params=_cparams(("parallel", "arbitrary")),
        name="stick_breaking",
    )(jnp.asarray(qi_tab), jnp.asarray(ki_tab), proj, proj, proj, u)


def _compress_kernel(gk_ref, gv_ref, pos_ref, w1_ref, w2_ref, kc_ref, vc_ref):
    half = CMP_STRIDE * HEAD_DIM
    for t, (g_ref, o_ref) in enumerate(((gk_ref, kc_ref), (gv_ref, vc_ref))):
        g = g_ref[0].astype(F32)
        top = (g + pos_ref[t, :, :half]).astype(BF16)
        bot = (g + pos_ref[t, :, half:]).astype(BF16)
        a = jnp.dot(top, w1_ref[t, :half, :], preferred_element_type=F32)
        b = jnp.dot(bot, w1_ref[t, half:, :], preferred_element_type=F32)
        hid = jax.nn.gelu(a + pltpu.roll(b, b.shape[0] - 1, 0))
        out = jnp.dot(hid.astype(BF16), w2_ref[t], preferred_element_type=F32)
        o_ref[0] = jnp.concatenate([out, jnp.zeros_like(out)], axis=1).astype(o_ref.dtype)


def _compress(gk, gv, pos, w1, w2):
    b, m, width = gk.shape
    out_spec = pl.BlockSpec((1, m, LANES), lambda i: (i, 0, 0))
    out_shape = jax.ShapeDtypeStruct((b, m, LANES), BF16)
    return pl.pallas_call(
        _compress_kernel,
        grid=(b,),
        in_specs=[
            pl.BlockSpec((1, m, width), lambda i: (i, 0, 0)),
            pl.BlockSpec((1, m, width), lambda i: (i, 0, 0)),
            pl.BlockSpec((2, 1, CMP_LEN * HEAD_DIM), lambda i: (0, 0, 0)),
            pl.BlockSpec((2, CMP_LEN * HEAD_DIM, CMP_HIDDEN), lambda i: (0, 0, 0)),
            pl.BlockSpec((2, CMP_HIDDEN, HEAD_DIM), lambda i: (0, 0, 0)),
        ],
        out_specs=[out_spec, out_spec],
        out_shape=[out_shape, out_shape],
        compiler_params=_cparams(("parallel",)),
        name="nsa_compress",
    )(gk, gv, pos, w1, w2)


NSA_TQ = 256
NSA_CK = 512


NSA_ROWS = N_HEADS * NSA_TQ
SEL_OFF = 1e30


def _per_head(x, fn):
    return jnp.concatenate([fn(x[h * NSA_TQ:(h + 1) * NSA_TQ]) for h in range(N_HEADS)], axis=0)


def _nsa_kernel(q_ref, kx_ref, vx_ref, kw_ref, vw_ref, kb_ref, kc_ref, vc_ref, ovt_ref, o_ref, m_ref, acc_ref):
    seq = kx_ref.shape[0]
    n_sel = seq // SEL_LEN
    n_cmp = kc_ref.shape[1]
    q0 = pl.program_id(1) * NSA_TQ
    NT = (((1,), (1,)), ((), ()))
    lo_half = lax.broadcasted_iota(jnp.int32, (NSA_TQ, LANES), 1) < HEAD_DIM
    qpos_c = q0 + lax.broadcasted_iota(jnp.int32, (NSA_TQ, 1), 0)

    qs = []
    for pair in range(N_HEADS // 2):
        x = q_ref[:, pair * LANES:(pair + 1) * LANES].astype(F32)
        qs += [jnp.where(lo_half, x, 0.0), jnp.where(lo_half, pltpu.roll(x, HEAD_DIM, 1), 0.0)]
    q4 = jnp.concatenate(qs, axis=0).astype(BF16)

    ones_half = lax.broadcasted_iota(jnp.int32, (1, LANES), 1) >= HEAD_DIM

    def masked_exp(s, mask):
        bias = jnp.where(mask, 0.0, NEG)
        s = _per_head(s, lambda t: t + bias)
        return jnp.exp2(s - jnp.max(s, axis=-1, keepdims=True))

    def attend(p, vals):
        return jnp.dot(p.astype(BF16), jnp.where(ones_half, 1.0, vals).astype(BF16), preferred_element_type=F32)

    def normalized(r):
        return r / jnp.where(ones_half, 1.0, pltpu.roll(r, HEAD_DIM, 1))

    cmp_end = lax.broadcasted_iota(jnp.int32, (NSA_TQ, n_cmp), 1) * CMP_STRIDE + (CMP_LEN - 1)
    p_cmp = masked_exp(lax.dot_general(q4, kc_ref[0], NT, preferred_element_type=F32), cmp_end <= qpos_c)
    has_cmp = jnp.concatenate([qpos_c >= CMP_LEN - 1] * N_HEADS, axis=0)
    p_cmp = p_cmp * jnp.where(has_cmp, 1.0 / jnp.sum(p_cmp, axis=-1, keepdims=True), 0.0)
    r_cmp = jnp.dot(p_cmp.astype(BF16), vc_ref[0], preferred_element_type=F32)
    p_sum = sum(p_cmp[h * NSA_TQ:(h + 1) * NSA_TQ] for h in range(N_HEADS))

    hi = p_sum.astype(BF16)
    lo = (p_sum - hi.astype(F32)).astype(BF16)
    imp = lax.dot_general(ovt_ref[...], jnp.concatenate([hi, lo], axis=1), NT, preferred_element_type=F32)
    blk = lax.broadcasted_iota(jnp.int32, (n_sel, NSA_TQ), 0)
    qpos_r = q0 + lax.broadcasted_iota(jnp.int32, (n_sel, NSA_TQ), 1)
    valid = blk * SEL_LEN <= qpos_r
    forced = (blk == 0) | (blk == jnp.right_shift(qpos_r, 6))
    score = jnp.where(valid, imp + jnp.where(forced, FORCE_SCORE, 0.0), -jnp.inf)
    groups = [score[8 * g:8 * g + 8] for g in range(n_sel // 8)]
    ranks = [jnp.zeros((8, NSA_TQ), F32) for _ in groups]
    row8 = lax.broadcasted_iota(jnp.int32, (8, NSA_TQ), 0)
    for i in range(n_sel):
        ci = jnp.broadcast_to(score[i:i + 1], (8, NSA_TQ))
        for g, sg in enumerate(groups):
            if 8 * g > i:
                beats = ci >= sg
            elif 8 * g + 7 < i:
                beats = ci > sg
            else:
                beats = (ci > sg) | ((ci == sg) & (row8 + 8 * g > i))
            ranks[g] = ranks[g] + jnp.where(beats, 1.0, 0.0)
    sel = jnp.where(valid & (jnp.concatenate(ranks, axis=0) < SEL_TOPN), 0.0, -SEL_OFF)
    sel = jnp.concatenate([sel, jnp.full((LANES - n_sel, NSA_TQ), -SEL_OFF, F32)], axis=0).T
    aug = lambda t: jnp.concatenate([q4, jnp.concatenate([t.astype(BF16)] * N_HEADS, axis=0)], axis=1)
    q_aug = aug(sel)
    blk_lane = lax.broadcasted_iota(jnp.int32, (NSA_TQ, LANES), 1)
    q_aug_past = aug(jnp.where(blk_lane * SEL_LEN >= q0, -SEL_OFF, sel))

    w_start = pl.multiple_of(jnp.maximum(q0 - WINDOW, 0), NSA_TQ)
    kwpos = w_start + lax.broadcasted_iota(jnp.int32, (NSA_TQ, WINDOW + NSA_TQ), 1)
    s_win = lax.dot_general(q4, kw_ref[pl.ds(w_start, WINDOW + NSA_TQ), :], NT, preferred_element_type=F32)
    p_win = masked_exp(s_win, (kwpos <= qpos_c) & (kwpos > qpos_c - WINDOW))
    r_win = normalized(attend(p_win, vw_ref[pl.ds(w_start, WINDOW + NSA_TQ), :]))

    def slab_stats(q, start, size, causal):
        keys = jnp.concatenate([kx_ref[pl.ds(start, size), :], kb_ref[pl.ds(start, size), :]], axis=1)
        s = lax.dot_general(q, keys, NT, preferred_element_type=F32)
        if causal:
            bias = jnp.where(start + lax.broadcasted_iota(jnp.int32, (NSA_TQ, size), 1) <= qpos_c, 0.0, NEG)
            s = _per_head(s, lambda t: t + bias)
        m_c = jnp.max(s, axis=-1, keepdims=True)
        return m_c, attend(jnp.exp2(s - m_c), vx_ref[pl.ds(start, size), :])

    m_d, a_d = slab_stats(q_aug, pl.multiple_of(q0, NSA_TQ), NSA_TQ, True)
    m_ref[...] = m_d
    acc_ref[...] = a_d

    def pair_body(i, carry):
        m0, a0 = slab_stats(q_aug_past, pl.multiple_of(2 * i * NSA_CK, NSA_CK), NSA_CK, False)
        m1, a1 = slab_stats(q_aug_past, pl.multiple_of((2 * i + 1) * NSA_CK, NSA_CK), NSA_CK, False)
        m_old = m_ref[...]
        m_new = jnp.maximum(m_old, jnp.maximum(m0, m1))
        acc_ref[...] = (jnp.exp2(m_old - m_new) * acc_ref[...] + jnp.exp2(m0 - m_new) * a0
                        + jnp.exp2(m1 - m_new) * a1)
        m_ref[...] = m_new
        return carry

    lax.fori_loop(0, (q0 + 2 * NSA_CK - 1) // (2 * NSA_CK), pair_body, 0)

    r_slc = normalized(acc_ref[...])
    gates = _sigmoid(vw_ref[pl.ds(pl.multiple_of(q0, NSA_TQ), NSA_TQ), :].astype(F32))
    g = [jnp.concatenate([gates[:, HEAD_DIM + 3 * h + t:HEAD_DIM + 3 * h + t + 1] for h in range(N_HEADS)], axis=0)
         for t in range(N_BRANCH)]
    y = g[0] * r_cmp + g[1] * r_slc + g[2] * r_win
    for pair in range(N_HEADS // 2):
        even = y[2 * pair * NSA_TQ:(2 * pair + 1) * NSA_TQ]
        odd = y[(2 * pair + 1) * NSA_TQ:(2 * pair + 2) * NSA_TQ]
        o_ref[:, pair * LANES:(pair + 1) * LANES] = jnp.where(lo_half, even, pltpu.roll(odd, HEAD_DIM, 1)).astype(o_ref.dtype)


def _nsa(proj, kc, vc, batch, seq):
    n = proj.shape[0]
    nqb = seq // NSA_TQ
    n_sel = seq // SEL_LEN
    n_cmp = kc.shape[1]
    assert n_sel % 8 == 0 and n_sel <= LANES and seq % (2 * NSA_CK) == 0
    cs = (np.arange(n_cmp) * CMP_STRIDE)[None, :]
    ss = (np.arange(n_sel) * SEL_LEN)[:, None]
    ovt = np.clip(np.minimum(cs + CMP_LEN, ss + SEL_LEN) - np.maximum(cs, ss), 0, None).astype(np.float32) / CMP_LEN
    ovt = jnp.asarray(np.concatenate([ovt, ovt], axis=1), BF16)
    kb = jnp.asarray((np.arange(seq)[:, None] // SEL_LEN == np.arange(LANES)[None, :]).astype(np.float32), BF16)
    seq_spec = lambda col: pl.BlockSpec((seq, LANES), lambda b, i: (b, col))
    cmp_spec = pl.BlockSpec((1, n_cmp, LANES), lambda b, i: (b, 0, 0))
    return pl.pallas_call(
        _nsa_kernel,
        grid=(batch, nqb),
        in_specs=[
            pl.BlockSpec((NSA_TQ, MIX_WIDTH), lambda b, i: (b * nqb + i, COL_CQ // MIX_WIDTH)),
            seq_spec(COL_KX // LANES),
            seq_spec(COL_VX // LANES),
            seq_spec(COL_KX // LANES + 1),
            seq_spec(COL_VX // LANES + 1),
            pl.BlockSpec((seq, LANES), lambda b, i: (0, 0)),
            cmp_spec, cmp_spec,
            pl.BlockSpec((n_sel, 2 * n_cmp), lambda b, i: (0, 0)),
        ],
        out_specs=pl.BlockSpec((NSA_TQ, MIX_WIDTH), lambda b, i: (b * nqb + i, 0)),
        out_shape=jax.ShapeDtypeStruct((n, MIX_WIDTH), BF16),
        scratch_shapes=[pltpu.VMEM((NSA_ROWS, 1), F32), pltpu.VMEM((NSA_ROWS, LANES), F32)],
        compiler_params=_cparams(("parallel", "arbitrary")),
        name="nsa",
    )(proj, proj, proj, proj, proj, kb, kc, vc, ovt)


def _top2_gate(xn, rw):
    lane = lax.broadcasted_iota(jnp.int32, (xn.shape[0], LANES), 1)
    logits = jnp.dot(xn, rw, preferred_element_type=F32, precision=lax.Precision.HIGHEST)
    logits = jnp.where(lane < N_EXPERTS, logits, -jnp.inf)
    m1 = jnp.max(logits, axis=-1, keepdims=True)
    i1 = jnp.min(jnp.where(logits == m1, lane, LANES), axis=-1, keepdims=True)
    rest = jnp.where(lane == i1, -jnp.inf, logits)
    m2 = jnp.max(rest, axis=-1, keepdims=True)
    i2 = jnp.min(jnp.where(rest == m2, lane, LANES), axis=-1, keepdims=True)
    e2 = jnp.exp(m2 - m1)
    return jnp.where(lane == i1, 1.0 / (1.0 + e2), 0.0) + jnp.where(lane == i2, e2 / (1.0 + e2), 0.0)


def _merge_kernel(ya_ref, yb_ref, yc_ref, g0_ref, g1_ref, g2_ref, wb_ref, wo_ref, h_ref, *rest):
    o_ref = rest[-2] if len(rest) == 4 else rest[0]
    merged = None
    for y_ref, g_ref, t in ((ya_ref, g0_ref, 0), (yb_ref, g1_ref, 1), (yc_ref, g2_ref, 2)):
        term = _sigmoid(g_ref[...].astype(F32)) * jnp.dot(y_ref[...], wb_ref[t], preferred_element_type=F32)
        merged = term if merged is None else merged + term
    h_new = h_ref[...] + jnp.dot(merged.astype(BF16), wo_ref[...], preferred_element_type=F32)
    o_ref[...] = h_new
    if len(rest) == 4:
        norm_ref, rw_ref, _, gate_ref = rest
        gate_ref[...] = _top2_gate(_rms(h_new, norm_ref[...]), rw_ref[...])


def _merge(ya, yb, yc, proj, wb, wo, h, router=None, tm=1024):
    n = h.shape[0]
    y_spec = pl.BlockSpec((tm, MIX_WIDTH), lambda i: (i, 0))
    g_spec = lambda t: pl.BlockSpec((tm, D_MODEL), lambda i: (i, COL_BRG // D_MODEL + t))
    in_specs = [y_spec, y_spec, y_spec, g_spec(0), g_spec(1), g_spec(2),
                pl.BlockSpec((N_BRANCH, MIX_WIDTH, D_MODEL), lambda i: (0, 0, 0)),
                pl.BlockSpec((D_MODEL, D_MODEL), lambda i: (0, 0)),
                pl.BlockSpec((tm, D_MODEL), lambda i: (i, 0))]
    out_specs = pl.BlockSpec((tm, D_MODEL), lambda i: (i, 0))
    out_shape = jax.ShapeDtypeStruct((n, D_MODEL), F32)
    args = (ya, yb, yc, proj, proj, proj, wb, wo, h)
    if router is not None:
        in_specs += [pl.BlockSpec((1, D_MODEL), lambda i: (0, 0)), pl.BlockSpec((D_MODEL, LANES), lambda i: (0, 0))]
        out_specs = [out_specs, pl.BlockSpec((tm, LANES), lambda i: (i, 0))]
        out_shape = [out_shape, jax.ShapeDtypeStruct((n, LANES), F32)]
        args += tuple(router)
    return pl.pallas_call(
        _merge_kernel,
        grid=(n // tm,),
        in_specs=in_specs,
        out_specs=out_specs,
        out_shape=out_shape,
        compiler_params=_cparams(("parallel",)),
        name="merge",
    )(*args)


def _ffn_kernel(h_ref, g_ref, wg_ref, wu_ref, wd_ref, o_ref, xn_ref, acc_ref):
    f = pl.program_id(1)

    @pl.when(f == 0)
    def _():
        xn_ref[...] = _rms(h_ref[...], g_ref[...]).astype(BF16)
        acc_ref[...] = h_ref[...]

    xn = xn_ref[...]
    a = jnp.dot(xn, wg_ref[...], preferred_element_type=F32)
    u = jnp.dot(xn, wu_ref[...], preferred_element_type=F32)
    hid = (a * _sigmoid(a) * u).astype(BF16)
    acc_ref[...] += jnp.dot(hid, wd_ref[...], preferred_element_type=F32)

    @pl.when(f == pl.num_programs(1) - 1)
    def _():
        o_ref[...] = acc_ref[...]


def _ffn(h, g, wg, wu, wd, tm=512, tf=1408):
    n = h.shape[0]
    dff = wg.shape[1]
    return pl.pallas_call(
        _ffn_kernel,
        grid=(n // tm, dff // tf),
        in_specs=[
            pl.BlockSpec((tm, D_MODEL), lambda i, f: (i, 0)),
            pl.BlockSpec((1, D_MODEL), lambda i, f: (0, 0)),
            pl.BlockSpec((D_MODEL, tf), lambda i, f: (0, f)),
            pl.BlockSpec((D_MODEL, tf), lambda i, f: (0, f)),
            pl.BlockSpec((tf, D_MODEL), lambda i, f: (f, 0)),
        ],
        out_specs=pl.BlockSpec((tm, D_MODEL), lambda i, f: (i, 0)),
        out_shape=jax.ShapeDtypeStruct((n, D_MODEL), F32),
        scratch_shapes=[pltpu.VMEM((tm, D_MODEL), BF16), pltpu.VMEM((tm, D_MODEL), F32)],
        compiler_params=_cparams(("parallel", "arbitrary")),
        name="ffn",
    )(h, g, wg, wu, wd)


RT_TM = 1024
RT_TF = 896
RT_TD = 512


def _row_copy(src_ref, src_row, dst_ref, dst_row, sem):
    return pltpu.make_async_copy(src_ref.at[pl.ds(src_row, 1), :], dst_ref.at[pl.ds(dst_row, 1), :], sem)


def _dispatch_kernel(pad_ref, pa_ref, pb_ref, h_ref, xs_ref, hbuf_ref, zero_ref, fetch_sems, row_sems, sems):
    i, n_steps = pl.program_id(0), pl.num_programs(0)
    slot, parity = i % 3, i % 2

    def fetch(block, s):
        return pltpu.make_async_copy(h_ref.at[pl.ds(block * RT_TD, RT_TD), :], hbuf_ref.at[s], fetch_sems.at[s])

    @pl.when(i == 0)
    def _():
        fetch(0, 0).start()

    @pl.when(i + 1 < n_steps)
    def _():
        fetch(i + 1, (i + 1) % 3).start()

    fetch(i, slot).wait()
    src = hbuf_ref.at[slot]

    def send(t, carry):
        _row_copy(src, t, xs_ref, pa_ref[0, 0, t], row_sems.at[0, parity]).start()
        _row_copy(src, t, xs_ref, pb_ref[0, 0, t], row_sems.at[1, parity]).start(priority=1)
        return carry

    lax.fori_loop(0, RT_TD, send, 0, unroll=4)

    @pl.when(i == 0)
    def _():
        zero_ref[...] = jnp.zeros_like(zero_ref)
        for e in range(N_EXPERTS + 1):
            lo, hi = pad_ref[2 * e], pad_ref[2 * e + 1]
            lax.fori_loop(lo, hi, lambda r, c: (_row_copy(zero_ref, 0, xs_ref, r, sems.at[0]).start(), c)[1], 0)
        for e in range(N_EXPERTS + 1):
            lo, hi = pad_ref[2 * e], pad_ref[2 * e + 1]
            lax.fori_loop(lo, hi, lambda r, c: (_row_copy(zero_ref, 0, xs_ref, r, sems.at[0]).wait(), c)[1], 0)

    def drain(p):
        for s in range(2):
            pltpu.make_async_copy(hbuf_ref.at[0], xs_ref.at[pl.ds(0, RT_TD), :], row_sems.at[s, p]).wait()

    @pl.when(i > 0)
    def _():
        drain(1 - parity)

    @pl.when(i == n_steps - 1)
    def _():
        drain(parity)


def _dispatch(pad_bounds, pa, pb, h, n_rows):
    n = h.shape[0]
    idx_spec = pl.BlockSpec((1, 1, RT_TD), lambda i, *_: (i, 0, 0), memory_space=pltpu.SMEM)
    grid_spec = pltpu.PrefetchScalarGridSpec(
        num_scalar_prefetch=1,
        grid=(n // RT_TD,),
        in_specs=[idx_spec, idx_spec, pl.BlockSpec(memory_space=pl.ANY)],
        out_specs=pl.BlockSpec(memory_space=pl.ANY),
        scratch_shapes=[pltpu.VMEM((3, RT_TD, D_MODEL), F32), pltpu.VMEM((8, D_MODEL), F32),
                        pltpu.SemaphoreType.DMA((3,)), pltpu.SemaphoreType.DMA((2, 2)),
                        pltpu.SemaphoreType.DMA((1,))],
    )
    return pl.pallas_call(
        _dispatch_kernel,
        grid_spec=grid_spec,
        out_shape=jax.ShapeDtypeStruct((n_rows, D_MODEL), F32),
        compiler_params=_cparams(("arbitrary",)),
        name="moe_dispatch",
    )(pad_bounds, pa.reshape(-1, 1, RT_TD), pb.reshape(-1, 1, RT_TD), h)


def _sorted_experts_kernel(te_ref, tx_ref, live_ref, xs_ref, g_ref, wg_ref, wu_ref, wd_ref, y_ref, xb_ref):
    i, f = pl.program_id(0), pl.program_id(1)

    @pl.when(f == 0)
    def _():
        xb_ref[...] = _rms(xs_ref[...], g_ref[...]).astype(BF16)
        y_ref[...] = jnp.zeros_like(y_ref)

    @pl.when(live_ref[i] > 0)
    def _():
        x = xb_ref[...]
        a = jnp.dot(x, wg_ref[0], preferred_element_type=F32)
        u = jnp.dot(x, wu_ref[0], preferred_element_type=F32)
        y_ref[...] += jnp.dot((a * _sigmoid(a) * u).astype(BF16), wd_ref[0], preferred_element_type=F32)


def _sorted_experts(tile_expert, tile_src, tile_live, xs, g, wg, wu, wd):
    n_rows = xs.shape[0]
    ne, _, dff = wg.shape
    grid_spec = pltpu.PrefetchScalarGridSpec(
        num_scalar_prefetch=3,
        grid=(n_rows // RT_TM, dff // RT_TF),
        in_specs=[
            pl.BlockSpec((RT_TM, D_MODEL), lambda i, f, te, tx, lv: (tx[i], 0)),
            pl.BlockSpec((1, D_MODEL), lambda i, f, *_: (0, 0)),
            pl.BlockSpec((1, D_MODEL, RT_TF), lambda i, f, te, tx, lv: (te[i], 0, f)),
            pl.BlockSpec((1, D_MODEL, RT_TF), lambda i, f, te, tx, lv: (te[i], 0, f)),
            pl.BlockSpec((1, RT_TF, D_MODEL), lambda i, f, te, tx, lv: (te[i], f, 0)),
        ],
        out_specs=pl.BlockSpec((RT_TM, D_MODEL), lambda i, f, *_: (i, 0)),
        scratch_shapes=[pltpu.VMEM((RT_TM, D_MODEL), BF16)],
    )
    return pl.pallas_call(
        _sorted_experts_kernel,
        grid_spec=grid_spec,
        out_shape=jax.ShapeDtypeStruct((n_rows, D_MODEL), F32),
        compiler_params=_cparams(("parallel", "arbitrary")),
        name="moe_experts",
    )(tile_expert, tile_src, tile_live, xs, g, wg, wu, wd)


def _combine_kernel(pa_ref, pb_ref, pa_next_ref, pb_next_ref, ga_ref, gb_ref, h_ref, y_ref, o_ref, ya_ref, yb_ref, sems):
    i, n_steps = pl.program_id(0), pl.num_programs(0)
    slot = i % 2

    def fetch(pa, pb, s):
        def body(t, carry):
            _row_copy(y_ref, pa[0, 0, t], ya_ref.at[s], t, sems.at[0, s]).start()
            _row_copy(y_ref, pb[0, 0, t], yb_ref.at[s], t, sems.at[1, s]).start(priority=1)
            return carry

        lax.fori_loop(0, RT_TD, body, 0, unroll=4)

    @pl.when(i == 0)
    def _():
        fetch(pa_ref, pb_ref, 0)

    @pl.when(i + 1 < n_steps)
    def _():
        fetch(pa_next_ref, pb_next_ref, 1 - slot)

    pltpu.make_async_copy(y_ref.at[pl.ds(0, RT_TD), :], ya_ref.at[slot], sems.at[0, slot]).wait()
    pltpu.make_async_copy(y_ref.at[pl.ds(0, RT_TD), :], yb_ref.at[slot], sems.at[1, slot]).wait()
    o_ref[...] = h_ref[...] + ga_ref[...] * ya_ref[slot] + gb_ref[...] * yb_ref[slot]


def _combine(pa, pb, ga, gb, h, y):
    n = h.shape[0]
    n_steps = n // RT_TD
    idx_spec = pl.BlockSpec((1, 1, RT_TD), lambda i: (i, 0, 0), memory_space=pltpu.SMEM)
    next_spec = pl.BlockSpec((1, 1, RT_TD), lambda i: (jnp.minimum(i + 1, n_steps - 1), 0, 0), memory_space=pltpu.SMEM)
    col_spec = pl.BlockSpec((RT_TD, 1), lambda i: (i, 0))
    row_spec = pl.BlockSpec((RT_TD, D_MODEL), lambda i: (i, 0))
    pa, pb = pa.reshape(-1, 1, RT_TD), pb.reshape(-1, 1, RT_TD)
    return pl.pallas_call(
        _combine_kernel,
        grid=(n_steps,),
        in_specs=[idx_spec, idx_spec, next_spec, next_spec, col_spec, col_spec, row_spec,
                  pl.BlockSpec(memory_space=pl.ANY)],
        out_specs=row_spec,
        out_shape=jax.ShapeDtypeStruct((n, D_MODEL), F32),
        scratch_shapes=[pltpu.VMEM((2, RT_TD, D_MODEL), F32), pltpu.VMEM((2, RT_TD, D_MODEL), F32),
                        pltpu.SemaphoreType.DMA((2, 2))],
        compiler_params=_cparams(("arbitrary",)),
        name="moe_combine",
    )(pa, pb, pa, pb, ga, gb, h, y)


def _moe_sorted(h, gate, g, wg, wu, wd):
    n = h.shape[0]
    n_tiles = 2 * n // RT_TM + N_EXPERTS
    gate = gate[:, :N_EXPERTS]
    routed = gate > 0
    r = routed.astype(jnp.int32)
    incl = jnp.cumsum(r, axis=0)
    count = incl[-1]
    padded = (count + RT_TM - 1) // RT_TM * RT_TM
    ends = jnp.cumsum(padded)
    pos = (ends - padded)[None, :] + incl - 1
    order = jnp.cumsum(r, axis=1)
    first, second = routed & (order == 1), routed & (order == 2)
    pick = lambda m, v: jnp.sum(jnp.where(m, v, 0), axis=1)
    pa = pick(first, pos)
    pb = jnp.where(jnp.any(second, axis=1), pick(second, pos), pa)
    ga, gb = pick(first, gate)[:, None], pick(second, gate)[:, None]
    pad_lo = jnp.concatenate([ends - padded + count, ends[-1:]])
    pad_hi = jnp.concatenate([ends, jnp.full((1,), n_tiles * RT_TM, ends.dtype)])
    pad_bounds = jnp.stack([pad_lo, pad_hi], axis=1).reshape(-1).astype(jnp.int32)
    tile_start = jnp.arange(n_tiles, dtype=jnp.int32) * RT_TM
    tile_live = (tile_start < ends[-1]).astype(jnp.int32)
    tile_expert = jnp.minimum(jnp.sum(tile_start[:, None] >= ends[None, :], axis=1), N_EXPERTS - 1).astype(jnp.int32)
    tile_src = jnp.minimum(jnp.arange(n_tiles, dtype=jnp.int32), ends[-1] // RT_TM - 1)
    xs = _dispatch(pad_bounds, pa.astype(jnp.int32), pb.astype(jnp.int32), h, n_tiles * RT_TM)
    y = _sorted_experts(tile_expert, tile_src, tile_live, xs, g, wg, wu, wd)
    return _combine(pa.astype(jnp.int32), pb.astype(jnp.int32), ga, gb, h, y)


def _final_norm_kernel(h_ref, g_ref, o_ref):
    o_ref[...] = _rms(h_ref[...], g_ref[...])


def _final_norm(h, g, tm=2048):
    n = h.shape[0]
    return pl.pallas_call(
        _final_norm_kernel,
        grid=(n // tm,),
        in_specs=[pl.BlockSpec((tm, D_MODEL), lambda i: (i, 0)), pl.BlockSpec((1, D_MODEL), lambda i: (0, 0))],
        out_specs=pl.BlockSpec((tm, D_MODEL), lambda i: (i, 0)),
        out_shape=jax.ShapeDtypeStruct((n, D_MODEL), F32),
        compiler_params=_cparams(("parallel",)),
        name="final_norm",
    )(h, g)


def _rope_tables(seq):
    half = ROPE_DIMS // 2
    inv_freq = ROPE_THETA ** (-jnp.arange(0, ROPE_DIMS, 2, dtype=F32) / ROPE_DIMS)
    ang = jnp.arange(seq, dtype=F32)[:, None] * inv_freq[None, :]
    cos, sin = jnp.cos(ang), jnp.sin(ang)
    ones = jnp.ones((seq, HEAD_DIM - ROPE_DIMS), F32)
    zeros_h = jnp.zeros((seq, half), F32)
    zeros_r = jnp.zeros((seq, HEAD_DIM - ROPE_DIMS), F32)
    c = jnp.concatenate([cos, cos, ones], axis=1)
    s1 = jnp.concatenate([-sin, zeros_h, zeros_r], axis=1)
    s2 = jnp.concatenate([zeros_h, sin, zeros_r], axis=1)
    rep = LANES // HEAD_DIM
    return tuple(jnp.tile(t, (1, rep)) for t in (c, s1, s2))


def _layout_w_in(w):
    scale = HEAD_DIM ** -0.5 * LOG2E
    a_in, sb_q, sb_k, sb_v, c_q, c_kv, c_g, br_g = jnp.split(
        w, np.cumsum([2 * MIX_WIDTH, MIX_WIDTH, MIX_WIDTH, MIX_WIDTH, MIX_WIDTH, 6 * HEAD_DIM, 3 * N_HEADS])[:].tolist(),
        axis=-1)
    k_cmp, v_cmp, k_slc, v_slc, k_win, v_win = jnp.split(c_kv, 6, axis=-1)
    zeros = lambda width: jnp.zeros((w.shape[0], width), w.dtype)
    cols = [a_in, sb_q * scale, sb_k, sb_v,
            v_slc, v_cmp, v_win, c_g, zeros(HEAD_DIM - 3 * N_HEADS),
            c_q * scale, k_slc, k_cmp, k_win, zeros(HEAD_DIM),
            br_g]
    out = jnp.concatenate(cols, axis=-1)
    assert out.shape[-1] == PROJ_W
    return out.astype(BF16)


def kernel(x, norm1_g, w_in, sgu_norm_g, sgu_w, sgu_b, cmp_pos, cmp_w1, cmp_w2, w_branch, w_out, norm2_g,
           ffn_w_gate, ffn_w_up, ffn_w_down, router_w, moe_w_gate, moe_w_up, moe_w_down, final_norm_g):
    batch, seq, _ = x.shape
    n = batch * seq
    depth = norm1_g.shape[0]
    rope_c, rope_s1, rope_s2 = _rope_tables(seq)
    dff_pad = -(-D_FF // 256) * 256
    h = x.reshape(n, D_MODEL)
    for layer in range(depth):
        proj = _norm_proj(h, norm1_g[layer][None, :], _layout_w_in(w_in[layer]), rope_c, rope_s1, rope_s2, seq)
        sgu_bias = jnp.repeat(sgu_b[layer].T, HEAD_DIM, axis=1)
        y_a = _sgu(proj, sgu_norm_g[layer][None, :], sgu_w[layer], sgu_bias)
        y_b = _stick_breaking(proj, batch, seq)
        groups = seq // CMP_STRIDE
        gk = proj[:, COL_KX + HEAD_DIM:COL_KX + 2 * HEAD_DIM].reshape(batch, groups, CMP_STRIDE * HEAD_DIM)
        gv = proj[:, COL_VX + HEAD_DIM:COL_VX + 2 * HEAD_DIM].reshape(batch, groups, CMP_STRIDE * HEAD_DIM)
        kc, vc = _compress(gk, gv, cmp_pos[layer].reshape(2, 1, CMP_LEN * HEAD_DIM),
                           cmp_w1[layer].astype(BF16), cmp_w2[layer].astype(BF16))
        y_c = _nsa(proj, kc, vc, batch, seq)
        j = layer // 2
        g2 = norm2_g[layer][None, :]
        wb, wo = w_branch[layer].astype(BF16), w_out[layer].astype(BF16)
        if layer % 2 == 0:
            h = _merge(y_a, y_b, y_c, proj, wb, wo, h)
            pad = dff_pad - D_FF
            wg = jnp.pad(ffn_w_gate[j], ((0, 0), (0, pad))).astype(BF16)
            wu = jnp.pad(ffn_w_up[j], ((0, 0), (0, pad))).astype(BF16)
            wd = jnp.pad(ffn_w_down[j], ((0, pad), (0, 0))).astype(BF16)
            h = _ffn(h, g2, wg, wu, wd)
        else:
            rw = jnp.pad(router_w[j], ((0, 0), (0, LANES - N_EXPERTS)))
            h, gate = _merge(y_a, y_b, y_c, proj, wb, wo, h, router=(g2, rw))
            h = _moe_sorted(h, gate, g2, moe_w_gate[j].astype(BF16), moe_w_up[j].astype(BF16), moe_w_down[j].astype(BF16))
    return _final_norm(h, final_norm_g[None, :]).reshape(batch, seq, D_MODEL)
```
